```python
import math
import jax, jax.numpy as jnp
from jax import lax
import numpy as np

D_MODEL = 1024
BATCH = 2
SEQ = 8192
DEPTH = 4
DEC_BATCH = 32
DEC_SEQ = 8
PAST_LEN = 8192
PAGE_SIZE = 128

HEAD_DIM = 64
N_EVEN = (DEPTH + 1) // 2
N_ODD = DEPTH // 2
NSA_HEADS = 8
NSA_KV = 2
NSA_REP = NSA_HEADS // NSA_KV
CMP_STRIDE = 16
CMP_BLOCK = 2 * CMP_STRIDE
CMP_HIDDEN = 2 * HEAD_DIM
SEL_BLOCK = 64
N_SELECT = 16
WINDOW = 512
Q_BLOCK = 128
ROPE_DIM = HEAD_DIM // 4
ROPE_THETA = 500000.0
RWKV_HEADS = 8
RWKV_WIDTH = RWKV_HEADS * HEAD_DIM
LORA_W = 64
LORA_A = 64
LORA_G = 128
RWKV_GN_EPS = 64e-5
GLA_HEADS = 4
GLA_DK = D_MODEL // 2 // GLA_HEADS
GLA_DV = D_MODEL // GLA_HEADS
GLA_RANK = 16
GLA_TAU = 16.0
GLA_CHUNK = 64
D_FF = ((8 * D_MODEL // 3 + 255) // 256) * 256
NSA_Q = NSA_HEADS * HEAD_DIM
NSA_KVW = 3 * 2 * NSA_KV * HEAD_DIM
NSA_IN = NSA_Q + NSA_KVW + 3 * NSA_HEADS
RWKV_IN = 3 * RWKV_WIDTH + LORA_W + LORA_A + LORA_G
EVEN_IN = NSA_IN + RWKV_IN
MIX_WIDTH = NSA_Q + RWKV_WIDTH
GLA_KW = GLA_HEADS * GLA_DK
GLA_VW = GLA_HEADS * GLA_DV
ODD_IN = 2 * GLA_KW + GLA_VW + GLA_RANK + GLA_VW

kernel_name = 'hybrid_nsa_rwkv7_gla_decode_step'


def rmsnorm(x, g, eps=1e-6):
    xf = x.astype(jnp.float32)
    y = xf * lax.rsqrt(jnp.mean(xf * xf, axis=-1, keepdims=True) + eps)
    return (y * g.astype(jnp.float32)).astype(x.dtype)


def rope(x, pos):
    half = ROPE_DIM // 2
    inv = ROPE_THETA ** (-jnp.arange(half, dtype=jnp.float32) / half)
    ang = pos.astype(jnp.float32)[:, None] * inv[None, :]
    shp = (pos.shape[0],) + (1,) * (x.ndim - 3) + (half,)
    cos, sin = jnp.cos(ang).reshape(shp), jnp.sin(ang).reshape(shp)
    xr = x[..., :ROPE_DIM].astype(jnp.float32)
    x1, x2 = xr[..., :half], xr[..., half:]
    rot = jnp.concatenate([x1 * cos - x2 * sin, x2 * cos + x1 * sin], axis=-1)
    return jnp.concatenate([rot.astype(x.dtype), x[..., ROPE_DIM:]], axis=-1)


def masked_softmax(s, mask):
    p = jax.nn.softmax(jnp.where(mask, s, -1e30), axis=-1)
    return jnp.where(mask, p, 0.0)


def swiglu(x, wg, wu, wd):
    return (jax.nn.silu(x @ wg) * (x @ wu)) @ wd


def nsa_project(h, pos, qn, kn):
    B, T = h.shape[:2]
    q = h[..., :NSA_Q].reshape(B, T, NSA_HEADS, HEAD_DIM)
    kv = h[..., NSA_Q:NSA_Q + NSA_KVW].reshape(B, T, 3, 2, NSA_KV, HEAD_DIM)
    gates = jax.nn.sigmoid(h[..., NSA_Q + NSA_KVW:NSA_IN].astype(jnp.float32)).reshape(B, T, NSA_HEADS, 3)
    q = rope(rmsnorm(q, qn), pos)
    k = rope(rmsnorm(kv[:, :, :, 0], kn[:, None, :]), pos)
    kv = jnp.stack([k, kv[:, :, :, 1]], axis=3)
    rows = kv[:, :, :2].reshape(B, T, 4, NSA_KV, HEAD_DIM)
    win = kv[:, :, 2]
    return q, gates, rows, win


def compress(k, w1, w2, pe):
    B, L = k.shape[:2]
    n_chunks = L // CMP_STRIDE
    c = k[:, :n_chunks * CMP_STRIDE].reshape(B, n_chunks, CMP_STRIDE, NSA_KV, HEAD_DIM)
    h = (jnp.einsum('bnlgd,ldh->bngh', c[:, :-1], w1[:CMP_STRIDE])
         + jnp.einsum('bnlgd,ldh->bngh', c[:, 1:], w1[CMP_STRIDE:])
         + jnp.einsum('ld,ldh->h', pe, w1))
    return jnp.einsum('bngh,hd->bngd', jax.nn.gelu(h), w2)


def nsa_keys(rows, cw1, cw2, cpe):
    B, L = rows.shape[:2]
    kc = compress(rows[:, :, 0], cw1[0], cw2[0], cpe[0])
    vc = compress(rows[:, :, 1], cw1[1], cw2[1], cpe[1])
    n_cmp = kc.shape[1]
    cmp_start = jnp.arange(n_cmp, dtype=jnp.int32) * CMP_STRIDE
    cmp_end = cmp_start + CMP_BLOCK - 1
    n_sel = -(-L // SEL_BLOCK)
    sel_start = jnp.arange(n_sel, dtype=jnp.int32) * SEL_BLOCK
    sel_map = ((cmp_start[:, None] <= sel_start[None, :] + SEL_BLOCK - 1)
               & (cmp_end[:, None] >= sel_start[None, :])).astype(jnp.float32)
    slc = jnp.pad(rows[:, :, 2:4], ((0, 0), (0, n_sel * SEL_BLOCK - L), (0, 0), (0, 0), (0, 0)))
    slc = slc.reshape(B, n_sel, SEL_BLOCK, 2, NSA_KV, HEAD_DIM).transpose(3, 0, 4, 1, 2, 5)
    return kc, vc, cmp_end, sel_map, slc[0], slc[1]


def nsa_block(q, gates, q_pos, kc, vc, cmp_end, sel_map, ksb, vsb, win, w_pos):
    f32 = jnp.float32
    B, QB = q.shape[:2]
    qg = q.reshape(B, QB, NSA_KV, NSA_REP, HEAD_DIM).astype(f32) * (HEAD_DIM ** -0.5)
    tq = q_pos[:, None]
    s = jnp.einsum('bqgrd,bngd->bqgrn', qg, kc.astype(f32))
    p = masked_softmax(s, (cmp_end[None, :] <= tq)[None, :, None, None, :])
    o_cmp = jnp.einsum('bqgrn,bngd->bqgrd', p, vc.astype(f32))
    imp = jnp.einsum('bqgrn,nj->bqgj', p, sel_map)
    n_sel = sel_map.shape[1]
    blk = jnp.arange(n_sel, dtype=jnp.int32)[None, :]
    cur = tq // SEL_BLOCK
    forced = (blk == 0) | (blk == cur) | (blk == cur - 1)
    valid = blk * SEL_BLOCK <= tq
    imp = jnp.where(valid[None, :, None, :], jnp.where(forced[None, :, None, :], 1e30, imp), -1e30)
    n_top = min(N_SELECT, n_sel)
    _, idx = lax.top_k(imp, n_top)
    bi = jnp.arange(B)[:, None, None, None]
    gi = jnp.arange(NSA_KV)[None, None, :, None]
    nk = n_top * SEL_BLOCK
    k_sel = ksb[bi, gi, idx].reshape(B, QB, NSA_KV, nk, HEAD_DIM).astype(f32)
    v_sel = vsb[bi, gi, idx].reshape(B, QB, NSA_KV, nk, HEAD_DIM).astype(f32)
    k_pos = (idx[..., None] * SEL_BLOCK + jnp.arange(SEL_BLOCK, dtype=jnp.int32)).reshape(B, QB, NSA_KV, nk)
    s = jnp.einsum('bqgrd,bqgkd->bqgrk', qg, k_sel)
    p = masked_softmax(s, (k_pos <= q_pos[None, :, None, None])[:, :, :, None, :])
    o_slc = jnp.einsum('bqgrk,bqgkd->bqgrd', p, v_sel)
    s = jnp.einsum('bqgrd,bkgd->bqgrk', qg, win[:, :, 0].astype(f32))
    wp = w_pos[None, :]
    m = (wp <= tq) & (wp > tq - WINDOW) & (wp >= 0)
    p = masked_softmax(s, m[None, :, None, None, :])
    o_win = jnp.einsum('bqgrk,bkgd->bqgrd', p, win[:, :, 1].astype(f32))
    g = gates.reshape(B, QB, NSA_KV, NSA_REP, 3)
    o = g[..., 0:1] * o_cmp + g[..., 1:2] * o_slc + g[..., 2:3] * o_win
    return o.reshape(B, QB, NSA_Q)


def nsa_prompt(q, gates, rows, win, cw1, cw2, cpe):
    B, T = q.shape[:2]
    kc, vc, cmp_end, sel_map, ksb, vsb = nsa_keys(rows, cw1, cw2, cpe)
    win_pad = jnp.pad(win, ((0, 0), (WINDOW, 0), (0, 0), (0, 0), (0, 0)))
    nb = T // Q_BLOCK
    qb = q.reshape(B, nb, Q_BLOCK, NSA_HEADS, HEAD_DIM).swapaxes(0, 1)
    gb = gates.reshape(B, nb, Q_BLOCK, NSA_HEADS, 3).swapaxes(0, 1)

    def body(args):
        i, q_i, g_i = args
        s0 = i * Q_BLOCK
        w_i = lax.dynamic_slice_in_dim(win_pad, s0, WINDOW + Q_BLOCK, axis=1)
        q_pos = s0 + jnp.arange(Q_BLOCK, dtype=jnp.int32)
        w_pos = s0 - WINDOW + jnp.arange(WINDOW + Q_BLOCK, dtype=jnp.int32)
        return nsa_block(q_i, g_i, q_pos, kc, vc, cmp_end, sel_map, ksb, vsb, w_i, w_pos)

    o = lax.map(body, (jnp.arange(nb, dtype=jnp.int32), qb, gb))
    return o.swapaxes(0, 1).reshape(B, T, NSA_Q)


def rwkv_scan(r, w, k, v, kk, a, S0):
    def step(S, inp):
        r_t, w_t, k_t, v_t, kk_t, a_t = inp
        sa = jnp.einsum('bhvk,bhk->bhv', S, -kk_t)
        S = (S * w_t[:, :, None, :] + sa[..., None] * (kk_t * a_t)[:, :, None, :]
             + v_t[..., None] * k_t[:, :, None, :])
        return S, jnp.einsum('bhvk,bhk->bhv', S, r_t)
    xs = tuple(jnp.moveaxis(t, 1, 0) for t in (r, w, k, v, kk, a))
    S, ys = lax.scan(step, S0, xs)
    return jnp.moveaxis(ys, 0, 1), S


def rwkv_mix(p, prev, S0, mu, w0, wB, a0, aB, gB, k_k, k_a, r_k, ln_w, ln_b):
    f32 = jnp.float32
    B, T = p.shape[:2]
    W = RWKV_WIDTH
    p = p.astype(f32)
    p_prev = jnp.concatenate([prev.astype(f32)[:, None], p[:, :-1]], axis=1)
    ps = p + mu * (p_prev - p)
    r, k, v = ps[..., :W], ps[..., W:2 * W], ps[..., 2 * W:3 * W]
    o = 3 * W
    xw, xa, xg = ps[..., o:o + LORA_W], ps[..., o + LORA_W:o + LORA_W + LORA_A], ps[..., o + LORA_W + LORA_A:]
    z = w0 + jnp.tanh(xw) @ wB
    w = jnp.exp(-jnp.exp(-jax.nn.softplus(-z) - 0.5))
    a = jax.nn.sigmoid(a0 + xa @ aB)
    g = jax.nn.sigmoid(xg) @ gB
    heads = lambda t: t.reshape(B, T, RWKV_HEADS, HEAD_DIM)
    kk = heads(k * k_k)
    kk = kk * lax.rsqrt(jnp.maximum(jnp.sum(kk * kk, axis=-1, keepdims=True), 1e-24))
    k = heads(k * (1.0 + (a - 1.0) * k_a))
    r, v, w, a = heads(r), heads(v), heads(w), heads(a)
    y, S = rwkv_scan(r, w, k, v, kk, a, S0.astype(f32))
    mean = jnp.mean(y, axis=-1, keepdims=True)
    var = jnp.mean(jnp.square(y - mean), axis=-1, keepdims=True)
    y = ((y - mean) * lax.rsqrt(var + RWKV_GN_EPS)).reshape(B, T, W) * ln_w + ln_b
    y = y + (jnp.sum(r * k * r_k, axis=-1, keepdims=True) * v).reshape(B, T, W)
    return y * g, S, p[:, -1]


def gla_chunked(q, k, v, logg, S0):
    B, T, H, DK = q.shape
    DV = v.shape[-1]
    C = GLA_CHUNK if T % GLA_CHUNK == 0 else T
    N = T // C
    to_chunks = lambda x: x.reshape(B, N, C, H, x.shape[-1]).transpose(1, 0, 3, 2, 4)
    causal = jnp.tril(jnp.ones((C, C), dtype=bool))[:, :, None]

    def step(S, inp):
        qc, kc, vc, gc = inp
        b = jnp.cumsum(gc, axis=2)
        diff = b[:, :, :, None, :] - b[:, :, None, :, :]
        decay = jnp.exp(jnp.where(causal, diff, -jnp.inf))
        A = jnp.einsum('bhtd,bhsd,bhtsd->bhts', qc, kc, decay)
        o = jnp.einsum('bhts,bhsv->bhtv', A, vc) + jnp.einsum('bhtk,bhkv->bhtv', qc * jnp.exp(b), S)
        b_last = b[:, :, -1:, :]
        S = jnp.exp(b_last[:, :, 0, :])[..., None] * S + jnp.einsum('bhsk,bhsv->bhkv', kc * jnp.exp(b_last - b), vc)
        return S, o

    S, o = lax.scan(step, S0, (to_chunks(q), to_chunks(k), to_chunks(v), to_chunks(logg)))
    return o.transpose(1, 0, 3, 2, 4).reshape(B, T, H, DV), S


def gla_mix(h, S0, gate_up, gate_b, gn):
    f32 = jnp.float32
    B, T = h.shape[:2]
    h = h.astype(f32)
    q = h[..., :GLA_KW].reshape(B, T, GLA_HEADS, GLA_DK) * (GLA_DK ** -0.5)
    k = h[..., GLA_KW:2 * GLA_KW].reshape(B, T, GLA_HEADS, GLA_DK)
    o = 2 * GLA_KW
    v = h[..., o:o + GLA_VW].reshape(B, T, GLA_HEADS, GLA_DV)
    gd = h[..., o + GLA_VW:o + GLA_VW + GLA_RANK]
    r = h[..., o + GLA_VW + GLA_RANK:]
    logg = (jax.nn.log_sigmoid(gd @ gate_up + gate_b) / GLA_TAU).reshape(B, T, GLA_HEADS, GLA_DK)
    out, S = gla_chunked(q, k, v, logg, S0.astype(f32))
    out = rmsnorm(out, gn).reshape(B, T, GLA_VW) * jax.nn.silu(r)
    return out, S


def setup_inputs(seed: int = 0) -> dict:
    key = jax.random.key(seed)
    ks = iter(jax.random.split(key, 64))
    f32 = jnp.float32
    nrm = lambda shape, scale: jax.random.normal(next(ks), shape, f32) * scale
    gain = lambda shape: 1.0 + nrm(shape, 0.02)
    n_pages = PAST_LEN // PAGE_SIZE
    n_pool = (DEC_BATCH * n_pages * 5) // 4
    wb = min(WINDOW, PAST_LEN)
    perm = jax.random.permutation(next(ks), n_pool)
    page_table = perm[:DEC_BATCH * n_pages].reshape(DEC_BATCH, n_pages).astype(jnp.int32)
    return {
        'x_prompt': nrm((BATCH, SEQ, D_MODEL), 1.0),
        'x_sample': nrm((DEC_BATCH, DEC_SEQ, D_MODEL), 1.0),
        'cache_nsa_kv': nrm((N_EVEN, n_pool, PAGE_SIZE, 4, NSA_KV, HEAD_DIM), 1.0),
        'cache_nsa_win': nrm((N_EVEN, DEC_BATCH, wb, 2, NSA_KV, HEAD_DIM), 1.0),
        'state_rwkv': nrm((N_EVEN, DEC_BATCH, RWKV_HEADS, HEAD_DIM, HEAD_DIM), 0.2),
        'state_rwkv_shift': nrm((N_EVEN, DEC_BATCH, RWKV_IN), 1.0),
        'state_gla': nrm((N_ODD, DEC_BATCH, GLA_HEADS, GLA_DK, GLA_DV), 0.2),
        'page_table': page_table,
        'norm_mix': gain((DEPTH, D_MODEL)),
        'norm_ffn': gain((DEPTH, D_MODEL)),
        'w_in_even': nrm((N_EVEN, D_MODEL, EVEN_IN), D_MODEL ** -0.5),
        'nsa_qnorm': gain((N_EVEN, HEAD_DIM)),
        'nsa_knorm': gain((N_EVEN, 3, HEAD_DIM)),
        'cmp_w1': nrm((N_EVEN, 2, CMP_BLOCK, HEAD_DIM, CMP_HIDDEN), (CMP_BLOCK * HEAD_DIM) ** -0.5),
        'cmp_w2': nrm((N_EVEN, 2, CMP_HIDDEN, HEAD_DIM), CMP_HIDDEN ** -0.5),
        'cmp_pe': nrm((N_EVEN, 2, CMP_BLOCK, HEAD_DIM), 0.1),
        'rwkv_mu': jax.random.uniform(next(ks), (N_EVEN, RWKV_IN), f32),
        'rwkv_w0': nrm((N_EVEN, RWKV_WIDTH), 0.5),
        'rwkv_wB': nrm((N_EVEN, LORA_W, RWKV_WIDTH), LORA_W ** -0.5),
        'rwkv_a0': nrm((N_EVEN, RWKV_WIDTH), 0.5),
        'rwkv_aB': nrm((N_EVEN, LORA_A, RWKV_WIDTH), LORA_A ** -0.5),
        'rwkv_gB': nrm((N_EVEN, LORA_G, RWKV_WIDTH), LORA_G ** -0.5),
        'rwkv_kk': 0.85 + nrm((N_EVEN, RWKV_WIDTH), 0.05),
        'rwkv_ka': 1.0 + nrm((N_EVEN, RWKV_WIDTH), 0.05),
        'rwkv_rk': nrm((N_EVEN, RWKV_HEADS, HEAD_DIM), 0.1),
        'rwkv_ln_w': gain((N_EVEN, RWKV_WIDTH)),
        'rwkv_ln_b': nrm((N_EVEN, RWKV_WIDTH), 0.02),
        'w_out_even': nrm((N_EVEN, MIX_WIDTH, D_MODEL), MIX_WIDTH ** -0.5),
        'w_in_odd': nrm((N_ODD, D_MODEL, ODD_IN), D_MODEL ** -0.5),
        'gla_gate_up': nrm((N_ODD, GLA_RANK, GLA_KW), GLA_RANK ** -0.5),
        'gla_gate_b': 1.0 + nrm((N_ODD, GLA_KW), 0.5),
        'gla_norm': gain((N_ODD, GLA_DV)),
        'w_out_odd': nrm((N_ODD, GLA_VW, D_MODEL), GLA_VW ** -0.5),
        'ffn_gate': nrm((DEPTH, D_MODEL, D_FF), D_MODEL ** -0.5),
        'ffn_up': nrm((DEPTH, D_MODEL, D_FF), D_MODEL ** -0.5),
        'ffn_down': nrm((DEPTH, D_FF, D_MODEL), D_FF ** -0.5),
    }


def reference(x_prompt, x_sample, cache_nsa_kv, cache_nsa_win, state_rwkv, state_rwkv_shift, state_gla,
              page_table, norm_mix, norm_ffn, w_in_even, nsa_qnorm, nsa_knorm, cmp_w1, cmp_w2, cmp_pe,
              rwkv_mu, rwkv_w0, rwkv_wB, rwkv_a0, rwkv_aB, rwkv_gB, rwkv_kk, rwkv_ka, rwkv_rk,
              rwkv_ln_w, rwkv_ln_b, w_out_even, w_in_odd, gla_gate_up, gla_gate_b, gla_norm, w_out_odd,
              ffn_gate, ffn_up, ffn_down):
    dt = x_prompt.dtype
    B, T = x_prompt.shape[:2]
    DB, S = x_sample.shape[:2]
    P = page_table.shape[1] * PAGE_SIZE
    WB = cache_nsa_win.shape[2]
    WP = min(WINDOW, T)
    pos_p = jnp.arange(T, dtype=jnp.int32)
    pos_s = P + jnp.arange(S, dtype=jnp.int32)
    win_pos_s = P - WB + jnp.arange(WB + S, dtype=jnp.int32)
    y_p, y_s = x_prompt, x_sample
    kv_p, kv_s, win_p, win_s, rw_p, rw_s, sh_p, sh_s, gl_p, gl_s = ([] for _ in range(10))
    for layer in range(DEPTH):
        li = layer // 2
        if layer % 2 == 0:
            h_p = rmsnorm(y_p, norm_mix[layer]) @ w_in_even[li]
            h_s = rmsnorm(y_s, norm_mix[layer]) @ w_in_even[li]
            q, g, rows, win = nsa_project(h_p[..., :NSA_IN], pos_p, nsa_qnorm[li], nsa_knorm[li])
            o_nsa_p = nsa_prompt(q, g, rows, win, cmp_w1[li], cmp_w2[li], cmp_pe[li])
            kv_p.append(rows.astype(dt))
            win_p.append(win[:, T - WP:].astype(dt))
            q, g, rows, win = nsa_project(h_s[..., :NSA_IN], pos_s, nsa_qnorm[li], nsa_knorm[li])
            past = cache_nsa_kv[li][page_table].reshape(DB, P, 4, NSA_KV, HEAD_DIM)
            rows_all = jnp.concatenate([past, rows.astype(past.dtype)], axis=1)
            win_all = jnp.concatenate([cache_nsa_win[li], win.astype(cache_nsa_win.dtype)], axis=1)
            kc, vc, cmp_end, sel_map, ksb, vsb = nsa_keys(rows_all, cmp_w1[li], cmp_w2[li], cmp_pe[li])
            o_nsa_s = nsa_block(q, g, pos_s, kc, vc, cmp_end, sel_map, ksb, vsb, win_all, win_pos_s)
            kv_s.append(rows.astype(dt))
            win_s.append(win_all[:, S:].astype(dt))
            rw_par = (rwkv_mu[li], rwkv_w0[li], rwkv_wB[li], rwkv_a0[li], rwkv_aB[li], rwkv_gB[li],
                      rwkv_kk[li], rwkv_ka[li], rwkv_rk[li], rwkv_ln_w[li], rwkv_ln_b[li])
            o_rw_p, st, last = rwkv_mix(h_p[..., NSA_IN:], jnp.zeros((B, RWKV_IN), dt),
                                        jnp.zeros((B, RWKV_HEADS, HEAD_DIM, HEAD_DIM), dt), *rw_par)
            rw_p.append(st.astype(dt))
            sh_p.append(last.astype(dt))
            o_rw_s, st, last = rwkv_mix(h_s[..., NSA_IN:], state_rwkv_shift[li], state_rwkv[li], *rw_par)
            rw_s.append(st.astype(dt))
            sh_s.append(last.astype(dt))
            mix_p = jnp.concatenate([o_nsa_p.astype(dt), o_rw_p.astype(dt)], axis=-1) @ w_out_even[li]
            mix_s = jnp.concatenate([o_nsa_s.astype(dt), o_rw_s.astype(dt)], axis=-1) @ w_out_even[li]
        else:
            h_p = rmsnorm(y_p, norm_mix[layer]) @ w_in_odd[li]
            h_s = rmsnorm(y_s, norm_mix[layer]) @ w_in_odd[li]
            o_p, st = gla_mix(h_p, jnp.zeros((B, GLA_HEADS, GLA_DK, GLA_DV), dt),
                              gla_gate_up[li], gla_gate_b[li], gla_norm[li])
            gl_p.append(st.astype(dt))
            o_s, st = gla_mix(h_s, state_gla[li], gla_gate_up[li], gla_gate_b[li], gla_norm[li])
            gl_s.append(st.astype(dt))
            mix_p = o_p.astype(dt) @ w_out_odd[li]
            mix_s = o_s.astype(dt) @ w_out_odd[li]
        y_p = y_p + mix_p
        y_s = y_s + mix_s
        y_p = y_p + swiglu(rmsnorm(y_p, norm_ffn[layer]), ffn_gate[layer], ffn_up[layer], ffn_down[layer])
        y_s = y_s + swiglu(rmsnorm(y_s, norm_ffn[layer]), ffn_gate[layer], ffn_up[layer], ffn_down[layer])
    return (y_p, y_s, jnp.stack(kv_p), jnp.stack(kv_s), jnp.stack(win_p), jnp.stack(win_s),
            jnp.stack(rw_p), jnp.stack(rw_s), jnp.stack(sh_p), jnp.stack(sh_s),
            jnp.stack(gl_p), jnp.stack(gl_s))
```

```python
import functools
import math

import jax
import jax.numpy as jnp
from jax import lax
from jax.experimental import pallas as pl
from jax.experimental.pallas import tpu as pltpu

D_MODEL = 1024
PAGE_SIZE = 128
HEAD_DIM = 64
NSA_HEADS = 8
NSA_KV = 2
NSA_REP = NSA_HEADS // NSA_KV
CMP_STRIDE = 16
CMP_BLOCK = 2 * CMP_STRIDE
CMP_HIDDEN = 2 * HEAD_DIM
SEL_BLOCK = 64
N_SELECT = 16
WINDOW = 512
Q_BLOCK = 128
ROPE_DIM = HEAD_DIM // 4
ROPE_THETA = 500000.0
RWKV_HEADS = 8
RWKV_WIDTH = RWKV_HEADS * HEAD_DIM
LORA_W = 64
LORA_A = 64
LORA_G = 128
RWKV_GN_EPS = 64e-5
GLA_HEADS = 4
GLA_DK = D_MODEL // 2 // GLA_HEADS
GLA_DV = D_MODEL // GLA_HEADS
GLA_RANK = 16
GLA_TAU = 16.0
GLA_CHUNK = 64
D_FF = ((8 * D_MODEL // 3 + 255) // 256) * 256
NSA_Q = NSA_HEADS * HEAD_DIM
NSA_KVW = 3 * 2 * NSA_KV * HEAD_DIM
NSA_IN = NSA_Q + NSA_KVW + 3 * NSA_HEADS
RWKV_IN = 3 * RWKV_WIDTH + LORA_W + LORA_A + LORA_G
EVEN_IN = NSA_IN + RWKV_IN
MIX_WIDTH = NSA_Q + RWKV_WIDTH
GLA_KW = GLA_HEADS * GLA_DK
GLA_VW = GLA_HEADS * GLA_DV
ODD_IN = 2 * GLA_KW + GLA_VW + GLA_RANK + GLA_VW

LANE = 128
VMEM_LIMIT = 48 * 1024 * 1024


def _mm_body(x_ref, w_ref, o_ref):
    o_ref[...] = jnp.dot(x_ref[...].astype(jnp.bfloat16), w_ref[...],
                         preferred_element_type=jnp.float32)


def _pick_tile(n, cap, unit):
    best = unit
    t = unit
    while t <= min(n, cap):
        if n % t == 0:
            best = t
        t += unit
    return best


def matmul(x, w):
    m, k = x.shape
    n = w.shape[1]
    tm = _pick_tile(m, 512, 8)
    tn = _pick_tile(n, 1024, LANE)
    return pl.pallas_call(
        _mm_body,
        grid=(m // tm, n // tn),
        in_specs=[pl.BlockSpec((tm, k), lambda i, j: (i, 0)),
                  pl.BlockSpec((k, tn), lambda i, j: (0, j))],
        out_specs=pl.BlockSpec((tm, tn), lambda i, j: (i, j)),
        out_shape=jax.ShapeDtypeStruct((m, n), jnp.float32),
        compiler_params=pltpu.CompilerParams(
            dimension_semantics=("parallel", "arbitrary"), vmem_limit_bytes=VMEM_LIMIT),
        name="matmul",
    )(x, w)


def mm3(x, w):
    b, t, k = x.shape
    return matmul(x.reshape(b * t, k), w).reshape(b, t, w.shape[1])


def _pad_cols(w, n):
    return jnp.pad(w, ((0, 0), (0, n - w.shape[1])))


def rmsnorm(x, g, eps=1e-6):
    xf = x.astype(jnp.float32)
    y = xf * lax.rsqrt(jnp.mean(xf * xf, axis=-1, keepdims=True) + eps)
    return (y * g.astype(jnp.float32)).astype(x.dtype)


def rope(x, pos):
    half = ROPE_DIM // 2
    inv = ROPE_THETA ** (-jnp.arange(half, dtype=jnp.float32) / half)
    ang = pos.astype(jnp.float32)[:, None] * inv[None, :]
    shp = (pos.shape[0],) + (1,) * (x.ndim - 3) + (half,)
    cos, sin = jnp.cos(ang).reshape(shp), jnp.sin(ang).reshape(shp)
    xr = x[..., :ROPE_DIM].astype(jnp.float32)
    x1, x2 = xr[..., :half], xr[..., half:]
    rot = jnp.concatenate([x1 * cos - x2 * sin, x2 * cos + x1 * sin], axis=-1)
    return jnp.concatenate([rot.astype(x.dtype), x[..., ROPE_DIM:]], axis=-1)


def masked_softmax(s, mask):
    p = jax.nn.softmax(jnp.where(mask, s, -1e30), axis=-1)
    return jnp.where(mask, p, 0.0)


def swiglu(x, wg, wu, wd):
    g = mm3(x, wg)
    u = mm3(x, wu)
    return mm3(jax.nn.silu(g) * u, wd)


def nsa_project(h, pos, qn, kn):
    B, T = h.shape[:2]
    q = h[..., :NSA_Q].reshape(B, T, NSA_HEADS, HEAD_DIM)
    kv = h[..., NSA_Q:NSA_Q + NSA_KVW].reshape(B, T, 3, 2, NSA_KV, HEAD_DIM)
    gates = jax.nn.sigmoid(h[..., NSA_Q + NSA_KVW:NSA_IN].astype(jnp.float32)).reshape(B, T, NSA_HEADS, 3)
    q = rope(rmsnorm(q, qn), pos)
    k = rope(rmsnorm(kv[:, :, :, 0], kn[:, None, :]), pos)
    kv = jnp.stack([k, kv[:, :, :, 1]], axis=3)
    rows = kv[:, :, :2].reshape(B, T, 4, NSA_KV, HEAD_DIM)
    win = kv[:, :, 2]
    return q, gates, rows, win


def compress(k, w1, w2, pe):
    B, L = k.shape[:2]
    n_chunks = L // CMP_STRIDE
    c = k[:, :n_chunks * CMP_STRIDE].reshape(B, n_chunks, CMP_STRIDE, NSA_KV, HEAD_DIM)
    h = (jnp.einsum('bnlgd,ldh->bngh', c[:, :-1], w1[:CMP_STRIDE])
         + jnp.einsum('bnlgd,ldh->bngh', c[:, 1:], w1[CMP_STRIDE:])
         + jnp.einsum('ld,ldh->h', pe, w1))
    return jnp.einsum('bngh,hd->bngd', jax.nn.gelu(h), w2)


def nsa_keys(rows, cw1, cw2, cpe):
    B, L = rows.shape[:2]
    kc = compress(rows[:, :, 0], cw1[0], cw2[0], cpe[0])
    vc = compress(rows[:, :, 1], cw1[1], cw2[1], cpe[1])
    n_cmp = kc.shape[1]
    cmp_start = jnp.arange(n_cmp, dtype=jnp.int32) * CMP_STRIDE
    cmp_end = cmp_start + CMP_BLOCK - 1
    n_sel = -(-L // SEL_BLOCK)
    sel_start = jnp.arange(n_sel, dtype=jnp.int32) * SEL_BLOCK
    sel_map = ((cmp_start[:, None] <= sel_start[None, :] + SEL_BLOCK - 1)
               & (cmp_end[:, None] >= sel_start[None, :])).astype(jnp.float32)
    slc = jnp.pad(rows[:, :, 2:4], ((0, 0), (0, n_sel * SEL_BLOCK - L), (0, 0), (0, 0), (0, 0)))
    slc = slc.reshape(B, n_sel, SEL_BLOCK, 2, NSA_KV, HEAD_DIM).transpose(3, 0, 4, 1, 2, 5)
    return kc, vc, cmp_end, sel_map, slc[0], slc[1]


def nsa_block(q, gates, q_pos, kc, vc, cmp_end, sel_map, ksb, vsb, win, w_pos):
    f32 = jnp.float32
    B, QB = q.shape[:2]
    qg = q.reshape(B, QB, NSA_KV, NSA_REP, HEAD_DIM).astype(f32) * (HEAD_DIM ** -0.5)
    tq = q_pos[:, None]
    s = jnp.einsum('bqgrd,bngd->bqgrn', qg, kc.astype(f32))
    p = masked_softmax(s, (cmp_end[None, :] <= tq)[None, :, None, None, :])
    o_cmp = jnp.einsum('bqgrn,bngd->bqgrd', p, vc.astype(f32))
    imp = jnp.einsum('bqgrn,nj->bqgj', p, sel_map)
    n_sel = sel_map.shape[1]
    blk = jnp.arange(n_sel, dtype=jnp.int32)[None, :]
    cur = tq // SEL_BLOCK
    forced = (blk == 0) | (blk == cur) | (blk == cur - 1)
    valid = blk * SEL_BLOCK <= tq
    imp = jnp.where(valid[None, :, None, :], jnp.where(forced[None, :, None, :], 1e30, imp), -1e30)
    n_top = min(N_SELECT, n_sel)
    _, idx = lax.top_k(imp, n_top)
    bi = jnp.arange(B)[:, None, None, None]
    gi = jnp.arange(NSA_KV)[None, None, :, None]
    nk = n_top * SEL_BLOCK
    k_sel = ksb[bi, gi, idx].reshape(B, QB, NSA_KV, nk, HEAD_DIM).astype(f32)
    v_sel = vsb[bi, gi, idx].reshape(B, QB, NSA_KV, nk, HEAD_DIM).astype(f32)
    k_pos = (idx[..., None] * SEL_BLOCK + jnp.arange(SEL_BLOCK, dtype=jnp.int32)).reshape(B, QB, NSA_KV, nk)
    s = jnp.einsum('bqgrd,bqgkd->bqgrk', qg, k_sel)
    p = masked_softmax(s, (k_pos <= q_pos[None, :, None, None])[:, :, :, None, :])
    o_slc = jnp.einsum('bqgrk,bqgkd->bqgrd', p, v_sel)
    s = jnp.einsum('bqgrd,bkgd->bqgrk', qg, win[:, :, 0].astype(f32))
    wp = w_pos[None, :]
    m = (wp <= tq) & (wp > tq - WINDOW) & (wp >= 0)
    p = masked_softmax(s, m[None, :, None, None, :])
    o_win = jnp.einsum('bqgrk,bkgd->bqgrd', p, win[:, :, 1].astype(f32))
    g = gates.reshape(B, QB, NSA_KV, NSA_REP, 3)
    o = g[..., 0:1] * o_cmp + g[..., 1:2] * o_slc + g[..., 2:3] * o_win
    return o.reshape(B, QB, NSA_Q)


def nsa_prompt(q, gates, rows, win, cw1, cw2, cpe):
    B, T = q.shape[:2]
    kc, vc, cmp_end, sel_map, ksb, vsb = nsa_keys(rows, cw1, cw2, cpe)
    win_pad = jnp.pad(win, ((0, 0), (WINDOW, 0), (0, 0), (0, 0), (0, 0)))
    nb = T // Q_BLOCK
    qb = q.reshape(B, nb, Q_BLOCK, NSA_HEADS, HEAD_DIM).swapaxes(0, 1)
    gb = gates.reshape(B, nb, Q_BLOCK, NSA_HEADS, 3).swapaxes(0, 1)

    def body(args):
        i, q_i, g_i = args
        s0 = i * Q_BLOCK
        w_i = lax.dynamic_slice_in_dim(win_pad, s0, WINDOW + Q_BLOCK, axis=1)
        q_pos = s0 + jnp.arange(Q_BLOCK, dtype=jnp.int32)
        w_pos = s0 - WINDOW + jnp.arange(WINDOW + Q_BLOCK, dtype=jnp.int32)
        return nsa_block(q_i, g_i, q_pos, kc, vc, cmp_end, sel_map, ksb, vsb, w_i, w_pos)

    o = lax.map(body, (jnp.arange(nb, dtype=jnp.int32), qb, gb))
    return o.swapaxes(0, 1).reshape(B, T, NSA_Q)


def rwkv_scan(r, w, k, v, kk, a, S0):
    def step(S, inp):
        r_t, w_t, k_t, v_t, kk_t, a_t = inp
        sa = jnp.einsum('bhvk,bhk->bhv', S, -kk_t)
        S = (S * w_t[:, :, None, :] + sa[..., None] * (kk_t * a_t)[:, :, None, :]
             + v_t[..., None] * k_t[:, :, None, :])
        return S, jnp.einsum('bhvk,bhk->bhv', S, r_t)
    xs = tuple(jnp.moveaxis(t, 1, 0) for t in (r, w, k, v, kk, a))
    S, ys = lax.scan(step, S0, xs)
    return jnp.moveaxis(ys, 0, 1), S


def rwkv_mix(p, prev, S0, mu, w0, wB, a0, aB, gB, k_k, k_a, r_k, ln_w, ln_b):
    f32 = jnp.float32
    B, T = p.shape[:2]
    W = RWKV_WIDTH
    p = p.astype(f32)
    p_prev = jnp.concatenate([prev.astype(f32)[:, None], p[:, :-1]], axis=1)
    ps = p + mu * (p_prev - p)
    r, k, v = ps[..., :W], ps[..., W:2 * W], ps[..., 2 * W:3 * W]
    o = 3 * W
    xw, xa, xg = ps[..., o:o + LORA_W], ps[..., o + LORA_W:o + LORA_W + LORA_A], ps[..., o + LORA_W + LORA_A:]
    z = w0 + jnp.tanh(xw) @ wB
    w = jnp.exp(-jnp.exp(-jax.nn.softplus(-z) - 0.5))
    a = jax.nn.sigmoid(a0 + xa @ aB)
    g = jax.nn.sigmoid(xg) @ gB
    heads = lambda t: t.reshape(B, T, RWKV_HEADS, HEAD_DIM)
    kk = heads(k * k_k)
    kk = kk * lax.rsqrt(jnp.maximum(jnp.sum(kk * kk, axis=-1, keepdims=True), 1e-24))
    k = heads(k * (1.0 + (a - 1.0) * k_a))
    r, v, w, a = heads(r), heads(v), heads(w), heads(a)
    y, S = rwkv_scan(r, w, k, v, kk, a, S0.astype(f32))
    mean = jnp.mean(y, axis=-1, keepdims=True)
    var = jnp.mean(jnp.square(y - mean), axis=-1, keepdims=True)
    y = ((y - mean) * lax.rsqrt(var + RWKV_GN_EPS)).reshape(B, T, W) * ln_w + ln_b
    y = y + (jnp.sum(r * k * r_k, axis=-1, keepdims=True) * v).reshape(B, T, W)
    return y * g, S, p[:, -1]


def gla_chunked(q, k, v, logg, S0):
    B, T, H, DK = q.shape
    DV = v.shape[-1]
    C = GLA_CHUNK if T % GLA_CHUNK == 0 else T
    N = T // C
    to_chunks = lambda x: x.reshape(B, N, C, H, x.shape[-1]).transpose(1, 0, 3, 2, 4)
    causal = jnp.tril(jnp.ones((C, C), dtype=bool))[:, :, None]

    def step(S, inp):
        qc, kc, vc, gc = inp
        b = jnp.cumsum(gc, axis=2)
        diff = b[:, :, :, None, :] - b[:, :, None, :, :]
        decay = jnp.exp(jnp.where(causal, diff, -jnp.inf))
        A = jnp.einsum('bhtd,bhsd,bhtsd->bhts', qc, kc, decay)
        o = jnp.einsum('bhts,bhsv->bhtv', A, vc) + jnp.einsum('bhtk,bhkv->bhtv', qc * jnp.exp(b), S)
        b_last = b[:, :, -1:, :]
        S = jnp.exp(b_last[:, :, 0, :])[..., None] * S + jnp.einsum('bhsk,bhsv->bhkv', kc * jnp.exp(b_last - b), vc)
        return S, o

    S, o = lax.scan(step, S0, (to_chunks(q), to_chunks(k), to_chunks(v), to_chunks(logg)))
    return o.transpose(1, 0, 3, 2, 4).reshape(B, T, H, DV), S


def gla_mix(h, S0, gate_up, gate_b, gn):
    f32 = jnp.float32
    B, T = h.shape[:2]
    h = h.astype(f32)
    q = h[..., :GLA_KW].reshape(B, T, GLA_HEADS, GLA_DK) * (GLA_DK ** -0.5)
    k = h[..., GLA_KW:2 * GLA_KW].reshape(B, T, GLA_HEADS, GLA_DK)
    o = 2 * GLA_KW
    v = h[..., o:o + GLA_VW].reshape(B, T, GLA_HEADS, GLA_DV)
    gd = h[..., o + GLA_VW:o + GLA_VW + GLA_RANK]
    r = h[..., o + GLA_VW + GLA_RANK:]
    logg = (jax.nn.log_sigmoid(gd @ gate_up + gate_b) / GLA_TAU).reshape(B, T, GLA_HEADS, GLA_DK)
    out, S = gla_chunked(q, k, v, logg, S0.astype(f32))
    out = rmsnorm(out, gn).reshape(B, T, GLA_VW) * jax.nn.silu(r)
    return out, S


def kernel(x_prompt, x_sample, cache_nsa_kv, cache_nsa_win, state_rwkv, state_rwkv_shift, state_gla, page_table, norm_mix, norm_ffn, w_in_even, nsa_qnorm, nsa_knorm, cmp_w1, cmp_w2, cmp_pe, rwkv_mu, rwkv_w0, rwkv_wB, rwkv_a0, rwkv_aB, rwkv_gB, rwkv_kk, rwkv_ka, rwkv_rk, rwkv_ln_w, rwkv_ln_b, w_out_even, w_in_odd, gla_gate_up, gla_gate_b, gla_norm, w_out_odd, ffn_gate, ffn_up, ffn_down):
    dt = x_prompt.dtype
    bf16 = jnp.bfloat16
    B, T = x_prompt.shape[:2]
    DB, S = x_sample.shape[:2]
    depth = norm_mix.shape[0]
    P = page_table.shape[1] * PAGE_SIZE
    WB = cache_nsa_win.shape[2]
    WP = min(WINDOW, T)
    pos_p = jnp.arange(T, dtype=jnp.int32)
    pos_s = P + jnp.arange(S, dtype=jnp.int32)
    win_pos_s = P - WB + jnp.arange(WB + S, dtype=jnp.int32)
    in_pad = -(-max(EVEN_IN, ODD_IN) // LANE) * LANE
    y_p, y_s = x_prompt, x_sample
    kv_p, kv_s, win_p, win_s, rw_p, rw_s, sh_p, sh_s, gl_p, gl_s = ([] for _ in range(10))
    for layer in range(depth):
        li = layer // 2
        if layer % 2 == 0:
            w_in = _pad_cols(w_in_even[li], in_pad).astype(bf16)
            w_out = w_out_even[li].astype(bf16)
            h_p = mm3(rmsnorm(y_p, norm_mix[layer]), w_in)[..., :EVEN_IN]
            h_s = mm3(rmsnorm(y_s, norm_mix[layer]), w_in)[..., :EVEN_IN]
            q, g, rows, win = nsa_project(h_p[..., :NSA_IN], pos_p, nsa_qnorm[li], nsa_knorm[li])
            o_nsa_p = nsa_prompt(q, g, rows, win, cmp_w1[li], cmp_w2[li], cmp_pe[li])
            kv_p.append(rows.astype(dt))
            win_p.append(win[:, T - WP:].astype(dt))
            q, g, rows, win = nsa_project(h_s[..., :NSA_IN], pos_s, nsa_qnorm[li], nsa_knorm[li])
            past = cache_nsa_kv[li][page_table].reshape(DB, P, 4, NSA_KV, HEAD_DIM)
            rows_all = jnp.concatenate([past, rows.astype(past.dtype)], axis=1)
            win_all = jnp.concatenate([cache_nsa_win[li], win.astype(cache_nsa_win.dtype)], axis=1)
            kc, vc, cmp_end, sel_map, ksb, vsb = nsa_keys(rows_all, cmp_w1[li], cmp_w2[li], cmp_pe[li])
            o_nsa_s = nsa_block(q, g, pos_s, kc, vc, cmp_end, sel_map, ksb, vsb, win_all, win_pos_s)
            kv_s.append(rows.astype(dt))
            win_s.append(win_all[:, S:].astype(dt))
            rw_par = (rwkv_mu[li], rwkv_w0[li], rwkv_wB[li], rwkv_a0[li], rwkv_aB[li], rwkv_gB[li],
                      rwkv_kk[li], rwkv_ka[li], rwkv_rk[li], rwkv_ln_w[li], rwkv_ln_b[li])
            o_rw_p, st, last = rwkv_mix(h_p[..., NSA_IN:], jnp.zeros((B, RWKV_IN), dt),
                                        jnp.zeros((B, RWKV_HEADS, HEAD_DIM, HEAD_DIM), dt), *rw_par)
            rw_p.append(st.astype(dt))
            sh_p.append(last.astype(dt))
            o_rw_s, st, last = rwkv_mix(h_s[..., NSA_IN:], state_rwkv_shift[li], state_rwkv[li], *rw_par)
            rw_s.append(st.astype(dt))
            sh_s.append(last.astype(dt))
            mix_p = mm3(jnp.concatenate([o_nsa_p.astype(dt), o_rw_p.astype(dt)], axis=-1), w_out)
            mix_s = mm3(jnp.concatenate([o_nsa_s.astype(dt), o_rw_s.astype(dt)], axis=-1), w_out)
        else:
            w_in = _pad_cols(w_in_odd[li], in_pad).astype(bf16)
            w_out = w_out_odd[li].astype(bf16)
            h_p = mm3(rmsnorm(y_p, norm_mix[layer]), w_in)[..., :ODD_IN]
            h_s = mm3(rmsnorm(y_s, norm_mix[layer]), w_in)[..., :ODD_IN]
            o_p, st = gla_mix(h_p, jnp.zeros((B, GLA_HEADS, GLA_DK, GLA_DV), dt),
                              gla_gate_up[li], gla_gate_b[li], gla_norm[li])
            gl_p.append(st.astype(dt))
            o_s, st = gla_mix(h_s, state_gla[li], gla_gate_up[li], gla_gate_b[li], gla_norm[li])
            gl_s.append(st.astype(dt))
            mix_p = mm3(o_p.astype(dt), w_out)
            mix_s = mm3(o_s.astype(dt), w_out)
        y_p = y_p + mix_p
        y_s = y_s + mix_s
        wg, wu, wd = ffn_gate[layer].astype(bf16), ffn_up[layer].astype(bf16), ffn_down[layer].astype(bf16)
        y_p = y_p + swiglu(rmsnorm(y_p, norm_ffn[layer]), wg, wu, wd)
        y_s = y_s + swiglu(rmsnorm(y_s, norm_ffn[layer]), wg, wu, wd)
    return (y_p, y_s, jnp.stack(kv_p), jnp.stack(kv_s), jnp.stack(win_p), jnp.stack(win_s),
            jnp.stack(rw_p), jnp.stack(rw_s), jnp.stack(sh_p), jnp.stack(sh_s),
            jnp.stack(gl_p), jnp.stack(gl_s))
```

```python
import functools
import math

import jax
import jax.numpy as jnp
from jax import lax
from jax.experimental import pallas as pl
from jax.experimental.pallas import tpu as pltpu

D_MODEL = 1024
PAGE_SIZE = 128
HEAD_DIM = 64
NSA_HEADS = 8
NSA_KV = 2
NSA_REP = NSA_HEADS // NSA_KV
CMP_STRIDE = 16
CMP_BLOCK = 2 * CMP_STRIDE
CMP_HIDDEN = 2 * HEAD_DIM
SEL_BLOCK = 64
N_SELECT = 16
WINDOW = 512
Q_BLOCK = 128
ROPE_DIM = HEAD_DIM // 4
ROPE_THETA = 500000.0
RWKV_HEADS = 8
RWKV_WIDTH = RWKV_HEADS * HEAD_DIM
LORA_W = 64
LORA_A = 64
LORA_G = 128
RWKV_GN_EPS = 64e-5
GLA_HEADS = 4
GLA_DK = D_MODEL // 2 // GLA_HEADS
GLA_DV = D_MODEL // GLA_HEADS
GLA_RANK = 16
GLA_TAU = 16.0
GLA_CHUNK = 64
D_FF = ((8 * D_MODEL // 3 + 255) // 256) * 256
NSA_Q = NSA_HEADS * HEAD_DIM
NSA_KVW = 3 * 2 * NSA_KV * HEAD_DIM
NSA_IN = NSA_Q + NSA_KVW + 3 * NSA_HEADS
RWKV_IN = 3 * RWKV_WIDTH + LORA_W + LORA_A + LORA_G
EVEN_IN = NSA_IN + RWKV_IN
MIX_WIDTH = NSA_Q + RWKV_WIDTH
GLA_KW = GLA_HEADS * GLA_DK
GLA_VW = GLA_HEADS * GLA_DV
ODD_IN = 2 * GLA_KW + GLA_VW + GLA_RANK + GLA_VW

LANE = 128
VMEM_LIMIT = 48 * 1024 * 1024


def _mm_body(x_ref, w_ref, o_ref):
    o_ref[...] = jnp.dot(x_ref[...].astype(jnp.bfloat16), w_ref[...],
                         preferred_element_type=jnp.float32)


def _pick_tile(n, cap, unit):
    best = unit
    t = unit
    while t <= min(n, cap):
        if n % t == 0:
            best = t
        t += unit
    return best


def matmul(x, w):
    m, k = x.shape
    n = w.shape[1]
    tm = _pick_tile(m, 512, 8)
    tn = _pick_tile(n, 1024, LANE)
    return pl.pallas_call(
        _mm_body,
        grid=(m // tm, n // tn),
        in_specs=[pl.BlockSpec((tm, k), lambda i, j: (i, 0)),
                  pl.BlockSpec((k, tn), lambda i, j: (0, j))],
        out_specs=pl.BlockSpec((tm, tn), lambda i, j: (i, j)),
        out_shape=jax.ShapeDtypeStruct((m, n), jnp.float32),
        compiler_params=pltpu.CompilerParams(
            dimension_semantics=("parallel", "arbitrary"), vmem_limit_bytes=VMEM_LIMIT),
        name="matmul",
    )(x, w)


def mm3(x, w):
    b, t, k = x.shape
    return matmul(x.reshape(b * t, k), w).reshape(b, t, w.shape[1])


def _pad_cols(w, n):
    return jnp.pad(w, ((0, 0), (0, n - w.shape[1])))


def rmsnorm(x, g, eps=1e-6):
    xf = x.astype(jnp.float32)
    y = xf * lax.rsqrt(jnp.mean(xf * xf, axis=-1, keepdims=True) + eps)
    return (y * g.astype(jnp.float32)).astype(x.dtype)


def rope(x, pos):
    half = ROPE_DIM // 2
    inv = ROPE_THETA ** (-jnp.arange(half, dtype=jnp.float32) / half)
    ang = pos.astype(jnp.float32)[:, None] * inv[None, :]
    shp = (pos.shape[0],) + (1,) * (x.ndim - 3) + (half,)
    cos, sin = jnp.cos(ang).reshape(shp), jnp.sin(ang).reshape(shp)
    xr = x[..., :ROPE_DIM].astype(jnp.float32)
    x1, x2 = xr[..., :half], xr[..., half:]
    rot = jnp.concatenate([x1 * cos - x2 * sin, x2 * cos + x1 * sin], axis=-1)
    return jnp.concatenate([rot.astype(x.dtype), x[..., ROPE_DIM:]], axis=-1)


def masked_softmax(s, mask):
    p = jax.nn.softmax(jnp.where(mask, s, -1e30), axis=-1)
    return jnp.where(mask, p, 0.0)


def swiglu(x, wg, wu, wd):
    g = mm3(x, wg)
    u = mm3(x, wu)
    return mm3(jax.nn.silu(g) * u, wd)


def nsa_project(h, pos, qn, kn):
    B, T = h.shape[:2]
    q = h[..., :NSA_Q].reshape(B, T, NSA_HEADS, HEAD_DIM)
    kv = h[..., NSA_Q:NSA_Q + NSA_KVW].reshape(B, T, 3, 2, NSA_KV, HEAD_DIM)
    gates = jax.nn.sigmoid(h[..., NSA_Q + NSA_KVW:NSA_IN].astype(jnp.float32)).reshape(B, T, NSA_HEADS, 3)
    q = rope(rmsnorm(q, qn), pos)
    k = rope(rmsnorm(kv[:, :, :, 0], kn[:, None, :]), pos)
    kv = jnp.stack([k, kv[:, :, :, 1]], axis=3)
    rows = kv[:, :, :2].reshape(B, T, 4, NSA_KV, HEAD_DIM)
    win = kv[:, :, 2]
    return q, gates, rows, win


def compress(k, w1, w2, pe):
    B, L = k.shape[:2]
    n_chunks = L // CMP_STRIDE
    c = k[:, :n_chunks * CMP_STRIDE].reshape(B, n_chunks, CMP_STRIDE, NSA_KV, HEAD_DIM)
    h = (jnp.einsum('bnlgd,ldh->bngh', c[:, :-1], w1[:CMP_STRIDE])
         + jnp.einsum('bnlgd,ldh->bngh', c[:, 1:], w1[CMP_STRIDE:])
         + jnp.einsum('ld,ldh->h', pe, w1))
    return jnp.einsum('bngh,hd->bngd', jax.nn.gelu(h), w2)


def nsa_keys(rows, cw1, cw2, cpe):
    B, L = rows.shape[:2]
    kc = compress(rows[:, :, 0], cw1[0], cw2[0], cpe[0])
    vc = compress(rows[:, :, 1], cw1[1], cw2[1], cpe[1])
    n_cmp = kc.shape[1]
    cmp_start = jnp.arange(n_cmp, dtype=jnp.int32) * CMP_STRIDE
    cmp_end = cmp_start + CMP_BLOCK - 1
    n_sel = -(-L // SEL_BLOCK)
    sel_start = jnp.arange(n_sel, dtype=jnp.int32) * SEL_BLOCK
    sel_map = ((cmp_start[:, None] <= sel_start[None, :] + SEL_BLOCK - 1)
               & (cmp_end[:, None] >= sel_start[None, :])).astype(jnp.float32)
    slc = jnp.pad(rows[:, :, 2:4], ((0, 0), (0, n_sel * SEL_BLOCK - L), (0, 0), (0, 0), (0, 0)))
    slc = slc.reshape(B, n_sel, SEL_BLOCK, 2, NSA_KV, HEAD_DIM).transpose(3, 0, 4, 1, 2, 5)
    return kc, vc, cmp_end, sel_map, slc[0], slc[1]


def nsa_block(q, gates, q_pos, kc, vc, cmp_end, sel_map, ksb, vsb, win, w_pos):
    f32 = jnp.float32
    B, QB = q.shape[:2]
    qg = q.reshape(B, QB, NSA_KV, NSA_REP, HEAD_DIM).astype(f32) * (HEAD_DIM ** -0.5)
    tq = q_pos[:, None]
    s = jnp.einsum('bqgrd,bngd->bqgrn', qg, kc.astype(f32))
    p = masked_softmax(s, (cmp_end[None, :] <= tq)[None, :, None, None, :])
    o_cmp = jnp.einsum('bqgrn,bngd->bqgrd', p, vc.astype(f32))
    imp = jnp.einsum('bqgrn,nj->bqgj', p, sel_map)
    n_sel = sel_map.shape[1]
    blk = jnp.arange(n_sel, dtype=jnp.int32)[None, :]
    cur = tq // SEL_BLOCK
    forced = (blk == 0) | (blk == cur) | (blk == cur - 1)
    valid = blk * SEL_BLOCK <= tq
    imp = jnp.where(valid[None, :, None, :], jnp.where(forced[None, :, None, :], 1e30, imp), -1e30)
    n_top = min(N_SELECT, n_sel)
    _, idx = lax.top_k(imp, n_top)
    bi = jnp.arange(B)[:, None, None, None]
    gi = jnp.arange(NSA_KV)[None, None, :, None]
    nk = n_top * SEL_BLOCK
    k_sel = ksb[bi, gi, idx].reshape(B, QB, NSA_KV, nk, HEAD_DIM).astype(f32)
    v_sel = vsb[bi, gi, idx].reshape(B, QB, NSA_KV, nk, HEAD_DIM).astype(f32)
    k_pos = (idx[..., None] * SEL_BLOCK + jnp.arange(SEL_BLOCK, dtype=jnp.int32)).reshape(B, QB, NSA_KV, nk)
    s = jnp.einsum('bqgrd,bqgkd->bqgrk', qg, k_sel)
    p = masked_softmax(s, (k_pos <= q_pos[None, :, None, None])[:, :, :, None, :])
    o_slc = jnp.einsum('bqgrk,bqgkd->bqgrd', p, v_sel)
    s = jnp.einsum('bqgrd,bkgd->bqgrk', qg, win[:, :, 0].astype(f32))
    wp = w_pos[None, :]
    m = (wp <= tq) & (wp > tq - WINDOW) & (wp >= 0)
    p = masked_softmax(s, m[None, :, None, None, :])
    o_win = jnp.einsum('bqgrk,bkgd->bqgrd', p, win[:, :, 1].astype(f32))
    g = gates.reshape(B, QB, NSA_KV, NSA_REP, 3)
    o = g[..., 0:1] * o_cmp + g[..., 1:2] * o_slc + g[..., 2:3] * o_win
    return o.reshape(B, QB, NSA_Q)


NEG = -1e30


def _nsa_cmp_body(q_ref, kt_ref, v_ref, map_ref, o_ref, sel_ref, *, tq, n_cmp, n_sel):
    f32, bf16 = jnp.float32, jnp.bfloat16
    i = pl.program_id(1)
    ncp = kt_ref.shape[2]
    pos = i * tq + lax.broadcasted_iota(jnp.int32, (tq, 1), 0)
    n = lax.broadcasted_iota(jnp.int32, (1, ncp), 1)
    ok = (n * CMP_STRIDE + (CMP_BLOCK - 1) <= pos) & (n < n_cmp)
    psum = jnp.zeros((tq, ncp), f32)
    for r in range(NSA_REP):
        s = jnp.dot(q_ref[0, r], kt_ref[0], preferred_element_type=f32)
        s = jnp.where(ok, s, NEG)
        m = jnp.max(s, axis=-1, keepdims=True)
        e = jnp.where(ok, jnp.exp(s - m), 0.0)
        l = jnp.sum(e, axis=-1, keepdims=True)
        p = e * jnp.where(l > 0.0, 1.0 / l, 0.0)
        o_ref[0, r] = jnp.dot(p.astype(bf16), v_ref[0], preferred_element_type=f32)
        psum = psum + p
    hi = psum.astype(bf16)
    lo = (psum - hi.astype(f32)).astype(bf16)
    imp = (jnp.dot(hi, map_ref[...], preferred_element_type=f32)
           + jnp.dot(lo, map_ref[...], preferred_element_type=f32))
    nsp = map_ref.shape[1]
    blk = lax.broadcasted_iota(jnp.int32, (1, nsp), 1)
    cur = pos // SEL_BLOCK
    valid = (blk * SEL_BLOCK <= pos) & (blk < n_sel)
    forced = (blk == 0) | (blk == cur) | (blk == cur - 1)
    x = jnp.where(valid, jnp.where(forced, 1e30, imp), NEG)
    x = jnp.where(blk < n_sel, x, -3e38)
    blkf = blk.astype(f32)
    sel = jnp.zeros((tq, nsp), f32)
    for _ in range(min(N_SELECT, n_sel)):
        m = jnp.max(x, axis=-1, keepdims=True)
        first = jnp.min(jnp.where(x == m, blkf, float(nsp)), axis=-1, keepdims=True)
        hit = blkf == first
        sel = jnp.where(hit, 1.0, sel)
        x = jnp.where(hit, -3e38, x)
    sel_ref[0] = jnp.where(valid, sel, 0.0).astype(bf16)


def nsa_cmp_select(q4, kct, vc, sel_map, n_cmp, n_sel, *, tq=128):
    BG, _, T, _ = q4.shape
    ncp, nsp = sel_map.shape
    return pl.pallas_call(
        functools.partial(_nsa_cmp_body, tq=tq, n_cmp=n_cmp, n_sel=n_sel),
        grid=(BG, T // tq),
        in_specs=[pl.BlockSpec((1, NSA_REP, tq, HEAD_DIM), lambda b, i: (b, 0, i, 0)),
                  pl.BlockSpec((1, HEAD_DIM, ncp), lambda b, i: (b, 0, 0)),
                  pl.BlockSpec((1, ncp, HEAD_DIM), lambda b, i: (b, 0, 0)),
                  pl.BlockSpec((ncp, nsp), lambda b, i: (0, 0))],
        out_specs=[pl.BlockSpec((1, NSA_REP, tq, HEAD_DIM), lambda b, i: (b, 0, i, 0)),
                   pl.BlockSpec((1, tq, nsp), lambda b, i: (b, i, 0))],
        out_shape=[jax.ShapeDtypeStruct((BG, NSA_REP, T, HEAD_DIM), jnp.float32),
                   jax.ShapeDtypeStruct((BG, T, nsp), jnp.bfloat16)],
        compiler_params=pltpu.CompilerParams(
            dimension_semantics=("parallel", "parallel"), vmem_limit_bytes=VMEM_LIMIT),
        name="nsa_cmp_select",
    )(q4, kct, vc, sel_map)


def _nsa_flash_body(*refs, tq, tk, mode, key_off):
    f32, bf16 = jnp.float32, jnp.bfloat16
    if mode == "select":
        q_ref, kt_ref, v_ref, sel_ref, o_ref, m_scr, l_scr, acc_scr = refs
    else:
        q_ref, kt_ref, v_ref, o_ref, m_scr, l_scr, acc_scr = refs
    i, j = pl.program_id(1), pl.program_id(2)
    nj = pl.num_programs(2)

    @pl.when(j == 0)
    def _():
        m_scr[...] = jnp.full_like(m_scr, NEG)
        l_scr[...] = jnp.zeros_like(l_scr)
        acc_scr[...] = jnp.zeros_like(acc_scr)

    qpos = i * tq + lax.broadcasted_iota(jnp.int32, (tq, 1), 0)
    lane = lax.broadcasted_iota(jnp.int32, (1, tk), 1)
    if mode == "select":
        k0 = j * tk
        active = k0 <= i * tq + (tq - 1)
    else:
        k0 = (i * tq // tk + j) * tk - key_off
        active = j >= 0
    kpos = k0 + lane

    @pl.when(active)
    def _():
        if mode == "select":
            nsp = sel_ref.shape[2]
            c = lax.broadcasted_iota(jnp.int32, (nsp, tk), 0)
            l2 = lax.broadcasted_iota(jnp.int32, (nsp, tk), 1)
            expand = (c == (k0 + l2) // SEL_BLOCK).astype(bf16)
            picked = jnp.dot(sel_ref[0], expand, preferred_element_type=f32)
            allowed = (picked > 0.5) & (kpos <= qpos)
        else:
            allowed = (kpos <= qpos) & (kpos > qpos - WINDOW) & (kpos >= 0)
        for r in range(NSA_REP):
            s = jnp.dot(q_ref[0, r], kt_ref[0], preferred_element_type=f32)
            s = jnp.where(allowed, s, NEG)
            m_old = m_scr[r]
            m_new = jnp.maximum(m_old, jnp.max(s, axis=-1, keepdims=True))
            alpha = jnp.exp(m_old - m_new)
            p = jnp.where(allowed, jnp.exp(s - m_new), 0.0)
            l_scr[r] = alpha * l_scr[r] + jnp.sum(p, axis=-1, keepdims=True)
            acc_scr[r] = alpha * acc_scr[r] + jnp.dot(p.astype(bf16), v_ref[0], preferred_element_type=f32)
            m_scr[r] = m_new

    @pl.when(j == nj - 1)
    def _():
        for r in range(NSA_REP):
            o_ref[0, r] = acc_scr[r] / l_scr[r]


def nsa_flash(q4, kt, v, sel=None, *, mode, tq, tk, key_off=0):
    BG, _, T, _ = q4.shape
    if mode == "select":
        nj = T // tk
        kidx = lambda b, i, j: jnp.minimum(j, (i * tq + tq - 1) // tk)
    else:
        nj = (key_off + tq) // tk
        kidx = lambda b, i, j: i * tq // tk + j
    in_specs = [pl.BlockSpec((1, NSA_REP, tq, HEAD_DIM), lambda b, i, j: (b, 0, i, 0)),
                pl.BlockSpec((1, HEAD_DIM, tk), lambda b, i, j: (b, 0, kidx(b, i, j))),
                pl.BlockSpec((1, tk, HEAD_DIM), lambda b, i, j: (b, kidx(b, i, j), 0))]
    args = [q4, kt, v]
    if mode == "select":
        in_specs.append(pl.BlockSpec((1, tq, sel.shape[2]), lambda b, i, j: (b, i, 0)))
        args.append(sel)
    return pl.pallas_call(
        functools.partial(_nsa_flash_body, tq=tq, tk=tk, mode=mode, key_off=key_off),
        grid=(BG, T // tq, nj),
        in_specs=in_specs,
        out_specs=pl.BlockSpec((1, NSA_REP, tq, HEAD_DIM), lambda b, i, j: (b, 0, i, 0)),
        out_shape=jax.ShapeDtypeStruct((BG, NSA_REP, T, HEAD_DIM), jnp.float32),
        scratch_shapes=[pltpu.VMEM((NSA_REP, tq, 1), jnp.float32),
                        pltpu.VMEM((NSA_REP, tq, 1), jnp.float32),
                        pltpu.VMEM((NSA_REP, tq, HEAD_DIM), jnp.float32)],
        compiler_params=pltpu.CompilerParams(
            dimension_semantics=("parallel", "parallel", "arbitrary"), vmem_limit_bytes=VMEM_LIMIT),
        name="nsa_flash_" + mode,
    )(*args)


def nsa_prompt(q, gates, rows, win, cw1, cw2, cpe):
    bf16 = jnp.bfloat16
    B, T = q.shape[:2]
    G = NSA_KV
    kc = compress(rows[:, :, 0], cw1[0], cw2[0], cpe[0])
    vc = compress(rows[:, :, 1], cw1[1], cw2[1], cpe[1])
    n_cmp = kc.shape[1]
    n_sel = T // SEL_BLOCK
    ncp = -(-n_cmp // LANE) * LANE
    nsp = -(-n_sel // LANE) * LANE
    n = jnp.arange(ncp)[:, None]
    j = jnp.arange(nsp)[None, :]
    sel_map = ((n * CMP_STRIDE <= j * SEL_BLOCK + SEL_BLOCK - 1) & (n * CMP_STRIDE + CMP_BLOCK - 1 >= j * SEL_BLOCK)
               & (n < n_cmp) & (j < n_sel)).astype(bf16)
    q4 = (q * (HEAD_DIM ** -0.5)).reshape(B, T, G, NSA_REP, HEAD_DIM).transpose(0, 2, 3, 1, 4)
    q4 = q4.reshape(B * G, NSA_REP, T, HEAD_DIM).astype(bf16)

    def keys_t(x, front=0, back=0):
        x = jnp.pad(x, ((0, 0), (front, back), (0, 0), (0, 0)))
        return x.transpose(0, 2, 3, 1).reshape(B * G, HEAD_DIM, -1).astype(bf16)

    def vals(x, front=0, back=0):
        x = jnp.pad(x, ((0, 0), (front, back), (0, 0), (0, 0)))
        return x.transpose(0, 2, 1, 3).reshape(B * G, -1, HEAD_DIM).astype(bf16)

    o_cmp, sel = nsa_cmp_select(q4, keys_t(kc, 0, ncp - n_cmp), vals(vc, 0, ncp - n_cmp), sel_map, n_cmp, n_sel)
    o_slc = nsa_flash(q4, keys_t(rows[:, :, 2]), vals(rows[:, :, 3]), sel, mode="select", tq=128, tk=512)
    o_win = nsa_flash(q4, keys_t(win[:, :, 0], WINDOW), vals(win[:, :, 1], WINDOW), mode="window",
                      tq=256, tk=256, key_off=WINDOW)
    g = gates.reshape(B, T, G, NSA_REP, 3)
    un = lambda o: o.reshape(B, G, NSA_REP, T, HEAD_DIM).transpose(0, 3, 1, 2, 4)
    o = g[..., 0:1] * un(o_cmp) + g[..., 1:2] * un(o_slc) + g[..., 2:3] * un(o_win)
    return o.reshape(B, T, NSA_Q)


RW_PAIRS = RWKV_HEADS // 2
RW_CHUNK = 64


def _rwkv_scan_body(r_ref, w_ref, k_ref, kk_ref, b_ref, vt_ref, s0_ref, oh_ref,
                    y_ref, st_ref, lhs_scr, py_scr, vhi_scr, vmid_scr, *, nb, tc):
    f32, bf16 = jnp.float32, jnp.bfloat16
    c = pl.program_id(1)

    @pl.when(c == 0)
    def _():
        st_ref[...] = s0_ref[...]

    row = lax.broadcasted_iota(jnp.int32, (2 * LANE, LANE), 0)
    col = lax.broadcasted_iota(jnp.int32, (2 * LANE, LANE), 1)
    ones2 = (((row // HEAD_DIM) % 2) == (col // HEAD_DIM)).astype(bf16)
    ones1 = ones2[:LANE]
    lane_t = lax.broadcasted_iota(jnp.int32, (HEAD_DIM, LANE), 1) % HEAD_DIM

    vt = vt_ref[...].reshape(nb * RW_PAIRS, HEAD_DIM, LANE)
    vhi = vt.astype(bf16)
    vhi_scr[...] = vhi
    vmid_scr[...] = (vt - vhi.astype(f32)).astype(bf16)
    y_ref[...] = jnp.zeros_like(y_ref)

    def group(t8, carry):
        t0 = pl.multiple_of(t8 * 8, 8)
        for j in range(8):
            t = t0 + j
            row = lambda ref, b, sl: ref[b, pl.ds(t0, 8), sl][j:j + 1]
            oh = oh_ref[t]
            for b in range(nb):
                for p in range(RW_PAIRS):
                    i = b * RW_PAIRS + p
                    sl = slice(LANE * p, LANE * (p + 1))
                    pk = st_ref[b, p] * row(kk_ref, b, sl)
                    hi = pk.astype(bf16)
                    mid = (pk - hi.astype(f32)).astype(bf16)
                    lhs_scr[j, 128 * i:128 * i + 64, 0:LANE] = hi
                    lhs_scr[j, 128 * i:128 * i + 64, LANE:2 * LANE] = mid
                    lhs_scr[j, 128 * i + 64:128 * i + 128, 0:LANE] = vhi_scr[i] * oh
                    lhs_scr[j, 128 * i + 64:128 * i + 128, LANE:2 * LANE] = vmid_scr[i] * oh
            res = jnp.dot(lhs_scr[j], ones2, preferred_element_type=f32)
            for b in range(nb):
                for p in range(RW_PAIRS):
                    i = b * RW_PAIRS + p
                    sl = slice(LANE * p, LANE * (p + 1))
                    sa = res[128 * i:128 * i + 64]
                    vb = res[128 * i + 64:128 * i + 128]
                    s_new = (st_ref[b, p] * row(w_ref, b, sl) - sa * row(b_ref, b, sl)
                             + vb * row(k_ref, b, sl))
                    st_ref[b, p] = s_new
                    py_scr[j, 64 * i:64 * i + 64, :] = (s_new * row(r_ref, b, sl)).astype(bf16)
            res2 = jnp.dot(py_scr[j], ones1, preferred_element_type=f32)
            m = lane_t == t
            for b in range(nb):
                for p in range(RW_PAIRS):
                    i = b * RW_PAIRS + p
                    y_ref[b, 0, p] = jnp.where(m, res2[64 * i:64 * i + 64], y_ref[b, 0, p])
        return carry

    lax.fori_loop(0, tc // 8, group, 0)


def rwkv_scan(r, w, k, v, kk, b, S0):
    f32 = jnp.float32
    B, T, W = r.shape
    tc = RW_CHUNK if T % RW_CHUNK == 0 else T
    assert tc <= RW_CHUNK and T % tc == 0 and tc % 8 == 0 and B % 2 == 0
    nc, nb = T // tc, 2
    vt = v.reshape(B, nc, tc, RW_PAIRS, 2, HEAD_DIM).transpose(0, 1, 3, 5, 4, 2)
    vt = jnp.pad(vt, ((0, 0),) * 5 + ((0, RW_CHUNK - tc),)).reshape(B, nc, RW_PAIRS, HEAD_DIM, LANE)
    s0 = S0.astype(f32).reshape(B, RW_PAIRS, 2, HEAD_DIM, HEAD_DIM).transpose(0, 1, 3, 2, 4)
    s0 = s0.reshape(B, RW_PAIRS, HEAD_DIM, LANE)
    oh = jnp.arange(LANE)[None, None, :] % HEAD_DIM == jnp.arange(RW_CHUNK)[:, None, None]
    oh = jnp.broadcast_to(oh, (RW_CHUNK, HEAD_DIM, LANE)).astype(jnp.bfloat16)
    tok = pl.BlockSpec((nb, tc, W), lambda i, c: (i, c, 0))
    chk = pl.BlockSpec((nb, 1, RW_PAIRS, HEAD_DIM, LANE), lambda i, c: (i, c, 0, 0, 0))
    stt = pl.BlockSpec((nb, RW_PAIRS, HEAD_DIM, LANE), lambda i, c: (i, 0, 0, 0))
    y, st = pl.pallas_call(
        functools.partial(_rwkv_scan_body, nb=nb, tc=tc),
        grid=(B // nb, nc),
        in_specs=[tok, tok, tok, tok, tok, chk, stt,
                  pl.BlockSpec((RW_CHUNK, HEAD_DIM, LANE), lambda i, c: (0, 0, 0))],
        out_specs=[chk, stt],
        out_shape=[jax.ShapeDtypeStruct((B, nc, RW_PAIRS, HEAD_DIM, LANE), f32),
                   jax.ShapeDtypeStruct((B, RW_PAIRS, HEAD_DIM, LANE), f32)],
        scratch_shapes=[pltpu.VMEM((8, nb * RW_PAIRS * 128, 2 * LANE), jnp.bfloat16),
                        pltpu.VMEM((8, nb * RW_PAIRS * 64, LANE), jnp.bfloat16),
                        pltpu.VMEM((nb * RW_PAIRS, HEAD_DIM, LANE), jnp.bfloat16),
                        pltpu.VMEM((nb * RW_PAIRS, HEAD_DIM, LANE), jnp.bfloat16)],
        compiler_params=pltpu.CompilerParams(
            dimension_semantics=("parallel", "arbitrary"), vmem_limit_bytes=VMEM_LIMIT),
        name="rwkv_scan",
    )(r, w, k, kk, b, vt, s0, oh)
    y = y.reshape(B, nc, RW_PAIRS, HEAD_DIM, 2, RW_CHUNK)[..., :tc]
    y = y.transpose(0, 1, 5, 2, 4, 3).reshape(B, T, W)
    st = st.reshape(B, RW_PAIRS, HEAD_DIM, 2, HEAD_DIM).transpose(0, 1, 3, 2, 4)
    return y, st.reshape(B, RWKV_HEADS, HEAD_DIM, HEAD_DIM)


def rwkv_mix(p, prev, S0, mu, w0, wB, a0, aB, gB, k_k, k_a, r_k, ln_w, ln_b):
    f32 = jnp.float32
    B, T = p.shape[:2]
    W = RWKV_WIDTH
    p = p.astype(f32)
    p_prev = jnp.concatenate([prev.astype(f32)[:, None], p[:, :-1]], axis=1)
    ps = p + mu * (p_prev - p)
    r, k, v = ps[..., :W], ps[..., W:2 * W], ps[..., 2 * W:3 * W]
    o = 3 * W
    xw, xa, xg = ps[..., o:o + LORA_W], ps[..., o + LORA_W:o + LORA_W + LORA_A], ps[..., o + LORA_W + LORA_A:]
    z = w0 + jnp.tanh(xw) @ wB
    w = jnp.exp(-jnp.exp(-jax.nn.softplus(-z) - 0.5))
    a = jax.nn.sigmoid(a0 + xa @ aB)
    g = jax.nn.sigmoid(xg) @ gB
    heads = lambda t: t.reshape(B, T, RWKV_HEADS, HEAD_DIM)
    kk = heads(k * k_k)
    kk = kk * lax.rsqrt(jnp.maximum(jnp.sum(kk * kk, axis=-1, keepdims=True), 1e-24))
    k = k * (1.0 + (a - 1.0) * k_a)
    kk = kk.reshape(B, T, W)
    y, S = rwkv_scan(r, w, k, v, kk, kk * a, S0)
    y, k = heads(y), heads(k)
    r, v = heads(r), heads(v)
    mean = jnp.mean(y, axis=-1, keepdims=True)
    var = jnp.mean(jnp.square(y - mean), axis=-1, keepdims=True)
    y = ((y - mean) * lax.rsqrt(var + RWKV_GN_EPS)).reshape(B, T, W) * ln_w + ln_b
    y = y + (jnp.sum(r * k * r_k, axis=-1, keepdims=True) * v).reshape(B, T, W)
    return y * g, S, p[:, -1]


def gla_chunked(q, k, v, logg, S0):
    B, T, H, DK = q.shape
    DV = v.shape[-1]
    C = GLA_CHUNK if T % GLA_CHUNK == 0 else T
    N = T // C
    to_chunks = lambda x: x.reshape(B, N, C, H, x.shape[-1]).transpose(1, 0, 3, 2, 4)
    causal = jnp.tril(jnp.ones((C, C), dtype=bool))[:, :, None]

    def step(S, inp):
        qc, kc, vc, gc = inp
        b = jnp.cumsum(gc, axis=2)
        diff = b[:, :, :, None, :] - b[:, :, None, :, :]
        decay = jnp.exp(jnp.where(causal, diff, -jnp.inf))
        A = jnp.einsum('bhtd,bhsd,bhtsd->bhts', qc, kc, decay)
        o = jnp.einsum('bhts,bhsv->bhtv', A, vc) + jnp.einsum('bhtk,bhkv->bhtv', qc * jnp.exp(b), S)
        b_last = b[:, :, -1:, :]
        S = jnp.exp(b_last[:, :, 0, :])[..., None] * S + jnp.einsum('bhsk,bhsv->bhkv', kc * jnp.exp(b_last - b), vc)
        return S, o

    S, o = lax.scan(step, S0, (to_chunks(q), to_chunks(k), to_chunks(v), to_chunks(logg)))
    return o.transpose(1, 0, 3, 2, 4).reshape(B, T, H, DV), S


def gla_mix(h, S0, gate_up, gate_b, gn):
    f32 = jnp.float32
    B, T = h.shape[:2]
    h = h.astype(f32)
    q = h[..., :GLA_KW].reshape(B, T, GLA_HEADS, GLA_DK) * (GLA_DK ** -0.5)
    k = h[..., GLA_KW:2 * GLA_KW].reshape(B, T, GLA_HEADS, GLA_DK)
    o = 2 * GLA_KW
    v = h[..., o:o + GLA_VW].reshape(B, T, GLA_HEADS, GLA_DV)
    gd = h[..., o + GLA_VW:o + GLA_VW + GLA_RANK]
    r = h[..., o + GLA_VW + GLA_RANK:]
    logg = (jax.nn.log_sigmoid(gd @ gate_up + gate_b) / GLA_TAU).reshape(B, T, GLA_HEADS, GLA_DK)
    out, S = gla_chunked(q, k, v, logg, S0.astype(f32))
    out = rmsnorm(out, gn).reshape(B, T, GLA_VW) * jax.nn.silu(r)
    return out, S


def kernel(x_prompt, x_sample, cache_nsa_kv, cache_nsa_win, state_rwkv, state_rwkv_shift, state_gla, page_table, norm_mix, norm_ffn, w_in_even, nsa_qnorm, nsa_knorm, cmp_w1, cmp_w2, cmp_pe, rwkv_mu, rwkv_w0, rwkv_wB, rwkv_a0, rwkv_aB, rwkv_gB, rwkv_kk, rwkv_ka, rwkv_rk, rwkv_ln_w, rwkv_ln_b, w_out_even, w_in_odd, gla_gate_up, gla_gate_b, gla_norm, w_out_odd, ffn_gate, ffn_up, ffn_down):
    dt = x_prompt.dtype
    bf16 = jnp.bfloat16
    B, T = x_prompt.shape[:2]
    DB, S = x_sample.shape[:2]
    depth = norm_mix.shape[0]
    P = page_table.shape[1] * PAGE_SIZE
    WB = cache_nsa_win.shape[2]
    WP = min(WINDOW, T)
    pos_p = jnp.arange(T, dtype=jnp.int32)
    pos_s = P + jnp.arange(S, dtype=jnp.int32)
    win_pos_s = P - WB + jnp.arange(WB + S, dtype=jnp.int32)
    in_pad = -(-max(EVEN_IN, ODD_IN) // LANE) * LANE
    y_p, y_s = x_prompt, x_sample
    kv_p, kv_s, win_p, win_s, rw_p, rw_s, sh_p, sh_s, gl_p, gl_s = ([] for _ in range(10))
    for layer in range(depth):
        li = layer // 2
        if layer % 2 == 0:
            w_in = _pad_cols(w_in_even[li], in_pad).astype(bf16)
            w_out = w_out_even[li].astype(bf16)
            h_p = mm3(rmsnorm(y_p, norm_mix[layer]), w_in)[..., :EVEN_IN]
            h_s = mm3(rmsnorm(y_s, norm_mix[layer]), w_in)[..., :EVEN_IN]
            q, g, rows, win = nsa_project(h_p[..., :NSA_IN], pos_p, nsa_qnorm[li], nsa_knorm[li])
            o_nsa_p = nsa_prompt(q, g, rows, win, cmp_w1[li], cmp_w2[li], cmp_pe[li])
            kv_p.append(rows.astype(dt))
            win_p.append(win[:, T - WP:].astype(dt))
            q, g, rows, win = nsa_project(h_s[..., :NSA_IN], pos_s, nsa_qnorm[li], nsa_knorm[li])
            past = cache_nsa_kv[li][page_table].reshape(DB, P, 4, NSA_KV, HEAD_DIM)
            rows_all = jnp.concatenate([past, rows.astype(past.dtype)], axis=1)
            win_all = jnp.concatenate([cache_nsa_win[li], win.astype(cache_nsa_win.dtype)], axis=1)
            kc, vc, cmp_end, sel_map, ksb, vsb = nsa_keys(rows_all, cmp_w1[li], cmp_w2[li], cmp_pe[li])
            o_nsa_s = nsa_block(q, g, pos_s, kc, vc, cmp_end, sel_map, ksb, vsb, win_all, win_pos_s)
            kv_s.append(rows.astype(dt))
            win_s.append(win_all[:, S:].astype(dt))
            rw_par = (rwkv_mu[li], rwkv_w0[li], rwkv_wB[li], rwkv_a0[li], rwkv_aB[li], rwkv_gB[li],
                      rwkv_kk[li], rwkv_ka[li], rwkv_rk[li], rwkv_ln_w[li], rwkv_ln_b[li])
            o_rw_p, st, last = rwkv_mix(h_p[..., NSA_IN:], jnp.zeros((B, RWKV_IN), dt),
                                        jnp.zeros((B, RWKV_HEADS, HEAD_DIM, HEAD_DIM), dt), *rw_par)
            rw_p.append(st.astype(dt))
            sh_p.append(last.astype(dt))
            o_rw_s, st, last = rwkv_mix(h_s[..., NSA_IN:], state_rwkv_shift[li], state_rwkv[li], *rw_par)
            rw_s.append(st.astype(dt))
            sh_s.append(last.astype(dt))
            mix_p = mm3(jnp.concatenate([o_nsa_p.astype(dt), o_rw_p.astype(dt)], axis=-1), w_out)
            mix_s = mm3(jnp.concatenate([o_nsa_s.astype(dt), o_rw_s.astype(dt)], axis=-1), w_out)
        else:
            w_in = _pad_cols(w_in_odd[li], in_pad).astype(bf16)
            w_out = w_out_odd[li].astype(bf16)
            h_p = mm3(rmsnorm(y_p, norm_mix[layer]), w_in)[..., :ODD_IN]
            h_s = mm3(rmsnorm(y_s, norm_mix[layer]), w_in)[..., :ODD_IN]
            o_p, st = gla_mix(h_p, jnp.zeros((B, GLA_HEADS, GLA_DK, GLA_DV), dt),
                              gla_gate_up[li], gla_gate_b[li], gla_norm[li])
            gl_p.append(st.astype(dt))
            o_s, st = gla_mix(h_s, state_gla[li], gla_gate_up[li], gla_gate_b[li], gla_norm[li])
            gl_s.append(st.astype(dt))
            mix_p = mm3(o_p.astype(dt), w_out)
            mix_s = mm3(o_s.astype(dt), w_out)
        y_p = y_p + mix_p
        y_s = y_s + mix_s
        wg, wu, wd = ffn_gate[layer].astype(bf16), ffn_up[layer].astype(bf16), ffn_down[layer].astype(bf16)
        y_p = y_p + swiglu(rmsnorm(y_p, norm_ffn[layer]), wg, wu, wd)
        y_s = y_s + swiglu(rmsnorm(y_s, norm_ffn[layer]), wg, wu, wd)
    return (y_p, y_s, jnp.stack(kv_p), jnp.stack(kv_s), jnp.stack(win_p), jnp.stack(win_s),
            jnp.stack(rw_p), jnp.stack(rw_s), jnp.stack(sh_p), jnp.stack(sh_s),
            jnp.stack(gl_p), jnp.stack(gl_s))
```

```python
import functools
import math

import jax
import jax.numpy as jnp
from jax import lax
from jax.experimental import pallas as pl
from jax.experimental.pallas import tpu as pltpu

D_MODEL = 1024
PAGE_SIZE = 128
HEAD_DIM = 64
NSA_HEADS = 8
NSA_KV = 2
NSA_REP = NSA_HEADS // NSA_KV
CMP_STRIDE = 16
CMP_BLOCK = 2 * CMP_STRIDE
CMP_HIDDEN = 2 * HEAD_DIM
SEL_BLOCK = 64
N_SELECT = 16
WINDOW = 512
Q_BLOCK = 128
ROPE_DIM = HEAD_DIM // 4
ROPE_THETA = 500000.0
RWKV_HEADS = 8
RWKV_WIDTH = RWKV_HEADS * HEAD_DIM
LORA_W = 64
LORA_A = 64
LORA_G = 128
RWKV_GN_EPS = 64e-5
GLA_HEADS = 4
GLA_DK = D_MODEL // 2 // GLA_HEADS
GLA_DV = D_MODEL // GLA_HEADS
GLA_RANK = 16
GLA_TAU = 16.0
GLA_CHUNK = 64
D_FF = ((8 * D_MODEL // 3 + 255) // 256) * 256
NSA_Q = NSA_HEADS * HEAD_DIM
NSA_KVW = 3 * 2 * NSA_KV * HEAD_DIM
NSA_IN = NSA_Q + NSA_KVW + 3 * NSA_HEADS
RWKV_IN = 3 * RWKV_WIDTH + LORA_W + LORA_A + LORA_G
EVEN_IN = NSA_IN + RWKV_IN
MIX_WIDTH = NSA_Q + RWKV_WIDTH
GLA_KW = GLA_HEADS * GLA_DK
GLA_VW = GLA_HEADS * GLA_DV
ODD_IN = 2 * GLA_KW + GLA_VW + GLA_RANK + GLA_VW

LANE = 128
VMEM_LIMIT = 48 * 1024 * 1024


def _mm_body(x_ref, w_ref, o_ref):
    o_ref[...] = jnp.dot(x_ref[...].astype(jnp.bfloat16), w_ref[...],
                         preferred_element_type=jnp.float32)


def _pick_tile(n, cap, unit):
    best = unit
    t = unit
    while t <= min(n, cap):
        if n % t == 0:
            best = t
        t += unit
    return best


def matmul(x, w):
    m, k = x.shape
    n = w.shape[1]
    tm = _pick_tile(m, 512, 8)
    tn = _pick_tile(n, 1024, LANE)
    return pl.pallas_call(
        _mm_body,
        grid=(m // tm, n // tn),
        in_specs=[pl.BlockSpec((tm, k), lambda i, j: (i, 0)),
                  pl.BlockSpec((k, tn), lambda i, j: (0, j))],
        out_specs=pl.BlockSpec((tm, tn), lambda i, j: (i, j)),
        out_shape=jax.ShapeDtypeStruct((m, n), jnp.float32),
        compiler_params=pltpu.CompilerParams(
            dimension_semantics=("parallel", "arbitrary"), vmem_limit_bytes=VMEM_LIMIT),
        name="matmul",
    )(x, w)


def mm3(x, w):
    b, t, k = x.shape
    return matmul(x.reshape(b * t, k), w).reshape(b, t, w.shape[1])


def _pad_cols(w, n):
    return jnp.pad(w, ((0, 0), (0, n - w.shape[1])))


def rmsnorm(x, g, eps=1e-6):
    xf = x.astype(jnp.float32)
    y = xf * lax.rsqrt(jnp.mean(xf * xf, axis=-1, keepdims=True) + eps)
    return (y * g.astype(jnp.float32)).astype(x.dtype)


def rope(x, pos):
    half = ROPE_DIM // 2
    inv = ROPE_THETA ** (-jnp.arange(half, dtype=jnp.float32) / half)
    ang = pos.astype(jnp.float32)[:, None] * inv[None, :]
    shp = (pos.shape[0],) + (1,) * (x.ndim - 3) + (half,)
    cos, sin = jnp.cos(ang).reshape(shp), jnp.sin(ang).reshape(shp)
    xr = x[..., :ROPE_DIM].astype(jnp.float32)
    x1, x2 = xr[..., :half], xr[..., half:]
    rot = jnp.concatenate([x1 * cos - x2 * sin, x2 * cos + x1 * sin], axis=-1)
    return jnp.concatenate([rot.astype(x.dtype), x[..., ROPE_DIM:]], axis=-1)


def masked_softmax(s, mask):
    p = jax.nn.softmax(jnp.where(mask, s, -1e30), axis=-1)
    return jnp.where(mask, p, 0.0)


def swiglu(x, wg, wu, wd):
    g = mm3(x, wg)
    u = mm3(x, wu)
    return mm3(jax.nn.silu(g) * u, wd)


def nsa_project(h, pos, qn, kn):
    B, T = h.shape[:2]
    q = h[..., :NSA_Q].reshape(B, T, NSA_HEADS, HEAD_DIM)
    kv = h[..., NSA_Q:NSA_Q + NSA_KVW].reshape(B, T, 3, 2, NSA_KV, HEAD_DIM)
    gates = jax.nn.sigmoid(h[..., NSA_Q + NSA_KVW:NSA_IN].astype(jnp.float32)).reshape(B, T, NSA_HEADS, 3)
    q = rope(rmsnorm(q, qn), pos)
    k = rope(rmsnorm(kv[:, :, :, 0], kn[:, None, :]), pos)
    kv = jnp.stack([k, kv[:, :, :, 1]], axis=3)
    rows = kv[:, :, :2].reshape(B, T, 4, NSA_KV, HEAD_DIM)
    win = kv[:, :, 2]
    return q, gates, rows, win


def compress(k, w1, w2, pe):
    B, L = k.shape[:2]
    n_chunks = L // CMP_STRIDE
    c = k[:, :n_chunks * CMP_STRIDE].reshape(B, n_chunks, CMP_STRIDE, NSA_KV, HEAD_DIM)
    h = (jnp.einsum('bnlgd,ldh->bngh', c[:, :-1], w1[:CMP_STRIDE])
         + jnp.einsum('bnlgd,ldh->bngh', c[:, 1:], w1[CMP_STRIDE:])
         + jnp.einsum('ld,ldh->h', pe, w1))
    return jnp.einsum('bngh,hd->bngd', jax.nn.gelu(h), w2)


def nsa_keys(rows, cw1, cw2, cpe):
    B, L = rows.shape[:2]
    kc = compress(rows[:, :, 0], cw1[0], cw2[0], cpe[0])
    vc = compress(rows[:, :, 1], cw1[1], cw2[1], cpe[1])
    n_cmp = kc.shape[1]
    cmp_start = jnp.arange(n_cmp, dtype=jnp.int32) * CMP_STRIDE
    cmp_end = cmp_start + CMP_BLOCK - 1
    n_sel = -(-L // SEL_BLOCK)
    sel_start = jnp.arange(n_sel, dtype=jnp.int32) * SEL_BLOCK
    sel_map = ((cmp_start[:, None] <= sel_start[None, :] + SEL_BLOCK - 1)
               & (cmp_end[:, None] >= sel_start[None, :])).astype(jnp.float32)
    slc = jnp.pad(rows[:, :, 2:4], ((0, 0), (0, n_sel * SEL_BLOCK - L), (0, 0), (0, 0), (0, 0)))
    slc = slc.reshape(B, n_sel, SEL_BLOCK, 2, NSA_KV, HEAD_DIM).transpose(3, 0, 4, 1, 2, 5)
    return kc, vc, cmp_end, sel_map, slc[0], slc[1]


def nsa_block(q, gates, q_pos, kc, vc, cmp_end, sel_map, ksb, vsb, win, w_pos):
    f32 = jnp.float32
    B, QB = q.shape[:2]
    qg = q.reshape(B, QB, NSA_KV, NSA_REP, HEAD_DIM).astype(f32) * (HEAD_DIM ** -0.5)
    tq = q_pos[:, None]
    s = jnp.einsum('bqgrd,bngd->bqgrn', qg, kc.astype(f32))
    p = masked_softmax(s, (cmp_end[None, :] <= tq)[None, :, None, None, :])
    o_cmp = jnp.einsum('bqgrn,bngd->bqgrd', p, vc.astype(f32))
    imp = jnp.einsum('bqgrn,nj->bqgj', p, sel_map)
    n_sel = sel_map.shape[1]
    blk = jnp.arange(n_sel, dtype=jnp.int32)[None, :]
    cur = tq // SEL_BLOCK
    forced = (blk == 0) | (blk == cur) | (blk == cur - 1)
    valid = blk * SEL_BLOCK <= tq
    imp = jnp.where(valid[None, :, None, :], jnp.where(forced[None, :, None, :], 1e30, imp), -1e30)
    n_top = min(N_SELECT, n_sel)
    _, idx = lax.top_k(imp, n_top)
    bi = jnp.arange(B)[:, None, None, None]
    gi = jnp.arange(NSA_KV)[None, None, :, None]
    nk = n_top * SEL_BLOCK
    k_sel = ksb[bi, gi, idx].reshape(B, QB, NSA_KV, nk, HEAD_DIM).astype(f32)
    v_sel = vsb[bi, gi, idx].reshape(B, QB, NSA_KV, nk, HEAD_DIM).astype(f32)
    k_pos = (idx[..., None] * SEL_BLOCK + jnp.arange(SEL_BLOCK, dtype=jnp.int32)).reshape(B, QB, NSA_KV, nk)
    s = jnp.einsum('bqgrd,bqgkd->bqgrk', qg, k_sel)
    p = masked_softmax(s, (k_pos <= q_pos[None, :, None, None])[:, :, :, None, :])
    o_slc = jnp.einsum('bqgrk,bqgkd->bqgrd', p, v_sel)
    s = jnp.einsum('bqgrd,bkgd->bqgrk', qg, win[:, :, 0].astype(f32))
    wp = w_pos[None, :]
    m = (wp <= tq) & (wp > tq - WINDOW) & (wp >= 0)
    p = masked_softmax(s, m[None, :, None, None, :])
    o_win = jnp.einsum('bqgrk,bkgd->bqgrd', p, win[:, :, 1].astype(f32))
    g = gates.reshape(B, QB, NSA_KV, NSA_REP, 3)
    o = g[..., 0:1] * o_cmp + g[..., 1:2] * o_slc + g[..., 2:3] * o_win
    return o.reshape(B, QB, NSA_Q)


NEG = -1e30


def _nsa_cmp_body(q_ref, kt_ref, v_ref, map_ref, o_ref, sel_ref, *, tq, n_cmp, n_sel):
    f32, bf16 = jnp.float32, jnp.bfloat16
    i = pl.program_id(1)
    ncp = kt_ref.shape[2]
    pos = i * tq + lax.broadcasted_iota(jnp.int32, (tq, 1), 0)
    n = lax.broadcasted_iota(jnp.int32, (1, ncp), 1)
    ok = (n * CMP_STRIDE + (CMP_BLOCK - 1) <= pos) & (n < n_cmp)
    psum = jnp.zeros((tq, ncp), f32)
    for r in range(NSA_REP):
        s = jnp.dot(q_ref[0, r], kt_ref[0], preferred_element_type=f32)
        s = jnp.where(ok, s, NEG)
        m = jnp.max(s, axis=-1, keepdims=True)
        e = jnp.where(ok, jnp.exp(s - m), 0.0)
        l = jnp.sum(e, axis=-1, keepdims=True)
        p = e * jnp.where(l > 0.0, 1.0 / l, 0.0)
        o_ref[0, r] = jnp.dot(p.astype(bf16), v_ref[0], preferred_element_type=f32)
        psum = psum + p
    hi = psum.astype(bf16)
    lo = (psum - hi.astype(f32)).astype(bf16)
    imp = (jnp.dot(hi, map_ref[...], preferred_element_type=f32)
           + jnp.dot(lo, map_ref[...], preferred_element_type=f32))
    nsp = map_ref.shape[1]
    blk = lax.broadcasted_iota(jnp.int32, (1, nsp), 1)
    cur = pos // SEL_BLOCK
    valid = (blk * SEL_BLOCK <= pos) & (blk < n_sel)
    forced = (blk == 0) | (blk == cur) | (blk == cur - 1)
    x = jnp.where(valid, jnp.where(forced, 1e30, imp), NEG)
    x = jnp.where(blk < n_sel, x, -3e38)
    blkf = blk.astype(f32)
    sel = jnp.zeros((tq, nsp), f32)
    for _ in range(min(N_SELECT, n_sel)):
        m = jnp.max(x, axis=-1, keepdims=True)
        first = jnp.min(jnp.where(x == m, blkf, float(nsp)), axis=-1, keepdims=True)
        hit = blkf == first
        sel = jnp.where(hit, 1.0, sel)
        x = jnp.where(hit, -3e38, x)
    sel_ref[0] = jnp.where(valid, sel, 0.0).astype(bf16)


def nsa_cmp_select(q4, kct, vc, sel_map, n_cmp, n_sel, *, tq=128):
    BG, _, T, _ = q4.shape
    ncp, nsp = sel_map.shape
    return pl.pallas_call(
        functools.partial(_nsa_cmp_body, tq=tq, n_cmp=n_cmp, n_sel=n_sel),
        grid=(BG, T // tq),
        in_specs=[pl.BlockSpec((1, NSA_REP, tq, HEAD_DIM), lambda b, i: (b, 0, i, 0)),
                  pl.BlockSpec((1, HEAD_DIM, ncp), lambda b, i: (b, 0, 0)),
                  pl.BlockSpec((1, ncp, HEAD_DIM), lambda b, i: (b, 0, 0)),
                  pl.BlockSpec((ncp, nsp), lambda b, i: (0, 0))],
        out_specs=[pl.BlockSpec((1, NSA_REP, tq, HEAD_DIM), lambda b, i: (b, 0, i, 0)),
                   pl.BlockSpec((1, tq, nsp), lambda b, i: (b, i, 0))],
        out_shape=[jax.ShapeDtypeStruct((BG, NSA_REP, T, HEAD_DIM), jnp.float32),
                   jax.ShapeDtypeStruct((BG, T, nsp), jnp.bfloat16)],
        compiler_params=pltpu.CompilerParams(
            dimension_semantics=("parallel", "parallel"), vmem_limit_bytes=VMEM_LIMIT),
        name="nsa_cmp_select",
    )(q4, kct, vc, sel_map)


def _nsa_flash_body(*refs, tq, tk, mode, key_off):
    f32, bf16 = jnp.float32, jnp.bfloat16
    if mode == "select":
        q_ref, kt_ref, v_ref, sel_ref, o_ref, m_scr, l_scr, acc_scr = refs
    else:
        q_ref, kt_ref, v_ref, o_ref, m_scr, l_scr, acc_scr = refs
    i, j = pl.program_id(1), pl.program_id(2)
    nj = pl.num_programs(2)

    @pl.when(j == 0)
    def _():
        m_scr[...] = jnp.full_like(m_scr, NEG)
        l_scr[...] = jnp.zeros_like(l_scr)
        acc_scr[...] = jnp.zeros_like(acc_scr)

    qpos = i * tq + lax.broadcasted_iota(jnp.int32, (tq, 1), 0)
    lane = lax.broadcasted_iota(jnp.int32, (1, tk), 1)
    if mode == "select":
        k0 = j * tk
        active = k0 <= i * tq + (tq - 1)
    else:
        k0 = (i * tq // tk + j) * tk - key_off
        active = j >= 0
    kpos = k0 + lane

    @pl.when(active)
    def _():
        if mode == "select":
            nsp = sel_ref.shape[2]
            c = lax.broadcasted_iota(jnp.int32, (nsp, tk), 0)
            l2 = lax.broadcasted_iota(jnp.int32, (nsp, tk), 1)
            expand = (c == (k0 + l2) // SEL_BLOCK).astype(bf16)
            picked = jnp.dot(sel_ref[0], expand, preferred_element_type=f32)
            allowed = (picked > 0.5) & (kpos <= qpos)
        else:
            allowed = (kpos <= qpos) & (kpos > qpos - WINDOW) & (kpos >= 0)
        for r in range(NSA_REP):
            s = jnp.dot(q_ref[0, r], kt_ref[0], preferred_element_type=f32)
            s = jnp.where(allowed, s, NEG)
            m_old = m_scr[r]
            m_new = jnp.maximum(m_old, jnp.max(s, axis=-1, keepdims=True))
            alpha = jnp.exp(m_old - m_new)
            p = jnp.where(allowed, jnp.exp(s - m_new), 0.0)
            l_scr[r] = alpha * l_scr[r] + jnp.sum(p, axis=-1, keepdims=True)
            acc_scr[r] = alpha * acc_scr[r] + jnp.dot(p.astype(bf16), v_ref[0], preferred_element_type=f32)
            m_scr[r] = m_new

    @pl.when(j == nj - 1)
    def _():
        for r in range(NSA_REP):
            o_ref[0, r] = acc_scr[r] / l_scr[r]


def nsa_flash(q4, kt, v, sel=None, *, mode, tq, tk, key_off=0):
    BG, _, T, _ = q4.shape
    if mode == "select":
        nj = T // tk
        kidx = lambda b, i, j: jnp.minimum(j, (i * tq + tq - 1) // tk)
    else:
        nj = (key_off + tq) // tk
        kidx = lambda b, i, j: i * tq // tk + j
    in_specs = [pl.BlockSpec((1, NSA_REP, tq, HEAD_DIM), lambda b, i, j: (b, 0, i, 0)),
                pl.BlockSpec((1, HEAD_DIM, tk), lambda b, i, j: (b, 0, kidx(b, i, j))),
                pl.BlockSpec((1, tk, HEAD_DIM), lambda b, i, j: (b, kidx(b, i, j), 0))]
    args = [q4, kt, v]
    if mode == "select":
        in_specs.append(pl.BlockSpec((1, tq, sel.shape[2]), lambda b, i, j: (b, i, 0)))
        args.append(sel)
    return pl.pallas_call(
        functools.partial(_nsa_flash_body, tq=tq, tk=tk, mode=mode, key_off=key_off),
        grid=(BG, T // tq, nj),
        in_specs=in_specs,
        out_specs=pl.BlockSpec((1, NSA_REP, tq, HEAD_DIM), lambda b, i, j: (b, 0, i, 0)),
        out_shape=jax.ShapeDtypeStruct((BG, NSA_REP, T, HEAD_DIM), jnp.float32),
        scratch_shapes=[pltpu.VMEM((NSA_REP, tq, 1), jnp.float32),
                        pltpu.VMEM((NSA_REP, tq, 1), jnp.float32),
                        pltpu.VMEM((NSA_REP, tq, HEAD_DIM), jnp.float32)],
        compiler_params=pltpu.CompilerParams(
            dimension_semantics=("parallel", "parallel", "arbitrary"), vmem_limit_bytes=VMEM_LIMIT),
        name="nsa_flash_" + mode,
    )(*args)


def nsa_prompt(q, gates, rows, win, cw1, cw2, cpe):
    bf16 = jnp.bfloat16
    B, T = q.shape[:2]
    G = NSA_KV
    kc = compress(rows[:, :, 0], cw1[0], cw2[0], cpe[0])
    vc = compress(rows[:, :, 1], cw1[1], cw2[1], cpe[1])
    n_cmp = kc.shape[1]
    n_sel = T // SEL_BLOCK
    ncp = -(-n_cmp // LANE) * LANE
    nsp = -(-n_sel // LANE) * LANE
    n = jnp.arange(ncp)[:, None]
    j = jnp.arange(nsp)[None, :]
    sel_map = ((n * CMP_STRIDE <= j * SEL_BLOCK + SEL_BLOCK - 1) & (n * CMP_STRIDE + CMP_BLOCK - 1 >= j * SEL_BLOCK)
               & (n < n_cmp) & (j < n_sel)).astype(bf16)
    q4 = (q * (HEAD_DIM ** -0.5)).reshape(B, T, G, NSA_REP, HEAD_DIM).transpose(0, 2, 3, 1, 4)
    q4 = q4.reshape(B * G, NSA_REP, T, HEAD_DIM).astype(bf16)

    def keys_t(x, front=0, back=0):
        x = jnp.pad(x, ((0, 0), (front, back), (0, 0), (0, 0)))
        return x.transpose(0, 2, 3, 1).reshape(B * G, HEAD_DIM, -1).astype(bf16)

    def vals(x, front=0, back=0):
        x = jnp.pad(x, ((0, 0), (front, back), (0, 0), (0, 0)))
        return x.transpose(0, 2, 1, 3).reshape(B * G, -1, HEAD_DIM).astype(bf16)

    o_cmp, sel = nsa_cmp_select(q4, keys_t(kc, 0, ncp - n_cmp), vals(vc, 0, ncp - n_cmp), sel_map, n_cmp, n_sel)
    o_slc = nsa_flash(q4, keys_t(rows[:, :, 2]), vals(rows[:, :, 3]), sel, mode="select", tq=128, tk=512)
    o_win = nsa_flash(q4, keys_t(win[:, :, 0], WINDOW), vals(win[:, :, 1], WINDOW), mode="window",
                      tq=256, tk=256, key_off=WINDOW)
    g = gates.reshape(B, T, G, NSA_REP, 3)
    un = lambda o: o.reshape(B, G, NSA_REP, T, HEAD_DIM).transpose(0, 3, 1, 2, 4)
    o = g[..., 0:1] * un(o_cmp) + g[..., 1:2] * un(o_slc) + g[..., 2:3] * un(o_win)
    return o.reshape(B, T, NSA_Q)


NSA_TQ = 128
NSA_TK = 512
NSA_TKW = 128


def _nsa_attn_body(qt_ref, kc_ref, vct_ref, map_ref, ks_ref, vst_ref, kw_ref, vwt_ref, g_ref, o_ref, thr_scr,
                   *, n_cmp, n_sel):
    f32, bf16 = jnp.float32, jnp.bfloat16
    tq, tk, tkw = NSA_TQ, NSA_TK, NSA_TKW
    nl = NSA_REP * tq
    i = pl.program_id(1)
    qt = qt_ref[0, 0]
    qpos = i * tq + lax.broadcasted_iota(jnp.int32, (1, nl), 1) % tq

    ncp = kc_ref.shape[1]
    n = lax.broadcasted_iota(jnp.int32, (ncp, 1), 0)
    ok = (n * CMP_STRIDE + (CMP_BLOCK - 1) <= qpos) & (n < n_cmp)
    s = jnp.dot(kc_ref[0], qt, preferred_element_type=f32)
    s = jnp.where(ok, s, NEG)
    m = jnp.max(s, axis=0, keepdims=True)
    e = jnp.where(ok, jnp.exp(s - m), 0.0)
    l = jnp.sum(e, axis=0, keepdims=True)
    p = e * jnp.where(l > 0.0, 1.0 / l, 0.0)
    o_cmp = jnp.dot(vct_ref[0], p.astype(bf16), preferred_element_type=f32)

    psum = p[:, 0:tq]
    for r in range(1, NSA_REP):
        psum = psum + p[:, r * tq:(r + 1) * tq]
    hi = psum.astype(bf16)
    lo = (psum - hi.astype(f32)).astype(bf16)
    imp = (jnp.dot(map_ref[...], hi, preferred_element_type=f32)
           + jnp.dot(map_ref[...], lo, preferred_element_type=f32))
    nsp = map_ref.shape[0]
    qp = qpos[:, 0:tq]
    blk = lax.broadcasted_iota(jnp.int32, (nsp, 1), 0)
    cur = qp // SEL_BLOCK
    valid = (blk * SEL_BLOCK <= qp) & (blk < n_sel)
    forced = (blk == 0) | (blk == cur) | (blk == cur - 1)
    x = jnp.where(valid, jnp.where(forced, 1e30, imp), NEG)
    x = jnp.where(blk < n_sel, x, -3e38)
    blkf = blk.astype(f32)
    sel = jnp.zeros((nsp, tq), f32)
    for _ in range(min(N_SELECT, n_sel)):
        mx = jnp.max(x, axis=0, keepdims=True)
        first = jnp.min(jnp.where(x == mx, blkf, float(nsp)), axis=0, keepdims=True)
        hit = blkf == first
        sel = jnp.where(hit, 1.0, sel)
        x = jnp.where(hit, -3e38, x)
    thr = jnp.where(valid & (sel > 0.5), qp, -1)
    thr_scr[...] = jnp.concatenate([thr] * NSA_REP, axis=1)

    def flash(s, allowed, vt, carry):
        m_old, l_old, acc = carry
        m_new = jnp.maximum(m_old, jnp.max(s, axis=0, keepdims=True))
        alpha = jnp.exp(m_old - m_new)
        p = jnp.exp(s - m_new)
        if allowed is not None:
            p = jnp.where(allowed, p, 0.0)
        l_new = alpha * l_old + jnp.sum(p, axis=0, keepdims=True)
        acc = alpha * acc + jnp.dot(vt, p.astype(bf16), preferred_element_type=f32)
        return m_new, l_new, acc

    init = (jnp.full((1, nl), NEG, f32), jnp.zeros((1, nl), f32), jnp.zeros((HEAD_DIM, nl), f32))

    krow = lax.broadcasted_iota(jnp.int32, (SEL_BLOCK, 1), 0)

    def sel_step(j, carry):
        s = jnp.dot(ks_ref[0, j], qt, preferred_element_type=f32)
        th = thr_scr[pl.ds(pl.multiple_of(j * (tk // SEL_BLOCK), 8), tk // SEL_BLOCK), :]
        s = jnp.concatenate(
            [jnp.where((j * tk + b * SEL_BLOCK + krow) <= th[b:b + 1], s[b * SEL_BLOCK:(b + 1) * SEL_BLOCK], NEG)
             for b in range(tk // SEL_BLOCK)], axis=0)
        return flash(s, None, vst_ref[0, j], carry)

    _, l_s, acc_s = lax.fori_loop(0, (i * tq + tq - 1) // tk + 1, sel_step, init)
    o_slc = acc_s / l_s

    kwrow = lax.broadcasted_iota(jnp.int32, (tkw, 1), 0)

    def win_step(j, carry):
        s = jnp.dot(kw_ref[0, j], qt, preferred_element_type=f32)
        kpos = j * tkw + kwrow
        allowed = (kpos <= qpos) & (kpos > qpos - WINDOW)
        return flash(jnp.where(allowed, s, NEG), allowed, vwt_ref[0, j], carry)

    j_lo = jnp.maximum((i * tq - (WINDOW - 1)) // tkw, 0)
    _, l_w, acc_w = lax.fori_loop(j_lo, (i * tq + tq - 1) // tkw + 1, win_step, init)
    o_win = acc_w / l_w

    g = g_ref[0, 0]
    o_ref[0, 0] = g[0:1] * o_cmp + g[1:2] * o_slc + g[2:3] * o_win


def nsa_attention(q, gates, kc, vc, ksl, vsl, kwin, vwin):
    f32, bf16 = jnp.float32, jnp.bfloat16
    B, T = q.shape[:2]
    G, R, D = NSA_KV, NSA_REP, HEAD_DIM
    tq, tk, tkw = NSA_TQ, NSA_TK, NSA_TKW
    nq, nl = T // tq, R * tq
    n_cmp, n_sel = kc.shape[1], T // SEL_BLOCK
    ncp = -(-n_cmp // LANE) * LANE
    nsp = -(-n_sel // LANE) * LANE
    n = jnp.arange(ncp)[None, :]
    jb = jnp.arange(nsp)[:, None]
    map_t = ((n * CMP_STRIDE <= jb * SEL_BLOCK + SEL_BLOCK - 1) & (n * CMP_STRIDE + CMP_BLOCK - 1 >= jb * SEL_BLOCK)
             & (n < n_cmp) & (jb < n_sel)).astype(bf16)
    qt = (q * (D ** -0.5)).reshape(B, nq, tq, G, R, D).transpose(0, 3, 1, 5, 4, 2).reshape(B * G, nq, D, nl).astype(bf16)
    gt = gates.reshape(B, nq, tq, G, R, 3).transpose(0, 3, 1, 5, 4, 2).reshape(B * G, nq, 3, nl)
    gt = jnp.pad(gt, ((0, 0), (0, 0), (0, 5), (0, 0)))
    rows = lambda x, t: x.transpose(0, 2, 1, 3).reshape(B * G, -1, t, D).astype(bf16)
    cols = lambda x, t: x.reshape(B, -1, t, G, D).transpose(0, 3, 1, 4, 2).reshape(B * G, -1, D, t).astype(bf16)
    kcp = jnp.pad(kc, ((0, 0), (0, ncp - n_cmp), (0, 0), (0, 0)))
    vcp = jnp.pad(vc, ((0, 0), (0, ncp - n_cmp), (0, 0), (0, 0)))
    kc_r = rows(kcp, ncp)[:, 0]
    vc_c = cols(vcp, ncp)[:, 0]
    full = lambda shape: pl.BlockSpec((1,) + shape, lambda b, i: (b,) + (0,) * len(shape))
    per_q = lambda shape: pl.BlockSpec((1, 1) + shape, lambda b, i: (b, i) + (0,) * len(shape))
    out = pl.pallas_call(
        functools.partial(_nsa_attn_body, n_cmp=n_cmp, n_sel=n_sel),
        grid=(B * G, nq),
        in_specs=[per_q((D, nl)), full((ncp, D)), full((D, ncp)),
                  pl.BlockSpec((nsp, ncp), lambda b, i: (0, 0)),
                  full((T // tk, tk, D)), full((T // tk, D, tk)),
                  full((T // tkw, tkw, D)), full((T // tkw, D, tkw)),
                  per_q((8, nl))],
        out_specs=per_q((D, nl)),
        out_shape=jax.ShapeDtypeStruct((B * G, nq, D, nl), f32),
        scratch_shapes=[pltpu.VMEM((nsp, nl), jnp.int32)],
        compiler_params=pltpu.CompilerParams(
            dimension_semantics=("parallel", "parallel"), vmem_limit_bytes=VMEM_LIMIT),
        name="nsa_attention",
    )(qt, kc_r, vc_c, map_t, rows(ksl, tk), cols(vsl, tk), rows(kwin, tkw), cols(vwin, tkw), gt)
    return out.reshape(B, G, nq, D, R, tq).transpose(0, 2, 5, 1, 4, 3).reshape(B, T, NSA_Q)


def nsa_prompt(q, gates, rows, win, cw1, cw2, cpe):
    kc = compress(rows[:, :, 0], cw1[0], cw2[0], cpe[0])
    vc = compress(rows[:, :, 1], cw1[1], cw2[1], cpe[1])
    return nsa_attention(q, gates, kc, vc, rows[:, :, 2], rows[:, :, 3], win[:, :, 0], win[:, :, 1])


RW_PAIRS = RWKV_HEADS // 2
RW_CHUNK = 64


def _rwkv_scan_body(r_ref, w_ref, k_ref, kk_ref, b_ref, vt_ref, s0_ref, oh_ref,
                    y_ref, st_ref, lhs_scr, py_scr, vhi_scr, vmid_scr, *, nb, tc):
    f32, bf16 = jnp.float32, jnp.bfloat16
    c = pl.program_id(1)

    @pl.when(c == 0)
    def _():
        st_ref[...] = s0_ref[...]

    row = lax.broadcasted_iota(jnp.int32, (2 * LANE, LANE), 0)
    col = lax.broadcasted_iota(jnp.int32, (2 * LANE, LANE), 1)
    ones2 = (((row // HEAD_DIM) % 2) == (col // HEAD_DIM)).astype(bf16)
    ones1 = ones2[:LANE]
    lane_t = lax.broadcasted_iota(jnp.int32, (HEAD_DIM, LANE), 1) % HEAD_DIM

    vt = vt_ref[...].reshape(nb * RW_PAIRS, HEAD_DIM, LANE)
    vhi = vt.astype(bf16)
    vhi_scr[...] = vhi
    vmid_scr[...] = (vt - vhi.astype(f32)).astype(bf16)
    y_ref[...] = jnp.zeros_like(y_ref)

    def group(t8, carry):
        t0 = pl.multiple_of(t8 * 8, 8)
        for j in range(8):
            t = t0 + j
            row = lambda ref, b, sl: ref[b, pl.ds(t0, 8), sl][j:j + 1]
            oh = oh_ref[t]
            for b in range(nb):
                for p in range(RW_PAIRS):
                    i = b * RW_PAIRS + p
                    sl = slice(LANE * p, LANE * (p + 1))
                    pk = st_ref[b, p] * row(kk_ref, b, sl)
                    hi = pk.astype(bf16)
                    mid = (pk - hi.astype(f32)).astype(bf16)
                    lhs_scr[j, 128 * i:128 * i + 64, 0:LANE] = hi
                    lhs_scr[j, 128 * i:128 * i + 64, LANE:2 * LANE] = mid
                    lhs_scr[j, 128 * i + 64:128 * i + 128, 0:LANE] = vhi_scr[i] * oh
                    lhs_scr[j, 128 * i + 64:128 * i + 128, LANE:2 * LANE] = vmid_scr[i] * oh
            res = jnp.dot(lhs_scr[j], ones2, preferred_element_type=f32)
            for b in range(nb):
                for p in range(RW_PAIRS):
                    i = b * RW_PAIRS + p
                    sl = slice(LANE * p, LANE * (p + 1))
                    sa = res[128 * i:128 * i + 64]
                    vb = res[128 * i + 64:128 * i + 128]
                    s_new = (st_ref[b, p] * row(w_ref, b, sl) - sa * row(b_ref, b, sl)
                             + vb * row(k_ref, b, sl))
                    st_ref[b, p] = s_new
                    py_scr[j, 64 * i:64 * i + 64, :] = (s_new * row(r_ref, b, sl)).astype(bf16)
            res2 = jnp.dot(py_scr[j], ones1, preferred_element_type=f32)
            m = lane_t == t
            for b in range(nb):
                for p in range(RW_PAIRS):
                    i = b * RW_PAIRS + p
                    y_ref[b, 0, p] = jnp.where(m, res2[64 * i:64 * i + 64], y_ref[b, 0, p])
        return carry

    lax.fori_loop(0, tc // 8, group, 0)


def rwkv_scan(r, w, k, v, kk, b, S0):
    f32 = jnp.float32
    B, T, W = r.shape
    tc = RW_CHUNK if T % RW_CHUNK == 0 else T
    assert tc <= RW_CHUNK and T % tc == 0 and tc % 8 == 0 and B % 2 == 0
    nc, nb = T // tc, 2
    vt = v.reshape(B, nc, tc, RW_PAIRS, 2, HEAD_DIM).transpose(0, 1, 3, 5, 4, 2)
    vt = jnp.pad(vt, ((0, 0),) * 5 + ((0, RW_CHUNK - tc),)).reshape(B, nc, RW_PAIRS, HEAD_DIM, LANE)
    s0 = S0.astype(f32).reshape(B, RW_PAIRS, 2, HEAD_DIM, HEAD_DIM).transpose(0, 1, 3, 2, 4)
    s0 = s0.reshape(B, RW_PAIRS, HEAD_DIM, LANE)
    oh = jnp.arange(LANE)[None, None, :] % HEAD_DIM == jnp.arange(RW_CHUNK)[:, None, None]
    oh = jnp.broadcast_to(oh, (RW_CHUNK, HEAD_DIM, LANE)).astype(jnp.bfloat16)
    tok = pl.BlockSpec((nb, tc, W), lambda i, c: (i, c, 0))
    chk = pl.BlockSpec((nb, 1, RW_PAIRS, HEAD_DIM, LANE), lambda i, c: (i, c, 0, 0, 0))
    stt = pl.BlockSpec((nb, RW_PAIRS, HEAD_DIM, LANE), lambda i, c: (i, 0, 0, 0))
    y, st = pl.pallas_call(
        functools.partial(_rwkv_scan_body, nb=nb, tc=tc),
        grid=(B // nb, nc),
        in_specs=[tok, tok, tok, tok, tok, chk, stt,
                  pl.BlockSpec((RW_CHUNK, HEAD_DIM, LANE), lambda i, c: (0, 0, 0))],
        out_specs=[chk, stt],
        out_shape=[jax.ShapeDtypeStruct((B, nc, RW_PAIRS, HEAD_DIM, LANE), f32),
                   jax.ShapeDtypeStruct((B, RW_PAIRS, HEAD_DIM, LANE), f32)],
        scratch_shapes=[pltpu.VMEM((8, nb * RW_PAIRS * 128, 2 * LANE), jnp.bfloat16),
                        pltpu.VMEM((8, nb * RW_PAIRS * 64, LANE), jnp.bfloat16),
                        pltpu.VMEM((nb * RW_PAIRS, HEAD_DIM, LANE), jnp.bfloat16),
                        pltpu.VMEM((nb * RW_PAIRS, HEAD_DIM, LANE), jnp.bfloat16)],
        compiler_params=pltpu.CompilerParams(
            dimension_semantics=("parallel", "arbitrary"), vmem_limit_bytes=VMEM_LIMIT),
        name="rwkv_scan",
    )(r, w, k, kk, b, vt, s0, oh)
    y = y.reshape(B, nc, RW_PAIRS, HEAD_DIM, 2, RW_CHUNK)[..., :tc]
    y = y.transpose(0, 1, 5, 2, 4, 3).reshape(B, T, W)
    st = st.reshape(B, RW_PAIRS, HEAD_DIM, 2, HEAD_DIM).transpose(0, 1, 3, 2, 4)
    return y, st.reshape(B, RWKV_HEADS, HEAD_DIM, HEAD_DIM)


def rwkv_mix(p, prev, S0, mu, w0, wB, a0, aB, gB, k_k, k_a, r_k, ln_w, ln_b):
    f32 = jnp.float32
    B, T = p.shape[:2]
    W = RWKV_WIDTH
    p = p.astype(f32)
    p_prev = jnp.concatenate([prev.astype(f32)[:, None], p[:, :-1]], axis=1)
    ps = p + mu * (p_prev - p)
    r, k, v = ps[..., :W], ps[..., W:2 * W], ps[..., 2 * W:3 * W]
    o = 3 * W
    xw, xa, xg = ps[..., o:o + LORA_W], ps[..., o + LORA_W:o + LORA_W + LORA_A], ps[..., o + LORA_W + LORA_A:]
    z = w0 + jnp.tanh(xw) @ wB
    w = jnp.exp(-jnp.exp(-jax.nn.softplus(-z) - 0.5))
    a = jax.nn.sigmoid(a0 + xa @ aB)
    g = jax.nn.sigmoid(xg) @ gB
    heads = lambda t: t.reshape(B, T, RWKV_HEADS, HEAD_DIM)
    kk = heads(k * k_k)
    kk = kk * lax.rsqrt(jnp.maximum(jnp.sum(kk * kk, axis=-1, keepdims=True), 1e-24))
    k = k * (1.0 + (a - 1.0) * k_a)
    kk = kk.reshape(B, T, W)
    y, S = rwkv_scan(r, w, k, v, kk, kk * a, S0)
    y, k = heads(y), heads(k)
    r, v = heads(r), heads(v)
    mean = jnp.mean(y, axis=-1, keepdims=True)
    var = jnp.mean(jnp.square(y - mean), axis=-1, keepdims=True)
    y = ((y - mean) * lax.rsqrt(var + RWKV_GN_EPS)).reshape(B, T, W) * ln_w + ln_b
    y = y + (jnp.sum(r * k * r_k, axis=-1, keepdims=True) * v).reshape(B, T, W)
    return y * g, S, p[:, -1]


def gla_chunked(q, k, v, logg, S0):
    B, T, H, DK = q.shape
    DV = v.shape[-1]
    C = GLA_CHUNK if T % GLA_CHUNK == 0 else T
    N = T // C
    to_chunks = lambda x: x.reshape(B, N, C, H, x.shape[-1]).transpose(1, 0, 3, 2, 4)
    causal = jnp.tril(jnp.ones((C, C), dtype=bool))[:, :, None]

    def step(S, inp):
        qc, kc, vc, gc = inp
        b = jnp.cumsum(gc, axis=2)
        diff = b[:, :, :, None, :] - b[:, :, None, :, :]
        decay = jnp.exp(jnp.where(causal, diff, -jnp.inf))
        A = jnp.einsum('bhtd,bhsd,bhtsd->bhts', qc, kc, decay)
        o = jnp.einsum('bhts,bhsv->bhtv', A, vc) + jnp.einsum('bhtk,bhkv->bhtv', qc * jnp.exp(b), S)
        b_last = b[:, :, -1:, :]
        S = jnp.exp(b_last[:, :, 0, :])[..., None] * S + jnp.einsum('bhsk,bhsv->bhkv', kc * jnp.exp(b_last - b), vc)
        return S, o

    S, o = lax.scan(step, S0, (to_chunks(q), to_chunks(k), to_chunks(v), to_chunks(logg)))
    return o.transpose(1, 0, 3, 2, 4).reshape(B, T, H, DV), S


GLA_SUB = 16


def _gla_body(q_ref, k_ref, v_ref, g_ref, r_ref, gn_ref, o_ref, st_ref):
    f32, bf16 = jnp.float32, jnp.bfloat16
    C, SUB = GLA_CHUNK, GLA_SUB
    nsub = C // SUB
    c = pl.program_id(1)

    @pl.when(c == 0)
    def _():
        st_ref[...] = jnp.zeros_like(st_ref)

    ti = lax.broadcasted_iota(jnp.int32, (C, C), 0)
    si = lax.broadcasted_iota(jnp.int32, (C, C), 1)
    tri = (si <= ti).astype(bf16)
    sub_t = lax.broadcasted_iota(jnp.int32, (SUB, 1), 0)
    sub_l = lax.broadcasted_iota(jnp.int32, (SUB, SUB), 1)
    for h in range(GLA_HEADS):
        kq = slice(h * GLA_DK, (h + 1) * GLA_DK)
        vv = slice(h * GLA_DV, (h + 1) * GLA_DV)
        q = q_ref[0, :, kq] * (GLA_DK ** -0.5)
        k = k_ref[0, :, kq]
        v = v_ref[0, :, vv].astype(bf16)
        g = g_ref[0, :, kq]
        g1 = g.astype(bf16)
        g2 = (g - g1.astype(f32)).astype(bf16)
        g3 = (g - g1.astype(f32) - g2.astype(f32)).astype(bf16)
        b = (jnp.dot(tri, g1, preferred_element_type=f32) + jnp.dot(tri, g2, preferred_element_type=f32)
             + jnp.dot(tri, g3, preferred_element_type=f32))
        b_last = b[C - 1:C]
        a_rows = []
        for I in range(nsub):
            rows = slice(I * SUB, (I + 1) * SUB)
            beta = b[I * SUB - 1:I * SUB] if I > 0 else jnp.zeros((1, GLA_DK), f32)
            qb, kb, bb = q[rows], k[rows], b[rows]
            a_diag = jnp.zeros((SUB, SUB), f32)
            for s in range(SUB):
                d = jnp.where(sub_t >= s, bb - bb[s:s + 1], -jnp.inf)
                col = jnp.sum(qb * kb[s:s + 1] * jnp.exp(d), axis=1, keepdims=True)
                a_diag = jnp.where(sub_l == s, col, a_diag)
            blocks = []
            if I > 0:
                qe = (qb * jnp.exp(bb - beta)).astype(bf16)
                ke = (k[:I * SUB] * jnp.exp(beta - b[:I * SUB])).astype(bf16)
                blocks.append(lax.dot_general(qe, ke, (((1,), (1,)), ((), ())), preferred_element_type=f32))
            blocks.append(a_diag)
            if I < nsub - 1:
                blocks.append(jnp.zeros((SUB, C - (I + 1) * SUB), f32))
            a_rows.append(jnp.concatenate(blocks, axis=1))
        a = jnp.concatenate(a_rows, axis=0).astype(bf16)
        st = st_ref[0, h]
        qd = (q * jnp.exp(b)).astype(bf16)
        o = (jnp.dot(a, v, preferred_element_type=f32)
             + lax.dot_general(qd, st.astype(bf16), (((1,), (1,)), ((), ())), preferred_element_type=f32))
        kd = (k * jnp.exp(b_last - b)).astype(bf16)
        st_ref[0, h] = (st * jnp.exp(b_last)
                        + lax.dot_general(v, kd, (((0,), (0,)), ((), ())), preferred_element_type=f32))
        y = o * lax.rsqrt(jnp.mean(o * o, axis=1, keepdims=True) + 1e-6) * gn_ref[...]
        rr = r_ref[0, :, vv]
        o_ref[0, :, vv] = y * (rr * jax.nn.sigmoid(rr))


def gla_prompt(q, k, v, logg, r, gn):
    f32 = jnp.float32
    B, T, KW = q.shape
    VW = v.shape[2]
    C = GLA_CHUNK
    tokk = pl.BlockSpec((1, C, KW), lambda b, c: (b, c, 0))
    tokv = pl.BlockSpec((1, C, VW), lambda b, c: (b, c, 0))
    out, st = pl.pallas_call(
        _gla_body,
        grid=(B, T // C),
        in_specs=[tokk, tokk, tokv, tokk, tokv, pl.BlockSpec((1, GLA_DV), lambda b, c: (0, 0))],
        out_specs=[tokv, pl.BlockSpec((1, GLA_HEADS, GLA_DV, GLA_DK), lambda b, c: (b, 0, 0, 0))],
        out_shape=[jax.ShapeDtypeStruct((B, T, VW), f32),
                   jax.ShapeDtypeStruct((B, GLA_HEADS, GLA_DV, GLA_DK), f32)],
        compiler_params=pltpu.CompilerParams(
            dimension_semantics=("parallel", "arbitrary"), vmem_limit_bytes=VMEM_LIMIT),
        name="gla_chunk",
    )(q, k, v, logg, r, gn.reshape(1, GLA_DV))
    return out, st.transpose(0, 1, 3, 2)


def gla_mix_prompt(h, gate_up, gate_b, gn):
    h = h.astype(jnp.float32)
    o = 2 * GLA_KW
    gd = h[..., o + GLA_VW:o + GLA_VW + GLA_RANK]
    logg = jax.nn.log_sigmoid(gd @ gate_up + gate_b) / GLA_TAU
    return gla_prompt(h[..., :GLA_KW], h[..., GLA_KW:o], h[..., o:o + GLA_VW], logg,
                      h[..., o + GLA_VW + GLA_RANK:], gn)


def gla_mix(h, S0, gate_up, gate_b, gn):
    f32 = jnp.float32
    B, T = h.shape[:2]
    h = h.astype(f32)
    q = h[..., :GLA_KW].reshape(B, T, GLA_HEADS, GLA_DK) * (GLA_DK ** -0.5)
    k = h[..., GLA_KW:2 * GLA_KW].reshape(B, T, GLA_HEADS, GLA_DK)
    o = 2 * GLA_KW
    v = h[..., o:o + GLA_VW].reshape(B, T, GLA_HEADS, GLA_DV)
    gd = h[..., o + GLA_VW:o + GLA_VW + GLA_RANK]
    r = h[..., o + GLA_VW + GLA_RANK:]
    logg = (jax.nn.log_sigmoid(gd @ gate_up + gate_b) / GLA_TAU).reshape(B, T, GLA_HEADS, GLA_DK)
    out, S = gla_chunked(q, k, v, logg, S0.astype(f32))
    out = rmsnorm(out, gn).reshape(B, T, GLA_VW) * jax.nn.silu(r)
    return out, S


def kernel(x_prompt, x_sample, cache_nsa_kv, cache_nsa_win, state_rwkv, state_rwkv_shift, state_gla, page_table, norm_mix, norm_ffn, w_in_even, nsa_qnorm, nsa_knorm, cmp_w1, cmp_w2, cmp_pe, rwkv_mu, rwkv_w0, rwkv_wB, rwkv_a0, rwkv_aB, rwkv_gB, rwkv_kk, rwkv_ka, rwkv_rk, rwkv_ln_w, rwkv_ln_b, w_out_even, w_in_odd, gla_gate_up, gla_gate_b, gla_norm, w_out_odd, ffn_gate, ffn_up, ffn_down):
    dt = x_prompt.dtype
    bf16 = jnp.bfloat16
    B, T = x_prompt.shape[:2]
    DB, S = x_sample.shape[:2]
    depth = norm_mix.shape[0]
    P = page_table.shape[1] * PAGE_SIZE
    WB = cache_nsa_win.shape[2]
    WP = min(WINDOW, T)
    pos_p = jnp.arange(T, dtype=jnp.int32)
    pos_s = P + jnp.arange(S, dtype=jnp.int32)
    win_pos_s = P - WB + jnp.arange(WB + S, dtype=jnp.int32)
    in_pad = -(-max(EVEN_IN, ODD_IN) // LANE) * LANE
    y_p, y_s = x_prompt, x_sample
    kv_p, kv_s, win_p, win_s, rw_p, rw_s, sh_p, sh_s, gl_p, gl_s = ([] for _ in range(10))
    for layer in range(depth):
        li = layer // 2
        if layer % 2 == 0:
            w_in = _pad_cols(w_in_even[li], in_pad).astype(bf16)
            w_out = w_out_even[li].astype(bf16)
            h_p = mm3(rmsnorm(y_p, norm_mix[layer]), w_in)[..., :EVEN_IN]
            h_s = mm3(rmsnorm(y_s, norm_mix[layer]), w_in)[..., :EVEN_IN]
            q, g, rows, win = nsa_project(h_p[..., :NSA_IN], pos_p, nsa_qnorm[li], nsa_knorm[li])
            o_nsa_p = nsa_prompt(q, g, rows, win, cmp_w1[li], cmp_w2[li], cmp_pe[li])
            kv_p.append(rows.astype(dt))
            win_p.append(win[:, T - WP:].astype(dt))
            q, g, rows, win = nsa_project(h_s[..., :NSA_IN], pos_s, nsa_qnorm[li], nsa_knorm[li])
            past = cache_nsa_kv[li][page_table].reshape(DB, P, 4, NSA_KV, HEAD_DIM)
            rows_all = jnp.concatenate([past, rows.astype(past.dtype)], axis=1)
            win_all = jnp.concatenate([cache_nsa_win[li], win.astype(cache_nsa_win.dtype)], axis=1)
            kc, vc, cmp_end, sel_map, ksb, vsb = nsa_keys(rows_all, cmp_w1[li], cmp_w2[li], cmp_pe[li])
            o_nsa_s = nsa_block(q, g, pos_s, kc, vc, cmp_end, sel_map, ksb, vsb, win_all, win_pos_s)
            kv_s.append(rows.astype(dt))
            win_s.append(win_all[:, S:].astype(dt))
            rw_par = (rwkv_mu[li], rwkv_w0[li], rwkv_wB[li], rwkv_a0[li], rwkv_aB[li], rwkv_gB[li],
                      rwkv_kk[li], rwkv_ka[li], rwkv_rk[li], rwkv_ln_w[li], rwkv_ln_b[li])
            o_rw_p, st, last = rwkv_mix(h_p[..., NSA_IN:], jnp.zeros((B, RWKV_IN), dt),
                                        jnp.zeros((B, RWKV_HEADS, HEAD_DIM, HEAD_DIM), dt), *rw_par)
            rw_p.append(st.astype(dt))
            sh_p.append(last.astype(dt))
            o_rw_s, st, last = rwkv_mix(h_s[..., NSA_IN:], state_rwkv_shift[li], state_rwkv[li], *rw_par)
            rw_s.append(st.astype(dt))
            sh_s.append(last.astype(dt))
            mix_p = mm3(jnp.concatenate([o_nsa_p.astype(dt), o_rw_p.astype(dt)], axis=-1), w_out)
            mix_s = mm3(jnp.concatenate([o_nsa_s.astype(dt), o_rw_s.astype(dt)], axis=-1), w_out)
        else:
            w_in = _pad_cols(w_in_odd[li], in_pad).astype(bf16)
            w_out = w_out_odd[li].astype(bf16)
            h_p = mm3(rmsnorm(y_p, norm_mix[layer]), w_in)[..., :ODD_IN]
            h_s = mm3(rmsnorm(y_s, norm_mix[layer]), w_in)[..., :ODD_IN]
            o_p, st = gla_mix_prompt(h_p, gla_gate_up[li], gla_gate_b[li], gla_norm[li])
            gl_p.append(st.astype(dt))
            o_s, st = gla_mix(h_s, state_gla[li], gla_gate_up[li], gla_gate_b[li], gla_norm[li])
            gl_s.append(st.astype(dt))
            mix_p = mm3(o_p.astype(dt), w_out)
            mix_s = mm3(o_s.astype(dt), w_out)
        y_p = y_p + mix_p
        y_s = y_s + mix_s
        wg, wu, wd = ffn_gate[layer].astype(bf16), ffn_up[layer].astype(bf16), ffn_down[layer].astype(bf16)
        y_p = y_p + swiglu(rmsnorm(y_p, norm_ffn[layer]), wg, wu, wd)
        y_s = y_s + swiglu(rmsnorm(y_s, norm_ffn[layer]), wg, wu, wd)
    return (y_p, y_s, jnp.stack(kv_p), jnp.stack(kv_s), jnp.stack(win_p), jnp.stack(win_s),
            jnp.stack(rw_p), jnp.stack(rw_s), jnp.stack(sh_p), jnp.stack(sh_s),
            jnp.stack(gl_p), jnp.stack(gl_s))
```

```python
import functools
import math

import jax
import jax.numpy as jnp
from jax import lax
from jax.experimental import pallas as pl
from jax.experimental.pallas import tpu as pltpu

D_MODEL = 1024
PAGE_SIZE = 128
HEAD_DIM = 64
NSA_HEADS = 8
NSA_KV = 2
NSA_REP = NSA_HEADS // NSA_KV
CMP_STRIDE = 16
CMP_BLOCK = 2 * CMP_STRIDE
CMP_HIDDEN = 2 * HEAD_DIM
SEL_BLOCK = 64
N_SELECT = 16
WINDOW = 512
Q_BLOCK = 128
ROPE_DIM = HEAD_DIM // 4
ROPE_THETA = 500000.0
RWKV_HEADS = 8
RWKV_WIDTH = RWKV_HEADS * HEAD_DIM
LORA_W = 64
LORA_A = 64
LORA_G = 128
RWKV_GN_EPS = 64e-5
GLA_HEADS = 4
GLA_DK = D_MODEL // 2 // GLA_HEADS
GLA_DV = D_MODEL // GLA_HEADS
GLA_RANK = 16
GLA_TAU = 16.0
GLA_CHUNK = 64
D_FF = ((8 * D_MODEL // 3 + 255) // 256) * 256
NSA_Q = NSA_HEADS * HEAD_DIM
NSA_KVW = 3 * 2 * NSA_KV * HEAD_DIM
NSA_IN = NSA_Q + NSA_KVW + 3 * NSA_HEADS
RWKV_IN = 3 * RWKV_WIDTH + LORA_W + LORA_A + LORA_G
EVEN_IN = NSA_IN + RWKV_IN
MIX_WIDTH = NSA_Q + RWKV_WIDTH
GLA_KW = GLA_HEADS * GLA_DK
GLA_VW = GLA_HEADS * GLA_DV
ODD_IN = 2 * GLA_KW + GLA_VW + GLA_RANK + GLA_VW

LANE = 128
VMEM_LIMIT = 48 * 1024 * 1024


ROW_TILE = 512
NORM_EPS = 1e-6


def _row_tile(m):
    return ROW_TILE if m % ROW_TILE == 0 else m


def _norm_bf16(x_ref, g_ref):
    x = x_ref[...]
    y = x * lax.rsqrt(jnp.mean(x * x, axis=-1, keepdims=True) + NORM_EPS)
    return (y * g_ref[...]).astype(jnp.bfloat16)


def _norm_proj_body(x_ref, g_ref, w_ref, *o_refs, splits):
    xn = _norm_bf16(x_ref, g_ref)
    for (a, b), o_ref in zip(splits, o_refs):
        o_ref[...] = jnp.dot(xn, w_ref[:, a:b], preferred_element_type=jnp.float32)


def norm_proj(x, gain, w, splits):
    bsz, t, k = x.shape
    m = bsz * t
    tm = _row_tile(m)
    outs = pl.pallas_call(
        functools.partial(_norm_proj_body, splits=splits),
        grid=(m // tm,),
        in_specs=[pl.BlockSpec((tm, k), lambda i: (i, 0)),
                  pl.BlockSpec((1, k), lambda i: (0, 0)),
                  pl.BlockSpec(w.shape, lambda i: (0, 0))],
        out_specs=[pl.BlockSpec((tm, b - a), lambda i: (i, 0)) for a, b in splits],
        out_shape=[jax.ShapeDtypeStruct((m, b - a), jnp.float32) for a, b in splits],
        compiler_params=pltpu.CompilerParams(dimension_semantics=("parallel",), vmem_limit_bytes=VMEM_LIMIT),
        name="norm_proj",
    )(x.reshape(m, k), gain.reshape(1, k), w)
    return [o.reshape(bsz, t, -1) for o in outs]


def _proj_res_body(*refs, n_in):
    x_refs, w_refs, res_ref, o_ref = refs[:n_in], refs[n_in:2 * n_in], refs[2 * n_in], refs[2 * n_in + 1]
    acc = res_ref[...]
    for x_ref, w_ref in zip(x_refs, w_refs):
        acc = acc + jnp.dot(x_ref[...].astype(jnp.bfloat16), w_ref[...], preferred_element_type=jnp.float32)
    o_ref[...] = acc


def proj_residual(xs, ws, res):
    bsz, t, n = res.shape
    m = bsz * t
    tm = _row_tile(m)
    out = pl.pallas_call(
        functools.partial(_proj_res_body, n_in=len(xs)),
        grid=(m // tm,),
        in_specs=([pl.BlockSpec((tm, x.shape[-1]), lambda i: (i, 0)) for x in xs]
                  + [pl.BlockSpec(w.shape, lambda i: (0, 0)) for w in ws]
                  + [pl.BlockSpec((tm, n), lambda i: (i, 0))]),
        out_specs=pl.BlockSpec((tm, n), lambda i: (i, 0)),
        out_shape=jax.ShapeDtypeStruct((m, n), jnp.float32),
        compiler_params=pltpu.CompilerParams(dimension_semantics=("parallel",), vmem_limit_bytes=VMEM_LIMIT),
        name="proj_residual",
    )(*[x.reshape(m, x.shape[-1]) for x in xs], *ws, res.reshape(m, n))
    return out.reshape(bsz, t, n)


FFN_COL_CHUNKS = 2


def _ffn_up_body(x_ref, g_ref, wg_ref, wu_ref, h_ref):
    xn = _norm_bf16(x_ref, g_ref)
    cw = h_ref.shape[1] // FFN_COL_CHUNKS
    for c in range(FFN_COL_CHUNKS):
        cols = slice(c * cw, (c + 1) * cw)
        g = jnp.dot(xn, wg_ref[:, cols], preferred_element_type=jnp.float32)
        u = jnp.dot(xn, wu_ref[:, cols], preferred_element_type=jnp.float32)
        h_ref[:, cols] = (g * jax.nn.sigmoid(g) * u).astype(h_ref.dtype)


def ffn(y, gain, wg, wu, wd):
    bsz, t, k = y.shape
    m = bsz * t
    tm = _row_tile(m)
    f = wg.shape[1]
    h = pl.pallas_call(
        _ffn_up_body,
        grid=(m // tm,),
        in_specs=[pl.BlockSpec((tm, k), lambda i: (i, 0)),
                  pl.BlockSpec((1, k), lambda i: (0, 0)),
                  pl.BlockSpec(wg.shape, lambda i: (0, 0)),
                  pl.BlockSpec(wu.shape, lambda i: (0, 0))],
        out_specs=pl.BlockSpec((tm, f), lambda i: (i, 0)),
        out_shape=jax.ShapeDtypeStruct((m, f), jnp.bfloat16),
        compiler_params=pltpu.CompilerParams(dimension_semantics=("parallel",), vmem_limit_bytes=VMEM_LIMIT),
        name="ffn_up",
    )(y.reshape(m, k), gain.reshape(1, k), wg, wu)
    return proj_residual([h.reshape(bsz, t, f)], [wd], y)


def rmsnorm(x, g, eps=1e-6):
    xf = x.astype(jnp.float32)
    y = xf * lax.rsqrt(jnp.mean(xf * xf, axis=-1, keepdims=True) + eps)
    return (y * g.astype(jnp.float32)).astype(x.dtype)


def rope(x, pos):
    half = ROPE_DIM // 2
    inv = ROPE_THETA ** (-jnp.arange(half, dtype=jnp.float32) / half)
    ang = pos.astype(jnp.float32)[:, None] * inv[None, :]
    shp = (pos.shape[0],) + (1,) * (x.ndim - 3) + (half,)
    cos, sin = jnp.cos(ang).reshape(shp), jnp.sin(ang).reshape(shp)
    xr = x[..., :ROPE_DIM].astype(jnp.float32)
    x1, x2 = xr[..., :half], xr[..., half:]
    rot = jnp.concatenate([x1 * cos - x2 * sin, x2 * cos + x1 * sin], axis=-1)
    return jnp.concatenate([rot.astype(x.dtype), x[..., ROPE_DIM:]], axis=-1)


def masked_softmax(s, mask):
    p = jax.nn.softmax(jnp.where(mask, s, -1e30), axis=-1)
    return jnp.where(mask, p, 0.0)


def nsa_project(h, pos, qn, kn):
    B, T = h.shape[:2]
    q = h[..., :NSA_Q].reshape(B, T, NSA_HEADS, HEAD_DIM)
    kv = h[..., NSA_Q:NSA_Q + NSA_KVW].reshape(B, T, 3, 2, NSA_KV, HEAD_DIM)
    gates = jax.nn.sigmoid(h[..., NSA_Q + NSA_KVW:NSA_IN].astype(jnp.float32)).reshape(B, T, NSA_HEADS, 3)
    q = rope(rmsnorm(q, qn), pos)
    k = rope(rmsnorm(kv[:, :, :, 0], kn[:, None, :]), pos)
    kv = jnp.stack([k, kv[:, :, :, 1]], axis=3)
    rows = kv[:, :, :2].reshape(B, T, 4, NSA_KV, HEAD_DIM)
    win = kv[:, :, 2]
    return q, gates, rows, win


def compress(k, w1, w2, pe):
    B, L = k.shape[:2]
    n_chunks = L // CMP_STRIDE
    c = k[:, :n_chunks * CMP_STRIDE].reshape(B, n_chunks, CMP_STRIDE, NSA_KV, HEAD_DIM)
    h = (jnp.einsum('bnlgd,ldh->bngh', c[:, :-1], w1[:CMP_STRIDE])
         + jnp.einsum('bnlgd,ldh->bngh', c[:, 1:], w1[CMP_STRIDE:])
         + jnp.einsum('ld,ldh->h', pe, w1))
    return jnp.einsum('bngh,hd->bngd', jax.nn.gelu(h), w2)


def nsa_keys(rows, cw1, cw2, cpe):
    B, L = rows.shape[:2]
    kc = compress(rows[:, :, 0], cw1[0], cw2[0], cpe[0])
    vc = compress(rows[:, :, 1], cw1[1], cw2[1], cpe[1])
    n_cmp = kc.shape[1]
    cmp_start = jnp.arange(n_cmp, dtype=jnp.int32) * CMP_STRIDE
    cmp_end = cmp_start + CMP_BLOCK - 1
    n_sel = -(-L // SEL_BLOCK)
    sel_start = jnp.arange(n_sel, dtype=jnp.int32) * SEL_BLOCK
    sel_map = ((cmp_start[:, None] <= sel_start[None, :] + SEL_BLOCK - 1)
               & (cmp_end[:, None] >= sel_start[None, :])).astype(jnp.float32)
    slc = jnp.pad(rows[:, :, 2:4], ((0, 0), (0, n_sel * SEL_BLOCK - L), (0, 0), (0, 0), (0, 0)))
    slc = slc.reshape(B, n_sel, SEL_BLOCK, 2, NSA_KV, HEAD_DIM).transpose(3, 0, 4, 1, 2, 5)
    return kc, vc, cmp_end, sel_map, slc[0], slc[1]


def nsa_block(q, gates, q_pos, kc, vc, cmp_end, sel_map, ksb, vsb, win, w_pos):
    f32 = jnp.float32
    B, QB = q.shape[:2]
    qg = q.reshape(B, QB, NSA_KV, NSA_REP, HEAD_DIM).astype(f32) * (HEAD_DIM ** -0.5)
    tq = q_pos[:, None]
    s = jnp.einsum('bqgrd,bngd->bqgrn', qg, kc.astype(f32))
    p = masked_softmax(s, (cmp_end[None, :] <= tq)[None, :, None, None, :])
    o_cmp = jnp.einsum('bqgrn,bngd->bqgrd', p, vc.astype(f32))
    imp = jnp.einsum('bqgrn,nj->bqgj', p, sel_map)
    n_sel = sel_map.shape[1]
    blk = jnp.arange(n_sel, dtype=jnp.int32)[None, :]
    cur = tq // SEL_BLOCK
    forced = (blk == 0) | (blk == cur) | (blk == cur - 1)
    valid = blk * SEL_BLOCK <= tq
    imp = jnp.where(valid[None, :, None, :], jnp.where(forced[None, :, None, :], 1e30, imp), -1e30)
    n_top = min(N_SELECT, n_sel)
    _, idx = lax.top_k(imp, n_top)
    bi = jnp.arange(B)[:, None, None, None]
    gi = jnp.arange(NSA_KV)[None, None, :, None]
    nk = n_top * SEL_BLOCK
    k_sel = ksb[bi, gi, idx].reshape(B, QB, NSA_KV, nk, HEAD_DIM).astype(f32)
    v_sel = vsb[bi, gi, idx].reshape(B, QB, NSA_KV, nk, HEAD_DIM).astype(f32)
    k_pos = (idx[..., None] * SEL_BLOCK + jnp.arange(SEL_BLOCK, dtype=jnp.int32)).reshape(B, QB, NSA_KV, nk)
    s = jnp.einsum('bqgrd,bqgkd->bqgrk', qg, k_sel)
    p = masked_softmax(s, (k_pos <= q_pos[None, :, None, None])[:, :, :, None, :])
    o_slc = jnp.einsum('bqgrk,bqgkd->bqgrd', p, v_sel)
    s = jnp.einsum('bqgrd,bkgd->bqgrk', qg, win[:, :, 0].astype(f32))
    wp = w_pos[None, :]
    m = (wp <= tq) & (wp > tq - WINDOW) & (wp >= 0)
    p = masked_softmax(s, m[None, :, None, None, :])
    o_win = jnp.einsum('bqgrk,bkgd->bqgrd', p, win[:, :, 1].astype(f32))
    g = gates.reshape(B, QB, NSA_KV, NSA_REP, 3)
    o = g[..., 0:1] * o_cmp + g[..., 1:2] * o_slc + g[..., 2:3] * o_win
    return o.reshape(B, QB, NSA_Q)


NEG = -1e30


def _nsa_cmp_body(q_ref, kt_ref, v_ref, map_ref, o_ref, sel_ref, *, tq, n_cmp, n_sel):
    f32, bf16 = jnp.float32, jnp.bfloat16
    i = pl.program_id(1)
    ncp = kt_ref.shape[2]
    pos = i * tq + lax.broadcasted_iota(jnp.int32, (tq, 1), 0)
    n = lax.broadcasted_iota(jnp.int32, (1, ncp), 1)
    ok = (n * CMP_STRIDE + (CMP_BLOCK - 1) <= pos) & (n < n_cmp)
    psum = jnp.zeros((tq, ncp), f32)
    for r in range(NSA_REP):
        s = jnp.dot(q_ref[0, r], kt_ref[0], preferred_element_type=f32)
        s = jnp.where(ok, s, NEG)
        m = jnp.max(s, axis=-1, keepdims=True)
        e = jnp.where(ok, jnp.exp(s - m), 0.0)
        l = jnp.sum(e, axis=-1, keepdims=True)
        p = e * jnp.where(l > 0.0, 1.0 / l, 0.0)
        o_ref[0, r] = jnp.dot(p.astype(bf16), v_ref[0], preferred_element_type=f32)
        psum = psum + p
    hi = psum.astype(bf16)
    lo = (psum - hi.astype(f32)).astype(bf16)
    imp = (jnp.dot(hi, map_ref[...], preferred_element_type=f32)
           + jnp.dot(lo, map_ref[...], preferred_element_type=f32))
    nsp = map_ref.shape[1]
    blk = lax.broadcasted_iota(jnp.int32, (1, nsp), 1)
    cur = pos // SEL_BLOCK
    valid = (blk * SEL_BLOCK <= pos) & (blk < n_sel)
    forced = (blk == 0) | (blk == cur) | (blk == cur - 1)
    x = jnp.where(valid, jnp.where(forced, 1e30, imp), NEG)
    x = jnp.where(blk < n_sel, x, -3e38)
    blkf = blk.astype(f32)
    sel = jnp.zeros((tq, nsp), f32)
    for _ in range(min(N_SELECT, n_sel)):
        m = jnp.max(x, axis=-1, keepdims=True)
        first = jnp.min(jnp.where(x == m, blkf, float(nsp)), axis=-1, keepdims=True)
        hit = blkf == first
        sel = jnp.where(hit, 1.0, sel)
        x = jnp.where(hit, -3e38, x)
    sel_ref[0] = jnp.where(valid, sel, 0.0).astype(bf16)


def nsa_cmp_select(q4, kct, vc, sel_map, n_cmp, n_sel, *, tq=128):
    BG, _, T, _ = q4.shape
    ncp, nsp = sel_map.shape
    return pl.pallas_call(
        functools.partial(_nsa_cmp_body, tq=tq, n_cmp=n_cmp, n_sel=n_sel),
        grid=(BG, T // tq),
        in_specs=[pl.BlockSpec((1, NSA_REP, tq, HEAD_DIM), lambda b, i: (b, 0, i, 0)),
                  pl.BlockSpec((1, HEAD_DIM, ncp), lambda b, i: (b, 0, 0)),
                  pl.BlockSpec((1, ncp, HEAD_DIM), lambda b, i: (b, 0, 0)),
                  pl.BlockSpec((ncp, nsp), lambda b, i: (0, 0))],
        out_specs=[pl.BlockSpec((1, NSA_REP, tq, HEAD_DIM), lambda b, i: (b, 0, i, 0)),
                   pl.BlockSpec((1, tq, nsp), lambda b, i: (b, i, 0))],
        out_shape=[jax.ShapeDtypeStruct((BG, NSA_REP, T, HEAD_DIM), jnp.float32),
                   jax.ShapeDtypeStruct((BG, T, nsp), jnp.bfloat16)],
        compiler_params=pltpu.CompilerParams(
            dimension_semantics=("parallel", "parallel"), vmem_limit_bytes=VMEM_LIMIT),
        name="nsa_cmp_select",
    )(q4, kct, vc, sel_map)


def _nsa_flash_body(*refs, tq, tk, mode, key_off):
    f32, bf16 = jnp.float32, jnp.bfloat16
    if mode == "select":
        q_ref, kt_ref, v_ref, sel_ref, o_ref, m_scr, l_scr, acc_scr = refs
    else:
        q_ref, kt_ref, v_ref, o_ref, m_scr, l_scr, acc_scr = refs
    i, j = pl.program_id(1), pl.program_id(2)
    nj = pl.num_programs(2)

    @pl.when(j == 0)
    def _():
        m_scr[...] = jnp.full_like(m_scr, NEG)
        l_scr[...] = jnp.zeros_like(l_scr)
        acc_scr[...] = jnp.zeros_like(acc_scr)

    qpos = i * tq + lax.broadcasted_iota(jnp.int32, (tq, 1), 0)
    lane = lax.broadcasted_iota(jnp.int32, (1, tk), 1)
    if mode == "select":
        k0 = j * tk
        active = k0 <= i * tq + (tq - 1)
    else:
        k0 = (i * tq // tk + j) * tk - key_off
        active = j >= 0
    kpos = k0 + lane

    @pl.when(active)
    def _():
        if mode == "select":
            nsp = sel_ref.shape[2]
            c = lax.broadcasted_iota(jnp.int32, (nsp, tk), 0)
            l2 = lax.broadcasted_iota(jnp.int32, (nsp, tk), 1)
            expand = (c == (k0 + l2) // SEL_BLOCK).astype(bf16)
            picked = jnp.dot(sel_ref[0], expand, preferred_element_type=f32)
            allowed = (picked > 0.5) & (kpos <= qpos)
        else:
            allowed = (kpos <= qpos) & (kpos > qpos - WINDOW) & (kpos >= 0)
        for r in range(NSA_REP):
            s = jnp.dot(q_ref[0, r], kt_ref[0], preferred_element_type=f32)
            s = jnp.where(allowed, s, NEG)
            m_old = m_scr[r]
            m_new = jnp.maximum(m_old, jnp.max(s, axis=-1, keepdims=True))
            alpha = jnp.exp(m_old - m_new)
            p = jnp.where(allowed, jnp.exp(s - m_new), 0.0)
            l_scr[r] = alpha * l_scr[r] + jnp.sum(p, axis=-1, keepdims=True)
            acc_scr[r] = alpha * acc_scr[r] + jnp.dot(p.astype(bf16), v_ref[0], preferred_element_type=f32)
            m_scr[r] = m_new

    @pl.when(j == nj - 1)
    def _():
        for r in range(NSA_REP):
            o_ref[0, r] = acc_scr[r] / l_scr[r]


def nsa_flash(q4, kt, v, sel=None, *, mode, tq, tk, key_off=0):
    BG, _, T, _ = q4.shape
    if mode == "select":
        nj = T // tk
        kidx = lambda b, i, j: jnp.minimum(j, (i * tq + tq - 1) // tk)
    else:
        nj = (key_off + tq) // tk
        kidx = lambda b, i, j: i * tq // tk + j
    in_specs = [pl.BlockSpec((1, NSA_REP, tq, HEAD_DIM), lambda b, i, j: (b, 0, i, 0)),
                pl.BlockSpec((1, HEAD_DIM, tk), lambda b, i, j: (b, 0, kidx(b, i, j))),
                pl.BlockSpec((1, tk, HEAD_DIM), lambda b, i, j: (b, kidx(b, i, j), 0))]
    args = [q4, kt, v]
    if mode == "select":
        in_specs.append(pl.BlockSpec((1, tq, sel.shape[2]), lambda b, i, j: (b, i, 0)))
        args.append(sel)
    return pl.pallas_call(
        functools.partial(_nsa_flash_body, tq=tq, tk=tk, mode=mode, key_off=key_off),
        grid=(BG, T // tq, nj),
        in_specs=in_specs,
        out_specs=pl.BlockSpec((1, NSA_REP, tq, HEAD_DIM), lambda b, i, j: (b, 0, i, 0)),
        out_shape=jax.ShapeDtypeStruct((BG, NSA_REP, T, HEAD_DIM), jnp.float32),
        scratch_shapes=[pltpu.VMEM((NSA_REP, tq, 1), jnp.float32),
                        pltpu.VMEM((NSA_REP, tq, 1), jnp.float32),
                        pltpu.VMEM((NSA_REP, tq, HEAD_DIM), jnp.float32)],
        compiler_params=pltpu.CompilerParams(
            dimension_semantics=("parallel", "parallel", "arbitrary"), vmem_limit_bytes=VMEM_LIMIT),
        name="nsa_flash_" + mode,
    )(*args)


def nsa_prompt(q, gates, rows, win, cw1, cw2, cpe):
    bf16 = jnp.bfloat16
    B, T = q.shape[:2]
    G = NSA_KV
    kc = compress(rows[:, :, 0], cw1[0], cw2[0], cpe[0])
    vc = compress(rows[:, :, 1], cw1[1], cw2[1], cpe[1])
    n_cmp = kc.shape[1]
    n_sel = T // SEL_BLOCK
    ncp = -(-n_cmp // LANE) * LANE
    nsp = -(-n_sel // LANE) * LANE
    n = jnp.arange(ncp)[:, None]
    j = jnp.arange(nsp)[None, :]
    sel_map = ((n * CMP_STRIDE <= j * SEL_BLOCK + SEL_BLOCK - 1) & (n * CMP_STRIDE + CMP_BLOCK - 1 >= j * SEL_BLOCK)
               & (n < n_cmp) & (j < n_sel)).astype(bf16)
    q4 = (q * (HEAD_DIM ** -0.5)).reshape(B, T, G, NSA_REP, HEAD_DIM).transpose(0, 2, 3, 1, 4)
    q4 = q4.reshape(B * G, NSA_REP, T, HEAD_DIM).astype(bf16)

    def keys_t(x, front=0, back=0):
        x = jnp.pad(x, ((0, 0), (front, back), (0, 0), (0, 0)))
        return x.transpose(0, 2, 3, 1).reshape(B * G, HEAD_DIM, -1).astype(bf16)

    def vals(x, front=0, back=0):
        x = jnp.pad(x, ((0, 0), (front, back), (0, 0), (0, 0)))
        return x.transpose(0, 2, 1, 3).reshape(B * G, -1, HEAD_DIM).astype(bf16)

    o_cmp, sel = nsa_cmp_select(q4, keys_t(kc, 0, ncp - n_cmp), vals(vc, 0, ncp - n_cmp), sel_map, n_cmp, n_sel)
    o_slc = nsa_flash(q4, keys_t(rows[:, :, 2]), vals(rows[:, :, 3]), sel, mode="select", tq=128, tk=512)
    o_win = nsa_flash(q4, keys_t(win[:, :, 0], WINDOW), vals(win[:, :, 1], WINDOW), mode="window",
                      tq=256, tk=256, key_off=WINDOW)
    g = gates.reshape(B, T, G, NSA_REP, 3)
    un = lambda o: o.reshape(B, G, NSA_REP, T, HEAD_DIM).transpose(0, 3, 1, 2, 4)
    o = g[..., 0:1] * un(o_cmp) + g[..., 1:2] * un(o_slc) + g[..., 2:3] * un(o_win)
    return o.reshape(B, T, NSA_Q)


NSA_TQ = 128
NSA_TK = 512
NSA_TKW = 128


def _nsa_attn_body(qt_ref, kc_ref, vct_ref, map_ref, ks_ref, vst_ref, kw_ref, vwt_ref, g_ref, o_ref, thr_scr,
                   *, n_cmp, n_sel):
    f32, bf16 = jnp.float32, jnp.bfloat16
    tq, tk, tkw = NSA_TQ, NSA_TK, NSA_TKW
    nl = NSA_REP * tq
    i = pl.program_id(1)
    qt = qt_ref[0, 0]
    qpos = i * tq + lax.broadcasted_iota(jnp.int32, (1, nl), 1) % tq

    ncp = kc_ref.shape[1]
    n = lax.broadcasted_iota(jnp.int32, (ncp, 1), 0)
    ok = (n * CMP_STRIDE + (CMP_BLOCK - 1) <= qpos) & (n < n_cmp)
    s = jnp.dot(kc_ref[0], qt, preferred_element_type=f32)
    s = jnp.where(ok, s, NEG)
    m = jnp.max(s, axis=0, keepdims=True)
    e = jnp.where(ok, jnp.exp(s - m), 0.0)
    l = jnp.sum(e, axis=0, keepdims=True)
    p = e * jnp.where(l > 0.0, 1.0 / l, 0.0)
    o_cmp = jnp.dot(vct_ref[0], p.astype(bf16), preferred_element_type=f32)

    psum = p[:, 0:tq]
    for r in range(1, NSA_REP):
        psum = psum + p[:, r * tq:(r + 1) * tq]
    hi = psum.astype(bf16)
    lo = (psum - hi.astype(f32)).astype(bf16)
    imp = (jnp.dot(map_ref[...], hi, preferred_element_type=f32)
           + jnp.dot(map_ref[...], lo, preferred_element_type=f32))
    nsp = map_ref.shape[0]
    qp = qpos[:, 0:tq]
    blk = lax.broadcasted_iota(jnp.int32, (nsp, 1), 0)
    cur = qp // SEL_BLOCK
    valid = (blk * SEL_BLOCK <= qp) & (blk < n_sel)
    forced = (blk == 0) | (blk == cur) | (blk == cur - 1)
    x = jnp.where(valid, jnp.where(forced, 1e30, imp), NEG)
    x = jnp.where(blk < n_sel, x, -3e38)
    blkf = blk.astype(f32)
    sel = jnp.zeros((nsp, tq), f32)
    for _ in range(min(N_SELECT, n_sel)):
        mx = jnp.max(x, axis=0, keepdims=True)
        first = jnp.min(jnp.where(x == mx, blkf, float(nsp)), axis=0, keepdims=True)
        hit = blkf == first
        sel = jnp.where(hit, 1.0, sel)
        x = jnp.where(hit, -3e38, x)
    thr = jnp.where(valid & (sel > 0.5), qp, -1)
    thr_scr[...] = jnp.concatenate([thr] * NSA_REP, axis=1)

    def flash(s, allowed, vt, carry):
        m_old, l_old, acc = carry
        m_new = jnp.maximum(m_old, jnp.max(s, axis=0, keepdims=True))
        alpha = jnp.exp(m_old - m_new)
        p = jnp.exp(s - m_new)
        if allowed is not None:
            p = jnp.where(allowed, p, 0.0)
        l_new = alpha * l_old + jnp.sum(p, axis=0, keepdims=True)
        acc = alpha * acc + jnp.dot(vt, p.astype(bf16), preferred_element_type=f32)
        return m_new, l_new, acc

    init = (jnp.full((1, nl), NEG, f32), jnp.zeros((1, nl), f32), jnp.zeros((HEAD_DIM, nl), f32))

    krow = lax.broadcasted_iota(jnp.int32, (SEL_BLOCK, 1), 0)

    def sel_step(j, carry):
        s = jnp.dot(ks_ref[0, j], qt, preferred_element_type=f32)
        th = thr_scr[pl.ds(pl.multiple_of(j * (tk // SEL_BLOCK), 8), tk // SEL_BLOCK), :]
        s = jnp.concatenate(
            [jnp.where((j * tk + b * SEL_BLOCK + krow) <= th[b:b + 1], s[b * SEL_BLOCK:(b + 1) * SEL_BLOCK], NEG)
             for b in range(tk // SEL_BLOCK)], axis=0)
        return flash(s, None, vst_ref[0, j], carry)

    _, l_s, acc_s = lax.fori_loop(0, (i * tq + tq - 1) // tk + 1, sel_step, init)
    o_slc = acc_s / l_s

    kwrow = lax.broadcasted_iota(jnp.int32, (tkw, 1), 0)

    def win_step(j, carry):
        s = jnp.dot(kw_ref[0, j], qt, preferred_element_type=f32)
        kpos = j * tkw + kwrow
        allowed = (kpos <= qpos) & (kpos > qpos - WINDOW)
        return flash(jnp.where(allowed, s, NEG), allowed, vwt_ref[0, j], carry)

    j_lo = jnp.maximum((i * tq - (WINDOW - 1)) // tkw, 0)
    _, l_w, acc_w = lax.fori_loop(j_lo, (i * tq + tq - 1) // tkw + 1, win_step, init)
    o_win = acc_w / l_w

    g = g_ref[0, 0]
    o_ref[0, 0] = g[0:1] * o_cmp + g[1:2] * o_slc + g[2:3] * o_win


def nsa_attention(q, gates, kc, vc, ksl, vsl, kwin, vwin):
    f32, bf16 = jnp.float32, jnp.bfloat16
    B, T = q.shape[:2]
    G, R, D = NSA_KV, NSA_REP, HEAD_DIM
    tq, tk, tkw = NSA_TQ, NSA_TK, NSA_TKW
    nq, nl = T // tq, R * tq
    n_cmp, n_sel = kc.shape[1], T // SEL_BLOCK
    ncp = -(-n_cmp // LANE) * LANE
    nsp = -(-n_sel // LANE) * LANE
    n = jnp.arange(ncp)[None, :]
    jb = jnp.arange(nsp)[:, None]
    map_t = ((n * CMP_STRIDE <= jb * SEL_BLOCK + SEL_BLOCK - 1) & (n * CMP_STRIDE + CMP_BLOCK - 1 >= jb * SEL_BLOCK)
             & (n < n_cmp) & (jb < n_sel)).astype(bf16)
    qt = (q * (D ** -0.5)).reshape(B, nq, tq, G, R, D).transpose(0, 3, 1, 5, 4, 2).reshape(B * G, nq, D, nl).astype(bf16)
    gt = gates.reshape(B, nq, tq, G, R, 3).transpose(0, 3, 1, 5, 4, 2).reshape(B * G, nq, 3, nl)
    gt = jnp.pad(gt, ((0, 0), (0, 0), (0, 5), (0, 0)))
    rows = lambda x, t: x.transpose(0, 2, 1, 3).reshape(B * G, -1, t, D).astype(bf16)
    cols = lambda x, t: x.reshape(B, -1, t, G, D).transpose(0, 3, 1, 4, 2).reshape(B * G, -1, D, t).astype(bf16)
    kcp = jnp.pad(kc, ((0, 0), (0, ncp - n_cmp), (0, 0), (0, 0)))
    vcp = jnp.pad(vc, ((0, 0), (0, ncp - n_cmp), (0, 0), (0, 0)))
    kc_r = rows(kcp, ncp)[:, 0]
    vc_c = cols(vcp, ncp)[:, 0]
    full = lambda shape: pl.BlockSpec((1,) + shape, lambda b, i: (b,) + (0,) * len(shape))
    per_q = lambda shape: pl.BlockSpec((1, 1) + shape, lambda b, i: (b, i) + (0,) * len(shape))
    out = pl.pallas_call(
        functools.partial(_nsa_attn_body, n_cmp=n_cmp, n_sel=n_sel),
        grid=(B * G, nq),
        in_specs=[per_q((D, nl)), full((ncp, D)), full((D, ncp)),
                  pl.BlockSpec((nsp, ncp), lambda b, i: (0, 0)),
                  full((T // tk, tk, D)), full((T // tk, D, tk)),
                  full((T // tkw, tkw, D)), full((T // tkw, D, tkw)),
                  per_q((8, nl))],
        out_specs=per_q((D, nl)),
        out_shape=jax.ShapeDtypeStruct((B * G, nq, D, nl), f32),
        scratch_shapes=[pltpu.VMEM((nsp, nl), jnp.int32)],
        compiler_params=pltpu.CompilerParams(
            dimension_semantics=("parallel", "parallel"), vmem_limit_bytes=VMEM_LIMIT),
        name="nsa_attention",
    )(qt, kc_r, vc_c, map_t, rows(ksl, tk), cols(vsl, tk), rows(kwin, tkw), cols(vwin, tkw), gt)
    return out.reshape(B, G, nq, D, R, tq).transpose(0, 2, 5, 1, 4, 3).reshape(B, T, NSA_Q)


def nsa_prompt(q, gates, rows, win, cw1, cw2, cpe):
    kc = compress(rows[:, :, 0], cw1[0], cw2[0], cpe[0])
    vc = compress(rows[:, :, 1], cw1[1], cw2[1], cpe[1])
    return nsa_attention(q, gates, kc, vc, rows[:, :, 2], rows[:, :, 3], win[:, :, 0], win[:, :, 1])


RW_PAIRS = RWKV_HEADS // 2
RW_CHUNK = 64


def _rwkv_scan_body(r_ref, w_ref, k_ref, kk_ref, b_ref, vt_ref, s0_ref, oh_ref,
                    y_ref, st_ref, vhi_scr, vmid_scr, *, nb, tc):
    f32, bf16 = jnp.float32, jnp.bfloat16
    c = pl.program_id(1)

    @pl.when(c == 0)
    def _():
        st_ref[...] = s0_ref[...]

    row = lax.broadcasted_iota(jnp.int32, (2 * LANE, LANE), 0)
    col = lax.broadcasted_iota(jnp.int32, (2 * LANE, LANE), 1)
    ones2 = (((row // HEAD_DIM) % 2) == (col // HEAD_DIM)).astype(bf16)
    ones1 = ones2[:LANE]
    lane_t = lax.broadcasted_iota(jnp.int32, (HEAD_DIM, LANE), 1) % HEAD_DIM

    vt = vt_ref[...].reshape(nb * RW_PAIRS, HEAD_DIM, LANE)
    vhi = vt.astype(bf16)
    vhi_scr[...] = vhi
    vmid_scr[...] = (vt - vhi.astype(f32)).astype(bf16)
    y_ref[...] = jnp.zeros_like(y_ref)

    def group(t8, carry):
        t0 = pl.multiple_of(t8 * 8, 8)
        for j in range(8):
            t = t0 + j
            row = lambda ref, b, sl: ref[b, pl.ds(t0, 8), sl][j:j + 1]
            oh = oh_ref[t]
            m = lane_t == t
            pairs = [(b, p) for b in range(nb) for p in range(RW_PAIRS)]
            vl = jnp.concatenate([jnp.concatenate([vhi_scr[i] * oh, vmid_scr[i] * oh], axis=1)
                                  for i in range(len(pairs))], axis=0)
            vb_all = jnp.dot(vl, ones2, preferred_element_type=f32)
            pieces = []
            for b, p in pairs:
                pk = st_ref[b, p] * row(kk_ref, b, slice(LANE * p, LANE * (p + 1)))
                hi = pk.astype(bf16)
                pieces.append(jnp.concatenate([hi, (pk - hi.astype(f32)).astype(bf16)], axis=1))
            sa_all = jnp.dot(jnp.concatenate(pieces, axis=0), ones2, preferred_element_type=f32)
            py = []
            for i, (b, p) in enumerate(pairs):
                sl = slice(LANE * p, LANE * (p + 1))
                rows = slice(HEAD_DIM * i, HEAD_DIM * (i + 1))
                s_new = (st_ref[b, p] * row(w_ref, b, sl) - sa_all[rows] * row(b_ref, b, sl)
                         + vb_all[rows] * row(k_ref, b, sl))
                st_ref[b, p] = s_new
                py.append((s_new * row(r_ref, b, sl)).astype(bf16))
            y_all = jnp.dot(jnp.concatenate(py, axis=0), ones1, preferred_element_type=f32)
            for i, (b, p) in enumerate(pairs):
                y_ref[b, 0, p] = jnp.where(m, y_all[HEAD_DIM * i:HEAD_DIM * (i + 1)], y_ref[b, 0, p])
        return carry

    lax.fori_loop(0, tc // 8, group, 0)


def rwkv_scan(r, w, k, v, kk, b, S0):
    f32 = jnp.float32
    B, T, W = r.shape
    tc = RW_CHUNK if T % RW_CHUNK == 0 else T
    assert tc <= RW_CHUNK and T % tc == 0 and tc % 8 == 0 and B % 2 == 0
    nc, nb = T // tc, 2
    vt = v.reshape(B, nc, tc, RW_PAIRS, 2, HEAD_DIM).transpose(0, 1, 3, 5, 4, 2)
    vt = jnp.pad(vt, ((0, 0),) * 5 + ((0, RW_CHUNK - tc),)).reshape(B, nc, RW_PAIRS, HEAD_DIM, LANE)
    s0 = S0.astype(f32).reshape(B, RW_PAIRS, 2, HEAD_DIM, HEAD_DIM).transpose(0, 1, 3, 2, 4)
    s0 = s0.reshape(B, RW_PAIRS, HEAD_DIM, LANE)
    oh = jnp.arange(LANE)[None, None, :] % HEAD_DIM == jnp.arange(RW_CHUNK)[:, None, None]
    oh = jnp.broadcast_to(oh, (RW_CHUNK, HEAD_DIM, LANE)).astype(jnp.bfloat16)
    tok = pl.BlockSpec((nb, tc, W), lambda i, c: (i, c, 0))
    chk = pl.BlockSpec((nb, 1, RW_PAIRS, HEAD_DIM, LANE), lambda i, c: (i, c, 0, 0, 0))
    stt = pl.BlockSpec((nb, RW_PAIRS, HEAD_DIM, LANE), lambda i, c: (i, 0, 0, 0))
    y, st = pl.pallas_call(
        functools.partial(_rwkv_scan_body, nb=nb, tc=tc),
        grid=(B // nb, nc),
        in_specs=[tok, tok, tok, tok, tok, chk, stt,
                  pl.BlockSpec((RW_CHUNK, HEAD_DIM, LANE), lambda i, c: (0, 0, 0))],
        out_specs=[chk, stt],
        out_shape=[jax.ShapeDtypeStruct((B, nc, RW_PAIRS, HEAD_DIM, LANE), f32),
                   jax.ShapeDtypeStruct((B, RW_PAIRS, HEAD_DIM, LANE), f32)],
        scratch_shapes=[pltpu.VMEM((nb * RW_PAIRS, HEAD_DIM, LANE), jnp.bfloat16),
                        pltpu.VMEM((nb * RW_PAIRS, HEAD_DIM, LANE), jnp.bfloat16)],
        compiler_params=pltpu.CompilerParams(
            dimension_semantics=("parallel", "arbitrary"), vmem_limit_bytes=VMEM_LIMIT),
        name="rwkv_scan",
    )(r, w, k, kk, b, vt, s0, oh)
    y = y.reshape(B, nc, RW_PAIRS, HEAD_DIM, 2, RW_CHUNK)[..., :tc]
    y = y.transpose(0, 1, 5, 2, 4, 3).reshape(B, T, W)
    st = st.reshape(B, RW_PAIRS, HEAD_DIM, 2, HEAD_DIM).transpose(0, 1, 3, 2, 4)
    return y, st.reshape(B, RWKV_HEADS, HEAD_DIM, HEAD_DIM)


def rwkv_mix(p, prev, S0, mu, w0, wB, a0, aB, gB, k_k, k_a, r_k, ln_w, ln_b):
    f32 = jnp.float32
    B, T = p.shape[:2]
    W = RWKV_WIDTH
    p = p.astype(f32)
    p_prev = jnp.concatenate([prev.astype(f32)[:, None], p[:, :-1]], axis=1)
    ps = p + mu * (p_prev - p)
    r, k, v = ps[..., :W], ps[..., W:2 * W], ps[..., 2 * W:3 * W]
    o = 3 * W
    xw, xa, xg = ps[..., o:o + LORA_W], ps[..., o + LORA_W:o + LORA_W + LORA_A], ps[..., o + LORA_W + LORA_A:]
    z = w0 + jnp.tanh(xw) @ wB
    w = jnp.exp(-jnp.exp(-jax.nn.softplus(-z) - 0.5))
    a = jax.nn.sigmoid(a0 + xa @ aB)
    g = jax.nn.sigmoid(xg) @ gB
    heads = lambda t: t.reshape(B, T, RWKV_HEADS, HEAD_DIM)
    kk = heads(k * k_k)
    kk = kk * lax.rsqrt(jnp.maximum(jnp.sum(kk * kk, axis=-1, keepdims=True), 1e-24))
    k = k * (1.0 + (a - 1.0) * k_a)
    kk = kk.reshape(B, T, W)
    y, S = rwkv_scan(r, w, k, v, kk, kk * a, S0)
    y, k = heads(y), heads(k)
    r, v = heads(r), heads(v)
    mean = jnp.mean(y, axis=-1, keepdims=True)
    var = jnp.mean(jnp.square(y - mean), axis=-1, keepdims=True)
    y = ((y - mean) * lax.rsqrt(var + RWKV_GN_EPS)).reshape(B, T, W) * ln_w + ln_b
    y = y + (jnp.sum(r * k * r_k, axis=-1, keepdims=True) * v).reshape(B, T, W)
    return y * g, S, p[:, -1]


def gla_chunked(q, k, v, logg, S0):
    B, T, H, DK = q.shape
    DV = v.shape[-1]
    C = GLA_CHUNK if T % GLA_CHUNK == 0 else T
    N = T // C
    to_chunks = lambda x: x.reshape(B, N, C, H, x.shape[-1]).transpose(1, 0, 3, 2, 4)
    causal = jnp.tril(jnp.ones((C, C), dtype=bool))[:, :, None]

    def step(S, inp):
        qc, kc, vc, gc = inp
        b = jnp.cumsum(gc, axis=2)
        diff = b[:, :, :, None, :] - b[:, :, None, :, :]
        decay = jnp.exp(jnp.where(causal, diff, -jnp.inf))
        A = jnp.einsum('bhtd,bhsd,bhtsd->bhts', qc, kc, decay)
        o = jnp.einsum('bhts,bhsv->bhtv', A, vc) + jnp.einsum('bhtk,bhkv->bhtv', qc * jnp.exp(b), S)
        b_last = b[:, :, -1:, :]
        S = jnp.exp(b_last[:, :, 0, :])[..., None] * S + jnp.einsum('bhsk,bhsv->bhkv', kc * jnp.exp(b_last - b), vc)
        return S, o

    S, o = lax.scan(step, S0, (to_chunks(q), to_chunks(k), to_chunks(v), to_chunks(logg)))
    return o.transpose(1, 0, 3, 2, 4).reshape(B, T, H, DV), S


GLA_SUB = 16


def _gla_body(q_ref, k_ref, v_ref, g_ref, r_ref, gn_ref, o_ref, st_ref):
    f32, bf16 = jnp.float32, jnp.bfloat16
    C, SUB = GLA_CHUNK, GLA_SUB
    nsub = C // SUB
    c = pl.program_id(1)

    @pl.when(c == 0)
    def _():
        st_ref[...] = jnp.zeros_like(st_ref)

    ti = lax.broadcasted_iota(jnp.int32, (C, C), 0)
    si = lax.broadcasted_iota(jnp.int32, (C, C), 1)
    tri = (si <= ti).astype(bf16)
    sub_t = lax.broadcasted_iota(jnp.int32, (SUB, 1), 0)
    sub_l = lax.broadcasted_iota(jnp.int32, (SUB, SUB), 1)
    for h in range(GLA_HEADS):
        kq = slice(h * GLA_DK, (h + 1) * GLA_DK)
        vv = slice(h * GLA_DV, (h + 1) * GLA_DV)
        q = q_ref[0, :, kq] * (GLA_DK ** -0.5)
        k = k_ref[0, :, kq]
        v = v_ref[0, :, vv].astype(bf16)
        g = g_ref[0, :, kq]
        g1 = g.astype(bf16)
        g2 = (g - g1.astype(f32)).astype(bf16)
        g3 = (g - g1.astype(f32) - g2.astype(f32)).astype(bf16)
        b = (jnp.dot(tri, g1, preferred_element_type=f32) + jnp.dot(tri, g2, preferred_element_type=f32)
             + jnp.dot(tri, g3, preferred_element_type=f32))
        b_last = b[C - 1:C]
        a_rows = []
        for I in range(nsub):
            rows = slice(I * SUB, (I + 1) * SUB)
            beta = b[I * SUB - 1:I * SUB] if I > 0 else jnp.zeros((1, GLA_DK), f32)
            qb, kb, bb = q[rows], k[rows], b[rows]
            a_diag = jnp.zeros((SUB, SUB), f32)
            for s in range(SUB):
                d = jnp.where(sub_t >= s, bb - bb[s:s + 1], -jnp.inf)
                col = jnp.sum(qb * kb[s:s + 1] * jnp.exp(d), axis=1, keepdims=True)
                a_diag = jnp.where(sub_l == s, col, a_diag)
            blocks = []
            if I > 0:
                qe = (qb * jnp.exp(bb - beta)).astype(bf16)
                ke = (k[:I * SUB] * jnp.exp(beta - b[:I * SUB])).astype(bf16)
                blocks.append(lax.dot_general(qe, ke, (((1,), (1,)), ((), ())), preferred_element_type=f32))
            blocks.append(a_diag)
            if I < nsub - 1:
                blocks.append(jnp.zeros((SUB, C - (I + 1) * SUB), f32))
            a_rows.append(jnp.concatenate(blocks, axis=1))
        a = jnp.concatenate(a_rows, axis=0).astype(bf16)
        st = st_ref[0, h]
        qd = (q * jnp.exp(b)).astype(bf16)
        o = (jnp.dot(a, v, preferred_element_type=f32)
             + lax.dot_general(qd, st.astype(bf16), (((1,), (1,)), ((), ())), preferred_element_type=f32))
        kd = (k * jnp.exp(b_last - b)).astype(bf16)
        st_ref[0, h] = (st * jnp.exp(b_last)
                        + lax.dot_general(v, kd, (((0,), (0,)), ((), ())), preferred_element_type=f32))
        y = o * lax.rsqrt(jnp.mean(o * o, axis=1, keepdims=True) + 1e-6) * gn_ref[...]
        rr = r_ref[0, :, vv]
        o_ref[0, :, vv] = y * (rr * jax.nn.sigmoid(rr))


def gla_mix_prompt(qk, v, gd, r, gate_up, gate_b, gn):
    f32 = jnp.float32
    B, T = qk.shape[:2]
    KW, VW, C = GLA_KW, GLA_VW, GLA_CHUNK
    logg = jax.nn.log_sigmoid(gd[..., :GLA_RANK] @ gate_up + gate_b) / GLA_TAU
    tokv = pl.BlockSpec((1, C, VW), lambda b, c: (b, c, 0))
    out, st = pl.pallas_call(
        _gla_body,
        grid=(B, T // C),
        in_specs=[pl.BlockSpec((1, C, KW), lambda b, c: (b, c, 0)),
                  pl.BlockSpec((1, C, KW), lambda b, c: (b, c, 1)),
                  tokv, pl.BlockSpec((1, C, KW), lambda b, c: (b, c, 0)), tokv,
                  pl.BlockSpec((1, GLA_DV), lambda b, c: (0, 0))],
        out_specs=[tokv, pl.BlockSpec((1, GLA_HEADS, GLA_DV, GLA_DK), lambda b, c: (b, 0, 0, 0))],
        out_shape=[jax.ShapeDtypeStruct((B, T, VW), f32),
                   jax.ShapeDtypeStruct((B, GLA_HEADS, GLA_DV, GLA_DK), f32)],
        compiler_params=pltpu.CompilerParams(
            dimension_semantics=("parallel", "arbitrary"), vmem_limit_bytes=VMEM_LIMIT),
        name="gla_chunk",
    )(qk, qk, v, logg, r, gn.reshape(1, GLA_DV))
    return out, st.transpose(0, 1, 3, 2)


def gla_mix(qk, v, gd, r, S0, gate_up, gate_b, gn):
    f32 = jnp.float32
    B, T = qk.shape[:2]
    q = qk[..., :GLA_KW].reshape(B, T, GLA_HEADS, GLA_DK) * (GLA_DK ** -0.5)
    k = qk[..., GLA_KW:].reshape(B, T, GLA_HEADS, GLA_DK)
    v = v.reshape(B, T, GLA_HEADS, GLA_DV)
    gd = gd[..., :GLA_RANK]
    logg = (jax.nn.log_sigmoid(gd @ gate_up + gate_b) / GLA_TAU).reshape(B, T, GLA_HEADS, GLA_DK)
    out, S = gla_chunked(q, k, v, logg, S0.astype(f32))
    out = rmsnorm(out, gn).reshape(B, T, GLA_VW) * jax.nn.silu(r)
    return out, S


def kernel(x_prompt, x_sample, cache_nsa_kv, cache_nsa_win, state_rwkv, state_rwkv_shift, state_gla, page_table, norm_mix, norm_ffn, w_in_even, nsa_qnorm, nsa_knorm, cmp_w1, cmp_w2, cmp_pe, rwkv_mu, rwkv_w0, rwkv_wB, rwkv_a0, rwkv_aB, rwkv_gB, rwkv_kk, rwkv_ka, rwkv_rk, rwkv_ln_w, rwkv_ln_b, w_out_even, w_in_odd, gla_gate_up, gla_gate_b, gla_norm, w_out_odd, ffn_gate, ffn_up, ffn_down):
    dt = x_prompt.dtype
    bf16 = jnp.bfloat16
    B, T = x_prompt.shape[:2]
    DB, S = x_sample.shape[:2]
    depth = norm_mix.shape[0]
    P = page_table.shape[1] * PAGE_SIZE
    WB = cache_nsa_win.shape[2]
    WP = min(WINDOW, T)
    pos_p = jnp.arange(T, dtype=jnp.int32)
    pos_s = P + jnp.arange(S, dtype=jnp.int32)
    win_pos_s = P - WB + jnp.arange(WB + S, dtype=jnp.int32)
    nsa_pad = -(-NSA_IN // LANE) * LANE
    y_p, y_s = x_prompt, x_sample
    kv_p, kv_s, win_p, win_s, rw_p, rw_s, sh_p, sh_s, gl_p, gl_s = ([] for _ in range(10))
    for layer in range(depth):
        li = layer // 2
        if layer % 2 == 0:
            w_in = jnp.concatenate([jnp.pad(w_in_even[li][:, :NSA_IN], ((0, 0), (0, nsa_pad - NSA_IN))),
                                    w_in_even[li][:, NSA_IN:]], axis=1).astype(bf16)
            even_splits = ((0, nsa_pad), (nsa_pad, nsa_pad + RWKV_IN))
            w_out = w_out_even[li].astype(bf16)
            hn_p, hr_p = norm_proj(y_p, norm_mix[layer], w_in, even_splits)
            hn_s, hr_s = norm_proj(y_s, norm_mix[layer], w_in, even_splits)
            q, g, rows, win = nsa_project(hn_p, pos_p, nsa_qnorm[li], nsa_knorm[li])
            o_nsa_p = nsa_prompt(q, g, rows, win, cmp_w1[li], cmp_w2[li], cmp_pe[li])
            kv_p.append(rows.astype(dt))
            win_p.append(win[:, T - WP:].astype(dt))
            q, g, rows, win = nsa_project(hn_s, pos_s, nsa_qnorm[li], nsa_knorm[li])
            past = cache_nsa_kv[li][page_table].reshape(DB, P, 4, NSA_KV, HEAD_DIM)
            rows_all = jnp.concatenate([past, rows.astype(past.dtype)], axis=1)
            win_all = jnp.concatenate([cache_nsa_win[li], win.astype(cache_nsa_win.dtype)], axis=1)
            kc, vc, cmp_end, sel_map, ksb, vsb = nsa_keys(rows_all, cmp_w1[li], cmp_w2[li], cmp_pe[li])
            o_nsa_s = nsa_block(q, g, pos_s, kc, vc, cmp_end, sel_map, ksb, vsb, win_all, win_pos_s)
            kv_s.append(rows.astype(dt))
            win_s.append(win_all[:, S:].astype(dt))
            rw_par = (rwkv_mu[li], rwkv_w0[li], rwkv_wB[li], rwkv_a0[li], rwkv_aB[li], rwkv_gB[li],
                      rwkv_kk[li], rwkv_ka[li], rwkv_rk[li], rwkv_ln_w[li], rwkv_ln_b[li])
            o_rw_p, st, last = rwkv_mix(hr_p, jnp.zeros((B, RWKV_IN), dt),
                                        jnp.zeros((B, RWKV_HEADS, HEAD_DIM, HEAD_DIM), dt), *rw_par)
            rw_p.append(st.astype(dt))
            sh_p.append(last.astype(dt))
            o_rw_s, st, last = rwkv_mix(hr_s, state_rwkv_shift[li], state_rwkv[li], *rw_par)
            rw_s.append(st.astype(dt))
            sh_s.append(last.astype(dt))
            w_outs = [w_out[:NSA_Q], w_out[NSA_Q:]]
            y_p = proj_residual([o_nsa_p.astype(dt), o_rw_p.astype(dt)], w_outs, y_p)
            y_s = proj_residual([o_nsa_s.astype(dt), o_rw_s.astype(dt)], w_outs, y_s)
        else:
            wo = w_in_odd[li]
            o2 = 2 * GLA_KW + GLA_VW
            w_in = jnp.concatenate([wo[:, :o2], wo[:, o2 + GLA_RANK:],
                                    jnp.pad(wo[:, o2:o2 + GLA_RANK], ((0, 0), (0, LANE - GLA_RANK)))],
                                   axis=1).astype(bf16)
            odd_splits = ((0, 2 * GLA_KW), (2 * GLA_KW, o2), (o2, o2 + GLA_VW), (o2 + GLA_VW, o2 + GLA_VW + LANE))
            w_out = w_out_odd[li].astype(bf16)
            qk_p, v_p, r_p, gd_p = norm_proj(y_p, norm_mix[layer], w_in, odd_splits)
            qk_s, v_s, r_s, gd_s = norm_proj(y_s, norm_mix[layer], w_in, odd_splits)
            o_p, st = gla_mix_prompt(qk_p, v_p, gd_p, r_p, gla_gate_up[li], gla_gate_b[li], gla_norm[li])
            gl_p.append(st.astype(dt))
            o_s, st = gla_mix(qk_s, v_s, gd_s, r_s, state_gla[li], gla_gate_up[li], gla_gate_b[li], gla_norm[li])
            gl_s.append(st.astype(dt))
            y_p = proj_residual([o_p.astype(dt)], [w_out], y_p)
            y_s = proj_residual([o_s.astype(dt)], [w_out], y_s)
        wg, wu, wd = ffn_gate[layer].astype(bf16), ffn_up[layer].astype(bf16), ffn_down[layer].astype(bf16)
        y_p = ffn(y_p, norm_ffn[layer], wg, wu, wd)
        y_s = ffn(y_s, norm_ffn[layer], wg, wu, wd)
    return (y_p, y_s, jnp.stack(kv_p), jnp.stack(kv_s), jnp.stack(win_p), jnp.stack(win_s),
            jnp.stack(rw_p), jnp.stack(rw_s), jnp.stack(sh_p), jnp.stack(sh_s),
            jnp.stack(gl_p), jnp.stack(gl_s))
```

```python
import functools
import math

import jax
import jax.numpy as jnp
from jax import lax
from jax.experimental import pallas as pl
from jax.experimental.pallas import tpu as pltpu

D_MODEL = 1024
PAGE_SIZE = 128
HEAD_DIM = 64
NSA_HEADS = 8
NSA_KV = 2
NSA_REP = NSA_HEADS // NSA_KV
CMP_STRIDE = 16
CMP_BLOCK = 2 * CMP_STRIDE
CMP_HIDDEN = 2 * HEAD_DIM
SEL_BLOCK = 64
N_SELECT = 16
WINDOW = 512
Q_BLOCK = 128
ROPE_DIM = HEAD_DIM // 4
ROPE_THETA = 500000.0
RWKV_HEADS = 8
RWKV_WIDTH = RWKV_HEADS * HEAD_DIM
LORA_W = 64
LORA_A = 64
LORA_G = 128
RWKV_GN_EPS = 64e-5
GLA_HEADS = 4
GLA_DK = D_MODEL // 2 // GLA_HEADS
GLA_DV = D_MODEL // GLA_HEADS
GLA_RANK = 16
GLA_TAU = 16.0
GLA_CHUNK = 64
D_FF = ((8 * D_MODEL // 3 + 255) // 256) * 256
NSA_Q = NSA_HEADS * HEAD_DIM
NSA_KVW = 3 * 2 * NSA_KV * HEAD_DIM
NSA_IN = NSA_Q + NSA_KVW + 3 * NSA_HEADS
RWKV_IN = 3 * RWKV_WIDTH + LORA_W + LORA_A + LORA_G
EVEN_IN = NSA_IN + RWKV_IN
MIX_WIDTH = NSA_Q + RWKV_WIDTH
GLA_KW = GLA_HEADS * GLA_DK
GLA_VW = GLA_HEADS * GLA_DV
ODD_IN = 2 * GLA_KW + GLA_VW + GLA_RANK + GLA_VW

LANE = 128
VMEM_LIMIT = 48 * 1024 * 1024


ROW_TILE = 512
NORM_EPS = 1e-6


def _row_tile(m):
    return ROW_TILE if m % ROW_TILE == 0 else m


def _norm_bf16(x_ref, g_ref):
    x = x_ref[...]
    y = x * lax.rsqrt(jnp.mean(x * x, axis=-1, keepdims=True) + NORM_EPS)
    return (y * g_ref[...]).astype(jnp.bfloat16)


def _norm_proj_body(x_ref, g_ref, w_ref, *o_refs, splits):
    xn = _norm_bf16(x_ref, g_ref)
    for (a, b), o_ref in zip(splits, o_refs):
        o_ref[...] = jnp.dot(xn, w_ref[:, a:b], preferred_element_type=jnp.float32)


def norm_proj(x, gain, w, splits):
    bsz, t, k = x.shape
    m = bsz * t
    tm = _row_tile(m)
    outs = pl.pallas_call(
        functools.partial(_norm_proj_body, splits=splits),
        grid=(m // tm,),
        in_specs=[pl.BlockSpec((tm, k), lambda i: (i, 0)),
                  pl.BlockSpec((1, k), lambda i: (0, 0)),
                  pl.BlockSpec(w.shape, lambda i: (0, 0))],
        out_specs=[pl.BlockSpec((tm, b - a), lambda i: (i, 0)) for a, b in splits],
        out_shape=[jax.ShapeDtypeStruct((m, b - a), jnp.float32) for a, b in splits],
        compiler_params=pltpu.CompilerParams(dimension_semantics=("parallel",), vmem_limit_bytes=VMEM_LIMIT),
        name="norm_proj",
    )(x.reshape(m, k), gain.reshape(1, k), w)
    return [o.reshape(bsz, t, -1) for o in outs]


def _proj_res_body(*refs, n_in):
    x_refs, w_refs, res_ref, o_ref = refs[:n_in], refs[n_in:2 * n_in], refs[2 * n_in], refs[2 * n_in + 1]
    acc = res_ref[...]
    for x_ref, w_ref in zip(x_refs, w_refs):
        acc = acc + jnp.dot(x_ref[...].astype(jnp.bfloat16), w_ref[...], preferred_element_type=jnp.float32)
    o_ref[...] = acc


def proj_residual(xs, ws, res):
    bsz, t, n = res.shape
    m = bsz * t
    tm = _row_tile(m)
    out = pl.pallas_call(
        functools.partial(_proj_res_body, n_in=len(xs)),
        grid=(m // tm,),
        in_specs=([pl.BlockSpec((tm, x.shape[-1]), lambda i: (i, 0)) for x in xs]
                  + [pl.BlockSpec(w.shape, lambda i: (0, 0)) for w in ws]
                  + [pl.BlockSpec((tm, n), lambda i: (i, 0))]),
        out_specs=pl.BlockSpec((tm, n), lambda i: (i, 0)),
        out_shape=jax.ShapeDtypeStruct((m, n), jnp.float32),
        compiler_params=pltpu.CompilerParams(dimension_semantics=("parallel",), vmem_limit_bytes=VMEM_LIMIT),
        name="proj_residual",
    )(*[x.reshape(m, x.shape[-1]) for x in xs], *ws, res.reshape(m, n))
    return out.reshape(bsz, t, n)


FFN_COL_CHUNKS = 2


def _ffn_up_body(x_ref, g_ref, wg_ref, wu_ref, h_ref):
    xn = _norm_bf16(x_ref, g_ref)
    cw = h_ref.shape[1] // FFN_COL_CHUNKS
    for c in range(FFN_COL_CHUNKS):
        cols = slice(c * cw, (c + 1) * cw)
        g = jnp.dot(xn, wg_ref[:, cols], preferred_element_type=jnp.float32)
        u = jnp.dot(xn, wu_ref[:, cols], preferred_element_type=jnp.float32)
        h_ref[:, cols] = (g * jax.nn.sigmoid(g) * u).astype(h_ref.dtype)


def ffn(y, gain, wg, wu, wd):
    bsz, t, k = y.shape
    m = bsz * t
    tm = _row_tile(m)
    f = wg.shape[1]
    h = pl.pallas_call(
        _ffn_up_body,
        grid=(m // tm,),
        in_specs=[pl.BlockSpec((tm, k), lambda i: (i, 0)),
                  pl.BlockSpec((1, k), lambda i: (0, 0)),
                  pl.BlockSpec(wg.shape, lambda i: (0, 0)),
                  pl.BlockSpec(wu.shape, lambda i: (0, 0))],
        out_specs=pl.BlockSpec((tm, f), lambda i: (i, 0)),
        out_shape=jax.ShapeDtypeStruct((m, f), jnp.bfloat16),
        compiler_params=pltpu.CompilerParams(dimension_semantics=("parallel",), vmem_limit_bytes=VMEM_LIMIT),
        name="ffn_up",
    )(y.reshape(m, k), gain.reshape(1, k), wg, wu)
    return proj_residual([h.reshape(bsz, t, f)], [wd], y)


def rmsnorm(x, g, eps=1e-6):
    xf = x.astype(jnp.float32)
    y = xf * lax.rsqrt(jnp.mean(xf * xf, axis=-1, keepdims=True) + eps)
    return (y * g.astype(jnp.float32)).astype(x.dtype)


def rope(x, pos):
    half = ROPE_DIM // 2
    inv = ROPE_THETA ** (-jnp.arange(half, dtype=jnp.float32) / half)
    ang = pos.astype(jnp.float32)[:, None] * inv[None, :]
    shp = (pos.shape[0],) + (1,) * (x.ndim - 3) + (half,)
    cos, sin = jnp.cos(ang).reshape(shp), jnp.sin(ang).reshape(shp)
    xr = x[..., :ROPE_DIM].astype(jnp.float32)
    x1, x2 = xr[..., :half], xr[..., half:]
    rot = jnp.concatenate([x1 * cos - x2 * sin, x2 * cos + x1 * sin], axis=-1)
    return jnp.concatenate([rot.astype(x.dtype), x[..., ROPE_DIM:]], axis=-1)


def masked_softmax(s, mask):
    p = jax.nn.softmax(jnp.where(mask, s, -1e30), axis=-1)
    return jnp.where(mask, p, 0.0)


def nsa_project(h, pos, qn, kn):
    B, T = h.shape[:2]
    q = h[..., :NSA_Q].reshape(B, T, NSA_HEADS, HEAD_DIM)
    kv = h[..., NSA_Q:NSA_Q + NSA_KVW].reshape(B, T, 3, 2, NSA_KV, HEAD_DIM)
    gates = jax.nn.sigmoid(h[..., NSA_Q + NSA_KVW:NSA_IN].astype(jnp.float32)).reshape(B, T, NSA_HEADS, 3)
    q = rope(rmsnorm(q, qn), pos)
    k = rope(rmsnorm(kv[:, :, :, 0], kn[:, None, :]), pos)
    kv = jnp.stack([k, kv[:, :, :, 1]], axis=3)
    rows = kv[:, :, :2].reshape(B, T, 4, NSA_KV, HEAD_DIM)
    win = kv[:, :, 2]
    return q, gates, rows, win


def compress(k, w1, w2, pe):
    B, L = k.shape[:2]
    n_chunks = L // CMP_STRIDE
    c = k[:, :n_chunks * CMP_STRIDE].reshape(B, n_chunks, CMP_STRIDE, NSA_KV, HEAD_DIM)
    h = (jnp.einsum('bnlgd,ldh->bngh', c[:, :-1], w1[:CMP_STRIDE])
         + jnp.einsum('bnlgd,ldh->bngh', c[:, 1:], w1[CMP_STRIDE:])
         + jnp.einsum('ld,ldh->h', pe, w1))
    return jnp.einsum('bngh,hd->bngd', jax.nn.gelu(h), w2)


def nsa_keys(rows, cw1, cw2, cpe):
    B, L = rows.shape[:2]
    kc = compress(rows[:, :, 0], cw1[0], cw2[0], cpe[0])
    vc = compress(rows[:, :, 1], cw1[1], cw2[1], cpe[1])
    n_cmp = kc.shape[1]
    cmp_start = jnp.arange(n_cmp, dtype=jnp.int32) * CMP_STRIDE
    cmp_end = cmp_start + CMP_BLOCK - 1
    n_sel = -(-L // SEL_BLOCK)
    sel_start = jnp.arange(n_sel, dtype=jnp.int32) * SEL_BLOCK
    sel_map = ((cmp_start[:, None] <= sel_start[None, :] + SEL_BLOCK - 1)
               & (cmp_end[:, None] >= sel_start[None, :])).astype(jnp.float32)
    slc = jnp.pad(rows[:, :, 2:4], ((0, 0), (0, n_sel * SEL_BLOCK - L), (0, 0), (0, 0), (0, 0)))
    slc = slc.reshape(B, n_sel, SEL_BLOCK, 2, NSA_KV, HEAD_DIM).transpose(3, 0, 4, 1, 2, 5)
    return kc, vc, cmp_end, sel_map, slc[0], slc[1]


def nsa_block(q, gates, q_pos, kc, vc, cmp_end, sel_map, ksb, vsb, win, w_pos):
    f32 = jnp.float32
    B, QB = q.shape[:2]
    qg = q.reshape(B, QB, NSA_KV, NSA_REP, HEAD_DIM).astype(f32) * (HEAD_DIM ** -0.5)
    tq = q_pos[:, None]
    s = jnp.einsum('bqgrd,bngd->bqgrn', qg, kc.astype(f32))
    p = masked_softmax(s, (cmp_end[None, :] <= tq)[None, :, None, None, :])
    o_cmp = jnp.einsum('bqgrn,bngd->bqgrd', p, vc.astype(f32))
    imp = jnp.einsum('bqgrn,nj->bqgj', p, sel_map)
    n_sel = sel_map.shape[1]
    blk = jnp.arange(n_sel, dtype=jnp.int32)[None, :]
    cur = tq // SEL_BLOCK
    forced = (blk == 0) | (blk == cur) | (blk == cur - 1)
    valid = blk * SEL_BLOCK <= tq
    imp = jnp.where(valid[None, :, None, :], jnp.where(forced[None, :, None, :], 1e30, imp), -1e30)
    n_top = min(N_SELECT, n_sel)
    _, idx = lax.top_k(imp, n_top)
    bi = jnp.arange(B)[:, None, None, None]
    gi = jnp.arange(NSA_KV)[None, None, :, None]
    nk = n_top * SEL_BLOCK
    k_sel = ksb[bi, gi, idx].reshape(B, QB, NSA_KV, nk, HEAD_DIM).astype(f32)
    v_sel = vsb[bi, gi, idx].reshape(B, QB, NSA_KV, nk, HEAD_DIM).astype(f32)
    k_pos = (idx[..., None] * SEL_BLOCK + jnp.arange(SEL_BLOCK, dtype=jnp.int32)).reshape(B, QB, NSA_KV, nk)
    s = jnp.einsum('bqgrd,bqgkd->bqgrk', qg, k_sel)
    p = masked_softmax(s, (k_pos <= q_pos[None, :, None, None])[:, :, :, None, :])
    o_slc = jnp.einsum('bqgrk,bqgkd->bqgrd', p, v_sel)
    s = jnp.einsum('bqgrd,bkgd->bqgrk', qg, win[:, :, 0].astype(f32))
    wp = w_pos[None, :]
    m = (wp <= tq) & (wp > tq - WINDOW) & (wp >= 0)
    p = masked_softmax(s, m[None, :, None, None, :])
    o_win = jnp.einsum('bqgrk,bkgd->bqgrd', p, win[:, :, 1].astype(f32))
    g = gates.reshape(B, QB, NSA_KV, NSA_REP, 3)
    o = g[..., 0:1] * o_cmp + g[..., 1:2] * o_slc + g[..., 2:3] * o_win
    return o.reshape(B, QB, NSA_Q)


NEG = -1e30


def _nsa_cmp_body(q_ref, kt_ref, v_ref, map_ref, o_ref, sel_ref, *, tq, n_cmp, n_sel):
    f32, bf16 = jnp.float32, jnp.bfloat16
    i = pl.program_id(1)
    ncp = kt_ref.shape[2]
    pos = i * tq + lax.broadcasted_iota(jnp.int32, (tq, 1), 0)
    n = lax.broadcasted_iota(jnp.int32, (1, ncp), 1)
    ok = (n * CMP_STRIDE + (CMP_BLOCK - 1) <= pos) & (n < n_cmp)
    psum = jnp.zeros((tq, ncp), f32)
    for r in range(NSA_REP):
        s = jnp.dot(q_ref[0, r], kt_ref[0], preferred_element_type=f32)
        s = jnp.where(ok, s, NEG)
        m = jnp.max(s, axis=-1, keepdims=True)
        e = jnp.where(ok, jnp.exp(s - m), 0.0)
        l = jnp.sum(e, axis=-1, keepdims=True)
        p = e * jnp.where(l > 0.0, 1.0 / l, 0.0)
        o_ref[0, r] = jnp.dot(p.astype(bf16), v_ref[0], preferred_element_type=f32)
        psum = psum + p
    hi = psum.astype(bf16)
    lo = (psum - hi.astype(f32)).astype(bf16)
    imp = (jnp.dot(hi, map_ref[...], preferred_element_type=f32)
           + jnp.dot(lo, map_ref[...], preferred_element_type=f32))
    nsp = map_ref.shape[1]
    blk = lax.broadcasted_iota(jnp.int32, (1, nsp), 1)
    cur = pos // SEL_BLOCK
    valid = (blk * SEL_BLOCK <= pos) & (blk < n_sel)
    forced = (blk == 0) | (blk == cur) | (blk == cur - 1)
    x = jnp.where(valid, jnp.where(forced, 1e30, imp), NEG)
    x = jnp.where(blk < n_sel, x, -3e38)
    blkf = blk.astype(f32)
    sel = jnp.zeros((tq, nsp), f32)
    for _ in range(min(N_SELECT, n_sel)):
        m = jnp.max(x, axis=-1, keepdims=True)
        first = jnp.min(jnp.where(x == m, blkf, float(nsp)), axis=-1, keepdims=True)
        hit = blkf == first
        sel = jnp.where(hit, 1.0, sel)
        x = jnp.where(hit, -3e38, x)
    sel_ref[0] = jnp.where(valid, sel, 0.0).astype(bf16)


def nsa_cmp_select(q4, kct, vc, sel_map, n_cmp, n_sel, *, tq=128):
    BG, _, T, _ = q4.shape
    ncp, nsp = sel_map.shape
    return pl.pallas_call(
        functools.partial(_nsa_cmp_body, tq=tq, n_cmp=n_cmp, n_sel=n_sel),
        grid=(BG, T // tq),
        in_specs=[pl.BlockSpec((1, NSA_REP, tq, HEAD_DIM), lambda b, i: (b, 0, i, 0)),
                  pl.BlockSpec((1, HEAD_DIM, ncp), lambda b, i: (b, 0, 0)),
                  pl.BlockSpec((1, ncp, HEAD_DIM), lambda b, i: (b, 0, 0)),
                  pl.BlockSpec((ncp, nsp), lambda b, i: (0, 0))],
        out_specs=[pl.BlockSpec((1, NSA_REP, tq, HEAD_DIM), lambda b, i: (b, 0, i, 0)),
                   pl.BlockSpec((1, tq, nsp), lambda b, i: (b, i, 0))],
        out_shape=[jax.ShapeDtypeStruct((BG, NSA_REP, T, HEAD_DIM), jnp.float32),
                   jax.ShapeDtypeStruct((BG, T, nsp), jnp.bfloat16)],
        compiler_params=pltpu.CompilerParams(
            dimension_semantics=("parallel", "parallel"), vmem_limit_bytes=VMEM_LIMIT),
        name="nsa_cmp_select",
    )(q4, kct, vc, sel_map)


def _nsa_flash_body(*refs, tq, tk, mode, key_off):
    f32, bf16 = jnp.float32, jnp.bfloat16
    if mode == "select":
        q_ref, kt_ref, v_ref, sel_ref, o_ref, m_scr, l_scr, acc_scr = refs
    else:
        q_ref, kt_ref, v_ref, o_ref, m_scr, l_scr, acc_scr = refs
    i, j = pl.program_id(1), pl.program_id(2)
    nj = pl.num_programs(2)

    @pl.when(j == 0)
    def _():
        m_scr[...] = jnp.full_like(m_scr, NEG)
        l_scr[...] = jnp.zeros_like(l_scr)
        acc_scr[...] = jnp.zeros_like(acc_scr)

    qpos = i * tq + lax.broadcasted_iota(jnp.int32, (tq, 1), 0)
    lane = lax.broadcasted_iota(jnp.int32, (1, tk), 1)
    if mode == "select":
        k0 = j * tk
        active = k0 <= i * tq + (tq - 1)
    else:
        k0 = (i * tq // tk + j) * tk - key_off
        active = j >= 0
    kpos = k0 + lane

    @pl.when(active)
    def _():
        if mode == "select":
            nsp = sel_ref.shape[2]
            c = lax.broadcasted_iota(jnp.int32, (nsp, tk), 0)
            l2 = lax.broadcasted_iota(jnp.int32, (nsp, tk), 1)
            expand = (c == (k0 + l2) // SEL_BLOCK).astype(bf16)
            picked = jnp.dot(sel_ref[0], expand, preferred_element_type=f32)
            allowed = (picked > 0.5) & (kpos <= qpos)
        else:
            allowed = (kpos <= qpos) & (kpos > qpos - WINDOW) & (kpos >= 0)
        for r in range(NSA_REP):
            s = jnp.dot(q_ref[0, r], kt_ref[0], preferred_element_type=f32)
            s = jnp.where(allowed, s, NEG)
            m_old = m_scr[r]
            m_new = jnp.maximum(m_old, jnp.max(s, axis=-1, keepdims=True))
            alpha = jnp.exp(m_old - m_new)
            p = jnp.where(allowed, jnp.exp(s - m_new), 0.0)
            l_scr[r] = alpha * l_scr[r] + jnp.sum(p, axis=-1, keepdims=True)
            acc_scr[r] = alpha * acc_scr[r] + jnp.dot(p.astype(bf16), v_ref[0], preferred_element_type=f32)
            m_scr[r] = m_new

    @pl.when(j == nj - 1)
    def _():
        for r in range(NSA_REP):
            o_ref[0, r] = acc_scr[r] / l_scr[r]


def nsa_flash(q4, kt, v, sel=None, *, mode, tq, tk, key_off=0):
    BG, _, T, _ = q4.shape
    if mode == "select":
        nj = T // tk
        kidx = lambda b, i, j: jnp.minimum(j, (i * tq + tq - 1) // tk)
    else:
        nj = (key_off + tq) // tk
        kidx = lambda b, i, j: i * tq // tk + j
    in_specs = [pl.BlockSpec((1, NSA_REP, tq, HEAD_DIM), lambda b, i, j: (b, 0, i, 0)),
                pl.BlockSpec((1, HEAD_DIM, tk), lambda b, i, j: (b, 0, kidx(b, i, j))),
                pl.BlockSpec((1, tk, HEAD_DIM), lambda b, i, j: (b, kidx(b, i, j), 0))]
    args = [q4, kt, v]
    if mode == "select":
        in_specs.append(pl.BlockSpec((1, tq, sel.shape[2]), lambda b, i, j: (b, i, 0)))
        args.append(sel)
    return pl.pallas_call(
        functools.partial(_nsa_flash_body, tq=tq, tk=tk, mode=mode, key_off=key_off),
        grid=(BG, T // tq, nj),
        in_specs=in_specs,
        out_specs=pl.BlockSpec((1, NSA_REP, tq, HEAD_DIM), lambda b, i, j: (b, 0, i, 0)),
        out_shape=jax.ShapeDtypeStruct((BG, NSA_REP, T, HEAD_DIM), jnp.float32),
        scratch_shapes=[pltpu.VMEM((NSA_REP, tq, 1), jnp.float32),
                        pltpu.VMEM((NSA_REP, tq, 1), jnp.float32),
                        pltpu.VMEM((NSA_REP, tq, HEAD_DIM), jnp.float32)],
        compiler_params=pltpu.CompilerParams(
            dimension_semantics=("parallel", "parallel", "arbitrary"), vmem_limit_bytes=VMEM_LIMIT),
        name="nsa_flash_" + mode,
    )(*args)


def nsa_prompt(q, gates, rows, win, cw1, cw2, cpe):
    bf16 = jnp.bfloat16
    B, T = q.shape[:2]
    G = NSA_KV
    kc = compress(rows[:, :, 0], cw1[0], cw2[0], cpe[0])
    vc = compress(rows[:, :, 1], cw1[1], cw2[1], cpe[1])
    n_cmp = kc.shape[1]
    n_sel = T // SEL_BLOCK
    ncp = -(-n_cmp // LANE) * LANE
    nsp = -(-n_sel // LANE) * LANE
    n = jnp.arange(ncp)[:, None]
    j = jnp.arange(nsp)[None, :]
    sel_map = ((n * CMP_STRIDE <= j * SEL_BLOCK + SEL_BLOCK - 1) & (n * CMP_STRIDE + CMP_BLOCK - 1 >= j * SEL_BLOCK)
               & (n < n_cmp) & (j < n_sel)).astype(bf16)
    q4 = (q * (HEAD_DIM ** -0.5)).reshape(B, T, G, NSA_REP, HEAD_DIM).transpose(0, 2, 3, 1, 4)
    q4 = q4.reshape(B * G, NSA_REP, T, HEAD_DIM).astype(bf16)

    def keys_t(x, front=0, back=0):
        x = jnp.pad(x, ((0, 0), (front, back), (0, 0), (0, 0)))
        return x.transpose(0, 2, 3, 1).reshape(B * G, HEAD_DIM, -1).astype(bf16)

    def vals(x, front=0, back=0):
        x = jnp.pad(x, ((0, 0), (front, back), (0, 0), (0, 0)))
        return x.transpose(0, 2, 1, 3).reshape(B * G, -1, HEAD_DIM).astype(bf16)

    o_cmp, sel = nsa_cmp_select(q4, keys_t(kc, 0, ncp - n_cmp), vals(vc, 0, ncp - n_cmp), sel_map, n_cmp, n_sel)
    o_slc = nsa_flash(q4, keys_t(rows[:, :, 2]), vals(rows[:, :, 3]), sel, mode="select", tq=128, tk=512)
    o_win = nsa_flash(q4, keys_t(win[:, :, 0], WINDOW), vals(win[:, :, 1], WINDOW), mode="window",
                      tq=256, tk=256, key_off=WINDOW)
    g = gates.reshape(B, T, G, NSA_REP, 3)
    un = lambda o: o.reshape(B, G, NSA_REP, T, HEAD_DIM).transpose(0, 3, 1, 2, 4)
    o = g[..., 0:1] * un(o_cmp) + g[..., 1:2] * un(o_slc) + g[..., 2:3] * un(o_win)
    return o.reshape(B, T, NSA_Q)


NSA_TQ = 128
NSA_TK = 512
NSA_TKW = 128


def _nsa_attn_body(qt_ref, kc_ref, vct_ref, map_ref, ks_ref, vst_ref, kw_ref, vwt_ref, g_ref, o_ref, thr_scr,
                   *, n_cmp, n_sel, q_tile0, win_tile0):
    f32, bf16 = jnp.float32, jnp.bfloat16
    tq, tk, tkw = NSA_TQ, NSA_TK, NSA_TKW
    nl = NSA_REP * tq
    i = pl.program_id(1) + q_tile0
    qt = qt_ref[0, 0]
    qpos = i * tq + lax.broadcasted_iota(jnp.int32, (1, nl), 1) % tq

    ncp = kc_ref.shape[1]
    n = lax.broadcasted_iota(jnp.int32, (ncp, 1), 0)
    ok = (n * CMP_STRIDE + (CMP_BLOCK - 1) <= qpos) & (n < n_cmp)
    s = jnp.dot(kc_ref[0], qt, preferred_element_type=f32)
    s = jnp.where(ok, s, NEG)
    m = jnp.max(s, axis=0, keepdims=True)
    e = jnp.where(ok, jnp.exp(s - m), 0.0)
    l = jnp.sum(e, axis=0, keepdims=True)
    p = e * jnp.where(l > 0.0, 1.0 / l, 0.0)
    o_cmp = jnp.dot(vct_ref[0], p.astype(bf16), preferred_element_type=f32)

    psum = p[:, 0:tq]
    for r in range(1, NSA_REP):
        psum = psum + p[:, r * tq:(r + 1) * tq]
    hi = psum.astype(bf16)
    lo = (psum - hi.astype(f32)).astype(bf16)
    imp = (jnp.dot(map_ref[...], hi, preferred_element_type=f32)
           + jnp.dot(map_ref[...], lo, preferred_element_type=f32))
    nsp = map_ref.shape[0]
    qp = qpos[:, 0:tq]
    blk = lax.broadcasted_iota(jnp.int32, (nsp, 1), 0)
    cur = qp // SEL_BLOCK
    valid = (blk * SEL_BLOCK <= qp) & (blk < n_sel)
    forced = (blk == 0) | (blk == cur) | (blk == cur - 1)
    x = jnp.where(valid, jnp.where(forced, 1e30, imp), NEG)
    x = jnp.where(blk < n_sel, x, -3e38)
    blkf = blk.astype(f32)
    sel = jnp.zeros((nsp, tq), f32)
    for _ in range(min(N_SELECT, n_sel)):
        mx = jnp.max(x, axis=0, keepdims=True)
        first = jnp.min(jnp.where(x == mx, blkf, float(nsp)), axis=0, keepdims=True)
        hit = blkf == first
        sel = jnp.where(hit, 1.0, sel)
        x = jnp.where(hit, -3e38, x)
    thr = jnp.where(valid & (sel > 0.5), qp, -1)
    thr_scr[...] = jnp.concatenate([thr] * NSA_REP, axis=1)

    def flash(s, allowed, vt, carry):
        m_old, l_old, acc = carry
        m_new = jnp.maximum(m_old, jnp.max(s, axis=0, keepdims=True))
        alpha = jnp.exp(m_old - m_new)
        p = jnp.exp(s - m_new)
        if allowed is not None:
            p = jnp.where(allowed, p, 0.0)
        l_new = alpha * l_old + jnp.sum(p, axis=0, keepdims=True)
        acc = alpha * acc + jnp.dot(vt, p.astype(bf16), preferred_element_type=f32)
        return m_new, l_new, acc

    init = (jnp.full((1, nl), NEG, f32), jnp.zeros((1, nl), f32), jnp.zeros((HEAD_DIM, nl), f32))

    krow = lax.broadcasted_iota(jnp.int32, (SEL_BLOCK, 1), 0)

    def sel_step(j, carry):
        s = jnp.dot(ks_ref[0, j], qt, preferred_element_type=f32)
        th = thr_scr[pl.ds(pl.multiple_of(j * (tk // SEL_BLOCK), 8), tk // SEL_BLOCK), :]
        s = jnp.concatenate(
            [jnp.where((j * tk + b * SEL_BLOCK + krow) <= th[b:b + 1], s[b * SEL_BLOCK:(b + 1) * SEL_BLOCK], NEG)
             for b in range(tk // SEL_BLOCK)], axis=0)
        return flash(s, None, vst_ref[0, j], carry)

    _, l_s, acc_s = lax.fori_loop(0, (i * tq + tq - 1) // tk + 1, sel_step, init)
    o_slc = acc_s / l_s

    kwrow = lax.broadcasted_iota(jnp.int32, (tkw, 1), 0)

    def win_step(j, carry):
        s = jnp.dot(kw_ref[0, j - win_tile0], qt, preferred_element_type=f32)
        kpos = j * tkw + kwrow
        allowed = (kpos <= qpos) & (kpos > qpos - WINDOW)
        return flash(jnp.where(allowed, s, NEG), allowed, vwt_ref[0, j - win_tile0], carry)

    j_lo = jnp.maximum((i * tq - (WINDOW - 1)) // tkw, win_tile0)
    _, l_w, acc_w = lax.fori_loop(j_lo, (i * tq + tq - 1) // tkw + 1, win_step, init)
    o_win = acc_w / l_w

    g = g_ref[0, 0]
    o_ref[0, 0] = g[0:1] * o_cmp + g[1:2] * o_slc + g[2:3] * o_win


def nsa_attention(q, gates, kc, vc, ksl, vsl, kwin, vwin, *, n_keys, q_pos0=0, win_pos0=0):
    f32, bf16 = jnp.float32, jnp.bfloat16
    B, T = q.shape[:2]
    G, R, D = NSA_KV, NSA_REP, HEAD_DIM
    tq, tk, tkw = NSA_TQ, NSA_TK, NSA_TKW
    nq, nl = T // tq, R * tq
    n_cmp, n_sel = kc.shape[1], -(-n_keys // SEL_BLOCK)
    lk, lw = ksl.shape[1], kwin.shape[1]
    assert T % tq == 0 and lk % tk == 0 and lw % tkw == 0 and q_pos0 % tq == 0 and win_pos0 % tkw == 0
    assert lk >= q_pos0 + T and win_pos0 + lw >= q_pos0 + T
    ncp = -(-n_cmp // LANE) * LANE
    nsp = -(-n_sel // LANE) * LANE
    n = jnp.arange(ncp)[None, :]
    jb = jnp.arange(nsp)[:, None]
    map_t = ((n * CMP_STRIDE <= jb * SEL_BLOCK + SEL_BLOCK - 1) & (n * CMP_STRIDE + CMP_BLOCK - 1 >= jb * SEL_BLOCK)
             & (n < n_cmp) & (jb < n_sel)).astype(bf16)
    qt = (q * (D ** -0.5)).reshape(B, nq, tq, G, R, D).transpose(0, 3, 1, 5, 4, 2).reshape(B * G, nq, D, nl).astype(bf16)
    gt = gates.reshape(B, nq, tq, G, R, 3).transpose(0, 3, 1, 5, 4, 2).reshape(B * G, nq, 3, nl)
    gt = jnp.pad(gt, ((0, 0), (0, 0), (0, 5), (0, 0)))
    rows = lambda x, t: x.transpose(0, 2, 1, 3).reshape(B * G, -1, t, D).astype(bf16)
    cols = lambda x, t: x.reshape(B, -1, t, G, D).transpose(0, 3, 1, 4, 2).reshape(B * G, -1, D, t).astype(bf16)
    kcp = jnp.pad(kc, ((0, 0), (0, ncp - n_cmp), (0, 0), (0, 0)))
    vcp = jnp.pad(vc, ((0, 0), (0, ncp - n_cmp), (0, 0), (0, 0)))
    kc_r = rows(kcp, ncp)[:, 0]
    vc_c = cols(vcp, ncp)[:, 0]
    full = lambda shape: pl.BlockSpec((1,) + shape, lambda b, i: (b,) + (0,) * len(shape))
    per_q = lambda shape: pl.BlockSpec((1, 1) + shape, lambda b, i: (b, i) + (0,) * len(shape))
    out = pl.pallas_call(
        functools.partial(_nsa_attn_body, n_cmp=n_cmp, n_sel=n_sel,
                          q_tile0=q_pos0 // tq, win_tile0=win_pos0 // tkw),
        grid=(B * G, nq),
        in_specs=[per_q((D, nl)), full((ncp, D)), full((D, ncp)),
                  pl.BlockSpec((nsp, ncp), lambda b, i: (0, 0)),
                  full((lk // tk, tk, D)), full((lk // tk, D, tk)),
                  full((lw // tkw, tkw, D)), full((lw // tkw, D, tkw)),
                  per_q((8, nl))],
        out_specs=per_q((D, nl)),
        out_shape=jax.ShapeDtypeStruct((B * G, nq, D, nl), f32),
        scratch_shapes=[pltpu.VMEM((nsp, nl), jnp.int32)],
        compiler_params=pltpu.CompilerParams(
            dimension_semantics=("parallel", "parallel"), vmem_limit_bytes=VMEM_LIMIT),
        name="nsa_attention",
    )(qt, kc_r, vc_c, map_t, rows(ksl, tk), cols(vsl, tk), rows(kwin, tkw), cols(vwin, tkw), gt)
    return out.reshape(B, G, nq, D, R, tq).transpose(0, 2, 5, 1, 4, 3).reshape(B, T, NSA_Q)


def nsa_prompt(q, gates, rows, win, cw1, cw2, cpe):
    kc = compress(rows[:, :, 0], cw1[0], cw2[0], cpe[0])
    vc = compress(rows[:, :, 1], cw1[1], cw2[1], cpe[1])
    return nsa_attention(q, gates, kc, vc, rows[:, :, 2], rows[:, :, 3], win[:, :, 0], win[:, :, 1],
                         n_keys=q.shape[1])


def nsa_sample(q, gates, past, rows, win_all, cw1, cw2, cpe):
    DB, S = q.shape[:2]
    P, WB = past.shape[1], win_all.shape[1] - S
    up = lambda n, m: -(-n // m) * m
    pad_rows = lambda x, n: jnp.pad(x, ((0, 0), (0, n - x.shape[1])) + ((0, 0),) * (x.ndim - 2))
    n_chunk_rows = (P + S) // CMP_STRIDE * CMP_STRIDE
    cmp_in = [past[:, :n_chunk_rows, w] if n_chunk_rows <= P
              else jnp.concatenate([past[:, :, w], rows[:, :n_chunk_rows - P, w]], axis=1) for w in (0, 1)]
    kc = compress(cmp_in[0], cw1[0], cw2[0], cpe[0])
    vc = compress(cmp_in[1], cw1[1], cw2[1], cpe[1])
    tq = up(S, NSA_TQ)
    lk = up(P + tq, NSA_TK)
    ksl = pad_rows(jnp.concatenate([past[:, :, 2], rows[:, :, 2]], axis=1), lk)
    vsl = pad_rows(jnp.concatenate([past[:, :, 3], rows[:, :, 3]], axis=1), lk)
    lw = up(WB + tq, NSA_TKW)
    o = nsa_attention(pad_rows(q, tq), pad_rows(gates, tq), kc, vc, ksl, vsl,
                      pad_rows(win_all[:, :, 0], lw), pad_rows(win_all[:, :, 1], lw),
                      n_keys=P + S, q_pos0=P, win_pos0=P - WB)
    return o[:, :S]


RW_PAIRS = RWKV_HEADS // 2
RW_CHUNK = 64


def _rwkv_scan_body(r_ref, w_ref, k_ref, kk_ref, b_ref, vt_ref, s0_ref, oh_ref,
                    y_ref, st_ref, vhi_scr, vmid_scr, *, nb, tc):
    f32, bf16 = jnp.float32, jnp.bfloat16
    c = pl.program_id(1)

    @pl.when(c == 0)
    def _():
        st_ref[...] = s0_ref[...]

    row = lax.broadcasted_iota(jnp.int32, (2 * LANE, LANE), 0)
    col = lax.broadcasted_iota(jnp.int32, (2 * LANE, LANE), 1)
    ones2 = (((row // HEAD_DIM) % 2) == (col // HEAD_DIM)).astype(bf16)
    ones1 = ones2[:LANE]
    lane_t = lax.broadcasted_iota(jnp.int32, (HEAD_DIM, LANE), 1) % HEAD_DIM

    vt = vt_ref[...].reshape(nb * RW_PAIRS, HEAD_DIM, LANE)
    vhi = vt.astype(bf16)
    vhi_scr[...] = vhi
    vmid_scr[...] = (vt - vhi.astype(f32)).astype(bf16)
    y_ref[...] = jnp.zeros_like(y_ref)

    def group(t8, carry):
        t0 = pl.multiple_of(t8 * 8, 8)
        for j in range(8):
            t = t0 + j
            row = lambda ref, b, sl: ref[b, pl.ds(t0, 8), sl][j:j + 1]
            oh = oh_ref[t]
            m = lane_t == t
            pairs = [(b, p) for b in range(nb) for p in range(RW_PAIRS)]
            vl = jnp.concatenate([jnp.concatenate([vhi_scr[i] * oh, vmid_scr[i] * oh], axis=1)
                                  for i in range(len(pairs))], axis=0)
            vb_all = jnp.dot(vl, ones2, preferred_element_type=f32)
            pieces = []
            for b, p in pairs:
                pk = st_ref[b, p] * row(kk_ref, b, slice(LANE * p, LANE * (p + 1)))
                hi = pk.astype(bf16)
                pieces.append(jnp.concatenate([hi, (pk - hi.astype(f32)).astype(bf16)], axis=1))
            sa_all = jnp.dot(jnp.concatenate(pieces, axis=0), ones2, preferred_element_type=f32)
            py = []
            for i, (b, p) in enumerate(pairs):
                sl = slice(LANE * p, LANE * (p + 1))
                rows = slice(HEAD_DIM * i, HEAD_DIM * (i + 1))
                s_new = (st_ref[b, p] * row(w_ref, b, sl) - sa_all[rows] * row(b_ref, b, sl)
                         + vb_all[rows] * row(k_ref, b, sl))
                st_ref[b, p] = s_new
                py.append((s_new * row(r_ref, b, sl)).astype(bf16))
            y_all = jnp.dot(jnp.concatenate(py, axis=0), ones1, preferred_element_type=f32)
            for i, (b, p) in enumerate(pairs):
                y_ref[b, 0, p] = jnp.where(m, y_all[HEAD_DIM * i:HEAD_DIM * (i + 1)], y_ref[b, 0, p])
        return carry

    lax.fori_loop(0, tc // 8, group, 0)


def rwkv_scan(r, w, k, v, kk, b, S0):
    f32 = jnp.float32
    B, T, W = r.shape
    tc = RW_CHUNK if T % RW_CHUNK == 0 else T
    assert tc <= RW_CHUNK and T % tc == 0 and tc % 8 == 0 and B % 2 == 0
    nc, nb = T // tc, 2
    vt = v.reshape(B, nc, tc, RW_PAIRS, 2, HEAD_DIM).transpose(0, 1, 3, 5, 4, 2)
    vt = jnp.pad(vt, ((0, 0),) * 5 + ((0, RW_CHUNK - tc),)).reshape(B, nc, RW_PAIRS, HEAD_DIM, LANE)
    s0 = S0.astype(f32).reshape(B, RW_PAIRS, 2, HEAD_DIM, HEAD_DIM).transpose(0, 1, 3, 2, 4)
    s0 = s0.reshape(B, RW_PAIRS, HEAD_DIM, LANE)
    oh = jnp.arange(LANE)[None, None, :] % HEAD_DIM == jnp.arange(RW_CHUNK)[:, None, None]
    oh = jnp.broadcast_to(oh, (RW_CHUNK, HEAD_DIM, LANE)).astype(jnp.bfloat16)
    tok = pl.BlockSpec((nb, tc, W), lambda i, c: (i, c, 0))
    chk = pl.BlockSpec((nb, 1, RW_PAIRS, HEAD_DIM, LANE), lambda i, c: (i, c, 0, 0, 0))
    stt = pl.BlockSpec((nb, RW_PAIRS, HEAD_DIM, LANE), lambda i, c: (i, 0, 0, 0))
    y, st = pl.pallas_call(
        functools.partial(_rwkv_scan_body, nb=nb, tc=tc),
        grid=(B // nb, nc),
        in_specs=[tok, tok, tok, tok, tok, chk, stt,
                  pl.BlockSpec((RW_CHUNK, HEAD_DIM, LANE), lambda i, c: (0, 0, 0))],
        out_specs=[chk, stt],
        out_shape=[jax.ShapeDtypeStruct((B, nc, RW_PAIRS, HEAD_DIM, LANE), f32),
                   jax.ShapeDtypeStruct((B, RW_PAIRS, HEAD_DIM, LANE), f32)],
        scratch_shapes=[pltpu.VMEM((nb * RW_PAIRS, HEAD_DIM, LANE), jnp.bfloat16),
                        pltpu.VMEM((nb * RW_PAIRS, HEAD_DIM, LANE), jnp.bfloat16)],
        compiler_params=pltpu.CompilerParams(
            dimension_semantics=("parallel", "arbitrary"), vmem_limit_bytes=VMEM_LIMIT),
        name="rwkv_scan",
    )(r, w, k, kk, b, vt, s0, oh)
    y = y.reshape(B, nc, RW_PAIRS, HEAD_DIM, 2, RW_CHUNK)[..., :tc]
    y = y.transpose(0, 1, 5, 2, 4, 3).reshape(B, T, W)
    st = st.reshape(B, RW_PAIRS, HEAD_DIM, 2, HEAD_DIM).transpose(0, 1, 3, 2, 4)
    return y, st.reshape(B, RWKV_HEADS, HEAD_DIM, HEAD_DIM)


def rwkv_mix(p, prev, S0, mu, w0, wB, a0, aB, gB, k_k, k_a, r_k, ln_w, ln_b):
    f32 = jnp.float32
    B, T = p.shape[:2]
    W = RWKV_WIDTH
    p = p.astype(f32)
    p_prev = jnp.concatenate([prev.astype(f32)[:, None], p[:, :-1]], axis=1)
    ps = p + mu * (p_prev - p)
    r, k, v = ps[..., :W], ps[..., W:2 * W], ps[..., 2 * W:3 * W]
    o = 3 * W
    xw, xa, xg = ps[..., o:o + LORA_W], ps[..., o + LORA_W:o + LORA_W + LORA_A], ps[..., o + LORA_W + LORA_A:]
    z = w0 + jnp.tanh(xw) @ wB
    w = jnp.exp(-jnp.exp(-jax.nn.softplus(-z) - 0.5))
    a = jax.nn.sigmoid(a0 + xa @ aB)
    g = jax.nn.sigmoid(xg) @ gB
    heads = lambda t: t.reshape(B, T, RWKV_HEADS, HEAD_DIM)
    kk = heads(k * k_k)
    kk = kk * lax.rsqrt(jnp.maximum(jnp.sum(kk * kk, axis=-1, keepdims=True), 1e-24))
    k = k * (1.0 + (a - 1.0) * k_a)
    kk = kk.reshape(B, T, W)
    y, S = rwkv_scan(r, w, k, v, kk, kk * a, S0)
    y, k = heads(y), heads(k)
    r, v = heads(r), heads(v)
    mean = jnp.mean(y, axis=-1, keepdims=True)
    var = jnp.mean(jnp.square(y - mean), axis=-1, keepdims=True)
    y = ((y - mean) * lax.rsqrt(var + RWKV_GN_EPS)).reshape(B, T, W) * ln_w + ln_b
    y = y + (jnp.sum(r * k * r_k, axis=-1, keepdims=True) * v).reshape(B, T, W)
    return y * g, S, p[:, -1]


def gla_chunked(q, k, v, logg, S0):
    B, T, H, DK = q.shape
    DV = v.shape[-1]
    C = GLA_CHUNK if T % GLA_CHUNK == 0 else T
    N = T // C
    to_chunks = lambda x: x.reshape(B, N, C, H, x.shape[-1]).transpose(1, 0, 3, 2, 4)
    causal = jnp.tril(jnp.ones((C, C), dtype=bool))[:, :, None]

    def step(S, inp):
        qc, kc, vc, gc = inp
        b = jnp.cumsum(gc, axis=2)
        diff = b[:, :, :, None, :] - b[:, :, None, :, :]
        decay = jnp.exp(jnp.where(causal, diff, -jnp.inf))
        A = jnp.einsum('bhtd,bhsd,bhtsd->bhts', qc, kc, decay)
        o = jnp.einsum('bhts,bhsv->bhtv', A, vc) + jnp.einsum('bhtk,bhkv->bhtv', qc * jnp.exp(b), S)
        b_last = b[:, :, -1:, :]
        S = jnp.exp(b_last[:, :, 0, :])[..., None] * S + jnp.einsum('bhsk,bhsv->bhkv', kc * jnp.exp(b_last - b), vc)
        return S, o

    S, o = lax.scan(step, S0, (to_chunks(q), to_chunks(k), to_chunks(v), to_chunks(logg)))
    return o.transpose(1, 0, 3, 2, 4).reshape(B, T, H, DV), S


GLA_SUB = 16


def _gla_body(q_ref, k_ref, v_ref, g_ref, r_ref, gn_ref, o_ref, st_ref):
    f32, bf16 = jnp.float32, jnp.bfloat16
    C, SUB = GLA_CHUNK, GLA_SUB
    nsub = C // SUB
    c = pl.program_id(1)

    @pl.when(c == 0)
    def _():
        st_ref[...] = jnp.zeros_like(st_ref)

    ti = lax.broadcasted_iota(jnp.int32, (C, C), 0)
    si = lax.broadcasted_iota(jnp.int32, (C, C), 1)
    tri = (si <= ti).astype(bf16)
    sub_t = lax.broadcasted_iota(jnp.int32, (SUB, 1), 0)
    sub_l = lax.broadcasted_iota(jnp.int32, (SUB, SUB), 1)
    for h in range(GLA_HEADS):
        kq = slice(h * GLA_DK, (h + 1) * GLA_DK)
        vv = slice(h * GLA_DV, (h + 1) * GLA_DV)
        q = q_ref[0, :, kq] * (GLA_DK ** -0.5)
        k = k_ref[0, :, kq]
        v = v_ref[0, :, vv].astype(bf16)
        g = g_ref[0, :, kq]
        g1 = g.astype(bf16)
        g2 = (g - g1.astype(f32)).astype(bf16)
        g3 = (g - g1.astype(f32) - g2.astype(f32)).astype(bf16)
        b = (jnp.dot(tri, g1, preferred_element_type=f32) + jnp.dot(tri, g2, preferred_element_type=f32)
             + jnp.dot(tri, g3, preferred_element_type=f32))
        b_last = b[C - 1:C]
        a_rows = []
        for I in range(nsub):
            rows = slice(I * SUB, (I + 1) * SUB)
            beta = b[I * SUB - 1:I * SUB] if I > 0 else jnp.zeros((1, GLA_DK), f32)
            qb, kb, bb = q[rows], k[rows], b[rows]
            a_diag = jnp.zeros((SUB, SUB), f32)
            for s in range(SUB):
                d = jnp.where(sub_t >= s, bb - bb[s:s + 1], -jnp.inf)
                col = jnp.sum(qb * kb[s:s + 1] * jnp.exp(d), axis=1, keepdims=True)
                a_diag = jnp.where(sub_l == s, col, a_diag)
            blocks = []
            if I > 0:
                qe = (qb * jnp.exp(bb - beta)).astype(bf16)
                ke = (k[:I * SUB] * jnp.exp(beta - b[:I * SUB])).astype(bf16)
                blocks.append(lax.dot_general(qe, ke, (((1,), (1,)), ((), ())), preferred_element_type=f32))
            blocks.append(a_diag)
            if I < nsub - 1:
                blocks.append(jnp.zeros((SUB, C - (I + 1) * SUB), f32))
            a_rows.append(jnp.concatenate(blocks, axis=1))
        a = jnp.concatenate(a_rows, axis=0).astype(bf16)
        st = st_ref[0, h]
        qd = (q * jnp.exp(b)).astype(bf16)
        o = (jnp.dot(a, v, preferred_element_type=f32)
             + lax.dot_general(qd, st.astype(bf16), (((1,), (1,)), ((), ())), preferred_element_type=f32))
        kd = (k * jnp.exp(b_last - b)).astype(bf16)
        st_ref[0, h] = (st * jnp.exp(b_last)
                        + lax.dot_general(v, kd, (((0,), (0,)), ((), ())), preferred_element_type=f32))
        y = o * lax.rsqrt(jnp.mean(o * o, axis=1, keepdims=True) + 1e-6) * gn_ref[...]
        rr = r_ref[0, :, vv]
        o_ref[0, :, vv] = y * (rr * jax.nn.sigmoid(rr))


def gla_mix_prompt(qk, v, gd, r, gate_up, gate_b, gn):
    f32 = jnp.float32
    B, T = qk.shape[:2]
    KW, VW, C = GLA_KW, GLA_VW, GLA_CHUNK
    logg = jax.nn.log_sigmoid(gd[..., :GLA_RANK] @ gate_up + gate_b) / GLA_TAU
    tokv = pl.BlockSpec((1, C, VW), lambda b, c: (b, c, 0))
    out, st = pl.pallas_call(
        _gla_body,
        grid=(B, T // C),
        in_specs=[pl.BlockSpec((1, C, KW), lambda b, c: (b, c, 0)),
                  pl.BlockSpec((1, C, KW), lambda b, c: (b, c, 1)),
                  tokv, pl.BlockSpec((1, C, KW), lambda b, c: (b, c, 0)), tokv,
                  pl.BlockSpec((1, GLA_DV), lambda b, c: (0, 0))],
        out_specs=[tokv, pl.BlockSpec((1, GLA_HEADS, GLA_DV, GLA_DK), lambda b, c: (b, 0, 0, 0))],
        out_shape=[jax.ShapeDtypeStruct((B, T, VW), f32),
                   jax.ShapeDtypeStruct((B, GLA_HEADS, GLA_DV, GLA_DK), f32)],
        compiler_params=pltpu.CompilerParams(
            dimension_semantics=("parallel", "arbitrary"), vmem_limit_bytes=VMEM_LIMIT),
        name="gla_chunk",
    )(qk, qk, v, logg, r, gn.reshape(1, GLA_DV))
    return out, st.transpose(0, 1, 3, 2)


def gla_mix(qk, v, gd, r, S0, gate_up, gate_b, gn):
    f32 = jnp.float32
    B, T = qk.shape[:2]
    q = qk[..., :GLA_KW].reshape(B, T, GLA_HEADS, GLA_DK) * (GLA_DK ** -0.5)
    k = qk[..., GLA_KW:].reshape(B, T, GLA_HEADS, GLA_DK)
    v = v.reshape(B, T, GLA_HEADS, GLA_DV)
    gd = gd[..., :GLA_RANK]
    logg = (jax.nn.log_sigmoid(gd @ gate_up + gate_b) / GLA_TAU).reshape(B, T, GLA_HEADS, GLA_DK)
    out, S = gla_chunked(q, k, v, logg, S0.astype(f32))
    out = rmsnorm(out, gn).reshape(B, T, GLA_VW) * jax.nn.silu(r)
    return out, S


def kernel(x_prompt, x_sample, cache_nsa_kv, cache_nsa_win, state_rwkv, state_rwkv_shift, state_gla, page_table, norm_mix, norm_ffn, w_in_even, nsa_qnorm, nsa_knorm, cmp_w1, cmp_w2, cmp_pe, rwkv_mu, rwkv_w0, rwkv_wB, rwkv_a0, rwkv_aB, rwkv_gB, rwkv_kk, rwkv_ka, rwkv_rk, rwkv_ln_w, rwkv_ln_b, w_out_even, w_in_odd, gla_gate_up, gla_gate_b, gla_norm, w_out_odd, ffn_gate, ffn_up, ffn_down):
    dt = x_prompt.dtype
    bf16 = jnp.bfloat16
    B, T = x_prompt.shape[:2]
    DB, S = x_sample.shape[:2]
    depth = norm_mix.shape[0]
    P = page_table.shape[1] * PAGE_SIZE
    WB = cache_nsa_win.shape[2]
    WP = min(WINDOW, T)
    pos_p = jnp.arange(T, dtype=jnp.int32)
    pos_s = P + jnp.arange(S, dtype=jnp.int32)
    win_pos_s = P - WB + jnp.arange(WB + S, dtype=jnp.int32)
    nsa_pad = -(-NSA_IN // LANE) * LANE
    y_p, y_s = x_prompt, x_sample
    kv_p, kv_s, win_p, win_s, rw_p, rw_s, sh_p, sh_s, gl_p, gl_s = ([] for _ in range(10))
    for layer in range(depth):
        li = layer // 2
        if layer % 2 == 0:
            w_in = jnp.concatenate([jnp.pad(w_in_even[li][:, :NSA_IN], ((0, 0), (0, nsa_pad - NSA_IN))),
                                    w_in_even[li][:, NSA_IN:]], axis=1).astype(bf16)
            even_splits = ((0, nsa_pad), (nsa_pad, nsa_pad + RWKV_IN))
            w_out = w_out_even[li].astype(bf16)
            hn_p, hr_p = norm_proj(y_p, norm_mix[layer], w_in, even_splits)
            hn_s, hr_s = norm_proj(y_s, norm_mix[layer], w_in, even_splits)
            q, g, rows, win = nsa_project(hn_p, pos_p, nsa_qnorm[li], nsa_knorm[li])
            o_nsa_p = nsa_prompt(q, g, rows, win, cmp_w1[li], cmp_w2[li], cmp_pe[li])
            kv_p.append(rows.astype(dt))
            win_p.append(win[:, T - WP:].astype(dt))
            q, g, rows, win = nsa_project(hn_s, pos_s, nsa_qnorm[li], nsa_knorm[li])
            past = cache_nsa_kv[li][page_table].reshape(DB, P, 4, NSA_KV, HEAD_DIM)
            win_all = jnp.concatenate([cache_nsa_win[li], win.astype(cache_nsa_win.dtype)], axis=1)
            o_nsa_s = nsa_sample(q, g, past, rows.astype(past.dtype), win_all, cmp_w1[li], cmp_w2[li], cmp_pe[li])
            kv_s.append(rows.astype(dt))
            win_s.append(win_all[:, S:].astype(dt))
            rw_par = (rwkv_mu[li], rwkv_w0[li], rwkv_wB[li], rwkv_a0[li], rwkv_aB[li], rwkv_gB[li],
                      rwkv_kk[li], rwkv_ka[li], rwkv_rk[li], rwkv_ln_w[li], rwkv_ln_b[li])
            o_rw_p, st, last = rwkv_mix(hr_p, jnp.zeros((B, RWKV_IN), dt),
                                        jnp.zeros((B, RWKV_HEADS, HEAD_DIM, HEAD_DIM), dt), *rw_par)
            rw_p.append(st.astype(dt))
            sh_p.append(last.astype(dt))
            o_rw_s, st, last = rwkv_mix(hr_s, state_rwkv_shift[li], state_rwkv[li], *rw_par)
            rw_s.append(st.astype(dt))
            sh_s.append(last.astype(dt))
            w_outs = [w_out[:NSA_Q], w_out[NSA_Q:]]
            y_p = proj_residual([o_nsa_p.astype(dt), o_rw_p.astype(dt)], w_outs, y_p)
            y_s = proj_residual([o_nsa_s.astype(dt), o_rw_s.astype(dt)], w_outs, y_s)
        else:
            wo = w_in_odd[li]
            o2 = 2 * GLA_KW + GLA_VW
            w_in = jnp.concatenate([wo[:, :o2], wo[:, o2 + GLA_RANK:],
                                    jnp.pad(wo[:, o2:o2 + GLA_RANK], ((0, 0), (0, LANE - GLA_RANK)))],
                                   axis=1).astype(bf16)
            odd_splits = ((0, 2 * GLA_KW), (2 * GLA_KW, o2), (o2, o2 + GLA_VW), (o2 + GLA_VW, o2 + GLA_VW + LANE))
            w_out = w_out_odd[li].astype(bf16)
            qk_p, v_p, r_p, gd_p = norm_proj(y_p, norm_mix[layer], w_in, odd_splits)
            qk_s, v_s, r_s, gd_s = norm_proj(y_s, norm_mix[layer], w_in, odd_splits)
            o_p, st = gla_mix_prompt(qk_p, v_p, gd_p, r_p, gla_gate_up[li], gla_gate_b[li], gla_norm[li])
            gl_p.append(st.astype(dt))
            o_s, st = gla_mix(qk_s, v_s, gd_s, r_s, state_gla[li], gla_gate_up[li], gla_gate_b[li], gla_norm[li])
            gl_s.append(st.astype(dt))
            y_p = proj_residual([o_p.astype(dt)], [w_out], y_p)
            y_s = proj_residual([o_s.astype(dt)], [w_out], y_s)
        wg, wu, wd = ffn_gate[layer].astype(bf16), ffn_up[layer].astype(bf16), ffn_down[layer].astype(bf16)
        y_p = ffn(y_p, norm_ffn[layer], wg, wu, wd)
        y_s = ffn(y_s, norm_ffn[layer], wg, wu, wd)
    return (y_p, y_s, jnp.stack(kv_p), jnp.stack(kv_s), jnp.stack(win_p), jnp.stack(win_s),
            jnp.stack(rw_p), jnp.stack(rw_s), jnp.stack(sh_p), jnp.stack(sh_s),
            jnp.stack(gl_p), jnp.stack(gl_s))
```

```python
import functools
import math

import jax
import jax.numpy as jnp
from jax import lax
from jax.experimental import pallas as pl
from jax.experimental.pallas import tpu as pltpu

D_MODEL = 1024
PAGE_SIZE = 128
HEAD_DIM = 64
NSA_HEADS = 8
NSA_KV = 2
NSA_REP = NSA_HEADS // NSA_KV
CMP_STRIDE = 16
CMP_BLOCK = 2 * CMP_STRIDE
CMP_HIDDEN = 2 * HEAD_DIM
SEL_BLOCK = 64
N_SELECT = 16
WINDOW = 512
Q_BLOCK = 128
ROPE_DIM = HEAD_DIM // 4
ROPE_THETA = 500000.0
RWKV_HEADS = 8
RWKV_WIDTH = RWKV_HEADS * HEAD_DIM
LORA_W = 64
LORA_A = 64
LORA_G = 128
RWKV_GN_EPS = 64e-5
GLA_HEADS = 4
GLA_DK = D_MODEL // 2 // GLA_HEADS
GLA_DV = D_MODEL // GLA_HEADS
GLA_RANK = 16
GLA_TAU = 16.0
GLA_CHUNK = 64
D_FF = ((8 * D_MODEL // 3 + 255) // 256) * 256
NSA_Q = NSA_HEADS * HEAD_DIM
NSA_KVW = 3 * 2 * NSA_KV * HEAD_DIM
NSA_IN = NSA_Q + NSA_KVW + 3 * NSA_HEADS
RWKV_IN = 3 * RWKV_WIDTH + LORA_W + LORA_A + LORA_G
EVEN_IN = NSA_IN + RWKV_IN
MIX_WIDTH = NSA_Q + RWKV_WIDTH
GLA_KW = GLA_HEADS * GLA_DK
GLA_VW = GLA_HEADS * GLA_DV
ODD_IN = 2 * GLA_KW + GLA_VW + GLA_RANK + GLA_VW

LANE = 128
VMEM_LIMIT = 48 * 1024 * 1024


ROW_TILE = 512
NORM_EPS = 1e-6


def _row_tile(m):
    return ROW_TILE if m % ROW_TILE == 0 else m


def _norm_bf16(x_ref, g_ref):
    x = x_ref[...]
    y = x * lax.rsqrt(jnp.mean(x * x, axis=-1, keepdims=True) + NORM_EPS)
    return (y * g_ref[...]).astype(jnp.bfloat16)


def _norm_proj_body(x_ref, g_ref, w_ref, *o_refs, splits):
    xn = _norm_bf16(x_ref, g_ref)
    for (a, b), o_ref in zip(splits, o_refs):
        o_ref[...] = jnp.dot(xn, w_ref[:, a:b], preferred_element_type=jnp.float32)


def norm_proj(x, gain, w, splits):
    bsz, t, k = x.shape
    m = bsz * t
    tm = _row_tile(m)
    outs = pl.pallas_call(
        functools.partial(_norm_proj_body, splits=splits),
        grid=(m // tm,),
        in_specs=[pl.BlockSpec((tm, k), lambda i: (i, 0)),
                  pl.BlockSpec((1, k), lambda i: (0, 0)),
                  pl.BlockSpec(w.shape, lambda i: (0, 0))],
        out_specs=[pl.BlockSpec((tm, b - a), lambda i: (i, 0)) for a, b in splits],
        out_shape=[jax.ShapeDtypeStruct((m, b - a), jnp.float32) for a, b in splits],
        compiler_params=pltpu.CompilerParams(dimension_semantics=("parallel",), vmem_limit_bytes=VMEM_LIMIT),
        name="norm_proj",
    )(x.reshape(m, k), gain.reshape(1, k), w)
    return [o.reshape(bsz, t, -1) for o in outs]


def _proj_res_body(*refs, n_in):
    x_refs, w_refs, res_ref, o_ref = refs[:n_in], refs[n_in:2 * n_in], refs[2 * n_in], refs[2 * n_in + 1]
    acc = res_ref[...]
    for x_ref, w_ref in zip(x_refs, w_refs):
        acc = acc + jnp.dot(x_ref[...].astype(jnp.bfloat16), w_ref[...], preferred_element_type=jnp.float32)
    o_ref[...] = acc


def proj_residual(xs, ws, res):
    bsz, t, n = res.shape
    m = bsz * t
    tm = _row_tile(m)
    out = pl.pallas_call(
        functools.partial(_proj_res_body, n_in=len(xs)),
        grid=(m // tm,),
        in_specs=([pl.BlockSpec((tm, x.shape[-1]), lambda i: (i, 0)) for x in xs]
                  + [pl.BlockSpec(w.shape, lambda i: (0, 0)) for w in ws]
                  + [pl.BlockSpec((tm, n), lambda i: (i, 0))]),
        out_specs=pl.BlockSpec((tm, n), lambda i: (i, 0)),
        out_shape=jax.ShapeDtypeStruct((m, n), jnp.float32),
        compiler_params=pltpu.CompilerParams(dimension_semantics=("parallel",), vmem_limit_bytes=VMEM_LIMIT),
        name="proj_residual",
    )(*[x.reshape(m, x.shape[-1]) for x in xs], *ws, res.reshape(m, n))
    return out.reshape(bsz, t, n)


FFN_COL_CHUNKS = 2


def _ffn_up_body(x_ref, g_ref, wg_ref, wu_ref, h_ref):
    xn = _norm_bf16(x_ref, g_ref)
    cw = h_ref.shape[1] // FFN_COL_CHUNKS
    for c in range(FFN_COL_CHUNKS):
        cols = slice(c * cw, (c + 1) * cw)
        g = jnp.dot(xn, wg_ref[:, cols], preferred_element_type=jnp.float32)
        u = jnp.dot(xn, wu_ref[:, cols], preferred_element_type=jnp.float32)
        h_ref[:, cols] = (g * jax.nn.sigmoid(g) * u).astype(h_ref.dtype)


def ffn(y, gain, wg, wu, wd):
    bsz, t, k = y.shape
    m = bsz * t
    tm = _row_tile(m)
    f = wg.shape[1]
    h = pl.pallas_call(
        _ffn_up_body,
        grid=(m // tm,),
        in_specs=[pl.BlockSpec((tm, k), lambda i: (i, 0)),
                  pl.BlockSpec((1, k), lambda i: (0, 0)),
                  pl.BlockSpec(wg.shape, lambda i: (0, 0)),
                  pl.BlockSpec(wu.shape, lambda i: (0, 0))],
        out_specs=pl.BlockSpec((tm, f), lambda i: (i, 0)),
        out_shape=jax.ShapeDtypeStruct((m, f), jnp.bfloat16),
        compiler_params=pltpu.CompilerParams(dimension_semantics=("parallel",), vmem_limit_bytes=VMEM_LIMIT),
        name="ffn_up",
    )(y.reshape(m, k), gain.reshape(1, k), wg, wu)
    return proj_residual([h.reshape(bsz, t, f)], [wd], y)


def rmsnorm(x, g, eps=1e-6):
    xf = x.astype(jnp.float32)
    y = xf * lax.rsqrt(jnp.mean(xf * xf, axis=-1, keepdims=True) + eps)
    return (y * g.astype(jnp.float32)).astype(x.dtype)


def rope(x, pos):
    half = ROPE_DIM // 2
    inv = ROPE_THETA ** (-jnp.arange(half, dtype=jnp.float32) / half)
    ang = pos.astype(jnp.float32)[:, None] * inv[None, :]
    shp = (pos.shape[0],) + (1,) * (x.ndim - 3) + (half,)
    cos, sin = jnp.cos(ang).reshape(shp), jnp.sin(ang).reshape(shp)
    xr = x[..., :ROPE_DIM].astype(jnp.float32)
    x1, x2 = xr[..., :half], xr[..., half:]
    rot = jnp.concatenate([x1 * cos - x2 * sin, x2 * cos + x1 * sin], axis=-1)
    return jnp.concatenate([rot.astype(x.dtype), x[..., ROPE_DIM:]], axis=-1)


def masked_softmax(s, mask):
    p = jax.nn.softmax(jnp.where(mask, s, -1e30), axis=-1)
    return jnp.where(mask, p, 0.0)


def nsa_project(h, pos, qn, kn):
    B, T = h.shape[:2]
    q = h[..., :NSA_Q].reshape(B, T, NSA_HEADS, HEAD_DIM)
    kv = h[..., NSA_Q:NSA_Q + NSA_KVW].reshape(B, T, 3, 2, NSA_KV, HEAD_DIM)
    gates = jax.nn.sigmoid(h[..., NSA_Q + NSA_KVW:NSA_IN].astype(jnp.float32)).reshape(B, T, NSA_HEADS, 3)
    q = rope(rmsnorm(q, qn), pos)
    k = rope(rmsnorm(kv[:, :, :, 0], kn[:, None, :]), pos)
    kv = jnp.stack([k, kv[:, :, :, 1]], axis=3)
    rows = kv[:, :, :2].reshape(B, T, 4, NSA_KV, HEAD_DIM)
    win = kv[:, :, 2]
    return q, gates, rows, win


def compress(k, w1, w2, pe):
    B, L = k.shape[:2]
    n_chunks = L // CMP_STRIDE
    c = k[:, :n_chunks * CMP_STRIDE].reshape(B, n_chunks, CMP_STRIDE, NSA_KV, HEAD_DIM)
    h = (jnp.einsum('bnlgd,ldh->bngh', c[:, :-1], w1[:CMP_STRIDE])
         + jnp.einsum('bnlgd,ldh->bngh', c[:, 1:], w1[CMP_STRIDE:])
         + jnp.einsum('ld,ldh->h', pe, w1))
    return jnp.einsum('bngh,hd->bngd', jax.nn.gelu(h), w2)


def nsa_keys(rows, cw1, cw2, cpe):
    B, L = rows.shape[:2]
    kc = compress(rows[:, :, 0], cw1[0], cw2[0], cpe[0])
    vc = compress(rows[:, :, 1], cw1[1], cw2[1], cpe[1])
    n_cmp = kc.shape[1]
    cmp_start = jnp.arange(n_cmp, dtype=jnp.int32) * CMP_STRIDE
    cmp_end = cmp_start + CMP_BLOCK - 1
    n_sel = -(-L // SEL_BLOCK)
    sel_start = jnp.arange(n_sel, dtype=jnp.int32) * SEL_BLOCK
    sel_map = ((cmp_start[:, None] <= sel_start[None, :] + SEL_BLOCK - 1)
               & (cmp_end[:, None] >= sel_start[None, :])).astype(jnp.float32)
    slc = jnp.pad(rows[:, :, 2:4], ((0, 0), (0, n_sel * SEL_BLOCK - L), (0, 0), (0, 0), (0, 0)))
    slc = slc.reshape(B, n_sel, SEL_BLOCK, 2, NSA_KV, HEAD_DIM).transpose(3, 0, 4, 1, 2, 5)
    return kc, vc, cmp_end, sel_map, slc[0], slc[1]


def nsa_block(q, gates, q_pos, kc, vc, cmp_end, sel_map, ksb, vsb, win, w_pos):
    f32 = jnp.float32
    B, QB = q.shape[:2]
    qg = q.reshape(B, QB, NSA_KV, NSA_REP, HEAD_DIM).astype(f32) * (HEAD_DIM ** -0.5)
    tq = q_pos[:, None]
    s = jnp.einsum('bqgrd,bngd->bqgrn', qg, kc.astype(f32))
    p = masked_softmax(s, (cmp_end[None, :] <= tq)[None, :, None, None, :])
    o_cmp = jnp.einsum('bqgrn,bngd->bqgrd', p, vc.astype(f32))
    imp = jnp.einsum('bqgrn,nj->bqgj', p, sel_map)
    n_sel = sel_map.shape[1]
    blk = jnp.arange(n_sel, dtype=jnp.int32)[None, :]
    cur = tq // SEL_BLOCK
    forced = (blk == 0) | (blk == cur) | (blk == cur - 1)
    valid = blk * SEL_BLOCK <= tq
    imp = jnp.where(valid[None, :, None, :], jnp.where(forced[None, :, None, :], 1e30, imp), -1e30)
    n_top = min(N_SELECT, n_sel)
    _, idx = lax.top_k(imp, n_top)
    bi = jnp.arange(B)[:, None, None, None]
    gi = jnp.arange(NSA_KV)[None, None, :, None]
    nk = n_top * SEL_BLOCK
    k_sel = ksb[bi, gi, idx].reshape(B, QB, NSA_KV, nk, HEAD_DIM).astype(f32)
    v_sel = vsb[bi, gi, idx].reshape(B, QB, NSA_KV, nk, HEAD_DIM).astype(f32)
    k_pos = (idx[..., None] * SEL_BLOCK + jnp.arange(SEL_BLOCK, dtype=jnp.int32)).reshape(B, QB, NSA_KV, nk)
    s = jnp.einsum('bqgrd,bqgkd->bqgrk', qg, k_sel)
    p = masked_softmax(s, (k_pos <= q_pos[None, :, None, None])[:, :, :, None, :])
    o_slc = jnp.einsum('bqgrk,bqgkd->bqgrd', p, v_sel)
    s = jnp.einsum('bqgrd,bkgd->bqgrk', qg, win[:, :, 0].astype(f32))
    wp = w_pos[None, :]
    m = (wp <= tq) & (wp > tq - WINDOW) & (wp >= 0)
    p = masked_softmax(s, m[None, :, None, None, :])
    o_win = jnp.einsum('bqgrk,bkgd->bqgrd', p, win[:, :, 1].astype(f32))
    g = gates.reshape(B, QB, NSA_KV, NSA_REP, 3)
    o = g[..., 0:1] * o_cmp + g[..., 1:2] * o_slc + g[..., 2:3] * o_win
    return o.reshape(B, QB, NSA_Q)


NEG = -1e30


def _nsa_cmp_body(q_ref, kt_ref, v_ref, map_ref, o_ref, sel_ref, *, tq, n_cmp, n_sel):
    f32, bf16 = jnp.float32, jnp.bfloat16
    i = pl.program_id(1)
    ncp = kt_ref.shape[2]
    pos = i * tq + lax.broadcasted_iota(jnp.int32, (tq, 1), 0)
    n = lax.broadcasted_iota(jnp.int32, (1, ncp), 1)
    ok = (n * CMP_STRIDE + (CMP_BLOCK - 1) <= pos) & (n < n_cmp)
    psum = jnp.zeros((tq, ncp), f32)
    for r in range(NSA_REP):
        s = jnp.dot(q_ref[0, r], kt_ref[0], preferred_element_type=f32)
        s = jnp.where(ok, s, NEG)
        m = jnp.max(s, axis=-1, keepdims=True)
        e = jnp.where(ok, jnp.exp(s - m), 0.0)
        l = jnp.sum(e, axis=-1, keepdims=True)
        p = e * jnp.where(l > 0.0, 1.0 / l, 0.0)
        o_ref[0, r] = jnp.dot(p.astype(bf16), v_ref[0], preferred_element_type=f32)
        psum = psum + p
    hi = psum.astype(bf16)
    lo = (psum - hi.astype(f32)).astype(bf16)
    imp = (jnp.dot(hi, map_ref[...], preferred_element_type=f32)
           + jnp.dot(lo, map_ref[...], preferred_element_type=f32))
    nsp = map_ref.shape[1]
    blk = lax.broadcasted_iota(jnp.int32, (1, nsp), 1)
    cur = pos // SEL_BLOCK
    valid = (blk * SEL_BLOCK <= pos) & (blk < n_sel)
    forced = (blk == 0) | (blk == cur) | (blk == cur - 1)
    x = jnp.where(valid, jnp.where(forced, 1e30, imp), NEG)
    x = jnp.where(blk < n_sel, x, -3e38)
    blkf = blk.astype(f32)
    sel = jnp.zeros((tq, nsp), f32)
    for _ in range(min(N_SELECT, n_sel)):
        m = jnp.max(x, axis=-1, keepdims=True)
        first = jnp.min(jnp.where(x == m, blkf, float(nsp)), axis=-1, keepdims=True)
        hit = blkf == first
        sel = jnp.where(hit, 1.0, sel)
        x = jnp.where(hit, -3e38, x)
    sel_ref[0] = jnp.where(valid, sel, 0.0).astype(bf16)


def nsa_cmp_select(q4, kct, vc, sel_map, n_cmp, n_sel, *, tq=128):
    BG, _, T, _ = q4.shape
    ncp, nsp = sel_map.shape
    return pl.pallas_call(
        functools.partial(_nsa_cmp_body, tq=tq, n_cmp=n_cmp, n_sel=n_sel),
        grid=(BG, T // tq),
        in_specs=[pl.BlockSpec((1, NSA_REP, tq, HEAD_DIM), lambda b, i: (b, 0, i, 0)),
                  pl.BlockSpec((1, HEAD_DIM, ncp), lambda b, i: (b, 0, 0)),
                  pl.BlockSpec((1, ncp, HEAD_DIM), lambda b, i: (b, 0, 0)),
                  pl.BlockSpec((ncp, nsp), lambda b, i: (0, 0))],
        out_specs=[pl.BlockSpec((1, NSA_REP, tq, HEAD_DIM), lambda b, i: (b, 0, i, 0)),
                   pl.BlockSpec((1, tq, nsp), lambda b, i: (b, i, 0))],
        out_shape=[jax.ShapeDtypeStruct((BG, NSA_REP, T, HEAD_DIM), jnp.float32),
                   jax.ShapeDtypeStruct((BG, T, nsp), jnp.bfloat16)],
        compiler_params=pltpu.CompilerParams(
            dimension_semantics=("parallel", "parallel"), vmem_limit_bytes=VMEM_LIMIT),
        name="nsa_cmp_select",
    )(q4, kct, vc, sel_map)


def _nsa_flash_body(*refs, tq, tk, mode, key_off):
    f32, bf16 = jnp.float32, jnp.bfloat16
    if mode == "select":
        q_ref, kt_ref, v_ref, sel_ref, o_ref, m_scr, l_scr, acc_scr = refs
    else:
        q_ref, kt_ref, v_ref, o_ref, m_scr, l_scr, acc_scr = refs
    i, j = pl.program_id(1), pl.program_id(2)
    nj = pl.num_programs(2)

    @pl.when(j == 0)
    def _():
        m_scr[...] = jnp.full_like(m_scr, NEG)
        l_scr[...] = jnp.zeros_like(l_scr)
        acc_scr[...] = jnp.zeros_like(acc_scr)

    qpos = i * tq + lax.broadcasted_iota(jnp.int32, (tq, 1), 0)
    lane = lax.broadcasted_iota(jnp.int32, (1, tk), 1)
    if mode == "select":
        k0 = j * tk
        active = k0 <= i * tq + (tq - 1)
    else:
        k0 = (i * tq // tk + j) * tk - key_off
        active = j >= 0
    kpos = k0 + lane

    @pl.when(active)
    def _():
        if mode == "select":
            nsp = sel_ref.shape[2]
            c = lax.broadcasted_iota(jnp.int32, (nsp, tk), 0)
            l2 = lax.broadcasted_iota(jnp.int32, (nsp, tk), 1)
            expand = (c == (k0 + l2) // SEL_BLOCK).astype(bf16)
            picked = jnp.dot(sel_ref[0], expand, preferred_element_type=f32)
            allowed = (picked > 0.5) & (kpos <= qpos)
        else:
            allowed = (kpos <= qpos) & (kpos > qpos - WINDOW) & (kpos >= 0)
        for r in range(NSA_REP):
            s = jnp.dot(q_ref[0, r], kt_ref[0], preferred_element_type=f32)
            s = jnp.where(allowed, s, NEG)
            m_old = m_scr[r]
            m_new = jnp.maximum(m_old, jnp.max(s, axis=-1, keepdims=True))
            alpha = jnp.exp(m_old - m_new)
            p = jnp.where(allowed, jnp.exp(s - m_new), 0.0)
            l_scr[r] = alpha * l_scr[r] + jnp.sum(p, axis=-1, keepdims=True)
            acc_scr[r] = alpha * acc_scr[r] + jnp.dot(p.astype(bf16), v_ref[0], preferred_element_type=f32)
            m_scr[r] = m_new

    @pl.when(j == nj - 1)
    def _():
        for r in range(NSA_REP):
            o_ref[0, r] = acc_scr[r] / l_scr[r]


def nsa_flash(q4, kt, v, sel=None, *, mode, tq, tk, key_off=0):
    BG, _, T, _ = q4.shape
    if mode == "select":
        nj = T // tk
        kidx = lambda b, i, j: jnp.minimum(j, (i * tq + tq - 1) // tk)
    else:
        nj = (key_off + tq) // tk
        kidx = lambda b, i, j: i * tq // tk + j
    in_specs = [pl.BlockSpec((1, NSA_REP, tq, HEAD_DIM), lambda b, i, j: (b, 0, i, 0)),
                pl.BlockSpec((1, HEAD_DIM, tk), lambda b, i, j: (b, 0, kidx(b, i, j))),
                pl.BlockSpec((1, tk, HEAD_DIM), lambda b, i, j: (b, kidx(b, i, j), 0))]
    args = [q4, kt, v]
    if mode == "select":
        in_specs.append(pl.BlockSpec((1, tq, sel.shape[2]), lambda b, i, j: (b, i, 0)))
        args.append(sel)
    return pl.pallas_call(
        functools.partial(_nsa_flash_body, tq=tq, tk=tk, mode=mode, key_off=key_off),
        grid=(BG, T // tq, nj),
        in_specs=in_specs,
        out_specs=pl.BlockSpec((1, NSA_REP, tq, HEAD_DIM), lambda b, i, j: (b, 0, i, 0)),
        out_shape=jax.ShapeDtypeStruct((BG, NSA_REP, T, HEAD_DIM), jnp.float32),
        scratch_shapes=[pltpu.VMEM((NSA_REP, tq, 1), jnp.float32),
                        pltpu.VMEM((NSA_REP, tq, 1), jnp.float32),
                        pltpu.VMEM((NSA_REP, tq, HEAD_DIM), jnp.float32)],
        compiler_params=pltpu.CompilerParams(
            dimension_semantics=("parallel", "parallel", "arbitrary"), vmem_limit_bytes=VMEM_LIMIT),
        name="nsa_flash_" + mode,
    )(*args)


def nsa_prompt(q, gates, rows, win, cw1, cw2, cpe):
    bf16 = jnp.bfloat16
    B, T = q.shape[:2]
    G = NSA_KV
    kc = compress(rows[:, :, 0], cw1[0], cw2[0], cpe[0])
    vc = compress(rows[:, :, 1], cw1[1], cw2[1], cpe[1])
    n_cmp = kc.shape[1]
    n_sel = T // SEL_BLOCK
    ncp = -(-n_cmp // LANE) * LANE
    nsp = -(-n_sel // LANE) * LANE
    n = jnp.arange(ncp)[:, None]
    j = jnp.arange(nsp)[None, :]
    sel_map = ((n * CMP_STRIDE <= j * SEL_BLOCK + SEL_BLOCK - 1) & (n * CMP_STRIDE + CMP_BLOCK - 1 >= j * SEL_BLOCK)
               & (n < n_cmp) & (j < n_sel)).astype(bf16)
    q4 = (q * (HEAD_DIM ** -0.5)).reshape(B, T, G, NSA_REP, HEAD_DIM).transpose(0, 2, 3, 1, 4)
    q4 = q4.reshape(B * G, NSA_REP, T, HEAD_DIM).astype(bf16)

    def keys_t(x, front=0, back=0):
        x = jnp.pad(x, ((0, 0), (front, back), (0, 0), (0, 0)))
        return x.transpose(0, 2, 3, 1).reshape(B * G, HEAD_DIM, -1).astype(bf16)

    def vals(x, front=0, back=0):
        x = jnp.pad(x, ((0, 0), (front, back), (0, 0), (0, 0)))
        return x.transpose(0, 2, 1, 3).reshape(B * G, -1, HEAD_DIM).astype(bf16)

    o_cmp, sel = nsa_cmp_select(q4, keys_t(kc, 0, ncp - n_cmp), vals(vc, 0, ncp - n_cmp), sel_map, n_cmp, n_sel)
    o_slc = nsa_flash(q4, keys_t(rows[:, :, 2]), vals(rows[:, :, 3]), sel, mode="select", tq=128, tk=512)
    o_win = nsa_flash(q4, keys_t(win[:, :, 0], WINDOW), vals(win[:, :, 1], WINDOW), mode="window",
                      tq=256, tk=256, key_off=WINDOW)
    g = gates.reshape(B, T, G, NSA_REP, 3)
    un = lambda o: o.reshape(B, G, NSA_REP, T, HEAD_DIM).transpose(0, 3, 1, 2, 4)
    o = g[..., 0:1] * un(o_cmp) + g[..., 1:2] * un(o_slc) + g[..., 2:3] * un(o_win)
    return o.reshape(B, T, NSA_Q)


NSA_TQ = 128
NSA_TK = 512
NSA_TQ_DECODE = 32


def _nsa_attn_body(qt_ref, kc_ref, vct_ref, map_ref, ks_ref, vst_ref, kw_ref, vw_ref, g_ref, o_ref, thr_scr,
                   *, tq, n_cmp, n_sel, q_tile0, win_base, natural):
    f32, bf16 = jnp.float32, jnp.bfloat16
    tk = NSA_TK
    nl = NSA_REP * tq
    i = pl.program_id(1) + q_tile0
    if natural:
        xq = qt_ref[0] * (HEAD_DIM ** -0.5)
        qt = jnp.concatenate([xq[:, HEAD_DIM * r:HEAD_DIM * (r + 1)].T for r in range(NSA_REP)], axis=1).astype(bf16)
    else:
        qt = qt_ref[0, 0]
    qpos = i * tq + lax.broadcasted_iota(jnp.int32, (1, nl), 1) % tq

    ncp = kc_ref.shape[1]
    n = lax.broadcasted_iota(jnp.int32, (ncp, 1), 0)
    ok = (n * CMP_STRIDE + (CMP_BLOCK - 1) <= qpos) & (n < n_cmp)
    s = jnp.dot(kc_ref[0], qt, preferred_element_type=f32)
    s = jnp.where(ok, s, NEG)
    m = jnp.max(s, axis=0, keepdims=True)
    e = jnp.where(ok, jnp.exp(s - m), 0.0)
    l = jnp.sum(e, axis=0, keepdims=True)
    p = e * jnp.where(l > 0.0, 1.0 / l, 0.0)
    o_cmp = jnp.dot(vct_ref[0], p.astype(bf16), preferred_element_type=f32)

    psum = p[:, 0:tq]
    for r in range(1, NSA_REP):
        psum = psum + p[:, r * tq:(r + 1) * tq]
    hi = psum.astype(bf16)
    lo = (psum - hi.astype(f32)).astype(bf16)
    imp = (jnp.dot(map_ref[...], hi, preferred_element_type=f32)
           + jnp.dot(map_ref[...], lo, preferred_element_type=f32))
    nsp = map_ref.shape[0]
    qp = qpos[:, 0:tq]
    blk = lax.broadcasted_iota(jnp.int32, (nsp, 1), 0)
    cur = qp // SEL_BLOCK
    valid = (blk * SEL_BLOCK <= qp) & (blk < n_sel)
    forced = (blk == 0) | (blk == cur) | (blk == cur - 1)
    x = jnp.where(valid, jnp.where(forced, 1e30, imp), NEG)
    x = jnp.where(blk < n_sel, x, -3e38)
    blkf = blk.astype(f32)
    sel = jnp.zeros((nsp, tq), f32)
    for _ in range(min(N_SELECT, n_sel)):
        mx = jnp.max(x, axis=0, keepdims=True)
        first = jnp.min(jnp.where(x == mx, blkf, float(nsp)), axis=0, keepdims=True)
        hit = blkf == first
        sel = jnp.where(hit, 1.0, sel)
        x = jnp.where(hit, -3e38, x)
    thr = jnp.where(valid & (sel > 0.5), qp, -1)
    thr_scr[...] = jnp.concatenate([thr] * NSA_REP, axis=1)

    def flash(s, allowed, vt, carry):
        m_old, l_old, acc = carry
        m_new = jnp.maximum(m_old, jnp.max(s, axis=0, keepdims=True))
        alpha = jnp.exp(m_old - m_new)
        p = jnp.exp(s - m_new)
        if allowed is not None:
            p = jnp.where(allowed, p, 0.0)
        l_new = alpha * l_old + jnp.sum(p, axis=0, keepdims=True)
        acc = alpha * acc + jnp.dot(vt, p.astype(bf16), preferred_element_type=f32)
        return m_new, l_new, acc

    init = (jnp.full((1, nl), NEG, f32), jnp.zeros((1, nl), f32), jnp.zeros((HEAD_DIM, nl), f32))

    krow = lax.broadcasted_iota(jnp.int32, (SEL_BLOCK, 1), 0)

    def sel_step(j, carry):
        s = jnp.dot(ks_ref[0, j], qt, preferred_element_type=f32)
        th = thr_scr[pl.ds(pl.multiple_of(j * (tk // SEL_BLOCK), 8), tk // SEL_BLOCK), :]
        s = jnp.concatenate(
            [jnp.where((j * tk + b * SEL_BLOCK + krow) <= th[b:b + 1], s[b * SEL_BLOCK:(b + 1) * SEL_BLOCK], NEG)
             for b in range(tk // SEL_BLOCK)], axis=0)
        return flash(s, None, vst_ref[0, j], carry)

    _, l_s, acc_s = lax.fori_loop(0, (i * tq + tq - 1) // tk + 1, sel_step, init)
    o_slc = acc_s / l_s

    nkw = WINDOW + tq
    k_first = i * tq - WINDOW
    row0 = pl.multiple_of(k_first - win_base, 16)
    kpos = k_first + lax.broadcasted_iota(jnp.int32, (nkw, 1), 0)
    allowed = (kpos <= qpos) & (kpos > qpos - WINDOW) & (kpos >= 0)
    s = jnp.dot(kw_ref[0, pl.ds(row0, nkw), :], qt, preferred_element_type=f32)
    s = jnp.where(allowed, s, NEG)
    e = jnp.where(allowed, jnp.exp(s - jnp.max(s, axis=0, keepdims=True)), 0.0)
    o_win = (lax.dot_general(vw_ref[0, pl.ds(row0, nkw), :], e.astype(bf16), (((0,), (0,)), ((), ())),
                             preferred_element_type=f32) / jnp.sum(e, axis=0, keepdims=True))

    g = g_ref[0, 0]
    res = g[0:1] * o_cmp + g[1:2] * o_slc + g[2:3] * o_win
    if natural:
        o_ref[0] = jnp.concatenate([res[:, tq * r:tq * (r + 1)].T for r in range(NSA_REP)], axis=1)
    else:
        o_ref[0, 0] = res


def nsa_attention(q, gates, kc, vc, ksl, vsl, kwin, vwin, *, n_keys, tq=NSA_TQ, q_pos0=0, win_pos0=0):
    f32, bf16 = jnp.float32, jnp.bfloat16
    B, T = q.shape[:2]
    G, R, D = NSA_KV, NSA_REP, HEAD_DIM
    tk = NSA_TK
    nq, nl = T // tq, R * tq
    n_cmp, n_sel = kc.shape[1], -(-n_keys // SEL_BLOCK)
    lk = ksl.shape[1]
    assert T % tq == 0 and nl % LANE == 0 and lk % tk == 0 and q_pos0 % tq == 0 and tq % 16 == 0
    assert lk >= q_pos0 + T and win_pos0 <= max(q_pos0 - WINDOW, 0)
    win_base = win_pos0 - WINDOW
    lw = q_pos0 + T - win_base
    wpad = lambda x: jnp.pad(x[:, :lw - WINDOW], ((0, 0), (WINDOW, max(lw - WINDOW - x.shape[1], 0)), (0, 0), (0, 0)))
    kwin, vwin = wpad(kwin), wpad(vwin)
    ncp = -(-n_cmp // LANE) * LANE
    nsp = -(-n_sel // LANE) * LANE
    n = jnp.arange(ncp)[None, :]
    jb = jnp.arange(nsp)[:, None]
    map_t = ((n * CMP_STRIDE <= jb * SEL_BLOCK + SEL_BLOCK - 1) & (n * CMP_STRIDE + CMP_BLOCK - 1 >= jb * SEL_BLOCK)
             & (n < n_cmp) & (jb < n_sel)).astype(bf16)
    natural = tq == LANE
    if natural:
        qt = q.reshape(B, T, NSA_Q)
        q_spec = pl.BlockSpec((1, tq, R * D), lambda b, i: (b // G, i, b % G))
        o_spec, o_shape = q_spec, jax.ShapeDtypeStruct((B, T, NSA_Q), f32)
    else:
        qt = (q * (D ** -0.5)).reshape(B, nq, tq, G, R, D).transpose(0, 3, 1, 5, 4, 2)
        qt = qt.reshape(B * G, nq, D, nl).astype(bf16)
        q_spec = pl.BlockSpec((1, 1, D, nl), lambda b, i: (b, i, 0, 0))
        o_spec, o_shape = q_spec, jax.ShapeDtypeStruct((B * G, nq, D, nl), f32)
    gt = gates.reshape(B, nq, tq, G, R, 3).transpose(0, 3, 1, 5, 4, 2).reshape(B * G, nq, 3, nl)
    gt = jnp.pad(gt, ((0, 0), (0, 0), (0, 5), (0, 0)))
    rows = lambda x, t: x.transpose(0, 2, 1, 3).reshape(B * G, -1, t, D).astype(bf16)
    cols = lambda x, t: x.reshape(B, -1, t, G, D).transpose(0, 3, 1, 4, 2).reshape(B * G, -1, D, t).astype(bf16)
    kcp = jnp.pad(kc, ((0, 0), (0, ncp - n_cmp), (0, 0), (0, 0)))
    vcp = jnp.pad(vc, ((0, 0), (0, ncp - n_cmp), (0, 0), (0, 0)))
    kc_r = rows(kcp, ncp)[:, 0]
    vc_c = cols(vcp, ncp)[:, 0]
    full = lambda shape: pl.BlockSpec((1,) + shape, lambda b, i: (b,) + (0,) * len(shape))
    per_q = lambda shape: pl.BlockSpec((1, 1) + shape, lambda b, i: (b, i) + (0,) * len(shape))
    out = pl.pallas_call(
        functools.partial(_nsa_attn_body, tq=tq, n_cmp=n_cmp, n_sel=n_sel,
                          q_tile0=q_pos0 // tq, win_base=win_base, natural=natural),
        grid=(B * G, nq),
        in_specs=[q_spec, full((ncp, D)), full((D, ncp)),
                  pl.BlockSpec((nsp, ncp), lambda b, i: (0, 0)),
                  full((lk // tk, tk, D)), full((lk // tk, D, tk)),
                  full((lw, D)), full((lw, D)),
                  per_q((8, nl))],
        out_specs=o_spec,
        out_shape=o_shape,
        scratch_shapes=[pltpu.VMEM((nsp, nl), jnp.int32)],
        compiler_params=pltpu.CompilerParams(
            dimension_semantics=("parallel", "parallel"), vmem_limit_bytes=VMEM_LIMIT),
        name="nsa_attention",
    )(qt, kc_r, vc_c, map_t, rows(ksl, tk), cols(vsl, tk), rows(kwin, lw)[:, 0], rows(vwin, lw)[:, 0], gt)
    if natural:
        return out
    return out.reshape(B, G, nq, D, R, tq).transpose(0, 2, 5, 1, 4, 3).reshape(B, T, NSA_Q)


def nsa_prompt(q, gates, rows, win, cw1, cw2, cpe):
    kc = compress(rows[:, :, 0], cw1[0], cw2[0], cpe[0])
    vc = compress(rows[:, :, 1], cw1[1], cw2[1], cpe[1])
    return nsa_attention(q, gates, kc, vc, rows[:, :, 2], rows[:, :, 3], win[:, :, 0], win[:, :, 1],
                         n_keys=q.shape[1])


def nsa_sample(q, gates, past, rows, win_all, cw1, cw2, cpe):
    DB, S = q.shape[:2]
    P, WB = past.shape[1], win_all.shape[1] - S
    up = lambda n, m: -(-n // m) * m
    pad_rows = lambda x, n: jnp.pad(x, ((0, 0), (0, n - x.shape[1])) + ((0, 0),) * (x.ndim - 2))
    n_chunk_rows = (P + S) // CMP_STRIDE * CMP_STRIDE
    cmp_in = [past[:, :n_chunk_rows, w] if n_chunk_rows <= P
              else jnp.concatenate([past[:, :, w], rows[:, :n_chunk_rows - P, w]], axis=1) for w in (0, 1)]
    kc = compress(cmp_in[0], cw1[0], cw2[0], cpe[0])
    vc = compress(cmp_in[1], cw1[1], cw2[1], cpe[1])
    tq = up(S, NSA_TQ_DECODE)
    lk = up(P + tq, NSA_TK)
    ksl = pad_rows(jnp.concatenate([past[:, :, 2], rows[:, :, 2]], axis=1), lk)
    vsl = pad_rows(jnp.concatenate([past[:, :, 3], rows[:, :, 3]], axis=1), lk)
    o = nsa_attention(pad_rows(q, tq), pad_rows(gates, tq), kc, vc, ksl, vsl, win_all[:, :, 0], win_all[:, :, 1],
                      n_keys=P + S, tq=tq, q_pos0=P, win_pos0=P - WB)
    return o[:, :S]


RW_PAIRS = RWKV_HEADS // 2
RW_CHUNK = 64


def _rwkv_scan_body(r_ref, w_ref, k_ref, kk_ref, b_ref, vt_ref, s0_ref, oh_ref,
                    y_ref, st_ref, vhi_scr, vmid_scr, *, nb, tc):
    f32, bf16 = jnp.float32, jnp.bfloat16
    c = pl.program_id(1)

    @pl.when(c == 0)
    def _():
        st_ref[...] = s0_ref[...]

    row = lax.broadcasted_iota(jnp.int32, (2 * LANE, LANE), 0)
    col = lax.broadcasted_iota(jnp.int32, (2 * LANE, LANE), 1)
    ones2 = (((row // HEAD_DIM) % 2) == (col // HEAD_DIM)).astype(bf16)
    ones1 = ones2[:LANE]
    lane_t = lax.broadcasted_iota(jnp.int32, (HEAD_DIM, LANE), 1) % HEAD_DIM

    vt = vt_ref[...].reshape(nb * RW_PAIRS, HEAD_DIM, LANE)
    vhi = vt.astype(bf16)
    vhi_scr[...] = vhi
    vmid_scr[...] = (vt - vhi.astype(f32)).astype(bf16)
    y_ref[...] = jnp.zeros_like(y_ref)

    def group(t8, carry):
        t0 = pl.multiple_of(t8 * 8, 8)
        for j in range(8):
            t = t0 + j
            row = lambda ref, b, sl: ref[b, pl.ds(t0, 8), sl][j:j + 1]
            oh = oh_ref[t]
            m = lane_t == t
            pairs = [(b, p) for b in range(nb) for p in range(RW_PAIRS)]
            vl = jnp.concatenate([jnp.concatenate([vhi_scr[i] * oh, vmid_scr[i] * oh], axis=1)
                                  for i in range(len(pairs))], axis=0)
            vb_all = jnp.dot(vl, ones2, preferred_element_type=f32)
            py = []
            for b in range(nb):
                pieces = []
                for p in range(RW_PAIRS):
                    pk = st_ref[b, p] * row(kk_ref, b, slice(LANE * p, LANE * (p + 1)))
                    hi = pk.astype(bf16)
                    pieces.append(jnp.concatenate([hi, (pk - hi.astype(f32)).astype(bf16)], axis=1))
                sa_b = jnp.dot(jnp.concatenate(pieces, axis=0), ones2, preferred_element_type=f32)
                for p in range(RW_PAIRS):
                    sl = slice(LANE * p, LANE * (p + 1))
                    i = b * RW_PAIRS + p
                    s_new = (st_ref[b, p] * row(w_ref, b, sl)
                             - sa_b[HEAD_DIM * p:HEAD_DIM * (p + 1)] * row(b_ref, b, sl)
                             + vb_all[HEAD_DIM * i:HEAD_DIM * (i + 1)] * row(k_ref, b, sl))
                    st_ref[b, p] = s_new
                    py.append((s_new * row(r_ref, b, sl)).astype(bf16))
            y_all = jnp.dot(jnp.concatenate(py, axis=0), ones1, preferred_element_type=f32)
            for i, (b, p) in enumerate(pairs):
                y_ref[b, 0, p] = jnp.where(m, y_all[HEAD_DIM * i:HEAD_DIM * (i + 1)], y_ref[b, 0, p])
        return carry

    lax.fori_loop(0, tc // 8, group, 0)


def rwkv_scan(r, w, k, v, kk, b, S0):
    f32 = jnp.float32
    B, T, W = r.shape
    tc = RW_CHUNK if T % RW_CHUNK == 0 else T
    assert tc <= RW_CHUNK and T % tc == 0 and tc % 8 == 0 and B % 2 == 0
    nc, nb = T // tc, 2
    vt = v.reshape(B, nc, tc, RW_PAIRS, 2, HEAD_DIM).transpose(0, 1, 3, 5, 4, 2)
    vt = jnp.pad(vt, ((0, 0),) * 5 + ((0, RW_CHUNK - tc),)).reshape(B, nc, RW_PAIRS, HEAD_DIM, LANE)
    s0 = S0.astype(f32).reshape(B, RW_PAIRS, 2, HEAD_DIM, HEAD_DIM).transpose(0, 1, 3, 2, 4)
    s0 = s0.reshape(B, RW_PAIRS, HEAD_DIM, LANE)
    oh = jnp.arange(LANE)[None, None, :] % HEAD_DIM == jnp.arange(RW_CHUNK)[:, None, None]
    oh = jnp.broadcast_to(oh, (RW_CHUNK, HEAD_DIM, LANE)).astype(jnp.bfloat16)
    tok = pl.BlockSpec((nb, tc, W), lambda i, c: (i, c, 0))
    chk = pl.BlockSpec((nb, 1, RW_PAIRS, HEAD_DIM, LANE), lambda i, c: (i, c, 0, 0, 0))
    stt = pl.BlockSpec((nb, RW_PAIRS, HEAD_DIM, LANE), lambda i, c: (i, 0, 0, 0))
    y, st = pl.pallas_call(
        functools.partial(_rwkv_scan_body, nb=nb, tc=tc),
        grid=(B // nb, nc),
        in_specs=[tok, tok, tok, tok, tok, chk, stt,
                  pl.BlockSpec((RW_CHUNK, HEAD_DIM, LANE), lambda i, c: (0, 0, 0))],
        out_specs=[chk, stt],
        out_shape=[jax.ShapeDtypeStruct((B, nc, RW_PAIRS, HEAD_DIM, LANE), f32),
                   jax.ShapeDtypeStruct((B, RW_PAIRS, HEAD_DIM, LANE), f32)],
        scratch_shapes=[pltpu.VMEM((nb * RW_PAIRS, HEAD_DIM, LANE), jnp.bfloat16),
                        pltpu.VMEM((nb * RW_PAIRS, HEAD_DIM, LANE), jnp.bfloat16)],
        compiler_params=pltpu.CompilerParams(
            dimension_semantics=("parallel", "arbitrary"), vmem_limit_bytes=VMEM_LIMIT),
        name="rwkv_scan",
    )(r, w, k, kk, b, vt, s0, oh)
    y = y.reshape(B, nc, RW_PAIRS, HEAD_DIM, 2, RW_CHUNK)[..., :tc]
    y = y.transpose(0, 1, 5, 2, 4, 3).reshape(B, T, W)
    st = st.reshape(B, RW_PAIRS, HEAD_DIM, 2, HEAD_DIM).transpose(0, 1, 3, 2, 4)
    return y, st.reshape(B, RWKV_HEADS, HEAD_DIM, HEAD_DIM)


def rwkv_mix(p, prev, S0, mu, w0, wB, a0, aB, gB, k_k, k_a, r_k, ln_w, ln_b):
    f32 = jnp.float32
    B, T = p.shape[:2]
    W = RWKV_WIDTH
    p = p.astype(f32)
    p_prev = jnp.concatenate([prev.astype(f32)[:, None], p[:, :-1]], axis=1)
    ps = p + mu * (p_prev - p)
    r, k, v = ps[..., :W], ps[..., W:2 * W], ps[..., 2 * W:3 * W]
    o = 3 * W
    xw, xa, xg = ps[..., o:o + LORA_W], ps[..., o + LORA_W:o + LORA_W + LORA_A], ps[..., o + LORA_W + LORA_A:]
    z = w0 + jnp.tanh(xw) @ wB
    w = jnp.exp(-jnp.exp(-jax.nn.softplus(-z) - 0.5))
    a = jax.nn.sigmoid(a0 + xa @ aB)
    g = jax.nn.sigmoid(xg) @ gB
    heads = lambda t: t.reshape(B, T, RWKV_HEADS, HEAD_DIM)
    kk = heads(k * k_k)
    kk = kk * lax.rsqrt(jnp.maximum(jnp.sum(kk * kk, axis=-1, keepdims=True), 1e-24))
    k = k * (1.0 + (a - 1.0) * k_a)
    kk = kk.reshape(B, T, W)
    y, S = rwkv_scan(r, w, k, v, kk, kk * a, S0)
    y, k = heads(y), heads(k)
    r, v = heads(r), heads(v)
    mean = jnp.mean(y, axis=-1, keepdims=True)
    var = jnp.mean(jnp.square(y - mean), axis=-1, keepdims=True)
    y = ((y - mean) * lax.rsqrt(var + RWKV_GN_EPS)).reshape(B, T, W) * ln_w + ln_b
    y = y + (jnp.sum(r * k * r_k, axis=-1, keepdims=True) * v).reshape(B, T, W)
    return y * g, S, p[:, -1]


def gla_chunked(q, k, v, logg, S0):
    B, T, H, DK = q.shape
    DV = v.shape[-1]
    C = GLA_CHUNK if T % GLA_CHUNK == 0 else T
    N = T // C
    to_chunks = lambda x: x.reshape(B, N, C, H, x.shape[-1]).transpose(1, 0, 3, 2, 4)
    causal = jnp.tril(jnp.ones((C, C), dtype=bool))[:, :, None]

    def step(S, inp):
        qc, kc, vc, gc = inp
        b = jnp.cumsum(gc, axis=2)
        diff = b[:, :, :, None, :] - b[:, :, None, :, :]
        decay = jnp.exp(jnp.where(causal, diff, -jnp.inf))
        A = jnp.einsum('bhtd,bhsd,bhtsd->bhts', qc, kc, decay)
        o = jnp.einsum('bhts,bhsv->bhtv', A, vc) + jnp.einsum('bhtk,bhkv->bhtv', qc * jnp.exp(b), S)
        b_last = b[:, :, -1:, :]
        S = jnp.exp(b_last[:, :, 0, :])[..., None] * S + jnp.einsum('bhsk,bhsv->bhkv', kc * jnp.exp(b_last - b), vc)
        return S, o

    S, o = lax.scan(step, S0, (to_chunks(q), to_chunks(k), to_chunks(v), to_chunks(logg)))
    return o.transpose(1, 0, 3, 2, 4).reshape(B, T, H, DV), S


GLA_SUB = 16


def _gla_body(q_ref, k_ref, v_ref, g_ref, r_ref, gn_ref, o_ref, st_ref):
    f32, bf16 = jnp.float32, jnp.bfloat16
    C, SUB = GLA_CHUNK, GLA_SUB
    nsub = C // SUB
    c = pl.program_id(1)

    @pl.when(c == 0)
    def _():
        st_ref[...] = jnp.zeros_like(st_ref)

    ti = lax.broadcasted_iota(jnp.int32, (C, C), 0)
    si = lax.broadcasted_iota(jnp.int32, (C, C), 1)
    tri = (si <= ti).astype(bf16)
    sub_t = lax.broadcasted_iota(jnp.int32, (SUB, 1), 0)
    sub_l = lax.broadcasted_iota(jnp.int32, (SUB, SUB), 1)
    for h in range(GLA_HEADS):
        kq = slice(h * GLA_DK, (h + 1) * GLA_DK)
        vv = slice(h * GLA_DV, (h + 1) * GLA_DV)
        q = q_ref[0, :, kq] * (GLA_DK ** -0.5)
        k = k_ref[0, :, kq]
        v = v_ref[0, :, vv].astype(bf16)
        g = g_ref[0, :, kq]
        g1 = g.astype(bf16)
        g2 = (g - g1.astype(f32)).astype(bf16)
        g3 = (g - g1.astype(f32) - g2.astype(f32)).astype(bf16)
        b = (jnp.dot(tri, g1, preferred_element_type=f32) + jnp.dot(tri, g2, preferred_element_type=f32)
             + jnp.dot(tri, g3, preferred_element_type=f32))
        b_last = b[C - 1:C]
        a_rows = []
        for I in range(nsub):
            rows = slice(I * SUB, (I + 1) * SUB)
            beta = b[I * SUB - 1:I * SUB] if I > 0 else jnp.zeros((1, GLA_DK), f32)
            qb, kb, bb = q[rows], k[rows], b[rows]
            a_diag = jnp.zeros((SUB, SUB), f32)
            for s in range(SUB):
                d = jnp.where(sub_t >= s, bb - bb[s:s + 1], -jnp.inf)
                col = jnp.sum(qb * kb[s:s + 1] * jnp.exp(d), axis=1, keepdims=True)
                a_diag = jnp.where(sub_l == s, col, a_diag)
            blocks = []
            if I > 0:
                qe = (qb * jnp.exp(bb - beta)).astype(bf16)
                ke = (k[:I * SUB] * jnp.exp(beta - b[:I * SUB])).astype(bf16)
                blocks.append(lax.dot_general(qe, ke, (((1,), (1,)), ((), ())), preferred_element_type=f32))
            blocks.append(a_diag)
            if I < nsub - 1:
                blocks.append(jnp.zeros((SUB, C - (I + 1) * SUB), f32))
            a_rows.append(jnp.concatenate(blocks, axis=1))
        a = jnp.concatenate(a_rows, axis=0).astype(bf16)
        st = st_ref[0, h]
        qd = (q * jnp.exp(b)).astype(bf16)
        o = (jnp.dot(a, v, preferred_element_type=f32)
             + lax.dot_general(qd, st.astype(bf16), (((1,), (1,)), ((), ())), preferred_element_type=f32))
        kd = (k * jnp.exp(b_last - b)).astype(bf16)
        st_ref[0, h] = (st * jnp.exp(b_last)
                        + lax.dot_general(v, kd, (((0,), (0,)), ((), ())), preferred_element_type=f32))
        y = o * lax.rsqrt(jnp.mean(o * o, axis=1, keepdims=True) + 1e-6) * gn_ref[...]
        rr = r_ref[0, :, vv]
        o_ref[0, :, vv] = y * (rr * jax.nn.sigmoid(rr))


def gla_mix_prompt(qk, v, gd, r, gate_up, gate_b, gn):
    f32 = jnp.float32
    B, T = qk.shape[:2]
    KW, VW, C = GLA_KW, GLA_VW, GLA_CHUNK
    logg = jax.nn.log_sigmoid(gd[..., :GLA_RANK] @ gate_up + gate_b) / GLA_TAU
    tokv = pl.BlockSpec((1, C, VW), lambda b, c: (b, c, 0))
    out, st = pl.pallas_call(
        _gla_body,
        grid=(B, T // C),
        in_specs=[pl.BlockSpec((1, C, KW), lambda b, c: (b, c, 0)),
                  pl.BlockSpec((1, C, KW), lambda b, c: (b, c, 1)),
                  tokv, pl.BlockSpec((1, C, KW), lambda b, c: (b, c, 0)), tokv,
                  pl.BlockSpec((1, GLA_DV), lambda b, c: (0, 0))],
        out_specs=[tokv, pl.BlockSpec((1, GLA_HEADS, GLA_DV, GLA_DK), lambda b, c: (b, 0, 0, 0))],
        out_shape=[jax.ShapeDtypeStruct((B, T, VW), f32),
                   jax.ShapeDtypeStruct((B, GLA_HEADS, GLA_DV, GLA_DK), f32)],
        compiler_params=pltpu.CompilerParams(
            dimension_semantics=("parallel", "arbitrary"), vmem_limit_bytes=VMEM_LIMIT),
        name="gla_chunk",
    )(qk, qk, v, logg, r, gn.reshape(1, GLA_DV))
    return out, st.transpose(0, 1, 3, 2)


def gla_mix(qk, v, gd, r, S0, gate_up, gate_b, gn):
    f32 = jnp.float32
    B, T = qk.shape[:2]
    q = qk[..., :GLA_KW].reshape(B, T, GLA_HEADS, GLA_DK) * (GLA_DK ** -0.5)
    k = qk[..., GLA_KW:].reshape(B, T, GLA_HEADS, GLA_DK)
    v = v.reshape(B, T, GLA_HEADS, GLA_DV)
    gd = gd[..., :GLA_RANK]
    logg = (jax.nn.log_sigmoid(gd @ gate_up + gate_b) / GLA_TAU).reshape(B, T, GLA_HEADS, GLA_DK)
    out, S = gla_chunked(q, k, v, logg, S0.astype(f32))
    out = rmsnorm(out, gn).reshape(B, T, GLA_VW) * jax.nn.silu(r)
    return out, S


def kernel(x_prompt, x_sample, cache_nsa_kv, cache_nsa_win, state_rwkv, state_rwkv_shift, state_gla, page_table, norm_mix, norm_ffn, w_in_even, nsa_qnorm, nsa_knorm, cmp_w1, cmp_w2, cmp_pe, rwkv_mu, rwkv_w0, rwkv_wB, rwkv_a0, rwkv_aB, rwkv_gB, rwkv_kk, rwkv_ka, rwkv_rk, rwkv_ln_w, rwkv_ln_b, w_out_even, w_in_odd, gla_gate_up, gla_gate_b, gla_norm, w_out_odd, ffn_gate, ffn_up, ffn_down):
    dt = x_prompt.dtype
    bf16 = jnp.bfloat16
    B, T = x_prompt.shape[:2]
    DB, S = x_sample.shape[:2]
    depth = norm_mix.shape[0]
    P = page_table.shape[1] * PAGE_SIZE
    WB = cache_nsa_win.shape[2]
    WP = min(WINDOW, T)
    pos_p = jnp.arange(T, dtype=jnp.int32)
    pos_s = P + jnp.arange(S, dtype=jnp.int32)
    win_pos_s = P - WB + jnp.arange(WB + S, dtype=jnp.int32)
    nsa_pad = -(-NSA_IN // LANE) * LANE
    y_p, y_s = x_prompt, x_sample
    kv_p, kv_s, win_p, win_s, rw_p, rw_s, sh_p, sh_s, gl_p, gl_s = ([] for _ in range(10))
    for layer in range(depth):
        li = layer // 2
        if layer % 2 == 0:
            w_in = jnp.concatenate([jnp.pad(w_in_even[li][:, :NSA_IN], ((0, 0), (0, nsa_pad - NSA_IN))),
                                    w_in_even[li][:, NSA_IN:]], axis=1).astype(bf16)
            even_splits = ((0, nsa_pad), (nsa_pad, nsa_pad + RWKV_IN))
            w_out = w_out_even[li].astype(bf16)
            hn_p, hr_p = norm_proj(y_p, norm_mix[layer], w_in, even_splits)
            hn_s, hr_s = norm_proj(y_s, norm_mix[layer], w_in, even_splits)
            q, g, rows, win = nsa_project(hn_p, pos_p, nsa_qnorm[li], nsa_knorm[li])
            o_nsa_p = nsa_prompt(q, g, rows, win, cmp_w1[li], cmp_w2[li], cmp_pe[li])
            kv_p.append(rows.astype(dt))
            win_p.append(win[:, T - WP:].astype(dt))
            q, g, rows, win = nsa_project(hn_s, pos_s, nsa_qnorm[li], nsa_knorm[li])
            past = cache_nsa_kv[li][page_table].reshape(DB, P, 4, NSA_KV, HEAD_DIM)
            win_all = jnp.concatenate([cache_nsa_win[li], win.astype(cache_nsa_win.dtype)], axis=1)
            o_nsa_s = nsa_sample(q, g, past, rows.astype(past.dtype), win_all, cmp_w1[li], cmp_w2[li], cmp_pe[li])
            kv_s.append(rows.astype(dt))
            win_s.append(win_all[:, S:].astype(dt))
            rw_par = (rwkv_mu[li], rwkv_w0[li], rwkv_wB[li], rwkv_a0[li], rwkv_aB[li], rwkv_gB[li],
                      rwkv_kk[li], rwkv_ka[li], rwkv_rk[li], rwkv_ln_w[li], rwkv_ln_b[li])
            o_rw_p, st, last = rwkv_mix(hr_p, jnp.zeros((B, RWKV_IN), dt),
                                        jnp.zeros((B, RWKV_HEADS, HEAD_DIM, HEAD_DIM), dt), *rw_par)
            rw_p.append(st.astype(dt))
            sh_p.append(last.astype(dt))
            o_rw_s, st, last = rwkv_mix(hr_s, state_rwkv_shift[li], state_rwkv[li], *rw_par)
            rw_s.append(st.astype(dt))
            sh_s.append(last.astype(dt))
            w_outs = [w_out[:NSA_Q], w_out[NSA_Q:]]
            y_p = proj_residual([o_nsa_p.astype(dt), o_rw_p.astype(dt)], w_outs, y_p)
            y_s = proj_residual([o_nsa_s.astype(dt), o_rw_s.astype(dt)], w_outs, y_s)
        else:
            wo = w_in_odd[li]
            o2 = 2 * GLA_KW + GLA_VW
            w_in = jnp.concatenate([wo[:, :o2], wo[:, o2 + GLA_RANK:],
                                    jnp.pad(wo[:, o2:o2 + GLA_RANK], ((0, 0), (0, LANE - GLA_RANK)))],
                                   axis=1).astype(bf16)
            odd_splits = ((0, 2 * GLA_KW), (2 * GLA_KW, o2), (o2, o2 + GLA_VW), (o2 + GLA_VW, o2 + GLA_VW + LANE))
            w_out = w_out_odd[li].astype(bf16)
            qk_p, v_p, r_p, gd_p = norm_proj(y_p, norm_mix[layer], w_in, odd_splits)
            qk_s, v_s, r_s, gd_s = norm_proj(y_s, norm_mix[layer], w_in, odd_splits)
            o_p, st = gla_mix_prompt(qk_p, v_p, gd_p, r_p, gla_gate_up[li], gla_gate_b[li], gla_norm[li])
            gl_p.append(st.astype(dt))
            o_s, st = gla_mix(qk_s, v_s, gd_s, r_s, state_gla[li], gla_gate_up[li], gla_gate_b[li], gla_norm[li])
            gl_s.append(st.astype(dt))
            y_p = proj_residual([o_p.astype(dt)], [w_out], y_p)
            y_s = proj_residual([o_s.astype(dt)], [w_out], y_s)
        wg, wu, wd = ffn_gate[layer].astype(bf16), ffn_up[layer].astype(bf16), ffn_down[layer].astype(bf16)
        y_p = ffn(y_p, norm_ffn[layer], wg, wu, wd)
        y_s = ffn(y_s, norm_ffn[layer], wg, wu, wd)
    return (y_p, y_s, jnp.stack(kv_p), jnp.stack(kv_s), jnp.stack(win_p), jnp.stack(win_s),
            jnp.stack(rw_p), jnp.stack(rw_s), jnp.stack(sh_p), jnp.stack(sh_s),
            jnp.stack(gl_p), jnp.stack(gl_s))
```

```python
import functools
import math

import jax
import jax.numpy as jnp
from jax import lax
from jax.experimental import pallas as pl
from jax.experimental.pallas import tpu as pltpu

D_MODEL = 1024
PAGE_SIZE = 128
HEAD_DIM = 64
NSA_HEADS = 8
NSA_KV = 2
NSA_REP = NSA_HEADS // NSA_KV
CMP_STRIDE = 16
CMP_BLOCK = 2 * CMP_STRIDE
CMP_HIDDEN = 2 * HEAD_DIM
SEL_BLOCK = 64
N_SELECT = 16
WINDOW = 512
Q_BLOCK = 128
ROPE_DIM = HEAD_DIM // 4
ROPE_THETA = 500000.0
RWKV_HEADS = 8
RWKV_WIDTH = RWKV_HEADS * HEAD_DIM
LORA_W = 64
LORA_A = 64
LORA_G = 128
RWKV_GN_EPS = 64e-5
GLA_HEADS = 4
GLA_DK = D_MODEL // 2 // GLA_HEADS
GLA_DV = D_MODEL // GLA_HEADS
GLA_RANK = 16
GLA_TAU = 16.0
GLA_CHUNK = 64
D_FF = ((8 * D_MODEL // 3 + 255) // 256) * 256
NSA_Q = NSA_HEADS * HEAD_DIM
NSA_KVW = 3 * 2 * NSA_KV * HEAD_DIM
NSA_IN = NSA_Q + NSA_KVW + 3 * NSA_HEADS
RWKV_IN = 3 * RWKV_WIDTH + LORA_W + LORA_A + LORA_G
EVEN_IN = NSA_IN + RWKV_IN
MIX_WIDTH = NSA_Q + RWKV_WIDTH
GLA_KW = GLA_HEADS * GLA_DK
GLA_VW = GLA_HEADS * GLA_DV
ODD_IN = 2 * GLA_KW + GLA_VW + GLA_RANK + GLA_VW

LANE = 128
VMEM_LIMIT = 48 * 1024 * 1024


ROW_TILE = 512
NORM_EPS = 1e-6


def _row_tile(m):
    return ROW_TILE if m % ROW_TILE == 0 else m


def _norm_bf16(x_ref, g_ref):
    x = x_ref[...]
    y = x * lax.rsqrt(jnp.mean(x * x, axis=-1, keepdims=True) + NORM_EPS)
    return (y * g_ref[...]).astype(jnp.bfloat16)


def _norm_proj_body(x_ref, g_ref, w_ref, *o_refs, splits):
    xn = _norm_bf16(x_ref, g_ref)
    for (a, b), o_ref in zip(splits, o_refs):
        o_ref[...] = jnp.dot(xn, w_ref[:, a:b], preferred_element_type=jnp.float32)


def norm_proj(x, gain, w, splits):
    bsz, t, k = x.shape
    m = bsz * t
    tm = _row_tile(m)
    outs = pl.pallas_call(
        functools.partial(_norm_proj_body, splits=splits),
        grid=(m // tm,),
        in_specs=[pl.BlockSpec((tm, k), lambda i: (i, 0)),
                  pl.BlockSpec((1, k), lambda i: (0, 0)),
                  pl.BlockSpec(w.shape, lambda i: (0, 0))],
        out_specs=[pl.BlockSpec((tm, b - a), lambda i: (i, 0)) for a, b in splits],
        out_shape=[jax.ShapeDtypeStruct((m, b - a), jnp.float32) for a, b in splits],
        compiler_params=pltpu.CompilerParams(dimension_semantics=("parallel",), vmem_limit_bytes=VMEM_LIMIT),
        name="norm_proj",
    )(x.reshape(m, k), gain.reshape(1, k), w)
    return [o.reshape(bsz, t, -1) for o in outs]


def _proj_res_body(*refs, n_in):
    x_refs, w_refs, res_ref, o_ref = refs[:n_in], refs[n_in:2 * n_in], refs[2 * n_in], refs[2 * n_in + 1]
    acc = res_ref[...]
    for x_ref, w_ref in zip(x_refs, w_refs):
        acc = acc + jnp.dot(x_ref[...].astype(jnp.bfloat16), w_ref[...], preferred_element_type=jnp.float32)
    o_ref[...] = acc


def proj_residual(xs, ws, res):
    bsz, t, n = res.shape
    m = bsz * t
    tm = _row_tile(m)
    out = pl.pallas_call(
        functools.partial(_proj_res_body, n_in=len(xs)),
        grid=(m // tm,),
        in_specs=([pl.BlockSpec((tm, x.shape[-1]), lambda i: (i, 0)) for x in xs]
                  + [pl.BlockSpec(w.shape, lambda i: (0, 0)) for w in ws]
                  + [pl.BlockSpec((tm, n), lambda i: (i, 0))]),
        out_specs=pl.BlockSpec((tm, n), lambda i: (i, 0)),
        out_shape=jax.ShapeDtypeStruct((m, n), jnp.float32),
        compiler_params=pltpu.CompilerParams(dimension_semantics=("parallel",), vmem_limit_bytes=VMEM_LIMIT),
        name="proj_residual",
    )(*[x.reshape(m, x.shape[-1]) for x in xs], *ws, res.reshape(m, n))
    return out.reshape(bsz, t, n)


FFN_COL_CHUNKS = 2


def _ffn_up_body(x_ref, g_ref, wg_ref, wu_ref, h_ref):
    xn = _norm_bf16(x_ref, g_ref)
    cw = h_ref.shape[1] // FFN_COL_CHUNKS
    for c in range(FFN_COL_CHUNKS):
        cols = slice(c * cw, (c + 1) * cw)
        g = jnp.dot(xn, wg_ref[:, cols], preferred_element_type=jnp.float32)
        u = jnp.dot(xn, wu_ref[:, cols], preferred_element_type=jnp.float32)
        h_ref[:, cols] = (g * jax.nn.sigmoid(g) * u).astype(h_ref.dtype)


def ffn(y, gain, wg, wu, wd):
    bsz, t, k = y.shape
    m = bsz * t
    tm = _row_tile(m)
    f = wg.shape[1]
    h = pl.pallas_call(
        _ffn_up_body,
        grid=(m // tm,),
        in_specs=[pl.BlockSpec((tm, k), lambda i: (i, 0)),
                  pl.BlockSpec((1, k), lambda i: (0, 0)),
                  pl.BlockSpec(wg.shape, lambda i: (0, 0)),
                  pl.BlockSpec(wu.shape, lambda i: (0, 0))],
        out_specs=pl.BlockSpec((tm, f), lambda i: (i, 0)),
        out_shape=jax.ShapeDtypeStruct((m, f), jnp.bfloat16),
        compiler_params=pltpu.CompilerParams(dimension_semantics=("parallel",), vmem_limit_bytes=VMEM_LIMIT),
        name="ffn_up",
    )(y.reshape(m, k), gain.reshape(1, k), wg, wu)
    return proj_residual([h.reshape(bsz, t, f)], [wd], y)


def rmsnorm(x, g, eps=1e-6):
    xf = x.astype(jnp.float32)
    y = xf * lax.rsqrt(jnp.mean(xf * xf, axis=-1, keepdims=True) + eps)
    return (y * g.astype(jnp.float32)).astype(x.dtype)


def rope(x, pos):
    half = ROPE_DIM // 2
    inv = ROPE_THETA ** (-jnp.arange(half, dtype=jnp.float32) / half)
    ang = pos.astype(jnp.float32)[:, None] * inv[None, :]
    shp = (pos.shape[0],) + (1,) * (x.ndim - 3) + (half,)
    cos, sin = jnp.cos(ang).reshape(shp), jnp.sin(ang).reshape(shp)
    xr = x[..., :ROPE_DIM].astype(jnp.float32)
    x1, x2 = xr[..., :half], xr[..., half:]
    rot = jnp.concatenate([x1 * cos - x2 * sin, x2 * cos + x1 * sin], axis=-1)
    return jnp.concatenate([rot.astype(x.dtype), x[..., ROPE_DIM:]], axis=-1)


def masked_softmax(s, mask):
    p = jax.nn.softmax(jnp.where(mask, s, -1e30), axis=-1)
    return jnp.where(mask, p, 0.0)


def nsa_project(h, pos, qn, kn):
    B, T = h.shape[:2]
    q = h[..., :NSA_Q].reshape(B, T, NSA_HEADS, HEAD_DIM)
    kv = h[..., NSA_Q:NSA_Q + NSA_KVW].reshape(B, T, 3, 2, NSA_KV, HEAD_DIM)
    gates = jax.nn.sigmoid(h[..., NSA_Q + NSA_KVW:NSA_IN].astype(jnp.float32)).reshape(B, T, NSA_HEADS, 3)
    q = rope(rmsnorm(q, qn), pos)
    k = rope(rmsnorm(kv[:, :, :, 0], kn[:, None, :]), pos)
    kv = jnp.stack([k, kv[:, :, :, 1]], axis=3)
    rows = kv[:, :, :2].reshape(B, T, 4, NSA_KV, HEAD_DIM)
    win = kv[:, :, 2]
    return q, gates, rows, win


def compress(k, w1, w2, pe):
    B, L = k.shape[:2]
    n_chunks = L // CMP_STRIDE
    c = k[:, :n_chunks * CMP_STRIDE].reshape(B, n_chunks, CMP_STRIDE, NSA_KV, HEAD_DIM)
    h = (jnp.einsum('bnlgd,ldh->bngh', c[:, :-1], w1[:CMP_STRIDE])
         + jnp.einsum('bnlgd,ldh->bngh', c[:, 1:], w1[CMP_STRIDE:])
         + jnp.einsum('ld,ldh->h', pe, w1))
    return jnp.einsum('bngh,hd->bngd', jax.nn.gelu(h), w2)


def nsa_keys(rows, cw1, cw2, cpe):
    B, L = rows.shape[:2]
    kc = compress(rows[:, :, 0], cw1[0], cw2[0], cpe[0])
    vc = compress(rows[:, :, 1], cw1[1], cw2[1], cpe[1])
    n_cmp = kc.shape[1]
    cmp_start = jnp.arange(n_cmp, dtype=jnp.int32) * CMP_STRIDE
    cmp_end = cmp_start + CMP_BLOCK - 1
    n_sel = -(-L // SEL_BLOCK)
    sel_start = jnp.arange(n_sel, dtype=jnp.int32) * SEL_BLOCK
    sel_map = ((cmp_start[:, None] <= sel_start[None, :] + SEL_BLOCK - 1)
               & (cmp_end[:, None] >= sel_start[None, :])).astype(jnp.float32)
    slc = jnp.pad(rows[:, :, 2:4], ((0, 0), (0, n_sel * SEL_BLOCK - L), (0, 0), (0, 0), (0, 0)))
    slc = slc.reshape(B, n_sel, SEL_BLOCK, 2, NSA_KV, HEAD_DIM).transpose(3, 0, 4, 1, 2, 5)
    return kc, vc, cmp_end, sel_map, slc[0], slc[1]


def nsa_block(q, gates, q_pos, kc, vc, cmp_end, sel_map, ksb, vsb, win, w_pos):
    f32 = jnp.float32
    B, QB = q.shape[:2]
    qg = q.reshape(B, QB, NSA_KV, NSA_REP, HEAD_DIM).astype(f32) * (HEAD_DIM ** -0.5)
    tq = q_pos[:, None]
    s = jnp.einsum('bqgrd,bngd->bqgrn', qg, kc.astype(f32))
    p = masked_softmax(s, (cmp_end[None, :] <= tq)[None, :, None, None, :])
    o_cmp = jnp.einsum('bqgrn,bngd->bqgrd', p, vc.astype(f32))
    imp = jnp.einsum('bqgrn,nj->bqgj', p, sel_map)
    n_sel = sel_map.shape[1]
    blk = jnp.arange(n_sel, dtype=jnp.int32)[None, :]
    cur = tq // SEL_BLOCK
    forced = (blk == 0) | (blk == cur) | (blk == cur - 1)
    valid = blk * SEL_BLOCK <= tq
    imp = jnp.where(valid[None, :, None, :], jnp.where(forced[None, :, None, :], 1e30, imp), -1e30)
    n_top = min(N_SELECT, n_sel)
    _, idx = lax.top_k(imp, n_top)
    bi = jnp.arange(B)[:, None, None, None]
    gi = jnp.arange(NSA_KV)[None, None, :, None]
    nk = n_top * SEL_BLOCK
    k_sel = ksb[bi, gi, idx].reshape(B, QB, NSA_KV, nk, HEAD_DIM).astype(f32)
    v_sel = vsb[bi, gi, idx].reshape(B, QB, NSA_KV, nk, HEAD_DIM).astype(f32)
    k_pos = (idx[..., None] * SEL_BLOCK + jnp.arange(SEL_BLOCK, dtype=jnp.int32)).reshape(B, QB, NSA_KV, nk)
    s = jnp.einsum('bqgrd,bqgkd->bqgrk', qg, k_sel)
    p = masked_softmax(s, (k_pos <= q_pos[None, :, None, None])[:, :, :, None, :])
    o_slc = jnp.einsum('bqgrk,bqgkd->bqgrd', p, v_sel)
    s = jnp.einsum('bqgrd,bkgd->bqgrk', qg, win[:, :, 0].astype(f32))
    wp = w_pos[None, :]
    m = (wp <= tq) & (wp > tq - WINDOW) & (wp >= 0)
    p = masked_softmax(s, m[None, :, None, None, :])
    o_win = jnp.einsum('bqgrk,bkgd->bqgrd', p, win[:, :, 1].astype(f32))
    g = gates.reshape(B, QB, NSA_KV, NSA_REP, 3)
    o = g[..., 0:1] * o_cmp + g[..., 1:2] * o_slc + g[..., 2:3] * o_win
    return o.reshape(B, QB, NSA_Q)


NEG = -1e30


def _nsa_cmp_body(q_ref, kt_ref, v_ref, map_ref, o_ref, sel_ref, *, tq, n_cmp, n_sel):
    f32, bf16 = jnp.float32, jnp.bfloat16
    i = pl.program_id(1)
    ncp = kt_ref.shape[2]
    pos = i * tq + lax.broadcasted_iota(jnp.int32, (tq, 1), 0)
    n = lax.broadcasted_iota(jnp.int32, (1, ncp), 1)
    ok = (n * CMP_STRIDE + (CMP_BLOCK - 1) <= pos) & (n < n_cmp)
    psum = jnp.zeros((tq, ncp), f32)
    for r in range(NSA_REP):
        s = jnp.dot(q_ref[0, r], kt_ref[0], preferred_element_type=f32)
        s = jnp.where(ok, s, NEG)
        m = jnp.max(s, axis=-1, keepdims=True)
        e = jnp.where(ok, jnp.exp(s - m), 0.0)
        l = jnp.sum(e, axis=-1, keepdims=True)
        p = e * jnp.where(l > 0.0, 1.0 / l, 0.0)
        o_ref[0, r] = jnp.dot(p.astype(bf16), v_ref[0], preferred_element_type=f32)
        psum = psum + p
    hi = psum.astype(bf16)
    lo = (psum - hi.astype(f32)).astype(bf16)
    imp = (jnp.dot(hi, map_ref[...], preferred_element_type=f32)
           + jnp.dot(lo, map_ref[...], preferred_element_type=f32))
    nsp = map_ref.shape[1]
    blk = lax.broadcasted_iota(jnp.int32, (1, nsp), 1)
    cur = pos // SEL_BLOCK
    valid = (blk * SEL_BLOCK <= pos) & (blk < n_sel)
    forced = (blk == 0) | (blk == cur) | (blk == cur - 1)
    x = jnp.where(valid, jnp.where(forced, 1e30, imp), NEG)
    x = jnp.where(blk < n_sel, x, -3e38)
    blkf = blk.astype(f32)
    sel = jnp.zeros((tq, nsp), f32)
    for _ in range(min(N_SELECT, n_sel)):
        m = jnp.max(x, axis=-1, keepdims=True)
        first = jnp.min(jnp.where(x == m, blkf, float(nsp)), axis=-1, keepdims=True)
        hit = blkf == first
        sel = jnp.where(hit, 1.0, sel)
        x = jnp.where(hit, -3e38, x)
    sel_ref[0] = jnp.where(valid, sel, 0.0).astype(bf16)


def nsa_cmp_select(q4, kct, vc, sel_map, n_cmp, n_sel, *, tq=128):
    BG, _, T, _ = q4.shape
    ncp, nsp = sel_map.shape
    return pl.pallas_call(
        functools.partial(_nsa_cmp_body, tq=tq, n_cmp=n_cmp, n_sel=n_sel),
        grid=(BG, T // tq),
        in_specs=[pl.BlockSpec((1, NSA_REP, tq, HEAD_DIM), lambda b, i: (b, 0, i, 0)),
                  pl.BlockSpec((1, HEAD_DIM, ncp), lambda b, i: (b, 0, 0)),
                  pl.BlockSpec((1, ncp, HEAD_DIM), lambda b, i: (b, 0, 0)),
                  pl.BlockSpec((ncp, nsp), lambda b, i: (0, 0))],
        out_specs=[pl.BlockSpec((1, NSA_REP, tq, HEAD_DIM), lambda b, i: (b, 0, i, 0)),
                   pl.BlockSpec((1, tq, nsp), lambda b, i: (b, i, 0))],
        out_shape=[jax.ShapeDtypeStruct((BG, NSA_REP, T, HEAD_DIM), jnp.float32),
                   jax.ShapeDtypeStruct((BG, T, nsp), jnp.bfloat16)],
        compiler_params=pltpu.CompilerParams(
            dimension_semantics=("parallel", "parallel"), vmem_limit_bytes=VMEM_LIMIT),
        name="nsa_cmp_select",
    )(q4, kct, vc, sel_map)


def _nsa_flash_body(*refs, tq, tk, mode, key_off):
    f32, bf16 = jnp.float32, jnp.bfloat16
    if mode == "select":
        q_ref, kt_ref, v_ref, sel_ref, o_ref, m_scr, l_scr, acc_scr = refs
    else:
        q_ref, kt_ref, v_ref, o_ref, m_scr, l_scr, acc_scr = refs
    i, j = pl.program_id(1), pl.program_id(2)
    nj = pl.num_programs(2)

    @pl.when(j == 0)
    def _():
        m_scr[...] = jnp.full_like(m_scr, NEG)
        l_scr[...] = jnp.zeros_like(l_scr)
        acc_scr[...] = jnp.zeros_like(acc_scr)

    qpos = i * tq + lax.broadcasted_iota(jnp.int32, (tq, 1), 0)
    lane = lax.broadcasted_iota(jnp.int32, (1, tk), 1)
    if mode == "select":
        k0 = j * tk
        active = k0 <= i * tq + (tq - 1)
    else:
        k0 = (i * tq // tk + j) * tk - key_off
        active = j >= 0
    kpos = k0 + lane

    @pl.when(active)
    def _():
        if mode == "select":
            nsp = sel_ref.shape[2]
            c = lax.broadcasted_iota(jnp.int32, (nsp, tk), 0)
            l2 = lax.broadcasted_iota(jnp.int32, (nsp, tk), 1)
            expand = (c == (k0 + l2) // SEL_BLOCK).astype(bf16)
            picked = jnp.dot(sel_ref[0], expand, preferred_element_type=f32)
            allowed = (picked > 0.5) & (kpos <= qpos)
        else:
            allowed = (kpos <= qpos) & (kpos > qpos - WINDOW) & (kpos >= 0)
        for r in range(NSA_REP):
            s = jnp.dot(q_ref[0, r], kt_ref[0], preferred_element_type=f32)
            s = jnp.where(allowed, s, NEG)
            m_old = m_scr[r]
            m_new = jnp.maximum(m_old, jnp.max(s, axis=-1, keepdims=True))
            alpha = jnp.exp(m_old - m_new)
            p = jnp.where(allowed, jnp.exp(s - m_new), 0.0)
            l_scr[r] = alpha * l_scr[r] + jnp.sum(p, axis=-1, keepdims=True)
            acc_scr[r] = alpha * acc_scr[r] + jnp.dot(p.astype(bf16), v_ref[0], preferred_element_type=f32)
            m_scr[r] = m_new

    @pl.when(j == nj - 1)
    def _():
        for r in range(NSA_REP):
            o_ref[0, r] = acc_scr[r] / l_scr[r]


def nsa_flash(q4, kt, v, sel=None, *, mode, tq, tk, key_off=0):
    BG, _, T, _ = q4.shape
    if mode == "select":
        nj = T // tk
        kidx = lambda b, i, j: jnp.minimum(j, (i * tq + tq - 1) // tk)
    else:
        nj = (key_off + tq) // tk
        kidx = lambda b, i, j: i * tq // tk + j
    in_specs = [pl.BlockSpec((1, NSA_REP, tq, HEAD_DIM), lambda b, i, j: (b, 0, i, 0)),
                pl.BlockSpec((1, HEAD_DIM, tk), lambda b, i, j: (b, 0, kidx(b, i, j))),
                pl.BlockSpec((1, tk, HEAD_DIM), lambda b, i, j: (b, kidx(b, i, j), 0))]
    args = [q4, kt, v]
    if mode == "select":
        in_specs.append(pl.BlockSpec((1, tq, sel.shape[2]), lambda b, i, j: (b, i, 0)))
        args.append(sel)
    return pl.pallas_call(
        functools.partial(_nsa_flash_body, tq=tq, tk=tk, mode=mode, key_off=key_off),
        grid=(BG, T // tq, nj),
        in_specs=in_specs,
        out_specs=pl.BlockSpec((1, NSA_REP, tq, HEAD_DIM), lambda b, i, j: (b, 0, i, 0)),
        out_shape=jax.ShapeDtypeStruct((BG, NSA_REP, T, HEAD_DIM), jnp.float32),
        scratch_shapes=[pltpu.VMEM((NSA_REP, tq, 1), jnp.float32),
                        pltpu.VMEM((NSA_REP, tq, 1), jnp.float32),
                        pltpu.VMEM((NSA_REP, tq, HEAD_DIM), jnp.float32)],
        compiler_params=pltpu.CompilerParams(
            dimension_semantics=("parallel", "parallel", "arbitrary"), vmem_limit_bytes=VMEM_LIMIT),
        name="nsa_flash_" + mode,
    )(*args)


def nsa_prompt(q, gates, rows, win, cw1, cw2, cpe):
    bf16 = jnp.bfloat16
    B, T = q.shape[:2]
    G = NSA_KV
    kc = compress(rows[:, :, 0], cw1[0], cw2[0], cpe[0])
    vc = compress(rows[:, :, 1], cw1[1], cw2[1], cpe[1])
    n_cmp = kc.shape[1]
    n_sel = T // SEL_BLOCK
    ncp = -(-n_cmp // LANE) * LANE
    nsp = -(-n_sel // LANE) * LANE
    n = jnp.arange(ncp)[:, None]
    j = jnp.arange(nsp)[None, :]
    sel_map = ((n * CMP_STRIDE <= j * SEL_BLOCK + SEL_BLOCK - 1) & (n * CMP_STRIDE + CMP_BLOCK - 1 >= j * SEL_BLOCK)
               & (n < n_cmp) & (j < n_sel)).astype(bf16)
    q4 = (q * (HEAD_DIM ** -0.5)).reshape(B, T, G, NSA_REP, HEAD_DIM).transpose(0, 2, 3, 1, 4)
    q4 = q4.reshape(B * G, NSA_REP, T, HEAD_DIM).astype(bf16)

    def keys_t(x, front=0, back=0):
        x = jnp.pad(x, ((0, 0), (front, back), (0, 0), (0, 0)))
        return x.transpose(0, 2, 3, 1).reshape(B * G, HEAD_DIM, -1).astype(bf16)

    def vals(x, front=0, back=0):
        x = jnp.pad(x, ((0, 0), (front, back), (0, 0), (0, 0)))
        return x.transpose(0, 2, 1, 3).reshape(B * G, -1, HEAD_DIM).astype(bf16)

    o_cmp, sel = nsa_cmp_select(q4, keys_t(kc, 0, ncp - n_cmp), vals(vc, 0, ncp - n_cmp), sel_map, n_cmp, n_sel)
    o_slc = nsa_flash(q4, keys_t(rows[:, :, 2]), vals(rows[:, :, 3]), sel, mode="select", tq=128, tk=512)
    o_win = nsa_flash(q4, keys_t(win[:, :, 0], WINDOW), vals(win[:, :, 1], WINDOW), mode="window",
                      tq=256, tk=256, key_off=WINDOW)
    g = gates.reshape(B, T, G, NSA_REP, 3)
    un = lambda o: o.reshape(B, G, NSA_REP, T, HEAD_DIM).transpose(0, 3, 1, 2, 4)
    o = g[..., 0:1] * un(o_cmp) + g[..., 1:2] * un(o_slc) + g[..., 2:3] * un(o_win)
    return o.reshape(B, T, NSA_Q)


NSA_TQ = 128
NSA_TK = 512
NSA_TQ_DECODE = 32


def _nsa_attn_body(qt_ref, kc_ref, vct_ref, map_ref, ks_ref, vst_ref, kw_ref, vw_ref, g_ref, o_ref, thr_scr,
                   *, tq, n_cmp, n_sel, q_tile0, win_base, natural):
    f32, bf16 = jnp.float32, jnp.bfloat16
    tk = NSA_TK
    nl = NSA_REP * tq
    i = pl.program_id(1) + q_tile0
    if natural:
        xq = qt_ref[0] * (HEAD_DIM ** -0.5)
        qt = jnp.concatenate([xq[:, HEAD_DIM * r:HEAD_DIM * (r + 1)].T for r in range(NSA_REP)], axis=1).astype(bf16)
    else:
        qt = qt_ref[0, 0]
    qpos = i * tq + lax.broadcasted_iota(jnp.int32, (1, nl), 1) % tq

    ncp = kc_ref.shape[1]
    n = lax.broadcasted_iota(jnp.int32, (ncp, 1), 0)
    ok = (n * CMP_STRIDE + (CMP_BLOCK - 1) <= qpos) & (n < n_cmp)
    s = jnp.dot(kc_ref[0], qt, preferred_element_type=f32)
    s = jnp.where(ok, s, NEG)
    m = jnp.max(s, axis=0, keepdims=True)
    e = jnp.where(ok, jnp.exp(s - m), 0.0)
    l = jnp.sum(e, axis=0, keepdims=True)
    p = e * jnp.where(l > 0.0, 1.0 / l, 0.0)
    o_cmp = jnp.dot(vct_ref[0], p.astype(bf16), preferred_element_type=f32)

    psum = p[:, 0:tq]
    for r in range(1, NSA_REP):
        psum = psum + p[:, r * tq:(r + 1) * tq]
    hi = psum.astype(bf16)
    lo = (psum - hi.astype(f32)).astype(bf16)
    imp = (jnp.dot(map_ref[...], hi, preferred_element_type=f32)
           + jnp.dot(map_ref[...], lo, preferred_element_type=f32))
    nsp = map_ref.shape[0]
    qp = qpos[:, 0:tq]
    blk = lax.broadcasted_iota(jnp.int32, (nsp, 1), 0)
    cur = qp // SEL_BLOCK
    valid = (blk * SEL_BLOCK <= qp) & (blk < n_sel)
    forced = (blk == 0) | (blk == cur) | (blk == cur - 1)
    x = jnp.where(valid, jnp.where(forced, 1e30, imp), NEG)
    x = jnp.where(blk < n_sel, x, -3e38)
    blkf = blk.astype(f32)
    sel = jnp.zeros((nsp, tq), f32)
    for _ in range(min(N_SELECT, n_sel)):
        mx = jnp.max(x, axis=0, keepdims=True)
        first = jnp.min(jnp.where(x == mx, blkf, float(nsp)), axis=0, keepdims=True)
        hit = blkf == first
        sel = jnp.where(hit, 1.0, sel)
        x = jnp.where(hit, -3e38, x)
    thr = jnp.where(valid & (sel > 0.5), qp, -1)
    thr_scr[...] = jnp.concatenate([thr] * NSA_REP, axis=1)

    def flash(s, allowed, vt, carry):
        m_old, l_old, acc = carry
        m_new = jnp.maximum(m_old, jnp.max(s, axis=0, keepdims=True))
        alpha = jnp.exp(m_old - m_new)
        p = jnp.exp(s - m_new)
        if allowed is not None:
            p = jnp.where(allowed, p, 0.0)
        l_new = alpha * l_old + jnp.sum(p, axis=0, keepdims=True)
        acc = alpha * acc + jnp.dot(vt, p.astype(bf16), preferred_element_type=f32)
        return m_new, l_new, acc

    init = (jnp.full((1, nl), NEG, f32), jnp.zeros((1, nl), f32), jnp.zeros((HEAD_DIM, nl), f32))

    krow = lax.broadcasted_iota(jnp.int32, (SEL_BLOCK, 1), 0)

    def sel_step(j, carry):
        s = jnp.dot(ks_ref[0, j], qt, preferred_element_type=f32)
        th = thr_scr[pl.ds(pl.multiple_of(j * (tk // SEL_BLOCK), 8), tk // SEL_BLOCK), :]
        s = jnp.concatenate(
            [jnp.where((j * tk + b * SEL_BLOCK + krow) <= th[b:b + 1], s[b * SEL_BLOCK:(b + 1) * SEL_BLOCK], NEG)
             for b in range(tk // SEL_BLOCK)], axis=0)
        return flash(s, None, vst_ref[0, j], carry)

    _, l_s, acc_s = lax.fori_loop(0, (i * tq + tq - 1) // tk + 1, sel_step, init)
    o_slc = acc_s / l_s

    nkw = WINDOW + tq
    k_first = i * tq - WINDOW
    row0 = pl.multiple_of(k_first - win_base, 16)
    kpos = k_first + lax.broadcasted_iota(jnp.int32, (nkw, 1), 0)
    allowed = (kpos <= qpos) & (kpos > qpos - WINDOW) & (kpos >= 0)
    s = jnp.dot(kw_ref[0, pl.ds(row0, nkw), :], qt, preferred_element_type=f32)
    s = jnp.where(allowed, s, NEG)
    e = jnp.where(allowed, jnp.exp(s - jnp.max(s, axis=0, keepdims=True)), 0.0)
    o_win = (lax.dot_general(vw_ref[0, pl.ds(row0, nkw), :], e.astype(bf16), (((0,), (0,)), ((), ())),
                             preferred_element_type=f32) / jnp.sum(e, axis=0, keepdims=True))

    g = g_ref[0, 0]
    res = g[0:1] * o_cmp + g[1:2] * o_slc + g[2:3] * o_win
    if natural:
        o_ref[0] = jnp.concatenate([res[:, tq * r:tq * (r + 1)].T for r in range(NSA_REP)], axis=1)
    else:
        o_ref[0, 0] = res


def nsa_attention(q, gates, kc, vc, ksl, vsl, kwin, vwin, *, n_keys, tq=NSA_TQ, q_pos0=0, win_pos0=0):
    f32, bf16 = jnp.float32, jnp.bfloat16
    B, T = q.shape[:2]
    G, R, D = NSA_KV, NSA_REP, HEAD_DIM
    tk = NSA_TK
    nq, nl = T // tq, R * tq
    n_cmp, n_sel = kc.shape[1], -(-n_keys // SEL_BLOCK)
    lk = ksl.shape[1]
    assert T % tq == 0 and nl % LANE == 0 and lk % tk == 0 and q_pos0 % tq == 0 and tq % 16 == 0
    assert lk >= q_pos0 + T and win_pos0 <= max(q_pos0 - WINDOW, 0)
    win_base = win_pos0 - WINDOW
    lw = q_pos0 + T - win_base
    wpad = lambda x: jnp.pad(x[:, :lw - WINDOW], ((0, 0), (WINDOW, max(lw - WINDOW - x.shape[1], 0)), (0, 0), (0, 0)))
    kwin, vwin = wpad(kwin), wpad(vwin)
    ncp = -(-n_cmp // LANE) * LANE
    nsp = -(-n_sel // LANE) * LANE
    n = jnp.arange(ncp)[None, :]
    jb = jnp.arange(nsp)[:, None]
    map_t = ((n * CMP_STRIDE <= jb * SEL_BLOCK + SEL_BLOCK - 1) & (n * CMP_STRIDE + CMP_BLOCK - 1 >= jb * SEL_BLOCK)
             & (n < n_cmp) & (jb < n_sel)).astype(bf16)
    natural = tq == LANE
    if natural:
        qt = q.reshape(B, T, NSA_Q)
        q_spec = pl.BlockSpec((1, tq, R * D), lambda b, i: (b // G, i, b % G))
        o_spec, o_shape = q_spec, jax.ShapeDtypeStruct((B, T, NSA_Q), f32)
    else:
        qt = (q * (D ** -0.5)).reshape(B, nq, tq, G, R, D).transpose(0, 3, 1, 5, 4, 2)
        qt = qt.reshape(B * G, nq, D, nl).astype(bf16)
        q_spec = pl.BlockSpec((1, 1, D, nl), lambda b, i: (b, i, 0, 0))
        o_spec, o_shape = q_spec, jax.ShapeDtypeStruct((B * G, nq, D, nl), f32)
    gt = gates.reshape(B, nq, tq, G, R, 3).transpose(0, 3, 1, 5, 4, 2).reshape(B * G, nq, 3, nl)
    gt = jnp.pad(gt, ((0, 0), (0, 0), (0, 5), (0, 0)))
    rows = lambda x, t: x.transpose(0, 2, 1, 3).reshape(B * G, -1, t, D).astype(bf16)
    cols = lambda x, t: x.reshape(B, -1, t, G, D).transpose(0, 3, 1, 4, 2).reshape(B * G, -1, D, t).astype(bf16)
    kcp = jnp.pad(kc, ((0, 0), (0, ncp - n_cmp), (0, 0), (0, 0)))
    vcp = jnp.pad(vc, ((0, 0), (0, ncp - n_cmp), (0, 0), (0, 0)))
    kc_r = rows(kcp, ncp)[:, 0]
    vc_c = cols(vcp, ncp)[:, 0]
    full = lambda shape: pl.BlockSpec((1,) + shape, lambda b, i: (b,) + (0,) * len(shape))
    per_q = lambda shape: pl.BlockSpec((1, 1) + shape, lambda b, i: (b, i) + (0,) * len(shape))
    out = pl.pallas_call(
        functools.partial(_nsa_attn_body, tq=tq, n_cmp=n_cmp, n_sel=n_sel,
                          q_tile0=q_pos0 // tq, win_base=win_base, natural=natural),
        grid=(B * G, nq),
        in_specs=[q_spec, full((ncp, D)), full((D, ncp)),
                  pl.BlockSpec((nsp, ncp), lambda b, i: (0, 0)),
                  full((lk // tk, tk, D)), full((lk // tk, D, tk)),
                  full((lw, D)), full((lw, D)),
                  per_q((8, nl))],
        out_specs=o_spec,
        out_shape=o_shape,
        scratch_shapes=[pltpu.VMEM((nsp, nl), jnp.int32)],
        compiler_params=pltpu.CompilerParams(
            dimension_semantics=("parallel", "parallel"), vmem_limit_bytes=VMEM_LIMIT),
        name="nsa_attention",
    )(qt, kc_r, vc_c, map_t, rows(ksl, tk), cols(vsl, tk), rows(kwin, lw)[:, 0], rows(vwin, lw)[:, 0], gt)
    if natural:
        return out
    return out.reshape(B, G, nq, D, R, tq).transpose(0, 2, 5, 1, 4, 3).reshape(B, T, NSA_Q)


def nsa_prompt(q, gates, rows, win, cw1, cw2, cpe):
    kc = compress(rows[:, :, 0], cw1[0], cw2[0], cpe[0])
    vc = compress(rows[:, :, 1], cw1[1], cw2[1], cpe[1])
    return nsa_attention(q, gates, kc, vc, rows[:, :, 2], rows[:, :, 3], win[:, :, 0], win[:, :, 1],
                         n_keys=q.shape[1])


def nsa_sample(q, gates, past, rows, win_all, cw1, cw2, cpe):
    DB, S = q.shape[:2]
    P, WB = past.shape[1], win_all.shape[1] - S
    up = lambda n, m: -(-n // m) * m
    pad_rows = lambda x, n: jnp.pad(x, ((0, 0), (0, n - x.shape[1])) + ((0, 0),) * (x.ndim - 2))
    n_chunk_rows = (P + S) // CMP_STRIDE * CMP_STRIDE
    cmp_in = [past[:, :n_chunk_rows, w] if n_chunk_rows <= P
              else jnp.concatenate([past[:, :, w], rows[:, :n_chunk_rows - P, w]], axis=1) for w in (0, 1)]
    kc = compress(cmp_in[0], cw1[0], cw2[0], cpe[0])
    vc = compress(cmp_in[1], cw1[1], cw2[1], cpe[1])
    tq = up(S, NSA_TQ_DECODE)
    lk = up(P + tq, NSA_TK)
    ksl = pad_rows(jnp.concatenate([past[:, :, 2], rows[:, :, 2]], axis=1), lk)
    vsl = pad_rows(jnp.concatenate([past[:, :, 3], rows[:, :, 3]], axis=1), lk)
    o = nsa_attention(pad_rows(q, tq), pad_rows(gates, tq), kc, vc, ksl, vsl, win_all[:, :, 0], win_all[:, :, 1],
                      n_keys=P + S, tq=tq, q_pos0=P, win_pos0=P - WB)
    return o[:, :S]


RW_PAIRS = RWKV_HEADS // 2
RW_CHUNK = 64


def _rwkv_scan_body(r_ref, w_ref, k_ref, kk_ref, b_ref, c1_ref, c2_ref, vt_ref, s0_ref, oh_ref,
                    y_ref, st_ref, vhi_scr, *, nb, tc):
    f32, bf16 = jnp.float32, jnp.bfloat16
    c = pl.program_id(1)

    @pl.when(c == 0)
    def _():
        st_ref[...] = s0_ref[...]

    row = lax.broadcasted_iota(jnp.int32, (2 * LANE, LANE), 0)
    col = lax.broadcasted_iota(jnp.int32, (2 * LANE, LANE), 1)
    ones2 = (((row // HEAD_DIM) % 2) == (col // HEAD_DIM)).astype(bf16)
    ones1 = ones2[:LANE]
    lane_t = lax.broadcasted_iota(jnp.int32, (HEAD_DIM, LANE), 1) % HEAD_DIM

    vt = vt_ref[...].reshape(nb * RW_PAIRS, HEAD_DIM, LANE)
    vhi_scr[...] = vt.astype(bf16)
    y_ref[...] = jnp.zeros_like(y_ref)

    pairs = [(b, p) for b in range(nb) for p in range(RW_PAIRS)]
    n_pairs = len(pairs)

    def split2(x):
        hi = x.astype(bf16)
        return jnp.concatenate([hi, (x - hi.astype(f32)).astype(bf16)], axis=1)

    def group(t8, carry):
        t0 = pl.multiple_of(t8 * 8, 8)
        for j in range(0, 8, 2):
            ra = lambda ref, b, sl: ref[b, pl.ds(t0, 8), sl][j:j + 1]
            rb = lambda ref, b, sl: ref[b, pl.ds(t0, 8), sl][j + 1:j + 2]
            oh_a, oh_b = oh_ref[t0 + j], oh_ref[t0 + j + 1]
            vl = jnp.concatenate([vhi_scr[i] * oh for oh in (oh_a, oh_b) for i in range(n_pairs)], axis=0)
            vb_all = jnp.dot(vl, ones1, preferred_element_type=f32)
            pieces = []
            for b, p in pairs:
                sl = slice(LANE * p, LANE * (p + 1))
                s = st_ref[b, p]
                pieces.append(split2(s * ra(kk_ref, b, sl)))
                pieces.append(split2(s * (ra(w_ref, b, sl) * rb(kk_ref, b, sl))))
            u_all = jnp.dot(jnp.concatenate(pieces, axis=0), ones2, preferred_element_type=f32)
            py = []
            for i, (b, p) in enumerate(pairs):
                sl = slice(LANE * p, LANE * (p + 1))
                u1 = u_all[2 * HEAD_DIM * i:2 * HEAD_DIM * i + HEAD_DIM]
                u2 = u_all[2 * HEAD_DIM * i + HEAD_DIM:2 * HEAD_DIM * (i + 1)]
                vb_a = vb_all[HEAD_DIM * i:HEAD_DIM * (i + 1)]
                vb_b = vb_all[HEAD_DIM * (n_pairs + i):HEAD_DIM * (n_pairs + i + 1)]
                sa_b = u2 - u1 * ra(c1_ref, b, sl) + vb_a * ra(c2_ref, b, sl)
                s_a = st_ref[b, p] * ra(w_ref, b, sl) - u1 * ra(b_ref, b, sl) + vb_a * ra(k_ref, b, sl)
                s_b = s_a * rb(w_ref, b, sl) - sa_b * rb(b_ref, b, sl) + vb_b * rb(k_ref, b, sl)
                st_ref[b, p] = s_b
                py.append((s_a * ra(r_ref, b, sl)).astype(bf16))
                py.append((s_b * rb(r_ref, b, sl)).astype(bf16))
            y_all = jnp.dot(jnp.concatenate(py, axis=0), ones1, preferred_element_type=f32)
            m_a, m_b = lane_t == t0 + j, lane_t == t0 + j + 1
            for i, (b, p) in enumerate(pairs):
                y_a = y_all[2 * HEAD_DIM * i:2 * HEAD_DIM * i + HEAD_DIM]
                y_b = y_all[2 * HEAD_DIM * i + HEAD_DIM:2 * HEAD_DIM * (i + 1)]
                y_ref[b, 0, p] = jnp.where(m_a, y_a, jnp.where(m_b, y_b, y_ref[b, 0, p]))
        return carry

    lax.fori_loop(0, tc // 8, group, 0)


def rwkv_scan(r, w, k, v, kk, b, S0):
    f32 = jnp.float32
    B, T, W = r.shape
    tc = RW_CHUNK if T % RW_CHUNK == 0 else T
    assert tc <= RW_CHUNK and T % tc == 0 and tc % 8 == 0 and B % 2 == 0
    nc, nb = T // tc, 2
    vt = v.reshape(B, nc, tc, RW_PAIRS, 2, HEAD_DIM).transpose(0, 1, 3, 5, 4, 2)
    vt = jnp.pad(vt, ((0, 0),) * 5 + ((0, RW_CHUNK - tc),)).reshape(B, nc, RW_PAIRS, HEAD_DIM, LANE)
    s0 = S0.astype(f32).reshape(B, RW_PAIRS, 2, HEAD_DIM, HEAD_DIM).transpose(0, 1, 3, 2, 4)
    s0 = s0.reshape(B, RW_PAIRS, HEAD_DIM, LANE)
    oh = jnp.arange(LANE)[None, None, :] % HEAD_DIM == jnp.arange(RW_CHUNK)[:, None, None]
    oh = jnp.broadcast_to(oh, (RW_CHUNK, HEAD_DIM, LANE)).astype(jnp.bfloat16)
    kk_next = jnp.concatenate([kk[:, 1:], kk[:, :1]], axis=1)

    def head_dot(x):
        d = jnp.sum((x * kk_next).reshape(B, T, RWKV_HEADS, HEAD_DIM), axis=-1, keepdims=True)
        return jnp.broadcast_to(d, (B, T, RWKV_HEADS, HEAD_DIM)).reshape(B, T, W)

    c1, c2 = head_dot(b), head_dot(k)
    tok = pl.BlockSpec((nb, tc, W), lambda i, c: (i, c, 0))
    chk = pl.BlockSpec((nb, 1, RW_PAIRS, HEAD_DIM, LANE), lambda i, c: (i, c, 0, 0, 0))
    stt = pl.BlockSpec((nb, RW_PAIRS, HEAD_DIM, LANE), lambda i, c: (i, 0, 0, 0))
    y, st = pl.pallas_call(
        functools.partial(_rwkv_scan_body, nb=nb, tc=tc),
        grid=(B // nb, nc),
        in_specs=[tok, tok, tok, tok, tok, tok, tok, chk, stt,
                  pl.BlockSpec((RW_CHUNK, HEAD_DIM, LANE), lambda i, c: (0, 0, 0))],
        out_specs=[chk, stt],
        out_shape=[jax.ShapeDtypeStruct((B, nc, RW_PAIRS, HEAD_DIM, LANE), f32),
                   jax.ShapeDtypeStruct((B, RW_PAIRS, HEAD_DIM, LANE), f32)],
        scratch_shapes=[pltpu.VMEM((nb * RW_PAIRS, HEAD_DIM, LANE), jnp.bfloat16)],
        compiler_params=pltpu.CompilerParams(
            dimension_semantics=("parallel", "arbitrary"), vmem_limit_bytes=VMEM_LIMIT),
        name="rwkv_scan",
    )(r, w, k, kk, b, c1, c2, vt, s0, oh)
    y = y.reshape(B, nc, RW_PAIRS, HEAD_DIM, 2, RW_CHUNK)[..., :tc]
    y = y.transpose(0, 1, 5, 2, 4, 3).reshape(B, T, W)
    st = st.reshape(B, RW_PAIRS, HEAD_DIM, 2, HEAD_DIM).transpose(0, 1, 3, 2, 4)
    return y, st.reshape(B, RWKV_HEADS, HEAD_DIM, HEAD_DIM)


def rwkv_mix(p, prev, S0, mu, w0, wB, a0, aB, gB, k_k, k_a, r_k, ln_w, ln_b):
    f32 = jnp.float32
    B, T = p.shape[:2]
    W = RWKV_WIDTH
    p = p.astype(f32)
    p_prev = jnp.concatenate([prev.astype(f32)[:, None], p[:, :-1]], axis=1)
    ps = p + mu * (p_prev - p)
    r, k, v = ps[..., :W], ps[..., W:2 * W], ps[..., 2 * W:3 * W]
    o = 3 * W
    xw, xa, xg = ps[..., o:o + LORA_W], ps[..., o + LORA_W:o + LORA_W + LORA_A], ps[..., o + LORA_W + LORA_A:]
    z = w0 + jnp.tanh(xw) @ wB
    w = jnp.exp(-jnp.exp(-jax.nn.softplus(-z) - 0.5))
    a = jax.nn.sigmoid(a0 + xa @ aB)
    g = jax.nn.sigmoid(xg) @ gB
    heads = lambda t: t.reshape(B, T, RWKV_HEADS, HEAD_DIM)
    kk = heads(k * k_k)
    kk = kk * lax.rsqrt(jnp.maximum(jnp.sum(kk * kk, axis=-1, keepdims=True), 1e-24))
    k = k * (1.0 + (a - 1.0) * k_a)
    kk = kk.reshape(B, T, W)
    y, S = rwkv_scan(r, w, k, v, kk, kk * a, S0)
    y, k = heads(y), heads(k)
    r, v = heads(r), heads(v)
    mean = jnp.mean(y, axis=-1, keepdims=True)
    var = jnp.mean(jnp.square(y - mean), axis=-1, keepdims=True)
    y = ((y - mean) * lax.rsqrt(var + RWKV_GN_EPS)).reshape(B, T, W) * ln_w + ln_b
    y = y + (jnp.sum(r * k * r_k, axis=-1, keepdims=True) * v).reshape(B, T, W)
    return y * g, S, p[:, -1]


def gla_chunked(q, k, v, logg, S0):
    B, T, H, DK = q.shape
    DV = v.shape[-1]
    C = GLA_CHUNK if T % GLA_CHUNK == 0 else T
    N = T // C
    to_chunks = lambda x: x.reshape(B, N, C, H, x.shape[-1]).transpose(1, 0, 3, 2, 4)
    causal = jnp.tril(jnp.ones((C, C), dtype=bool))[:, :, None]

    def step(S, inp):
        qc, kc, vc, gc = inp
        b = jnp.cumsum(gc, axis=2)
        diff = b[:, :, :, None, :] - b[:, :, None, :, :]
        decay = jnp.exp(jnp.where(causal, diff, -jnp.inf))
        A = jnp.einsum('bhtd,bhsd,bhtsd->bhts', qc, kc, decay)
        o = jnp.einsum('bhts,bhsv->bhtv', A, vc) + jnp.einsum('bhtk,bhkv->bhtv', qc * jnp.exp(b), S)
        b_last = b[:, :, -1:, :]
        S = jnp.exp(b_last[:, :, 0, :])[..., None] * S + jnp.einsum('bhsk,bhsv->bhkv', kc * jnp.exp(b_last - b), vc)
        return S, o

    S, o = lax.scan(step, S0, (to_chunks(q), to_chunks(k), to_chunks(v), to_chunks(logg)))
    return o.transpose(1, 0, 3, 2, 4).reshape(B, T, H, DV), S


GLA_SUB = 16


def _gla_body(q_ref, k_ref, v_ref, g_ref, r_ref, gn_ref, o_ref, st_ref):
    f32, bf16 = jnp.float32, jnp.bfloat16
    C, SUB = GLA_CHUNK, GLA_SUB
    nsub = C // SUB
    c = pl.program_id(1)

    @pl.when(c == 0)
    def _():
        st_ref[...] = jnp.zeros_like(st_ref)

    ti = lax.broadcasted_iota(jnp.int32, (C, C), 0)
    si = lax.broadcasted_iota(jnp.int32, (C, C), 1)
    tri = (si <= ti).astype(bf16)
    sub_t = lax.broadcasted_iota(jnp.int32, (SUB, 1), 0)
    sub_l = lax.broadcasted_iota(jnp.int32, (SUB, SUB), 1)
    for h in range(GLA_HEADS):
        kq = slice(h * GLA_DK, (h + 1) * GLA_DK)
        vv = slice(h * GLA_DV, (h + 1) * GLA_DV)
        q = q_ref[0, :, kq] * (GLA_DK ** -0.5)
        k = k_ref[0, :, kq]
        v = v_ref[0, :, vv].astype(bf16)
        g = g_ref[0, :, kq]
        g1 = g.astype(bf16)
        g2 = (g - g1.astype(f32)).astype(bf16)
        g3 = (g - g1.astype(f32) - g2.astype(f32)).astype(bf16)
        b = (jnp.dot(tri, g1, preferred_element_type=f32) + jnp.dot(tri, g2, preferred_element_type=f32)
             + jnp.dot(tri, g3, preferred_element_type=f32))
        b_last = b[C - 1:C]
        a_rows = []
        for I in range(nsub):
            rows = slice(I * SUB, (I + 1) * SUB)
            beta = b[I * SUB - 1:I * SUB] if I > 0 else jnp.zeros((1, GLA_DK), f32)
            qb, kb, bb = q[rows], k[rows], b[rows]
            a_diag = jnp.zeros((SUB, SUB), f32)
            for s in range(SUB):
                d = jnp.where(sub_t >= s, bb - bb[s:s + 1], -jnp.inf)
                col = jnp.sum(qb * kb[s:s + 1] * jnp.exp(d), axis=1, keepdims=True)
                a_diag = jnp.where(sub_l == s, col, a_diag)
            blocks = []
            if I > 0:
                qe = (qb * jnp.exp(bb - beta)).astype(bf16)
                ke = (k[:I * SUB] * jnp.exp(beta - b[:I * SUB])).astype(bf16)
                blocks.append(lax.dot_general(qe, ke, (((1,), (1,)), ((), ())), preferred_element_type=f32))
            blocks.append(a_diag)
            if I < nsub - 1:
                blocks.append(jnp.zeros((SUB, C - (I + 1) * SUB), f32))
            a_rows.append(jnp.concatenate(blocks, axis=1))
        a = jnp.concatenate(a_rows, axis=0).astype(bf16)
        st = st_ref[0, h]
        qd = (q * jnp.exp(b)).astype(bf16)
        o = (jnp.dot(a, v, preferred_element_type=f32)
             + lax.dot_general(qd, st.astype(bf16), (((1,), (1,)), ((), ())), preferred_element_type=f32))
        kd = (k * jnp.exp(b_last - b)).astype(bf16)
        st_ref[0, h] = (st * jnp.exp(b_last)
                        + lax.dot_general(v, kd, (((0,), (0,)), ((), ())), preferred_element_type=f32))
        y = o * lax.rsqrt(jnp.mean(o * o, axis=1, keepdims=True) + 1e-6) * gn_ref[...]
        rr = r_ref[0, :, vv]
        o_ref[0, :, vv] = y * (rr * jax.nn.sigmoid(rr))


def gla_mix_prompt(qk, v, gd, r, gate_up, gate_b, gn):
    f32 = jnp.float32
    B, T = qk.shape[:2]
    KW, VW, C = GLA_KW, GLA_VW, GLA_CHUNK
    logg = jax.nn.log_sigmoid(gd[..., :GLA_RANK] @ gate_up + gate_b) / GLA_TAU
    tokv = pl.BlockSpec((1, C, VW), lambda b, c: (b, c, 0))
    out, st = pl.pallas_call(
        _gla_body,
        grid=(B, T // C),
        in_specs=[pl.BlockSpec((1, C, KW), lambda b, c: (b, c, 0)),
                  pl.BlockSpec((1, C, KW), lambda b, c: (b, c, 1)),
                  tokv, pl.BlockSpec((1, C, KW), lambda b, c: (b, c, 0)), tokv,
                  pl.BlockSpec((1, GLA_DV), lambda b, c: (0, 0))],
        out_specs=[tokv, pl.BlockSpec((1, GLA_HEADS, GLA_DV, GLA_DK), lambda b, c: (b, 0, 0, 0))],
        out_shape=[jax.ShapeDtypeStruct((B, T, VW), f32),
                   jax.ShapeDtypeStruct((B, GLA_HEADS, GLA_DV, GLA_DK), f32)],
        compiler_params=pltpu.CompilerParams(
            dimension_semantics=("parallel", "arbitrary"), vmem_limit_bytes=VMEM_LIMIT),
        name="gla_chunk",
    )(qk, qk, v, logg, r, gn.reshape(1, GLA_DV))
    return out, st.transpose(0, 1, 3, 2)


def gla_mix(qk, v, gd, r, S0, gate_up, gate_b, gn):
    f32 = jnp.float32
    B, T = qk.shape[:2]
    q = qk[..., :GLA_KW].reshape(B, T, GLA_HEADS, GLA_DK) * (GLA_DK ** -0.5)
    k = qk[..., GLA_KW:].reshape(B, T, GLA_HEADS, GLA_DK)
    v = v.reshape(B, T, GLA_HEADS, GLA_DV)
    gd = gd[..., :GLA_RANK]
    logg = (jax.nn.log_sigmoid(gd @ gate_up + gate_b) / GLA_TAU).reshape(B, T, GLA_HEADS, GLA_DK)
    out, S = gla_chunked(q, k, v, logg, S0.astype(f32))
    out = rmsnorm(out, gn).reshape(B, T, GLA_VW) * jax.nn.silu(r)
    return out, S


def kernel(x_prompt, x_sample, cache_nsa_kv, cache_nsa_win, state_rwkv, state_rwkv_shift, state_gla, page_table, norm_mix, norm_ffn, w_in_even, nsa_qnorm, nsa_knorm, cmp_w1, cmp_w2, cmp_pe, rwkv_mu, rwkv_w0, rwkv_wB, rwkv_a0, rwkv_aB, rwkv_gB, rwkv_kk, rwkv_ka, rwkv_rk, rwkv_ln_w, rwkv_ln_b, w_out_even, w_in_odd, gla_gate_up, gla_gate_b, gla_norm, w_out_odd, ffn_gate, ffn_up, ffn_down):
    dt = x_prompt.dtype
    bf16 = jnp.bfloat16
    B, T = x_prompt.shape[:2]
    DB, S = x_sample.shape[:2]
    depth = norm_mix.shape[0]
    P = page_table.shape[1] * PAGE_SIZE
    WB = cache_nsa_win.shape[2]
    WP = min(WINDOW, T)
    pos_p = jnp.arange(T, dtype=jnp.int32)
    pos_s = P + jnp.arange(S, dtype=jnp.int32)
    win_pos_s = P - WB + jnp.arange(WB + S, dtype=jnp.int32)
    nsa_pad = -(-NSA_IN // LANE) * LANE
    y_p, y_s = x_prompt, x_sample
    kv_p, kv_s, win_p, win_s, rw_p, rw_s, sh_p, sh_s, gl_p, gl_s = ([] for _ in range(10))
    for layer in range(depth):
        li = layer // 2
        if layer % 2 == 0:
            w_in = jnp.concatenate([jnp.pad(w_in_even[li][:, :NSA_IN], ((0, 0), (0, nsa_pad - NSA_IN))),
                                    w_in_even[li][:, NSA_IN:]], axis=1).astype(bf16)
            even_splits = ((0, nsa_pad), (nsa_pad, nsa_pad + RWKV_IN))
            w_out = w_out_even[li].astype(bf16)
            hn_p, hr_p = norm_proj(y_p, norm_mix[layer], w_in, even_splits)
            hn_s, hr_s = norm_proj(y_s, norm_mix[layer], w_in, even_splits)
            q, g, rows, win = nsa_project(hn_p, pos_p, nsa_qnorm[li], nsa_knorm[li])
            o_nsa_p = nsa_prompt(q, g, rows, win, cmp_w1[li], cmp_w2[li], cmp_pe[li])
            kv_p.append(rows.astype(dt))
            win_p.append(win[:, T - WP:].astype(dt))
            q, g, rows, win = nsa_project(hn_s, pos_s, nsa_qnorm[li], nsa_knorm[li])
            pool = cache_nsa_kv.reshape((-1,) + cache_nsa_kv.shape[2:])
            past = pool[li * cache_nsa_kv.shape[1] + page_table].reshape(DB, P, 4, NSA_KV, HEAD_DIM)
            win_all = jnp.concatenate([cache_nsa_win[li], win.astype(cache_nsa_win.dtype)], axis=1)
            o_nsa_s = nsa_sample(q, g, past, rows.astype(past.dtype), win_all, cmp_w1[li], cmp_w2[li], cmp_pe[li])
            kv_s.append(rows.astype(dt))
            win_s.append(win_all[:, S:].astype(dt))
            rw_par = (rwkv_mu[li], rwkv_w0[li], rwkv_wB[li], rwkv_a0[li], rwkv_aB[li], rwkv_gB[li],
                      rwkv_kk[li], rwkv_ka[li], rwkv_rk[li], rwkv_ln_w[li], rwkv_ln_b[li])
            o_rw_p, st, last = rwkv_mix(hr_p, jnp.zeros((B, RWKV_IN), dt),
                                        jnp.zeros((B, RWKV_HEADS, HEAD_DIM, HEAD_DIM), dt), *rw_par)
            rw_p.append(st.astype(dt))
            sh_p.append(last.astype(dt))
            o_rw_s, st, last = rwkv_mix(hr_s, state_rwkv_shift[li], state_rwkv[li], *rw_par)
            rw_s.append(st.astype(dt))
            sh_s.append(last.astype(dt))
            w_outs = [w_out[:NSA_Q], w_out[NSA_Q:]]
            y_p = proj_residual([o_nsa_p.astype(dt), o_rw_p.astype(dt)], w_outs, y_p)
            y_s = proj_residual([o_nsa_s.astype(dt), o_rw_s.astype(dt)], w_outs, y_s)
        else:
            wo = w_in_odd[li]
            o2 = 2 * GLA_KW + GLA_VW
            w_in = jnp.concatenate([wo[:, :o2], wo[:, o2 + GLA_RANK:],
                                    jnp.pad(wo[:, o2:o2 + GLA_RANK], ((0, 0), (0, LANE - GLA_RANK)))],
                                   axis=1).astype(bf16)
            odd_splits = ((0, 2 * GLA_KW), (2 * GLA_KW, o2), (o2, o2 + GLA_VW), (o2 + GLA_VW, o2 + GLA_VW + LANE))
            w_out = w_out_odd[li].astype(bf16)
            qk_p, v_p, r_p, gd_p = norm_proj(y_p, norm_mix[layer], w_in, odd_splits)
            qk_s, v_s, r_s, gd_s = norm_proj(y_s, norm_mix[layer], w_in, odd_splits)
            o_p, st = gla_mix_prompt(qk_p, v_p, gd_p, r_p, gla_gate_up[li], gla_gate_b[li], gla_norm[li])
            gl_p.append(st.astype(dt))
            o_s, st = gla_mix(qk_s, v_s, gd_s, r_s, state_gla[li], gla_gate_up[li], gla_gate_b[li], gla_norm[li])
            gl_s.append(st.astype(dt))
            y_p = proj_residual([o_p.astype(dt)], [w_out], y_p)
            y_s = proj_residual([o_s.astype(dt)], [w_out], y_s)
        wg, wu, wd = ffn_gate[layer].astype(bf16), ffn_up[layer].astype(bf16), ffn_down[layer].astype(bf16)
        y_p = ffn(y_p, norm_ffn[layer], wg, wu, wd)
        y_s = ffn(y_s, norm_ffn[layer], wg, wu, wd)
    return (y_p, y_s, jnp.stack(kv_p), jnp.stack(kv_s), jnp.stack(win_p), jnp.stack(win_s),
            jnp.stack(rw_p), jnp.stack(rw_s), jnp.stack(sh_p), jnp.stack(sh_s),
            jnp.stack(gl_p), jnp.stack(gl_s))
```

```python
import functools
import math

import jax
import jax.numpy as jnp
from jax import lax
from jax.experimental import pallas as pl
from jax.experimental.pallas import tpu as pltpu

D_MODEL = 1024
PAGE_SIZE = 128
HEAD_DIM = 64
NSA_HEADS = 8
NSA_KV = 2
NSA_REP = NSA_HEADS // NSA_KV
CMP_STRIDE = 16
CMP_BLOCK = 2 * CMP_STRIDE
CMP_HIDDEN = 2 * HEAD_DIM
SEL_BLOCK = 64
N_SELECT = 16
WINDOW = 512
Q_BLOCK = 128
ROPE_DIM = HEAD_DIM // 4
ROPE_THETA = 500000.0
RWKV_HEADS = 8
RWKV_WIDTH = RWKV_HEADS * HEAD_DIM
LORA_W = 64
LORA_A = 64
LORA_G = 128
RWKV_GN_EPS = 64e-5
GLA_HEADS = 4
GLA_DK = D_MODEL // 2 // GLA_HEADS
GLA_DV = D_MODEL // GLA_HEADS
GLA_RANK = 16
GLA_TAU = 16.0
GLA_CHUNK = 64
D_FF = ((8 * D_MODEL // 3 + 255) // 256) * 256
NSA_Q = NSA_HEADS * HEAD_DIM
NSA_KVW = 3 * 2 * NSA_KV * HEAD_DIM
NSA_IN = NSA_Q + NSA_KVW + 3 * NSA_HEADS
RWKV_IN = 3 * RWKV_WIDTH + LORA_W + LORA_A + LORA_G
EVEN_IN = NSA_IN + RWKV_IN
MIX_WIDTH = NSA_Q + RWKV_WIDTH
GLA_KW = GLA_HEADS * GLA_DK
GLA_VW = GLA_HEADS * GLA_DV
ODD_IN = 2 * GLA_KW + GLA_VW + GLA_RANK + GLA_VW

LANE = 128
VMEM_LIMIT = 48 * 1024 * 1024


ROW_TILE = 512
NORM_EPS = 1e-6


def _row_tile(m):
    return ROW_TILE if m % ROW_TILE == 0 else m


def _norm_bf16(x_ref, g_ref):
    x = x_ref[...]
    y = x * lax.rsqrt(jnp.mean(x * x, axis=-1, keepdims=True) + NORM_EPS)
    return (y * g_ref[...]).astype(jnp.bfloat16)


def _norm_proj_body(x_ref, g_ref, w_ref, *o_refs, splits):
    xn = _norm_bf16(x_ref, g_ref)
    for (a, b), o_ref in zip(splits, o_refs):
        o_ref[...] = jnp.dot(xn, w_ref[:, a:b], preferred_element_type=jnp.float32)


def norm_proj(x, gain, w, splits):
    bsz, t, k = x.shape
    m = bsz * t
    tm = _row_tile(m)
    outs = pl.pallas_call(
        functools.partial(_norm_proj_body, splits=splits),
        grid=(m // tm,),
        in_specs=[pl.BlockSpec((tm, k), lambda i: (i, 0)),
                  pl.BlockSpec((1, k), lambda i: (0, 0)),
                  pl.BlockSpec(w.shape, lambda i: (0, 0))],
        out_specs=[pl.BlockSpec((tm, b - a), lambda i: (i, 0)) for a, b in splits],
        out_shape=[jax.ShapeDtypeStruct((m, b - a), jnp.float32) for a, b in splits],
        compiler_params=pltpu.CompilerParams(dimension_semantics=("parallel",), vmem_limit_bytes=VMEM_LIMIT),
        name="norm_proj",
    )(x.reshape(m, k), gain.reshape(1, k), w)
    return [o.reshape(bsz, t, -1) for o in outs]


def _proj_res_body(*refs, n_in):
    x_refs, w_refs, res_ref, o_ref = refs[:n_in], refs[n_in:2 * n_in], refs[2 * n_in], refs[2 * n_in + 1]
    acc = res_ref[...]
    for x_ref, w_ref in zip(x_refs, w_refs):
        acc = acc + jnp.dot(x_ref[...].astype(jnp.bfloat16), w_ref[...], preferred_element_type=jnp.float32)
    o_ref[...] = acc


def proj_residual(xs, ws, res):
    bsz, t, n = res.shape
    m = bsz * t
    tm = _row_tile(m)
    out = pl.pallas_call(
        functools.partial(_proj_res_body, n_in=len(xs)),
        grid=(m // tm,),
        in_specs=([pl.BlockSpec((tm, x.shape[-1]), lambda i: (i, 0)) for x in xs]
                  + [pl.BlockSpec(w.shape, lambda i: (0, 0)) for w in ws]
                  + [pl.BlockSpec((tm, n), lambda i: (i, 0))]),
        out_specs=pl.BlockSpec((tm, n), lambda i: (i, 0)),
        out_shape=jax.ShapeDtypeStruct((m, n), jnp.float32),
        compiler_params=pltpu.CompilerParams(dimension_semantics=("parallel",), vmem_limit_bytes=VMEM_LIMIT),
        name="proj_residual",
    )(*[x.reshape(m, x.shape[-1]) for x in xs], *ws, res.reshape(m, n))
    return out.reshape(bsz, t, n)


FFN_COL_CHUNKS = 2


def _ffn_up_body(x_ref, g_ref, wg_ref, wu_ref, h_ref):
    xn = _norm_bf16(x_ref, g_ref)
    cw = h_ref.shape[1] // FFN_COL_CHUNKS
    for c in range(FFN_COL_CHUNKS):
        cols = slice(c * cw, (c + 1) * cw)
        g = jnp.dot(xn, wg_ref[:, cols], preferred_element_type=jnp.float32)
        u = jnp.dot(xn, wu_ref[:, cols], preferred_element_type=jnp.float32)
        h_ref[:, cols] = (g * jax.nn.sigmoid(g) * u).astype(h_ref.dtype)


def ffn(y, gain, wg, wu, wd):
    bsz, t, k = y.shape
    m = bsz * t
    tm = _row_tile(m)
    f = wg.shape[1]
    h = pl.pallas_call(
        _ffn_up_body,
        grid=(m // tm,),
        in_specs=[pl.BlockSpec((tm, k), lambda i: (i, 0)),
                  pl.BlockSpec((1, k), lambda i: (0, 0)),
                  pl.BlockSpec(wg.shape, lambda i: (0, 0)),
                  pl.BlockSpec(wu.shape, lambda i: (0, 0))],
        out_specs=pl.BlockSpec((tm, f), lambda i: (i, 0)),
        out_shape=jax.ShapeDtypeStruct((m, f), jnp.bfloat16),
        compiler_params=pltpu.CompilerParams(dimension_semantics=("parallel",), vmem_limit_bytes=VMEM_LIMIT),
        name="ffn_up",
    )(y.reshape(m, k), gain.reshape(1, k), wg, wu)
    return proj_residual([h.reshape(bsz, t, f)], [wd], y)


def rmsnorm(x, g, eps=1e-6):
    xf = x.astype(jnp.float32)
    y = xf * lax.rsqrt(jnp.mean(xf * xf, axis=-1, keepdims=True) + eps)
    return (y * g.astype(jnp.float32)).astype(x.dtype)


def rope(x, pos):
    half = ROPE_DIM // 2
    inv = ROPE_THETA ** (-jnp.arange(half, dtype=jnp.float32) / half)
    ang = pos.astype(jnp.float32)[:, None] * inv[None, :]
    shp = (pos.shape[0],) + (1,) * (x.ndim - 3) + (half,)
    cos, sin = jnp.cos(ang).reshape(shp), jnp.sin(ang).reshape(shp)
    xr = x[..., :ROPE_DIM].astype(jnp.float32)
    x1, x2 = xr[..., :half], xr[..., half:]
    rot = jnp.concatenate([x1 * cos - x2 * sin, x2 * cos + x1 * sin], axis=-1)
    return jnp.concatenate([rot.astype(x.dtype), x[..., ROPE_DIM:]], axis=-1)


def masked_softmax(s, mask):
    p = jax.nn.softmax(jnp.where(mask, s, -1e30), axis=-1)
    return jnp.where(mask, p, 0.0)


def nsa_project(h, pos, qn, kn):
    B, T = h.shape[:2]
    q = h[..., :NSA_Q].reshape(B, T, NSA_HEADS, HEAD_DIM)
    kv = h[..., NSA_Q:NSA_Q + NSA_KVW].reshape(B, T, 3, 2, NSA_KV, HEAD_DIM)
    gates = jax.nn.sigmoid(h[..., NSA_Q + NSA_KVW:NSA_IN].astype(jnp.float32)).reshape(B, T, NSA_HEADS, 3)
    q = rope(rmsnorm(q, qn), pos)
    k = rope(rmsnorm(kv[:, :, :, 0], kn[:, None, :]), pos)
    kv = jnp.stack([k, kv[:, :, :, 1]], axis=3)
    rows = kv[:, :, :2].reshape(B, T, 4, NSA_KV, HEAD_DIM)
    win = kv[:, :, 2]
    return q, gates, rows, win


def compress(k, w1, w2, pe):
    B, L = k.shape[:2]
    n_chunks = L // CMP_STRIDE
    c = k[:, :n_chunks * CMP_STRIDE].reshape(B, n_chunks, CMP_STRIDE, NSA_KV, HEAD_DIM)
    h = (jnp.einsum('bnlgd,ldh->bngh', c[:, :-1], w1[:CMP_STRIDE])
         + jnp.einsum('bnlgd,ldh->bngh', c[:, 1:], w1[CMP_STRIDE:])
         + jnp.einsum('ld,ldh->h', pe, w1))
    return jnp.einsum('bngh,hd->bngd', jax.nn.gelu(h), w2)


def nsa_keys(rows, cw1, cw2, cpe):
    B, L = rows.shape[:2]
    kc = compress(rows[:, :, 0], cw1[0], cw2[0], cpe[0])
    vc = compress(rows[:, :, 1], cw1[1], cw2[1], cpe[1])
    n_cmp = kc.shape[1]
    cmp_start = jnp.arange(n_cmp, dtype=jnp.int32) * CMP_STRIDE
    cmp_end = cmp_start + CMP_BLOCK - 1
    n_sel = -(-L // SEL_BLOCK)
    sel_start = jnp.arange(n_sel, dtype=jnp.int32) * SEL_BLOCK
    sel_map = ((cmp_start[:, None] <= sel_start[None, :] + SEL_BLOCK - 1)
               & (cmp_end[:, None] >= sel_start[None, :])).astype(jnp.float32)
    slc = jnp.pad(rows[:, :, 2:4], ((0, 0), (0, n_sel * SEL_BLOCK - L), (0, 0), (0, 0), (0, 0)))
    slc = slc.reshape(B, n_sel, SEL_BLOCK, 2, NSA_KV, HEAD_DIM).transpose(3, 0, 4, 1, 2, 5)
    return kc, vc, cmp_end, sel_map, slc[0], slc[1]


def nsa_block(q, gates, q_pos, kc, vc, cmp_end, sel_map, ksb, vsb, win, w_pos):
    f32 = jnp.float32
    B, QB = q.shape[:2]
    qg = q.reshape(B, QB, NSA_KV, NSA_REP, HEAD_DIM).astype(f32) * (HEAD_DIM ** -0.5)
    tq = q_pos[:, None]
    s = jnp.einsum('bqgrd,bngd->bqgrn', qg, kc.astype(f32))
    p = masked_softmax(s, (cmp_end[None, :] <= tq)[None, :, None, None, :])
    o_cmp = jnp.einsum('bqgrn,bngd->bqgrd', p, vc.astype(f32))
    imp = jnp.einsum('bqgrn,nj->bqgj', p, sel_map)
    n_sel = sel_map.shape[1]
    blk = jnp.arange(n_sel, dtype=jnp.int32)[None, :]
    cur = tq // SEL_BLOCK
    forced = (blk == 0) | (blk == cur) | (blk == cur - 1)
    valid = blk * SEL_BLOCK <= tq
    imp = jnp.where(valid[None, :, None, :], jnp.where(forced[None, :, None, :], 1e30, imp), -1e30)
    n_top = min(N_SELECT, n_sel)
    _, idx = lax.top_k(imp, n_top)
    bi = jnp.arange(B)[:, None, None, None]
    gi = jnp.arange(NSA_KV)[None, None, :, None]
    nk = n_top * SEL_BLOCK
    k_sel = ksb[bi, gi, idx].reshape(B, QB, NSA_KV, nk, HEAD_DIM).astype(f32)
    v_sel = vsb[bi, gi, idx].reshape(B, QB, NSA_KV, nk, HEAD_DIM).astype(f32)
    k_pos = (idx[..., None] * SEL_BLOCK + jnp.arange(SEL_BLOCK, dtype=jnp.int32)).reshape(B, QB, NSA_KV, nk)
    s = jnp.einsum('bqgrd,bqgkd->bqgrk', qg, k_sel)
    p = masked_softmax(s, (k_pos <= q_pos[None, :, None, None])[:, :, :, None, :])
    o_slc = jnp.einsum('bqgrk,bqgkd->bqgrd', p, v_sel)
    s = jnp.einsum('bqgrd,bkgd->bqgrk', qg, win[:, :, 0].astype(f32))
    wp = w_pos[None, :]
    m = (wp <= tq) & (wp > tq - WINDOW) & (wp >= 0)
    p = masked_softmax(s, m[None, :, None, None, :])
    o_win = jnp.einsum('bqgrk,bkgd->bqgrd', p, win[:, :, 1].astype(f32))
    g = gates.reshape(B, QB, NSA_KV, NSA_REP, 3)
    o = g[..., 0:1] * o_cmp + g[..., 1:2] * o_slc + g[..., 2:3] * o_win
    return o.reshape(B, QB, NSA_Q)


NEG = -1e30


def _nsa_cmp_body(q_ref, kt_ref, v_ref, map_ref, o_ref, sel_ref, *, tq, n_cmp, n_sel):
    f32, bf16 = jnp.float32, jnp.bfloat16
    i = pl.program_id(1)
    ncp = kt_ref.shape[2]
    pos = i * tq + lax.broadcasted_iota(jnp.int32, (tq, 1), 0)
    n = lax.broadcasted_iota(jnp.int32, (1, ncp), 1)
    ok = (n * CMP_STRIDE + (CMP_BLOCK - 1) <= pos) & (n < n_cmp)
    psum = jnp.zeros((tq, ncp), f32)
    for r in range(NSA_REP):
        s = jnp.dot(q_ref[0, r], kt_ref[0], preferred_element_type=f32)
        s = jnp.where(ok, s, NEG)
        m = jnp.max(s, axis=-1, keepdims=True)
        e = jnp.where(ok, jnp.exp(s - m), 0.0)
        l = jnp.sum(e, axis=-1, keepdims=True)
        p = e * jnp.where(l > 0.0, 1.0 / l, 0.0)
        o_ref[0, r] = jnp.dot(p.astype(bf16), v_ref[0], preferred_element_type=f32)
        psum = psum + p
    hi = psum.astype(bf16)
    lo = (psum - hi.astype(f32)).astype(bf16)
    imp = (jnp.dot(hi, map_ref[...], preferred_element_type=f32)
           + jnp.dot(lo, map_ref[...], preferred_element_type=f32))
    nsp = map_ref.shape[1]
    blk = lax.broadcasted_iota(jnp.int32, (1, nsp), 1)
    cur = pos // SEL_BLOCK
    valid = (blk * SEL_BLOCK <= pos) & (blk < n_sel)
    forced = (blk == 0) | (blk == cur) | (blk == cur - 1)
    x = jnp.where(valid, jnp.where(forced, 1e30, imp), NEG)
    x = jnp.where(blk < n_sel, x, -3e38)
    blkf = blk.astype(f32)
    sel = jnp.zeros((tq, nsp), f32)
    for _ in range(min(N_SELECT, n_sel)):
        m = jnp.max(x, axis=-1, keepdims=True)
        first = jnp.min(jnp.where(x == m, blkf, float(nsp)), axis=-1, keepdims=True)
        hit = blkf == first
        sel = jnp.where(hit, 1.0, sel)
        x = jnp.where(hit, -3e38, x)
    sel_ref[0] = jnp.where(valid, sel, 0.0).astype(bf16)


def nsa_cmp_select(q4, kct, vc, sel_map, n_cmp, n_sel, *, tq=128):
    BG, _, T, _ = q4.shape
    ncp, nsp = sel_map.shape
    return pl.pallas_call(
        functools.partial(_nsa_cmp_body, tq=tq, n_cmp=n_cmp, n_sel=n_sel),
        grid=(BG, T // tq),
        in_specs=[pl.BlockSpec((1, NSA_REP, tq, HEAD_DIM), lambda b, i: (b, 0, i, 0)),
                  pl.BlockSpec((1, HEAD_DIM, ncp), lambda b, i: (b, 0, 0)),
                  pl.BlockSpec((1, ncp, HEAD_DIM), lambda b, i: (b, 0, 0)),
                  pl.BlockSpec((ncp, nsp), lambda b, i: (0, 0))],
        out_specs=[pl.BlockSpec((1, NSA_REP, tq, HEAD_DIM), lambda b, i: (b, 0, i, 0)),
                   pl.BlockSpec((1, tq, nsp), lambda b, i: (b, i, 0))],
        out_shape=[jax.ShapeDtypeStruct((BG, NSA_REP, T, HEAD_DIM), jnp.float32),
                   jax.ShapeDtypeStruct((BG, T, nsp), jnp.bfloat16)],
        compiler_params=pltpu.CompilerParams(
            dimension_semantics=("parallel", "parallel"), vmem_limit_bytes=VMEM_LIMIT),
        name="nsa_cmp_select",
    )(q4, kct, vc, sel_map)


def _nsa_flash_body(*refs, tq, tk, mode, key_off):
    f32, bf16 = jnp.float32, jnp.bfloat16
    if mode == "select":
        q_ref, kt_ref, v_ref, sel_ref, o_ref, m_scr, l_scr, acc_scr = refs
    else:
        q_ref, kt_ref, v_ref, o_ref, m_scr, l_scr, acc_scr = refs
    i, j = pl.program_id(1), pl.program_id(2)
    nj = pl.num_programs(2)

    @pl.when(j == 0)
    def _():
        m_scr[...] = jnp.full_like(m_scr, NEG)
        l_scr[...] = jnp.zeros_like(l_scr)
        acc_scr[...] = jnp.zeros_like(acc_scr)

    qpos = i * tq + lax.broadcasted_iota(jnp.int32, (tq, 1), 0)
    lane = lax.broadcasted_iota(jnp.int32, (1, tk), 1)
    if mode == "select":
        k0 = j * tk
        active = k0 <= i * tq + (tq - 1)
    else:
        k0 = (i * tq // tk + j) * tk - key_off
        active = j >= 0
    kpos = k0 + lane

    @pl.when(active)
    def _():
        if mode == "select":
            nsp = sel_ref.shape[2]
            c = lax.broadcasted_iota(jnp.int32, (nsp, tk), 0)
            l2 = lax.broadcasted_iota(jnp.int32, (nsp, tk), 1)
            expand = (c == (k0 + l2) // SEL_BLOCK).astype(bf16)
            picked = jnp.dot(sel_ref[0], expand, preferred_element_type=f32)
            allowed = (picked > 0.5) & (kpos <= qpos)
        else:
            allowed = (kpos <= qpos) & (kpos > qpos - WINDOW) & (kpos >= 0)
        for r in range(NSA_REP):
            s = jnp.dot(q_ref[0, r], kt_ref[0], preferred_element_type=f32)
            s = jnp.where(allowed, s, NEG)
            m_old = m_scr[r]
            m_new = jnp.maximum(m_old, jnp.max(s, axis=-1, keepdims=True))
            alpha = jnp.exp(m_old - m_new)
            p = jnp.where(allowed, jnp.exp(s - m_new), 0.0)
            l_scr[r] = alpha * l_scr[r] + jnp.sum(p, axis=-1, keepdims=True)
            acc_scr[r] = alpha * acc_scr[r] + jnp.dot(p.astype(bf16), v_ref[0], preferred_element_type=f32)
            m_scr[r] = m_new

    @pl.when(j == nj - 1)
    def _():
        for r in range(NSA_REP):
            o_ref[0, r] = acc_scr[r] / l_scr[r]


def nsa_flash(q4, kt, v, sel=None, *, mode, tq, tk, key_off=0):
    BG, _, T, _ = q4.shape
    if mode == "select":
        nj = T // tk
        kidx = lambda b, i, j: jnp.minimum(j, (i * tq + tq - 1) // tk)
    else:
        nj = (key_off + tq) // tk
        kidx = lambda b, i, j: i * tq // tk + j
    in_specs = [pl.BlockSpec((1, NSA_REP, tq, HEAD_DIM), lambda b, i, j: (b, 0, i, 0)),
                pl.BlockSpec((1, HEAD_DIM, tk), lambda b, i, j: (b, 0, kidx(b, i, j))),
                pl.BlockSpec((1, tk, HEAD_DIM), lambda b, i, j: (b, kidx(b, i, j), 0))]
    args = [q4, kt, v]
    if mode == "select":
        in_specs.append(pl.BlockSpec((1, tq, sel.shape[2]), lambda b, i, j: (b, i, 0)))
        args.append(sel)
    return pl.pallas_call(
        functools.partial(_nsa_flash_body, tq=tq, tk=tk, mode=mode, key_off=key_off),
        grid=(BG, T // tq, nj),
        in_specs=in_specs,
        out_specs=pl.BlockSpec((1, NSA_REP, tq, HEAD_DIM), lambda b, i, j: (b, 0, i, 0)),
        out_shape=jax.ShapeDtypeStruct((BG, NSA_REP, T, HEAD_DIM), jnp.float32),
        scratch_shapes=[pltpu.VMEM((NSA_REP, tq, 1), jnp.float32),
                        pltpu.VMEM((NSA_REP, tq, 1), jnp.float32),
                        pltpu.VMEM((NSA_REP, tq, HEAD_DIM), jnp.float32)],
        compiler_params=pltpu.CompilerParams(
            dimension_semantics=("parallel", "parallel", "arbitrary"), vmem_limit_bytes=VMEM_LIMIT),
        name="nsa_flash_" + mode,
    )(*args)


def nsa_prompt(q, gates, rows, win, cw1, cw2, cpe):
    bf16 = jnp.bfloat16
    B, T = q.shape[:2]
    G = NSA_KV
    kc = compress(rows[:, :, 0], cw1[0], cw2[0], cpe[0])
    vc = compress(rows[:, :, 1], cw1[1], cw2[1], cpe[1])
    n_cmp = kc.shape[1]
    n_sel = T // SEL_BLOCK
    ncp = -(-n_cmp // LANE) * LANE
    nsp = -(-n_sel // LANE) * LANE
    n = jnp.arange(ncp)[:, None]
    j = jnp.arange(nsp)[None, :]
    sel_map = ((n * CMP_STRIDE <= j * SEL_BLOCK + SEL_BLOCK - 1) & (n * CMP_STRIDE + CMP_BLOCK - 1 >= j * SEL_BLOCK)
               & (n < n_cmp) & (j < n_sel)).astype(bf16)
    q4 = (q * (HEAD_DIM ** -0.5)).reshape(B, T, G, NSA_REP, HEAD_DIM).transpose(0, 2, 3, 1, 4)
    q4 = q4.reshape(B * G, NSA_REP, T, HEAD_DIM).astype(bf16)

    def keys_t(x, front=0, back=0):
        x = jnp.pad(x, ((0, 0), (front, back), (0, 0), (0, 0)))
        return x.transpose(0, 2, 3, 1).reshape(B * G, HEAD_DIM, -1).astype(bf16)

    def vals(x, front=0, back=0):
        x = jnp.pad(x, ((0, 0), (front, back), (0, 0), (0, 0)))
        return x.transpose(0, 2, 1, 3).reshape(B * G, -1, HEAD_DIM).astype(bf16)

    o_cmp, sel = nsa_cmp_select(q4, keys_t(kc, 0, ncp - n_cmp), vals(vc, 0, ncp - n_cmp), sel_map, n_cmp, n_sel)
    o_slc = nsa_flash(q4, keys_t(rows[:, :, 2]), vals(rows[:, :, 3]), sel, mode="select", tq=128, tk=512)
    o_win = nsa_flash(q4, keys_t(win[:, :, 0], WINDOW), vals(win[:, :, 1], WINDOW), mode="window",
                      tq=256, tk=256, key_off=WINDOW)
    g = gates.reshape(B, T, G, NSA_REP, 3)
    un = lambda o: o.reshape(B, G, NSA_REP, T, HEAD_DIM).transpose(0, 3, 1, 2, 4)
    o = g[..., 0:1] * un(o_cmp) + g[..., 1:2] * un(o_slc) + g[..., 2:3] * un(o_win)
    return o.reshape(B, T, NSA_Q)


NSA_TQ = 128
NSA_TK = 512
NSA_TQ_DECODE = 32


def _nsa_attn_body(qt_ref, kc_ref, vct_ref, map_ref, ks_ref, vst_ref, kw_ref, vw_ref, g_ref, o_ref, thr_scr,
                   *, tq, n_cmp, n_sel, q_tile0, win_base, natural):
    f32, bf16 = jnp.float32, jnp.bfloat16
    tk = NSA_TK
    nl = NSA_REP * tq
    i = pl.program_id(1) + q_tile0
    if natural:
        xq = qt_ref[0] * (HEAD_DIM ** -0.5)
        qt = jnp.concatenate([xq[:, HEAD_DIM * r:HEAD_DIM * (r + 1)].T for r in range(NSA_REP)], axis=1).astype(bf16)
    else:
        qt = qt_ref[0, 0]
    qpos = i * tq + lax.broadcasted_iota(jnp.int32, (1, nl), 1) % tq

    ncp = kc_ref.shape[1]
    n = lax.broadcasted_iota(jnp.int32, (ncp, 1), 0)
    ok = (n * CMP_STRIDE + (CMP_BLOCK - 1) <= qpos) & (n < n_cmp)
    s = jnp.dot(kc_ref[0], qt, preferred_element_type=f32)
    s = jnp.where(ok, s, NEG)
    m = jnp.max(s, axis=0, keepdims=True)
    e = jnp.where(ok, jnp.exp(s - m), 0.0)
    l = jnp.sum(e, axis=0, keepdims=True)
    p = e * jnp.where(l > 0.0, 1.0 / l, 0.0)
    o_cmp = jnp.dot(vct_ref[0], p.astype(bf16), preferred_element_type=f32)

    psum = p[:, 0:tq]
    for r in range(1, NSA_REP):
        psum = psum + p[:, r * tq:(r + 1) * tq]
    hi = psum.astype(bf16)
    lo = (psum - hi.astype(f32)).astype(bf16)
    imp = (jnp.dot(map_ref[...], hi, preferred_element_type=f32)
           + jnp.dot(map_ref[...], lo, preferred_element_type=f32))
    nsp = map_ref.shape[0]
    qp = qpos[:, 0:tq]
    blk = lax.broadcasted_iota(jnp.int32, (nsp, 1), 0)
    cur = qp // SEL_BLOCK
    valid = (blk * SEL_BLOCK <= qp) & (blk < n_sel)
    forced = (blk == 0) | (blk == cur) | (blk == cur - 1)
    x = jnp.where(valid, jnp.where(forced, 1e30, imp), NEG)
    x = jnp.where(blk < n_sel, x, -3e38)
    blkf = blk.astype(f32)
    sel = jnp.zeros((nsp, tq), f32)
    for _ in range(min(N_SELECT, n_sel)):
        mx = jnp.max(x, axis=0, keepdims=True)
        first = jnp.min(jnp.where(x == mx, blkf, float(nsp)), axis=0, keepdims=True)
        hit = blkf == first
        sel = jnp.where(hit, 1.0, sel)
        x = jnp.where(hit, -3e38, x)
    thr = jnp.where(valid & (sel > 0.5), qp, -1)
    thr_scr[...] = jnp.concatenate([thr] * NSA_REP, axis=1)

    def flash(s, allowed, vt, carry):
        m_old, l_old, acc = carry
        m_new = jnp.maximum(m_old, jnp.max(s, axis=0, keepdims=True))
        alpha = jnp.exp(m_old - m_new)
        p = jnp.exp(s - m_new)
        if allowed is not None:
            p = jnp.where(allowed, p, 0.0)
        l_new = alpha * l_old + jnp.sum(p, axis=0, keepdims=True)
        acc = alpha * acc + jnp.dot(vt, p.astype(bf16), preferred_element_type=f32)
        return m_new, l_new, acc

    init = (jnp.full((1, nl), NEG, f32), jnp.zeros((1, nl), f32), jnp.zeros((HEAD_DIM, nl), f32))

    krow = lax.broadcasted_iota(jnp.int32, (SEL_BLOCK, 1), 0)

    def masked_scores(j):
        s = jnp.dot(ks_ref[0, j], qt, preferred_element_type=f32)
        th = thr_scr[pl.ds(pl.multiple_of(j * (tk // SEL_BLOCK), 8), tk // SEL_BLOCK), :]
        return jnp.concatenate(
            [jnp.where((j * tk + b * SEL_BLOCK + krow) <= th[b:b + 1], s[b * SEL_BLOCK:(b + 1) * SEL_BLOCK], NEG)
             for b in range(tk // SEL_BLOCK)], axis=0)

    def sel_pair(j2, carry):
        m_old, l_old, acc = carry
        s0, s1 = masked_scores(2 * j2), masked_scores(2 * j2 + 1)
        m_new = jnp.maximum(m_old, jnp.maximum(jnp.max(s0, axis=0, keepdims=True), jnp.max(s1, axis=0, keepdims=True)))
        alpha = jnp.exp(m_old - m_new)
        p0, p1 = jnp.exp(s0 - m_new), jnp.exp(s1 - m_new)
        l_new = alpha * l_old + jnp.sum(p0, axis=0, keepdims=True) + jnp.sum(p1, axis=0, keepdims=True)
        acc = (alpha * acc + jnp.dot(vst_ref[0, 2 * j2], p0.astype(bf16), preferred_element_type=f32)
               + jnp.dot(vst_ref[0, 2 * j2 + 1], p1.astype(bf16), preferred_element_type=f32))
        return m_new, l_new, acc

    n_tiles = (i * tq + tq - 1) // tk + 1
    carry = lax.fori_loop(0, n_tiles // 2, sel_pair, init)
    carry = lax.cond(n_tiles % 2 == 1,
                     lambda c: flash(masked_scores(n_tiles - 1), None, vst_ref[0, n_tiles - 1], c),
                     lambda c: c, carry)
    _, l_s, acc_s = carry
    o_slc = acc_s / l_s

    nkw = WINDOW + tq
    k_first = i * tq - WINDOW
    row0 = pl.multiple_of(k_first - win_base, 16)
    kpos = k_first + lax.broadcasted_iota(jnp.int32, (nkw, 1), 0)
    allowed = (kpos <= qpos) & (kpos > qpos - WINDOW) & (kpos >= 0)
    s = jnp.dot(kw_ref[0, pl.ds(row0, nkw), :], qt, preferred_element_type=f32)
    s = jnp.where(allowed, s, NEG)
    e = jnp.where(allowed, jnp.exp(s - jnp.max(s, axis=0, keepdims=True)), 0.0)
    o_win = (lax.dot_general(vw_ref[0, pl.ds(row0, nkw), :], e.astype(bf16), (((0,), (0,)), ((), ())),
                             preferred_element_type=f32) / jnp.sum(e, axis=0, keepdims=True))

    g = g_ref[0, 0]
    res = g[0:1] * o_cmp + g[1:2] * o_slc + g[2:3] * o_win
    if natural:
        o_ref[0] = jnp.concatenate([res[:, tq * r:tq * (r + 1)].T for r in range(NSA_REP)], axis=1)
    else:
        o_ref[0, 0] = res


def nsa_attention(q, gates, kc, vc, ksl, vsl, kwin, vwin, *, n_keys, tq=NSA_TQ, q_pos0=0, win_pos0=0):
    f32, bf16 = jnp.float32, jnp.bfloat16
    B, T = q.shape[:2]
    G, R, D = NSA_KV, NSA_REP, HEAD_DIM
    tk = NSA_TK
    nq, nl = T // tq, R * tq
    n_cmp, n_sel = kc.shape[1], -(-n_keys // SEL_BLOCK)
    lk = ksl.shape[1]
    assert T % tq == 0 and nl % LANE == 0 and lk % tk == 0 and q_pos0 % tq == 0 and tq % 16 == 0
    assert lk >= q_pos0 + T and win_pos0 <= max(q_pos0 - WINDOW, 0)
    win_base = win_pos0 - WINDOW
    lw = q_pos0 + T - win_base
    wpad = lambda x: jnp.pad(x[:, :lw - WINDOW], ((0, 0), (WINDOW, max(lw - WINDOW - x.shape[1], 0)), (0, 0), (0, 0)))
    kwin, vwin = wpad(kwin), wpad(vwin)
    ncp = -(-n_cmp // LANE) * LANE
    nsp = -(-n_sel // LANE) * LANE
    n = jnp.arange(ncp)[None, :]
    jb = jnp.arange(nsp)[:, None]
    map_t = ((n * CMP_STRIDE <= jb * SEL_BLOCK + SEL_BLOCK - 1) & (n * CMP_STRIDE + CMP_BLOCK - 1 >= jb * SEL_BLOCK)
             & (n < n_cmp) & (jb < n_sel)).astype(bf16)
    natural = tq == LANE
    if natural:
        qt = q.reshape(B, T, NSA_Q)
        q_spec = pl.BlockSpec((1, tq, R * D), lambda b, i: (b // G, i, b % G))
        o_spec, o_shape = q_spec, jax.ShapeDtypeStruct((B, T, NSA_Q), f32)
    else:
        qt = (q * (D ** -0.5)).reshape(B, nq, tq, G, R, D).transpose(0, 3, 1, 5, 4, 2)
        qt = qt.reshape(B * G, nq, D, nl).astype(bf16)
        q_spec = pl.BlockSpec((1, 1, D, nl), lambda b, i: (b, i, 0, 0))
        o_spec, o_shape = q_spec, jax.ShapeDtypeStruct((B * G, nq, D, nl), f32)
    gt = gates.reshape(B, nq, tq, G, R, 3).transpose(0, 3, 1, 5, 4, 2).reshape(B * G, nq, 3, nl)
    gt = jnp.pad(gt, ((0, 0), (0, 0), (0, 5), (0, 0)))
    rows = lambda x, t: x.transpose(0, 2, 1, 3).reshape(B * G, -1, t, D).astype(bf16)
    cols = lambda x, t: x.reshape(B, -1, t, G, D).transpose(0, 3, 1, 4, 2).reshape(B * G, -1, D, t).astype(bf16)
    kcp = jnp.pad(kc, ((0, 0), (0, ncp - n_cmp), (0, 0), (0, 0)))
    vcp = jnp.pad(vc, ((0, 0), (0, ncp - n_cmp), (0, 0), (0, 0)))
    kc_r = rows(kcp, ncp)[:, 0]
    vc_c = cols(vcp, ncp)[:, 0]
    full = lambda shape: pl.BlockSpec((1,) + shape, lambda b, i: (b,) + (0,) * len(shape))
    per_q = lambda shape: pl.BlockSpec((1, 1) + shape, lambda b, i: (b, i) + (0,) * len(shape))
    out = pl.pallas_call(
        functools.partial(_nsa_attn_body, tq=tq, n_cmp=n_cmp, n_sel=n_sel,
                          q_tile0=q_pos0 // tq, win_base=win_base, natural=natural),
        grid=(B * G, nq),
        in_specs=[q_spec, full((ncp, D)), full((D, ncp)),
                  pl.BlockSpec((nsp, ncp), lambda b, i: (0, 0)),
                  full((lk // tk, tk, D)), full((lk // tk, D, tk)),
                  full((lw, D)), full((lw, D)),
                  per_q((8, nl))],
        out_specs=o_spec,
        out_shape=o_shape,
        scratch_shapes=[pltpu.VMEM((nsp, nl), jnp.int32)],
        compiler_params=pltpu.CompilerParams(
            dimension_semantics=("parallel", "parallel"), vmem_limit_bytes=VMEM_LIMIT),
        name="nsa_attention",
    )(qt, kc_r, vc_c, map_t, rows(ksl, tk), cols(vsl, tk), rows(kwin, lw)[:, 0], rows(vwin, lw)[:, 0], gt)
    if natural:
        return out
    return out.reshape(B, G, nq, D, R, tq).transpose(0, 2, 5, 1, 4, 3).reshape(B, T, NSA_Q)


def nsa_prompt(q, gates, rows, win, cw1, cw2, cpe):
    B, T = q.shape[:2]
    kc = compress_pair(rows[:, :, 0].reshape(B, T, -1), cw1[0], cw2[0], cpe[0])
    vc = compress_pair(rows[:, :, 1].reshape(B, T, -1), cw1[1], cw2[1], cpe[1])
    return nsa_attention(q, gates, kc, vc, rows[:, :, 2], rows[:, :, 3], win[:, :, 0], win[:, :, 1],
                         n_keys=T)


def compress_pair(x, w1, w2, pe):
    B, L, gd = x.shape
    n_chunks = L // CMP_STRIDE
    c = x[:, :n_chunks * CMP_STRIDE].reshape(B, n_chunks, CMP_STRIDE, gd)
    eye = jnp.eye(NSA_KV, dtype=w1.dtype)
    big = lambda w: jnp.einsum('ldh,gk->lgdkh', w, eye).reshape(CMP_STRIDE, gd, NSA_KV * CMP_HIDDEN)
    a = jnp.einsum('bnlx,lxy->bny', c, big(w1[:CMP_STRIDE]))
    bc = jnp.einsum('bnlx,lxy->bny', c, big(w1[CMP_STRIDE:]))
    h = a[:, :-1] + bc[:, 1:] + jnp.tile(jnp.einsum('ld,ldh->h', pe, w1), NSA_KV)
    w2_big = jnp.einsum('hd,gk->ghkd', w2, eye).reshape(NSA_KV * CMP_HIDDEN, gd)
    return (jax.nn.gelu(h) @ w2_big).reshape(B, n_chunks - 1, NSA_KV, HEAD_DIM)


def nsa_sample(q, gates, past, rows, win_all, cw1, cw2, cpe):
    DB, S = q.shape[:2]
    P, WB = past.shape[1], win_all.shape[1] - S
    gd = NSA_KV * HEAD_DIM
    kind = lambda w: past[:, :, w * gd:(w + 1) * gd]
    new = lambda w: rows[:, :, w].reshape(DB, S, gd)
    up = lambda n, m: -(-n // m) * m
    pad_rows = lambda x, n: jnp.pad(x, ((0, 0), (0, n - x.shape[1])) + ((0, 0),) * (x.ndim - 2))
    n_chunk_rows = (P + S) // CMP_STRIDE * CMP_STRIDE
    cmp_in = [kind(w)[:, :n_chunk_rows] if n_chunk_rows <= P
              else jnp.concatenate([kind(w), new(w)[:, :n_chunk_rows - P]], axis=1) for w in (0, 1)]
    kc = compress_pair(cmp_in[0], cw1[0], cw2[0], cpe[0])
    vc = compress_pair(cmp_in[1], cw1[1], cw2[1], cpe[1])
    tq = up(S, NSA_TQ_DECODE)
    lk = up(P + tq, NSA_TK)
    groups = lambda x: x.reshape(DB, -1, NSA_KV, HEAD_DIM)
    ksl = groups(pad_rows(jnp.concatenate([kind(2), new(2)], axis=1), lk))
    vsl = groups(pad_rows(jnp.concatenate([kind(3), new(3)], axis=1), lk))
    o = nsa_attention(pad_rows(q, tq), pad_rows(gates, tq), kc, vc, ksl, vsl, win_all[:, :, 0], win_all[:, :, 1],
                      n_keys=P + S, tq=tq, q_pos0=P, win_pos0=P - WB)
    return o[:, :S]


RW_PAIRS = RWKV_HEADS // 2
RW_CHUNK = 64


def _rwkv_scan_body(r_ref, w_ref, k_ref, kk_ref, b_ref, c1_ref, c2_ref, vt_ref, s0_ref, oh_ref,
                    y_ref, st_ref, vhi_scr, *, nb, tc):
    f32, bf16 = jnp.float32, jnp.bfloat16
    c = pl.program_id(1)

    @pl.when(c == 0)
    def _():
        st_ref[...] = s0_ref[...]

    row = lax.broadcasted_iota(jnp.int32, (2 * LANE, LANE), 0)
    col = lax.broadcasted_iota(jnp.int32, (2 * LANE, LANE), 1)
    ones2 = (((row // HEAD_DIM) % 2) == (col // HEAD_DIM)).astype(bf16)
    ones1 = ones2[:LANE]
    lane_t = lax.broadcasted_iota(jnp.int32, (HEAD_DIM, LANE), 1) % HEAD_DIM

    vt = vt_ref[...].reshape(nb * RW_PAIRS, HEAD_DIM, LANE)
    vhi_scr[...] = vt.astype(bf16)
    y_ref[...] = jnp.zeros_like(y_ref)

    pairs = [(b, p) for b in range(nb) for p in range(RW_PAIRS)]
    n_pairs = len(pairs)

    def split2(x):
        hi = x.astype(bf16)
        return jnp.concatenate([hi, (x - hi.astype(f32)).astype(bf16)], axis=1)

    def group(t8, carry):
        t0 = pl.multiple_of(t8 * 8, 8)
        for j in range(0, 8, 2):
            ra = lambda ref, b, sl: ref[b, pl.ds(t0, 8), sl][j:j + 1]
            rb = lambda ref, b, sl: ref[b, pl.ds(t0, 8), sl][j + 1:j + 2]
            oh_a, oh_b = oh_ref[t0 + j], oh_ref[t0 + j + 1]
            vl = jnp.concatenate([vhi_scr[i] * oh for oh in (oh_a, oh_b) for i in range(n_pairs)], axis=0)
            vb_all = jnp.dot(vl, ones1, preferred_element_type=f32)
            pieces = []
            for b, p in pairs:
                sl = slice(LANE * p, LANE * (p + 1))
                s = st_ref[b, p]
                pieces.append(split2(s * ra(kk_ref, b, sl)))
                pieces.append(split2(s * (ra(w_ref, b, sl) * rb(kk_ref, b, sl))))
            u_all = jnp.dot(jnp.concatenate(pieces, axis=0), ones2, preferred_element_type=f32)
            py = []
            for i, (b, p) in enumerate(pairs):
                sl = slice(LANE * p, LANE * (p + 1))
                u1 = u_all[2 * HEAD_DIM * i:2 * HEAD_DIM * i + HEAD_DIM]
                u2 = u_all[2 * HEAD_DIM * i + HEAD_DIM:2 * HEAD_DIM * (i + 1)]
                vb_a = vb_all[HEAD_DIM * i:HEAD_DIM * (i + 1)]
                vb_b = vb_all[HEAD_DIM * (n_pairs + i):HEAD_DIM * (n_pairs + i + 1)]
                sa_b = u2 - u1 * ra(c1_ref, b, sl) + vb_a * ra(c2_ref, b, sl)
                s_a = st_ref[b, p] * ra(w_ref, b, sl) - u1 * ra(b_ref, b, sl) + vb_a * ra(k_ref, b, sl)
                s_b = s_a * rb(w_ref, b, sl) - sa_b * rb(b_ref, b, sl) + vb_b * rb(k_ref, b, sl)
                st_ref[b, p] = s_b
                py.append((s_a * ra(r_ref, b, sl)).astype(bf16))
                py.append((s_b * rb(r_ref, b, sl)).astype(bf16))
            y_all = jnp.dot(jnp.concatenate(py, axis=0), ones1, preferred_element_type=f32)
            m_a, m_b = lane_t == t0 + j, lane_t == t0 + j + 1
            for i, (b, p) in enumerate(pairs):
                y_a = y_all[2 * HEAD_DIM * i:2 * HEAD_DIM * i + HEAD_DIM]
                y_b = y_all[2 * HEAD_DIM * i + HEAD_DIM:2 * HEAD_DIM * (i + 1)]
                y_ref[b, 0, p] = jnp.where(m_a, y_a, jnp.where(m_b, y_b, y_ref[b, 0, p]))
        return carry

    lax.fori_loop(0, tc // 8, group, 0)


def rwkv_scan(r, w, k, v, kk, b, S0):
    f32 = jnp.float32
    B, T, W = r.shape
    tc = RW_CHUNK if T % RW_CHUNK == 0 else T
    assert tc <= RW_CHUNK and T % tc == 0 and tc % 8 == 0 and B % 2 == 0
    nc, nb = T // tc, 2
    vt = v.reshape(B, nc, tc, RW_PAIRS, 2, HEAD_DIM).transpose(0, 1, 3, 5, 4, 2)
    vt = jnp.pad(vt, ((0, 0),) * 5 + ((0, RW_CHUNK - tc),)).reshape(B, nc, RW_PAIRS, HEAD_DIM, LANE)
    s0 = S0.astype(f32).reshape(B, RW_PAIRS, 2, HEAD_DIM, HEAD_DIM).transpose(0, 1, 3, 2, 4)
    s0 = s0.reshape(B, RW_PAIRS, HEAD_DIM, LANE)
    oh = jnp.arange(LANE)[None, None, :] % HEAD_DIM == jnp.arange(RW_CHUNK)[:, None, None]
    oh = jnp.broadcast_to(oh, (RW_CHUNK, HEAD_DIM, LANE)).astype(jnp.bfloat16)
    kk_next = jnp.concatenate([kk[:, 1:], kk[:, :1]], axis=1)

    def head_dot(x):
        d = jnp.sum((x * kk_next).reshape(B, T, RWKV_HEADS, HEAD_DIM), axis=-1, keepdims=True)
        return jnp.broadcast_to(d, (B, T, RWKV_HEADS, HEAD_DIM)).reshape(B, T, W)

    c1, c2 = head_dot(b), head_dot(k)
    tok = pl.BlockSpec((nb, tc, W), lambda i, c: (i, c, 0))
    chk = pl.BlockSpec((nb, 1, RW_PAIRS, HEAD_DIM, LANE), lambda i, c: (i, c, 0, 0, 0))
    stt = pl.BlockSpec((nb, RW_PAIRS, HEAD_DIM, LANE), lambda i, c: (i, 0, 0, 0))
    y, st = pl.pallas_call(
        functools.partial(_rwkv_scan_body, nb=nb, tc=tc),
        grid=(B // nb, nc),
        in_specs=[tok, tok, tok, tok, tok, tok, tok, chk, stt,
                  pl.BlockSpec((RW_CHUNK, HEAD_DIM, LANE), lambda i, c: (0, 0, 0))],
        out_specs=[chk, stt],
        out_shape=[jax.ShapeDtypeStruct((B, nc, RW_PAIRS, HEAD_DIM, LANE), f32),
                   jax.ShapeDtypeStruct((B, RW_PAIRS, HEAD_DIM, LANE), f32)],
        scratch_shapes=[pltpu.VMEM((nb * RW_PAIRS, HEAD_DIM, LANE), jnp.bfloat16)],
        compiler_params=pltpu.CompilerParams(
            dimension_semantics=("parallel", "arbitrary"), vmem_limit_bytes=VMEM_LIMIT),
        name="rwkv_scan",
    )(r, w, k, kk, b, c1, c2, vt, s0, oh)
    y = y.reshape(B, nc, RW_PAIRS, HEAD_DIM, 2, RW_CHUNK)[..., :tc]
    y = y.transpose(0, 1, 5, 2, 4, 3).reshape(B, T, W)
    st = st.reshape(B, RW_PAIRS, HEAD_DIM, 2, HEAD_DIM).transpose(0, 1, 3, 2, 4)
    return y, st.reshape(B, RWKV_HEADS, HEAD_DIM, HEAD_DIM)


def rwkv_mix(p, prev, S0, mu, w0, wB, a0, aB, gB, k_k, k_a, r_k, ln_w, ln_b):
    f32 = jnp.float32
    B, T = p.shape[:2]
    W = RWKV_WIDTH
    p = p.astype(f32)
    p_prev = jnp.concatenate([prev.astype(f32)[:, None], p[:, :-1]], axis=1)
    ps = p + mu * (p_prev - p)
    r, k, v = ps[..., :W], ps[..., W:2 * W], ps[..., 2 * W:3 * W]
    o = 3 * W
    xw, xa, xg = ps[..., o:o + LORA_W], ps[..., o + LORA_W:o + LORA_W + LORA_A], ps[..., o + LORA_W + LORA_A:]
    z = w0 + jnp.tanh(xw) @ wB
    w = jnp.exp(-jnp.exp(-jax.nn.softplus(-z) - 0.5))
    a = jax.nn.sigmoid(a0 + xa @ aB)
    g = jax.nn.sigmoid(xg) @ gB
    heads = lambda t: t.reshape(B, T, RWKV_HEADS, HEAD_DIM)
    kk = heads(k * k_k)
    kk = kk * lax.rsqrt(jnp.maximum(jnp.sum(kk * kk, axis=-1, keepdims=True), 1e-24))
    k = k * (1.0 + (a - 1.0) * k_a)
    kk = kk.reshape(B, T, W)
    y, S = rwkv_scan(r, w, k, v, kk, kk * a, S0)
    y, k = heads(y), heads(k)
    r, v = heads(r), heads(v)
    mean = jnp.mean(y, axis=-1, keepdims=True)
    var = jnp.mean(jnp.square(y - mean), axis=-1, keepdims=True)
    y = ((y - mean) * lax.rsqrt(var + RWKV_GN_EPS)).reshape(B, T, W) * ln_w + ln_b
    y = y + (jnp.sum(r * k * r_k, axis=-1, keepdims=True) * v).reshape(B, T, W)
    return y * g, S, p[:, -1]


def gla_chunked(q, k, v, logg, S0):
    B, T, H, DK = q.shape
    DV = v.shape[-1]
    C = GLA_CHUNK if T % GLA_CHUNK == 0 else T
    N = T // C
    to_chunks = lambda x: x.reshape(B, N, C, H, x.shape[-1]).transpose(1, 0, 3, 2, 4)
    causal = jnp.tril(jnp.ones((C, C), dtype=bool))[:, :, None]

    def step(S, inp):
        qc, kc, vc, gc = inp
        b = jnp.cumsum(gc, axis=2)
        diff = b[:, :, :, None, :] - b[:, :, None, :, :]
        decay = jnp.exp(jnp.where(causal, diff, -jnp.inf))
        A = jnp.einsum('bhtd,bhsd,bhtsd->bhts', qc, kc, decay)
        o = jnp.einsum('bhts,bhsv->bhtv', A, vc) + jnp.einsum('bhtk,bhkv->bhtv', qc * jnp.exp(b), S)
        b_last = b[:, :, -1:, :]
        S = jnp.exp(b_last[:, :, 0, :])[..., None] * S + jnp.einsum('bhsk,bhsv->bhkv', kc * jnp.exp(b_last - b), vc)
        return S, o

    S, o = lax.scan(step, S0, (to_chunks(q), to_chunks(k), to_chunks(v), to_chunks(logg)))
    return o.transpose(1, 0, 3, 2, 4).reshape(B, T, H, DV), S


GLA_SUB = 16


def _gla_body(q_ref, k_ref, v_ref, g_ref, r_ref, gn_ref, o_ref, st_ref):
    f32, bf16 = jnp.float32, jnp.bfloat16
    C, SUB = GLA_CHUNK, GLA_SUB
    nsub = C // SUB
    c = pl.program_id(1)

    @pl.when(c == 0)
    def _():
        st_ref[...] = jnp.zeros_like(st_ref)

    ti = lax.broadcasted_iota(jnp.int32, (C, C), 0)
    si = lax.broadcasted_iota(jnp.int32, (C, C), 1)
    tri = (si <= ti).astype(bf16)
    sub_t = lax.broadcasted_iota(jnp.int32, (SUB, 1), 0)
    sub_l = lax.broadcasted_iota(jnp.int32, (SUB, SUB), 1)
    for h in range(GLA_HEADS):
        kq = slice(h * GLA_DK, (h + 1) * GLA_DK)
        vv = slice(h * GLA_DV, (h + 1) * GLA_DV)
        q = q_ref[0, :, kq] * (GLA_DK ** -0.5)
        k = k_ref[0, :, kq]
        v = v_ref[0, :, vv].astype(bf16)
        g = g_ref[0, :, kq]
        g1 = g.astype(bf16)
        g2 = (g - g1.astype(f32)).astype(bf16)
        g3 = (g - g1.astype(f32) - g2.astype(f32)).astype(bf16)
        b = (jnp.dot(tri, g1, preferred_element_type=f32) + jnp.dot(tri, g2, preferred_element_type=f32)
             + jnp.dot(tri, g3, preferred_element_type=f32))
        b_last = b[C - 1:C]
        a_rows = []
        for I in range(nsub):
            rows = slice(I * SUB, (I + 1) * SUB)
            beta = b[I * SUB - 1:I * SUB] if I > 0 else jnp.zeros((1, GLA_DK), f32)
            qb, kb, bb = q[rows], k[rows], b[rows]
            a_diag = jnp.zeros((SUB, SUB), f32)
            for s in range(SUB):
                d = jnp.where(sub_t >= s, bb - bb[s:s + 1], -jnp.inf)
                col = jnp.sum(qb * kb[s:s + 1] * jnp.exp(d), axis=1, keepdims=True)
                a_diag = jnp.where(sub_l == s, col, a_diag)
            blocks = []
            if I > 0:
                qe = (qb * jnp.exp(bb - beta)).astype(bf16)
                ke = (k[:I * SUB] * jnp.exp(beta - b[:I * SUB])).astype(bf16)
                blocks.append(lax.dot_general(qe, ke, (((1,), (1,)), ((), ())), preferred_element_type=f32))
            blocks.append(a_diag)
            if I < nsub - 1:
                blocks.append(jnp.zeros((SUB, C - (I + 1) * SUB), f32))
            a_rows.append(jnp.concatenate(blocks, axis=1))
        a = jnp.concatenate(a_rows, axis=0).astype(bf16)
        st = st_ref[0, h]
        qd = (q * jnp.exp(b)).astype(bf16)
        o = (jnp.dot(a, v, preferred_element_type=f32)
             + lax.dot_general(qd, st.astype(bf16), (((1,), (1,)), ((), ())), preferred_element_type=f32))
        kd = (k * jnp.exp(b_last - b)).astype(bf16)
        st_ref[0, h] = (st * jnp.exp(b_last)
                        + lax.dot_general(v, kd, (((0,), (0,)), ((), ())), preferred_element_type=f32))
        y = o * lax.rsqrt(jnp.mean(o * o, axis=1, keepdims=True) + 1e-6) * gn_ref[...]
        rr = r_ref[0, :, vv]
        o_ref[0, :, vv] = y * (rr * jax.nn.sigmoid(rr))


def gla_mix_prompt(qk, v, gd, r, gate_up, gate_b, gn):
    f32 = jnp.float32
    B, T = qk.shape[:2]
    KW, VW, C = GLA_KW, GLA_VW, GLA_CHUNK
    logg = jax.nn.log_sigmoid(gd[..., :GLA_RANK] @ gate_up + gate_b) / GLA_TAU
    tokv = pl.BlockSpec((1, C, VW), lambda b, c: (b, c, 0))
    out, st = pl.pallas_call(
        _gla_body,
        grid=(B, T // C),
        in_specs=[pl.BlockSpec((1, C, KW), lambda b, c: (b, c, 0)),
                  pl.BlockSpec((1, C, KW), lambda b, c: (b, c, 1)),
                  tokv, pl.BlockSpec((1, C, KW), lambda b, c: (b, c, 0)), tokv,
                  pl.BlockSpec((1, GLA_DV), lambda b, c: (0, 0))],
        out_specs=[tokv, pl.BlockSpec((1, GLA_HEADS, GLA_DV, GLA_DK), lambda b, c: (b, 0, 0, 0))],
        out_shape=[jax.ShapeDtypeStruct((B, T, VW), f32),
                   jax.ShapeDtypeStruct((B, GLA_HEADS, GLA_DV, GLA_DK), f32)],
        compiler_params=pltpu.CompilerParams(
            dimension_semantics=("parallel", "arbitrary"), vmem_limit_bytes=VMEM_LIMIT),
        name="gla_chunk",
    )(qk, qk, v, logg, r, gn.reshape(1, GLA_DV))
    return out, st.transpose(0, 1, 3, 2)


def gla_mix(qk, v, gd, r, S0, gate_up, gate_b, gn):
    f32 = jnp.float32
    B, T = qk.shape[:2]
    q = qk[..., :GLA_KW].reshape(B, T, GLA_HEADS, GLA_DK) * (GLA_DK ** -0.5)
    k = qk[..., GLA_KW:].reshape(B, T, GLA_HEADS, GLA_DK)
    v = v.reshape(B, T, GLA_HEADS, GLA_DV)
    gd = gd[..., :GLA_RANK]
    logg = (jax.nn.log_sigmoid(gd @ gate_up + gate_b) / GLA_TAU).reshape(B, T, GLA_HEADS, GLA_DK)
    out, S = gla_chunked(q, k, v, logg, S0.astype(f32))
    out = rmsnorm(out, gn).reshape(B, T, GLA_VW) * jax.nn.silu(r)
    return out, S


def kernel(x_prompt, x_sample, cache_nsa_kv, cache_nsa_win, state_rwkv, state_rwkv_shift, state_gla, page_table, norm_mix, norm_ffn, w_in_even, nsa_qnorm, nsa_knorm, cmp_w1, cmp_w2, cmp_pe, rwkv_mu, rwkv_w0, rwkv_wB, rwkv_a0, rwkv_aB, rwkv_gB, rwkv_kk, rwkv_ka, rwkv_rk, rwkv_ln_w, rwkv_ln_b, w_out_even, w_in_odd, gla_gate_up, gla_gate_b, gla_norm, w_out_odd, ffn_gate, ffn_up, ffn_down):
    dt = x_prompt.dtype
    bf16 = jnp.bfloat16
    B, T = x_prompt.shape[:2]
    DB, S = x_sample.shape[:2]
    depth = norm_mix.shape[0]
    P = page_table.shape[1] * PAGE_SIZE
    WB = cache_nsa_win.shape[2]
    WP = min(WINDOW, T)
    pos_p = jnp.arange(T, dtype=jnp.int32)
    pos_s = P + jnp.arange(S, dtype=jnp.int32)
    win_pos_s = P - WB + jnp.arange(WB + S, dtype=jnp.int32)
    nsa_pad = -(-NSA_IN // LANE) * LANE
    y_p, y_s = x_prompt, x_sample
    kv_p, kv_s, win_p, win_s, rw_p, rw_s, sh_p, sh_s, gl_p, gl_s = ([] for _ in range(10))
    for layer in range(depth):
        li = layer // 2
        if layer % 2 == 0:
            w_in = jnp.concatenate([jnp.pad(w_in_even[li][:, :NSA_IN], ((0, 0), (0, nsa_pad - NSA_IN))),
                                    w_in_even[li][:, NSA_IN:]], axis=1).astype(bf16)
            even_splits = ((0, nsa_pad), (nsa_pad, nsa_pad + RWKV_IN))
            w_out = w_out_even[li].astype(bf16)
            hn_p, hr_p = norm_proj(y_p, norm_mix[layer], w_in, even_splits)
            hn_s, hr_s = norm_proj(y_s, norm_mix[layer], w_in, even_splits)
            q, g, rows, win = nsa_project(hn_p, pos_p, nsa_qnorm[li], nsa_knorm[li])
            o_nsa_p = nsa_prompt(q, g, rows, win, cmp_w1[li], cmp_w2[li], cmp_pe[li])
            kv_p.append(rows.astype(dt))
            win_p.append(win[:, T - WP:].astype(dt))
            q, g, rows, win = nsa_project(hn_s, pos_s, nsa_qnorm[li], nsa_knorm[li])
            pool = cache_nsa_kv.reshape(-1, PAGE_SIZE, 4 * NSA_KV * HEAD_DIM)
            past = pool[li * cache_nsa_kv.shape[1] + page_table].reshape(DB, P, 4 * NSA_KV * HEAD_DIM)
            win_all = jnp.concatenate([cache_nsa_win[li], win.astype(cache_nsa_win.dtype)], axis=1)
            o_nsa_s = nsa_sample(q, g, past, rows.astype(past.dtype), win_all, cmp_w1[li], cmp_w2[li], cmp_pe[li])
            kv_s.append(rows.astype(dt))
            win_s.append(win_all[:, S:].astype(dt))
            rw_par = (rwkv_mu[li], rwkv_w0[li], rwkv_wB[li], rwkv_a0[li], rwkv_aB[li], rwkv_gB[li],
                      rwkv_kk[li], rwkv_ka[li], rwkv_rk[li], rwkv_ln_w[li], rwkv_ln_b[li])
            o_rw_p, st, last = rwkv_mix(hr_p, jnp.zeros((B, RWKV_IN), dt),
                                        jnp.zeros((B, RWKV_HEADS, HEAD_DIM, HEAD_DIM), dt), *rw_par)
            rw_p.append(st.astype(dt))
            sh_p.append(last.astype(dt))
            o_rw_s, st, last = rwkv_mix(hr_s, state_rwkv_shift[li], state_rwkv[li], *rw_par)
            rw_s.append(st.astype(dt))
            sh_s.append(last.astype(dt))
            w_outs = [w_out[:NSA_Q], w_out[NSA_Q:]]
            y_p = proj_residual([o_nsa_p.astype(dt), o_rw_p.astype(dt)], w_outs, y_p)
            y_s = proj_residual([o_nsa_s.astype(dt), o_rw_s.astype(dt)], w_outs, y_s)
        else:
            wo = w_in_odd[li]
            o2 = 2 * GLA_KW + GLA_VW
            w_in = jnp.concatenate([wo[:, :o2], wo[:, o2 + GLA_RANK:],
                                    jnp.pad(wo[:, o2:o2 + GLA_RANK], ((0, 0), (0, LANE - GLA_RANK)))],
                                   axis=1).astype(bf16)
            odd_splits = ((0, 2 * GLA_KW), (2 * GLA_KW, o2), (o2, o2 + GLA_VW), (o2 + GLA_VW, o2 + GLA_VW + LANE))
            w_out = w_out_odd[li].astype(bf16)
            qk_p, v_p, r_p, gd_p = norm_proj(y_p, norm_mix[layer], w_in, odd_splits)
            qk_s, v_s, r_s, gd_s = norm_proj(y_s, norm_mix[layer], w_in, odd_splits)
            o_p, st = gla_mix_prompt(qk_p, v_p, gd_p, r_p, gla_gate_up[li], gla_gate_b[li], gla_norm[li])
            gl_p.append(st.astype(dt))
            o_s, st = gla_mix(qk_s, v_s, gd_s, r_s, state_gla[li], gla_gate_up[li], gla_gate_b[li], gla_norm[li])
            gl_s.append(st.astype(dt))
            y_p = proj_residual([o_p.astype(dt)], [w_out], y_p)
            y_s = proj_residual([o_s.astype(dt)], [w_out], y_s)
        wg, wu, wd = ffn_gate[layer].astype(bf16), ffn_up[layer].astype(bf16), ffn_down[layer].astype(bf16)
        y_p = ffn(y_p, norm_ffn[layer], wg, wu, wd)
        y_s = ffn(y_s, norm_ffn[layer], wg, wu, wd)
    return (y_p, y_s, jnp.stack(kv_p), jnp.stack(kv_s), jnp.stack(win_p), jnp.stack(win_s),
            jnp.stack(rw_p), jnp.stack(rw_s), jnp.stack(sh_p), jnp.stack(sh_s),
            jnp.stack(gl_p), jnp.stack(gl_s))
```

```python
import functools

import jax
import jax.numpy as jnp
from jax import lax
from jax.experimental import pallas as pl
from jax.experimental.pallas import tpu as pltpu

D_MODEL = 1024
PAGE_SIZE = 128
HEAD_DIM = 64
NSA_HEADS = 8
NSA_KV = 2
NSA_REP = NSA_HEADS // NSA_KV
CMP_STRIDE = 16
CMP_BLOCK = 2 * CMP_STRIDE
CMP_HIDDEN = 2 * HEAD_DIM
SEL_BLOCK = 64
N_SELECT = 16
WINDOW = 512
ROPE_DIM = HEAD_DIM // 4
ROPE_THETA = 500000.0
RWKV_HEADS = 8
RWKV_WIDTH = RWKV_HEADS * HEAD_DIM
LORA_W = 64
LORA_A = 64
LORA_G = 128
RWKV_GN_EPS = 64e-5
GLA_HEADS = 4
GLA_DK = D_MODEL // 2 // GLA_HEADS
GLA_DV = D_MODEL // GLA_HEADS
GLA_RANK = 16
GLA_TAU = 16.0
GLA_CHUNK = 64
D_FF = ((8 * D_MODEL // 3 + 255) // 256) * 256
NSA_Q = NSA_HEADS * HEAD_DIM
NSA_KVW = 3 * 2 * NSA_KV * HEAD_DIM
NSA_IN = NSA_Q + NSA_KVW + 3 * NSA_HEADS
RWKV_IN = 3 * RWKV_WIDTH + LORA_W + LORA_A + LORA_G
EVEN_IN = NSA_IN + RWKV_IN
MIX_WIDTH = NSA_Q + RWKV_WIDTH
GLA_KW = GLA_HEADS * GLA_DK
GLA_VW = GLA_HEADS * GLA_DV
ODD_IN = 2 * GLA_KW + GLA_VW + GLA_RANK + GLA_VW

LANE = 128
VMEM_LIMIT = 48 * 1024 * 1024


ROW_TILE = 512
NORM_EPS = 1e-6


def _row_tile(m):
    return ROW_TILE if m % ROW_TILE == 0 else m


def _norm_bf16(x_ref, g_ref):
    x = x_ref[...]
    y = x * lax.rsqrt(jnp.mean(x * x, axis=-1, keepdims=True) + NORM_EPS)
    return (y * g_ref[...]).astype(jnp.bfloat16)


def _norm_proj_body(x_ref, g_ref, w_ref, *o_refs, splits):
    xn = _norm_bf16(x_ref, g_ref)
    for (a, b), o_ref in zip(splits, o_refs):
        o_ref[...] = jnp.dot(xn, w_ref[:, a:b], preferred_element_type=jnp.float32)


def norm_proj(x, gain, w, splits):
    bsz, t, k = x.shape
    m = bsz * t
    tm = _row_tile(m)
    outs = pl.pallas_call(
        functools.partial(_norm_proj_body, splits=splits),
        grid=(m // tm,),
        in_specs=[pl.BlockSpec((tm, k), lambda i: (i, 0)),
                  pl.BlockSpec((1, k), lambda i: (0, 0)),
                  pl.BlockSpec(w.shape, lambda i: (0, 0))],
        out_specs=[pl.BlockSpec((tm, b - a), lambda i: (i, 0)) for a, b in splits],
        out_shape=[jax.ShapeDtypeStruct((m, b - a), jnp.float32) for a, b in splits],
        compiler_params=pltpu.CompilerParams(dimension_semantics=("parallel",), vmem_limit_bytes=VMEM_LIMIT),
        name="norm_proj",
    )(x.reshape(m, k), gain.reshape(1, k), w)
    return [o.reshape(bsz, t, -1) for o in outs]


def _proj_res_body(*refs, n_in):
    x_refs, w_refs, res_ref, o_ref = refs[:n_in], refs[n_in:2 * n_in], refs[2 * n_in], refs[2 * n_in + 1]
    acc = res_ref[...]
    for x_ref, w_ref in zip(x_refs, w_refs):
        acc = acc + jnp.dot(x_ref[...].astype(jnp.bfloat16), w_ref[...], preferred_element_type=jnp.float32)
    o_ref[...] = acc


def proj_residual(xs, ws, res):
    bsz, t, n = res.shape
    m = bsz * t
    tm = _row_tile(m)
    out = pl.pallas_call(
        functools.partial(_proj_res_body, n_in=len(xs)),
        grid=(m // tm,),
        in_specs=([pl.BlockSpec((tm, x.shape[-1]), lambda i: (i, 0)) for x in xs]
                  + [pl.BlockSpec(w.shape, lambda i: (0, 0)) for w in ws]
                  + [pl.BlockSpec((tm, n), lambda i: (i, 0))]),
        out_specs=pl.BlockSpec((tm, n), lambda i: (i, 0)),
        out_shape=jax.ShapeDtypeStruct((m, n), jnp.float32),
        compiler_params=pltpu.CompilerParams(dimension_semantics=("parallel",), vmem_limit_bytes=VMEM_LIMIT),
        name="proj_residual",
    )(*[x.reshape(m, x.shape[-1]) for x in xs], *ws, res.reshape(m, n))
    return out.reshape(bsz, t, n)


FFN_COL_CHUNKS = 2


def _ffn_up_body(x_ref, g_ref, wg_ref, wu_ref, h_ref):
    xn = _norm_bf16(x_ref, g_ref)
    cw = h_ref.shape[1] // FFN_COL_CHUNKS
    for c in range(FFN_COL_CHUNKS):
        cols = slice(c * cw, (c + 1) * cw)
        g = jnp.dot(xn, wg_ref[:, cols], preferred_element_type=jnp.float32)
        u = jnp.dot(xn, wu_ref[:, cols], preferred_element_type=jnp.float32)
        h_ref[:, cols] = (g * jax.nn.sigmoid(g) * u).astype(h_ref.dtype)


def ffn(y, gain, wg, wu, wd):
    bsz, t, k = y.shape
    m = bsz * t
    tm = _row_tile(m)
    f = wg.shape[1]
    h = pl.pallas_call(
        _ffn_up_body,
        grid=(m // tm,),
        in_specs=[pl.BlockSpec((tm, k), lambda i: (i, 0)),
                  pl.BlockSpec((1, k), lambda i: (0, 0)),
                  pl.BlockSpec(wg.shape, lambda i: (0, 0)),
                  pl.BlockSpec(wu.shape, lambda i: (0, 0))],
        out_specs=pl.BlockSpec((tm, f), lambda i: (i, 0)),
        out_shape=jax.ShapeDtypeStruct((m, f), jnp.bfloat16),
        compiler_params=pltpu.CompilerParams(dimension_semantics=("parallel",), vmem_limit_bytes=VMEM_LIMIT),
        name="ffn_up",
    )(y.reshape(m, k), gain.reshape(1, k), wg, wu)
    return proj_residual([h.reshape(bsz, t, f)], [wd], y)


def rmsnorm(x, g, eps=1e-6):
    xf = x.astype(jnp.float32)
    y = xf * lax.rsqrt(jnp.mean(xf * xf, axis=-1, keepdims=True) + eps)
    return (y * g.astype(jnp.float32)).astype(x.dtype)


def rope(x, pos):
    half = ROPE_DIM // 2
    inv = ROPE_THETA ** (-jnp.arange(half, dtype=jnp.float32) / half)
    ang = pos.astype(jnp.float32)[:, None] * inv[None, :]
    shp = (pos.shape[0],) + (1,) * (x.ndim - 3) + (half,)
    cos, sin = jnp.cos(ang).reshape(shp), jnp.sin(ang).reshape(shp)
    xr = x[..., :ROPE_DIM].astype(jnp.float32)
    x1, x2 = xr[..., :half], xr[..., half:]
    rot = jnp.concatenate([x1 * cos - x2 * sin, x2 * cos + x1 * sin], axis=-1)
    return jnp.concatenate([rot.astype(x.dtype), x[..., ROPE_DIM:]], axis=-1)


def nsa_project(h, pos, qn, kn):
    B, T = h.shape[:2]
    q = h[..., :NSA_Q].reshape(B, T, NSA_HEADS, HEAD_DIM)
    kv = h[..., NSA_Q:NSA_Q + NSA_KVW].reshape(B, T, 3, 2, NSA_KV, HEAD_DIM)
    gates = jax.nn.sigmoid(h[..., NSA_Q + NSA_KVW:NSA_IN].astype(jnp.float32)).reshape(B, T, NSA_HEADS, 3)
    q = rope(rmsnorm(q, qn), pos)
    k = rope(rmsnorm(kv[:, :, :, 0], kn[:, None, :]), pos)
    kv = jnp.stack([k, kv[:, :, :, 1]], axis=3)
    rows = kv[:, :, :2].reshape(B, T, 4, NSA_KV, HEAD_DIM)
    win = kv[:, :, 2]
    return q, gates, rows, win


NEG = -1e30


NSA_TQ = 128
NSA_TK = 512
NSA_TQ_DECODE = 32


def _nsa_attn_body(qt_ref, kc_ref, vct_ref, map_ref, ks_ref, vst_ref, kw_ref, vw_ref, g_ref, o_ref, thr_scr,
                   *, tq, n_cmp, n_sel, q_tile0, win_base, natural):
    f32, bf16 = jnp.float32, jnp.bfloat16
    tk = NSA_TK
    nl = NSA_REP * tq
    i = pl.program_id(1) + q_tile0
    if natural:
        xq = qt_ref[0] * (HEAD_DIM ** -0.5)
        qt = jnp.concatenate([xq[:, HEAD_DIM * r:HEAD_DIM * (r + 1)].T for r in range(NSA_REP)], axis=1).astype(bf16)
    else:
        qt = qt_ref[0, 0]
    qpos = i * tq + lax.broadcasted_iota(jnp.int32, (1, nl), 1) % tq

    ncp = kc_ref.shape[1]
    n = lax.broadcasted_iota(jnp.int32, (ncp, 1), 0)
    ok = (n * CMP_STRIDE + (CMP_BLOCK - 1) <= qpos) & (n < n_cmp)
    s = jnp.dot(kc_ref[0], qt, preferred_element_type=f32)
    s = jnp.where(ok, s, NEG)
    m = jnp.max(s, axis=0, keepdims=True)
    e = jnp.where(ok, jnp.exp(s - m), 0.0)
    l = jnp.sum(e, axis=0, keepdims=True)
    p = e * jnp.where(l > 0.0, 1.0 / l, 0.0)
    o_cmp = jnp.dot(vct_ref[0], p.astype(bf16), preferred_element_type=f32)

    psum = p[:, 0:tq]
    for r in range(1, NSA_REP):
        psum = psum + p[:, r * tq:(r + 1) * tq]
    hi = psum.astype(bf16)
    lo = (psum - hi.astype(f32)).astype(bf16)
    imp = (jnp.dot(map_ref[...], hi, preferred_element_type=f32)
           + jnp.dot(map_ref[...], lo, preferred_element_type=f32))
    nsp = map_ref.shape[0]
    qp = qpos[:, 0:tq]
    blk = lax.broadcasted_iota(jnp.int32, (nsp, 1), 0)
    cur = qp // SEL_BLOCK
    valid = (blk * SEL_BLOCK <= qp) & (blk < n_sel)
    forced = (blk == 0) | (blk == cur) | (blk == cur - 1)
    x = jnp.where(valid, jnp.where(forced, 1e30, imp), NEG)
    x = jnp.where(blk < n_sel, x, -3e38)
    blkf = blk.astype(f32)
    sel = jnp.zeros((nsp, tq), f32)
    for _ in range(min(N_SELECT, n_sel)):
        mx = jnp.max(x, axis=0, keepdims=True)
        first = jnp.min(jnp.where(x == mx, blkf, float(nsp)), axis=0, keepdims=True)
        hit = blkf == first
        sel = jnp.where(hit, 1.0, sel)
        x = jnp.where(hit, -3e38, x)
    thr = jnp.where(valid & (sel > 0.5), qp, -1)
    thr_scr[...] = jnp.concatenate([thr] * NSA_REP, axis=1)

    def flash(s, vt, carry):
        m_old, l_old, acc = carry
        m_new = jnp.maximum(m_old, jnp.max(s, axis=0, keepdims=True))
        alpha = jnp.exp(m_old - m_new)
        p = jnp.exp(s - m_new)
        l_new = alpha * l_old + jnp.sum(p, axis=0, keepdims=True)
        acc = alpha * acc + jnp.dot(vt, p.astype(bf16), preferred_element_type=f32)
        return m_new, l_new, acc

    init = (jnp.full((1, nl), NEG, f32), jnp.zeros((1, nl), f32), jnp.zeros((HEAD_DIM, nl), f32))

    krow = lax.broadcasted_iota(jnp.int32, (SEL_BLOCK, 1), 0)

    def masked_scores(j):
        s = jnp.dot(ks_ref[0, j], qt, preferred_element_type=f32)
        th = thr_scr[pl.ds(pl.multiple_of(j * (tk // SEL_BLOCK), 8), tk // SEL_BLOCK), :]
        return jnp.concatenate(
            [jnp.where((j * tk + b * SEL_BLOCK + krow) <= th[b:b + 1], s[b * SEL_BLOCK:(b + 1) * SEL_BLOCK], NEG)
             for b in range(tk // SEL_BLOCK)], axis=0)

    def sel_pair(j2, carry):
        m_old, l_old, acc = carry
        s0, s1 = masked_scores(2 * j2), masked_scores(2 * j2 + 1)
        m_new = jnp.maximum(m_old, jnp.maximum(jnp.max(s0, axis=0, keepdims=True), jnp.max(s1, axis=0, keepdims=True)))
        alpha = jnp.exp(m_old - m_new)
        p0, p1 = jnp.exp(s0 - m_new), jnp.exp(s1 - m_new)
        l_new = alpha * l_old + jnp.sum(p0, axis=0, keepdims=True) + jnp.sum(p1, axis=0, keepdims=True)
        acc = (alpha * acc + jnp.dot(vst_ref[0, 2 * j2], p0.astype(bf16), preferred_element_type=f32)
               + jnp.dot(vst_ref[0, 2 * j2 + 1], p1.astype(bf16), preferred_element_type=f32))
        return m_new, l_new, acc

    n_tiles = (i * tq + tq - 1) // tk + 1
    carry = lax.fori_loop(0, n_tiles // 2, sel_pair, init)
    carry = lax.cond(n_tiles % 2 == 1,
                     lambda c: flash(masked_scores(n_tiles - 1), vst_ref[0, n_tiles - 1], c),
                     lambda c: c, carry)
    _, l_s, acc_s = carry
    o_slc = acc_s / l_s

    nkw = WINDOW + tq
    k_first = i * tq - WINDOW
    row0 = pl.multiple_of(k_first - win_base, 16)
    kpos = k_first + lax.broadcasted_iota(jnp.int32, (nkw, 1), 0)
    allowed = (kpos <= qpos) & (kpos > qpos - WINDOW) & (kpos >= 0)
    s = jnp.dot(kw_ref[0, pl.ds(row0, nkw), :], qt, preferred_element_type=f32)
    s = jnp.where(allowed, s, NEG)
    e = jnp.where(allowed, jnp.exp(s - jnp.max(s, axis=0, keepdims=True)), 0.0)
    o_win = (lax.dot_general(vw_ref[0, pl.ds(row0, nkw), :], e.astype(bf16), (((0,), (0,)), ((), ())),
                             preferred_element_type=f32) / jnp.sum(e, axis=0, keepdims=True))

    g = g_ref[0, 0]
    res = g[0:1] * o_cmp + g[1:2] * o_slc + g[2:3] * o_win
    if natural:
        o_ref[0] = jnp.concatenate([res[:, tq * r:tq * (r + 1)].T for r in range(NSA_REP)], axis=1)
    else:
        o_ref[0, 0] = res


def nsa_attention(q, gates, kc, vc, ksl, vsl, kwin, vwin, *, n_keys, tq=NSA_TQ, q_pos0=0, win_pos0=0):
    f32, bf16 = jnp.float32, jnp.bfloat16
    B, T = q.shape[:2]
    G, R, D = NSA_KV, NSA_REP, HEAD_DIM
    tk = NSA_TK
    nq, nl = T // tq, R * tq
    n_cmp, n_sel = kc.shape[1], -(-n_keys // SEL_BLOCK)
    lk = ksl.shape[1]
    assert T % tq == 0 and nl % LANE == 0 and lk % tk == 0 and q_pos0 % tq == 0 and tq % 16 == 0
    assert lk >= q_pos0 + T and win_pos0 <= max(q_pos0 - WINDOW, 0)
    win_base = win_pos0 - WINDOW
    lw = q_pos0 + T - win_base
    wpad = lambda x: jnp.pad(x[:, :lw - WINDOW], ((0, 0), (WINDOW, max(lw - WINDOW - x.shape[1], 0)), (0, 0), (0, 0)))
    kwin, vwin = wpad(kwin), wpad(vwin)
    ncp = -(-n_cmp // LANE) * LANE
    nsp = -(-n_sel // LANE) * LANE
    n = jnp.arange(ncp)[None, :]
    jb = jnp.arange(nsp)[:, None]
    map_t = ((n * CMP_STRIDE <= jb * SEL_BLOCK + SEL_BLOCK - 1) & (n * CMP_STRIDE + CMP_BLOCK - 1 >= jb * SEL_BLOCK)
             & (n < n_cmp) & (jb < n_sel)).astype(bf16)
    natural = tq == LANE
    if natural:
        qt = q.reshape(B, T, NSA_Q)
        q_spec = pl.BlockSpec((1, tq, R * D), lambda b, i: (b // G, i, b % G))
        o_spec, o_shape = q_spec, jax.ShapeDtypeStruct((B, T, NSA_Q), f32)
    else:
        qt = (q * (D ** -0.5)).reshape(B, nq, tq, G, R, D).transpose(0, 3, 1, 5, 4, 2)
        qt = qt.reshape(B * G, nq, D, nl).astype(bf16)
        q_spec = pl.BlockSpec((1, 1, D, nl), lambda b, i: (b, i, 0, 0))
        o_spec, o_shape = q_spec, jax.ShapeDtypeStruct((B * G, nq, D, nl), f32)
    gt = gates.reshape(B, nq, tq, G, R, 3).transpose(0, 3, 1, 5, 4, 2).reshape(B * G, nq, 3, nl)
    gt = jnp.pad(gt, ((0, 0), (0, 0), (0, 5), (0, 0)))
    rows = lambda x, t: x.transpose(0, 2, 1, 3).reshape(B * G, -1, t, D).astype(bf16)
    cols = lambda x, t: x.reshape(B, -1, t, G, D).transpose(0, 3, 1, 4, 2).reshape(B * G, -1, D, t).astype(bf16)
    kcp = jnp.pad(kc, ((0, 0), (0, ncp - n_cmp), (0, 0), (0, 0)))
    vcp = jnp.pad(vc, ((0, 0), (0, ncp - n_cmp), (0, 0), (0, 0)))
    kc_r = rows(kcp, ncp)[:, 0]
    vc_c = cols(vcp, ncp)[:, 0]
    full = lambda shape: pl.BlockSpec((1,) + shape, lambda b, i: (b,) + (0,) * len(shape))
    per_q = lambda shape: pl.BlockSpec((1, 1) + shape, lambda b, i: (b, i) + (0,) * len(shape))
    out = pl.pallas_call(
        functools.partial(_nsa_attn_body, tq=tq, n_cmp=n_cmp, n_sel=n_sel,
                          q_tile0=q_pos0 // tq, win_base=win_base, natural=natural),
        grid=(B * G, nq),
        in_specs=[q_spec, full((ncp, D)), full((D, ncp)),
                  pl.BlockSpec((nsp, ncp), lambda b, i: (0, 0)),
                  full((lk // tk, tk, D)), full((lk // tk, D, tk)),
                  full((lw, D)), full((lw, D)),
                  per_q((8, nl))],
        out_specs=o_spec,
        out_shape=o_shape,
        scratch_shapes=[pltpu.VMEM((nsp, nl), jnp.int32)],
        compiler_params=pltpu.CompilerParams(
            dimension_semantics=("parallel", "parallel"), vmem_limit_bytes=VMEM_LIMIT),
        name="nsa_attention",
    )(qt, kc_r, vc_c, map_t, rows(ksl, tk), cols(vsl, tk), rows(kwin, lw)[:, 0], rows(vwin, lw)[:, 0], gt)
    if natural:
        return out
    return out.reshape(B, G, nq, D, R, tq).transpose(0, 2, 5, 1, 4, 3).reshape(B, T, NSA_Q)


def nsa_prompt(q, gates, rows, win, cw1, cw2, cpe):
    B, T = q.shape[:2]
    kc = compress_pair(rows[:, :, 0].reshape(B, T, -1), cw1[0], cw2[0], cpe[0])
    vc = compress_pair(rows[:, :, 1].reshape(B, T, -1), cw1[1], cw2[1], cpe[1])
    return nsa_attention(q, gates, kc, vc, rows[:, :, 2], rows[:, :, 3], win[:, :, 0], win[:, :, 1],
                         n_keys=T)


def compress_pair(x, w1, w2, pe):
    B, L, gd = x.shape
    n_chunks = L // CMP_STRIDE
    c = x[:, :n_chunks * CMP_STRIDE].reshape(B, n_chunks, CMP_STRIDE, gd)
    eye = jnp.eye(NSA_KV, dtype=w1.dtype)
    big = lambda w: jnp.einsum('ldh,gk->lgdkh', w, eye).reshape(CMP_STRIDE, gd, NSA_KV * CMP_HIDDEN)
    a = jnp.einsum('bnlx,lxy->bny', c, big(w1[:CMP_STRIDE]))
    bc = jnp.einsum('bnlx,lxy->bny', c, big(w1[CMP_STRIDE:]))
    h = a[:, :-1] + bc[:, 1:] + jnp.tile(jnp.einsum('ld,ldh->h', pe, w1), NSA_KV)
    w2_big = jnp.einsum('hd,gk->ghkd', w2, eye).reshape(NSA_KV * CMP_HIDDEN, gd)
    return (jax.nn.gelu(h) @ w2_big).reshape(B, n_chunks - 1, NSA_KV, HEAD_DIM)


def nsa_sample(q, gates, past, rows, win_all, cw1, cw2, cpe):
    DB, S = q.shape[:2]
    P, WB = past.shape[1], win_all.shape[1] - S
    gd = NSA_KV * HEAD_DIM
    kind = lambda w: past[:, :, w * gd:(w + 1) * gd]
    new = lambda w: rows[:, :, w].reshape(DB, S, gd)
    up = lambda n, m: -(-n // m) * m
    pad_rows = lambda x, n: jnp.pad(x, ((0, 0), (0, n - x.shape[1])) + ((0, 0),) * (x.ndim - 2))
    n_chunk_rows = (P + S) // CMP_STRIDE * CMP_STRIDE
    cmp_in = [kind(w)[:, :n_chunk_rows] if n_chunk_rows <= P
              else jnp.concatenate([kind(w), new(w)[:, :n_chunk_rows - P]], axis=1) for w in (0, 1)]
    kc = compress_pair(cmp_in[0], cw1[0], cw2[0], cpe[0])
    vc = compress_pair(cmp_in[1], cw1[1], cw2[1], cpe[1])
    tq = up(S, NSA_TQ_DECODE)
    lk = up(P + tq, NSA_TK)
    groups = lambda x: x.reshape(DB, -1, NSA_KV, HEAD_DIM)
    ksl = groups(pad_rows(jnp.concatenate([kind(2), new(2)], axis=1), lk))
    vsl = groups(pad_rows(jnp.concatenate([kind(3), new(3)], axis=1), lk))
    o = nsa_attention(pad_rows(q, tq), pad_rows(gates, tq), kc, vc, ksl, vsl, win_all[:, :, 0], win_all[:, :, 1],
                      n_keys=P + S, tq=tq, q_pos0=P, win_pos0=P - WB)
    return o[:, :S]


RW_PAIRS = RWKV_HEADS // 2
RW_CHUNK = 64


def _rwkv_scan_body(r_ref, w_ref, k_ref, kk_ref, b_ref, c1_ref, c2_ref, vt_ref, s0_ref, oh_ref,
                    y_ref, st_ref, vhi_scr, *, nb, tc):
    f32, bf16 = jnp.float32, jnp.bfloat16
    c = pl.program_id(1)

    @pl.when(c == 0)
    def _():
        st_ref[...] = s0_ref[...]

    row = lax.broadcasted_iota(jnp.int32, (LANE, LANE), 0)
    col = lax.broadcasted_iota(jnp.int32, (LANE, LANE), 1)
    ones1 = ((row // HEAD_DIM) == (col // HEAD_DIM)).astype(bf16)
    lane_t = lax.broadcasted_iota(jnp.int32, (HEAD_DIM, LANE), 1) % HEAD_DIM

    vt = vt_ref[...].reshape(nb * RW_PAIRS, HEAD_DIM, LANE)
    vhi_scr[...] = vt.astype(bf16)
    y_ref[...] = jnp.zeros_like(y_ref)

    pairs = [(b, p) for b in range(nb) for p in range(RW_PAIRS)]
    n_pairs = len(pairs)

    def group(t8, carry):
        t0 = pl.multiple_of(t8 * 8, 8)
        for j in range(0, 8, 2):
            ra = lambda ref, b, sl: ref[b, pl.ds(t0, 8), sl][j:j + 1]
            rb = lambda ref, b, sl: ref[b, pl.ds(t0, 8), sl][j + 1:j + 2]
            oh_a, oh_b = oh_ref[t0 + j], oh_ref[t0 + j + 1]
            vl = jnp.concatenate([vhi_scr[i] * oh for oh in (oh_a, oh_b) for i in range(n_pairs)], axis=0)
            vb_all = jnp.dot(vl, ones1, preferred_element_type=f32)
            pieces = []
            for b, p in pairs:
                sl = slice(LANE * p, LANE * (p + 1))
                s = st_ref[b, p]
                pieces.append((s * ra(kk_ref, b, sl)).astype(bf16))
                pieces.append((s * (ra(w_ref, b, sl) * rb(kk_ref, b, sl))).astype(bf16))
            u_all = jnp.dot(jnp.concatenate(pieces, axis=0), ones1, preferred_element_type=f32)
            py = []
            for i, (b, p) in enumerate(pairs):
                sl = slice(LANE * p, LANE * (p + 1))
                u1 = u_all[2 * HEAD_DIM * i:2 * HEAD_DIM * i + HEAD_DIM]
                u2 = u_all[2 * HEAD_DIM * i + HEAD_DIM:2 * HEAD_DIM * (i + 1)]
                vb_a = vb_all[HEAD_DIM * i:HEAD_DIM * (i + 1)]
                vb_b = vb_all[HEAD_DIM * (n_pairs + i):HEAD_DIM * (n_pairs + i + 1)]
                sa_b = u2 - u1 * ra(c1_ref, b, sl) + vb_a * ra(c2_ref, b, sl)
                s_a = st_ref[b, p] * ra(w_ref, b, sl) - u1 * ra(b_ref, b, sl) + vb_a * ra(k_ref, b, sl)
                s_b = s_a * rb(w_ref, b, sl) - sa_b * rb(b_ref, b, sl) + vb_b * rb(k_ref, b, sl)
                st_ref[b, p] = s_b
                py.append((s_a * ra(r_ref, b, sl)).astype(bf16))
                py.append((s_b * rb(r_ref, b, sl)).astype(bf16))
            y_all = jnp.dot(jnp.concatenate(py, axis=0), ones1, preferred_element_type=f32)
            m_a, m_b = lane_t == t0 + j, lane_t == t0 + j + 1
            for i, (b, p) in enumerate(pairs):
                y_a = y_all[2 * HEAD_DIM * i:2 * HEAD_DIM * i + HEAD_DIM]
                y_b = y_all[2 * HEAD_DIM * i + HEAD_DIM:2 * HEAD_DIM * (i + 1)]
                y_ref[b, 0, p] = jnp.where(m_a, y_a, jnp.where(m_b, y_b, y_ref[b, 0, p]))
        return carry

    lax.fori_loop(0, tc // 8, group, 0)


def rwkv_scan(r, w, k, v, kk, b, S0):
    f32 = jnp.float32
    B, T, W = r.shape
    tc = RW_CHUNK if T % RW_CHUNK == 0 else T
    assert tc <= RW_CHUNK and T % tc == 0 and tc % 8 == 0 and B % 2 == 0
    nc, nb = T // tc, 2
    vt = v.reshape(B, nc, tc, RW_PAIRS, 2, HEAD_DIM).transpose(0, 1, 3, 5, 4, 2)
    vt = jnp.pad(vt, ((0, 0),) * 5 + ((0, RW_CHUNK - tc),)).reshape(B, nc, RW_PAIRS, HEAD_DIM, LANE)
    s0 = S0.astype(f32).reshape(B, RW_PAIRS, 2, HEAD_DIM, HEAD_DIM).transpose(0, 1, 3, 2, 4)
    s0 = s0.reshape(B, RW_PAIRS, HEAD_DIM, LANE)
    oh = jnp.arange(LANE)[None, None, :] % HEAD_DIM == jnp.arange(RW_CHUNK)[:, None, None]
    oh = jnp.broadcast_to(oh, (RW_CHUNK, HEAD_DIM, LANE)).astype(jnp.bfloat16)
    kk_next = jnp.concatenate([kk[:, 1:], kk[:, :1]], axis=1)

    def head_dot(x):
        d = jnp.sum((x * kk_next).reshape(B, T, RWKV_HEADS, HEAD_DIM), axis=-1, keepdims=True)
        return jnp.broadcast_to(d, (B, T, RWKV_HEADS, HEAD_DIM)).reshape(B, T, W)

    c1, c2 = head_dot(b), head_dot(k)
    tok = pl.BlockSpec((nb, tc, W), lambda i, c: (i, c, 0))
    chk = pl.BlockSpec((nb, 1, RW_PAIRS, HEAD_DIM, LANE), lambda i, c: (i, c, 0, 0, 0))
    stt = pl.BlockSpec((nb, RW_PAIRS, HEAD_DIM, LANE), lambda i, c: (i, 0, 0, 0))
    y, st = pl.pallas_call(
        functools.partial(_rwkv_scan_body, nb=nb, tc=tc),
        grid=(B // nb, nc),
        in_specs=[tok, tok, tok, tok, tok, tok, tok, chk, stt,
                  pl.BlockSpec((RW_CHUNK, HEAD_DIM, LANE), lambda i, c: (0, 0, 0))],
        out_specs=[chk, stt],
        out_shape=[jax.ShapeDtypeStruct((B, nc, RW_PAIRS, HEAD_DIM, LANE), f32),
                   jax.ShapeDtypeStruct((B, RW_PAIRS, HEAD_DIM, LANE), f32)],
        scratch_shapes=[pltpu.VMEM((nb * RW_PAIRS, HEAD_DIM, LANE), jnp.bfloat16)],
        compiler_params=pltpu.CompilerParams(
            dimension_semantics=("parallel", "arbitrary"), vmem_limit_bytes=VMEM_LIMIT),
        name="rwkv_scan",
    )(r, w, k, kk, b, c1, c2, vt, s0, oh)
    y = y.reshape(B, nc, RW_PAIRS, HEAD_DIM, 2, RW_CHUNK)[..., :tc]
    y = y.transpose(0, 1, 5, 2, 4, 3).reshape(B, T, W)
    st = st.reshape(B, RW_PAIRS, HEAD_DIM, 2, HEAD_DIM).transpose(0, 1, 3, 2, 4)
    return y, st.reshape(B, RWKV_HEADS, HEAD_DIM, HEAD_DIM)


def rwkv_mix(p, prev, S0, mu, w0, wB, a0, aB, gB, k_k, k_a, r_k, ln_w, ln_b):
    f32 = jnp.float32
    B, T = p.shape[:2]
    W = RWKV_WIDTH
    p = p.astype(f32)
    p_prev = jnp.concatenate([prev.astype(f32)[:, None], p[:, :-1]], axis=1)
    ps = p + mu * (p_prev - p)
    r, k, v = ps[..., :W], ps[..., W:2 * W], ps[..., 2 * W:3 * W]
    o = 3 * W
    xw, xa, xg = ps[..., o:o + LORA_W], ps[..., o + LORA_W:o + LORA_W + LORA_A], ps[..., o + LORA_W + LORA_A:]
    z = w0 + jnp.tanh(xw) @ wB
    w = jnp.exp(-jnp.exp(-jax.nn.softplus(-z) - 0.5))
    a = jax.nn.sigmoid(a0 + xa @ aB)
    g = jax.nn.sigmoid(xg) @ gB
    heads = lambda t: t.reshape(B, T, RWKV_HEADS, HEAD_DIM)
    kk = heads(k * k_k)
    kk = kk * lax.rsqrt(jnp.maximum(jnp.sum(kk * kk, axis=-1, keepdims=True), 1e-24))
    k = k * (1.0 + (a - 1.0) * k_a)
    kk = kk.reshape(B, T, W)
    y, S = rwkv_scan(r, w, k, v, kk, kk * a, S0)
    y, k = heads(y), heads(k)
    r, v = heads(r), heads(v)
    mean = jnp.mean(y, axis=-1, keepdims=True)
    var = jnp.mean(jnp.square(y - mean), axis=-1, keepdims=True)
    y = ((y - mean) * lax.rsqrt(var + RWKV_GN_EPS)).reshape(B, T, W) * ln_w + ln_b
    y = y + (jnp.sum(r * k * r_k, axis=-1, keepdims=True) * v).reshape(B, T, W)
    return y * g, S, p[:, -1]


def gla_chunked(q, k, v, logg, S0):
    B, T, H, DK = q.shape
    DV = v.shape[-1]
    C = GLA_CHUNK if T % GLA_CHUNK == 0 else T
    N = T // C
    to_chunks = lambda x: x.reshape(B, N, C, H, x.shape[-1]).transpose(1, 0, 3, 2, 4)
    causal = jnp.tril(jnp.ones((C, C), dtype=bool))[:, :, None]

    def step(S, inp):
        qc, kc, vc, gc = inp
        b = jnp.cumsum(gc, axis=2)
        diff = b[:, :, :, None, :] - b[:, :, None, :, :]
        decay = jnp.exp(jnp.where(causal, diff, -jnp.inf))
        A = jnp.einsum('bhtd,bhsd,bhtsd->bhts', qc, kc, decay)
        o = jnp.einsum('bhts,bhsv->bhtv', A, vc) + jnp.einsum('bhtk,bhkv->bhtv', qc * jnp.exp(b), S)
        b_last = b[:, :, -1:, :]
        S = jnp.exp(b_last[:, :, 0, :])[..., None] * S + jnp.einsum('bhsk,bhsv->bhkv', kc * jnp.exp(b_last - b), vc)
        return S, o

    S, o = lax.scan(step, S0, (to_chunks(q), to_chunks(k), to_chunks(v), to_chunks(logg)))
    return o.transpose(1, 0, 3, 2, 4).reshape(B, T, H, DV), S


GLA_SUB = 16


def _gla_body(q_ref, k_ref, v_ref, g_ref, r_ref, gn_ref, o_ref, st_ref):
    f32, bf16 = jnp.float32, jnp.bfloat16
    C, SUB = GLA_CHUNK, GLA_SUB
    nsub = C // SUB
    c = pl.program_id(1)

    @pl.when(c == 0)
    def _():
        st_ref[...] = jnp.zeros_like(st_ref)

    ti = lax.broadcasted_iota(jnp.int32, (C, C), 0)
    si = lax.broadcasted_iota(jnp.int32, (C, C), 1)
    tri = (si <= ti).astype(bf16)
    sub_t = lax.broadcasted_iota(jnp.int32, (SUB, 1), 0)
    sub_l = lax.broadcasted_iota(jnp.int32, (SUB, SUB), 1)
    for h in range(GLA_HEADS):
        kq = slice(h * GLA_DK, (h + 1) * GLA_DK)
        vv = slice(h * GLA_DV, (h + 1) * GLA_DV)
        q = q_ref[0, :, kq] * (GLA_DK ** -0.5)
        k = k_ref[0, :, kq]
        v = v_ref[0, :, vv].astype(bf16)
        g = g_ref[0, :, kq]
        g1 = g.astype(bf16)
        g2 = (g - g1.astype(f32)).astype(bf16)
        g3 = (g - g1.astype(f32) - g2.astype(f32)).astype(bf16)
        b = (jnp.dot(tri, g1, preferred_element_type=f32) + jnp.dot(tri, g2, preferred_element_type=f32)
             + jnp.dot(tri, g3, preferred_element_type=f32))
        b_last = b[C - 1:C]
        a_rows = []
        for I in range(nsub):
            rows = slice(I * SUB, (I + 1) * SUB)
            beta = b[I * SUB - 1:I * SUB] if I > 0 else jnp.zeros((1, GLA_DK), f32)
            qb, kb, bb = q[rows], k[rows], b[rows]
            a_diag = jnp.zeros((SUB, SUB), f32)
            for s in range(SUB):
                d = jnp.where(sub_t >= s, bb - bb[s:s + 1], -jnp.inf)
                col = jnp.sum(qb * kb[s:s + 1] * jnp.exp(d), axis=1, keepdims=True)
                a_diag = jnp.where(sub_l == s, col, a_diag)
            blocks = []
            if I > 0:
                qe = (qb * jnp.exp(bb - beta)).astype(bf16)
                ke = (k[:I * SUB] * jnp.exp(beta - b[:I * SUB])).astype(bf16)
                blocks.append(lax.dot_general(qe, ke, (((1,), (1,)), ((), ())), preferred_element_type=f32))
            blocks.append(a_diag)
            if I < nsub - 1:
                blocks.append(jnp.zeros((SUB, C - (I + 1) * SUB), f32))
            a_rows.append(jnp.concatenate(blocks, axis=1))
        a = jnp.concatenate(a_rows, axis=0).astype(bf16)
        st = st_ref[0, h]
        qd = (q * jnp.exp(b)).astype(bf16)
        o = (jnp.dot(a, v, preferred_element_type=f32)
             + lax.dot_general(qd, st.astype(bf16), (((1,), (1,)), ((), ())), preferred_element_type=f32))
        kd = (k * jnp.exp(b_last - b)).astype(bf16)
        st_ref[0, h] = (st * jnp.exp(b_last)
                        + lax.dot_general(v, kd, (((0,), (0,)), ((), ())), preferred_element_type=f32))
        y = o * lax.rsqrt(jnp.mean(o * o, axis=1, keepdims=True) + 1e-6) * gn_ref[...]
        rr = r_ref[0, :, vv]
        o_ref[0, :, vv] = y * (rr * jax.nn.sigmoid(rr))


def gla_mix_prompt(qk, v, gd, r, gate_up, gate_b, gn):
    f32 = jnp.float32
    B, T = qk.shape[:2]
    KW, VW, C = GLA_KW, GLA_VW, GLA_CHUNK
    logg = jax.nn.log_sigmoid(gd[..., :GLA_RANK] @ gate_up + gate_b) / GLA_TAU
    tokv = pl.BlockSpec((1, C, VW), lambda b, c: (b, c, 0))
    out, st = pl.pallas_call(
        _gla_body,
        grid=(B, T // C),
        in_specs=[pl.BlockSpec((1, C, KW), lambda b, c: (b, c, 0)),
                  pl.BlockSpec((1, C, KW), lambda b, c: (b, c, 1)),
                  tokv, pl.BlockSpec((1, C, KW), lambda b, c: (b, c, 0)), tokv,
                  pl.BlockSpec((1, GLA_DV), lambda b, c: (0, 0))],
        out_specs=[tokv, pl.BlockSpec((1, GLA_HEADS, GLA_DV, GLA_DK), lambda b, c: (b, 0, 0, 0))],
        out_shape=[jax.ShapeDtypeStruct((B, T, VW), f32),
                   jax.ShapeDtypeStruct((B, GLA_HEADS, GLA_DV, GLA_DK), f32)],
        compiler_params=pltpu.CompilerParams(
            dimension_semantics=("parallel", "arbitrary"), vmem_limit_bytes=VMEM_LIMIT),
        name="gla_chunk",
    )(qk, qk, v, logg, r, gn.reshape(1, GLA_DV))
    return out, st.transpose(0, 1, 3, 2)


def gla_mix(qk, v, gd, r, S0, gate_up, gate_b, gn):
    f32 = jnp.float32
    B, T = qk.shape[:2]
    q = qk[..., :GLA_KW].reshape(B, T, GLA_HEADS, GLA_DK) * (GLA_DK ** -0.5)
    k = qk[..., GLA_KW:].reshape(B, T, GLA_HEADS, GLA_DK)
    v = v.reshape(B, T, GLA_HEADS, GLA_DV)
    gd = gd[..., :GLA_RANK]
    logg = (jax.nn.log_sigmoid(gd @ gate_up + gate_b) / GLA_TAU).reshape(B, T, GLA_HEADS, GLA_DK)
    out, S = gla_chunked(q, k, v, logg, S0.astype(f32))
    out = rmsnorm(out, gn).reshape(B, T, GLA_VW) * jax.nn.silu(r)
    return out, S


def kernel(x_prompt, x_sample, cache_nsa_kv, cache_nsa_win, state_rwkv, state_rwkv_shift, state_gla, page_table, norm_mix, norm_ffn, w_in_even, nsa_qnorm, nsa_knorm, cmp_w1, cmp_w2, cmp_pe, rwkv_mu, rwkv_w0, rwkv_wB, rwkv_a0, rwkv_aB, rwkv_gB, rwkv_kk, rwkv_ka, rwkv_rk, rwkv_ln_w, rwkv_ln_b, w_out_even, w_in_odd, gla_gate_up, gla_gate_b, gla_norm, w_out_odd, ffn_gate, ffn_up, ffn_down):
    dt = x_prompt.dtype
    bf16 = jnp.bfloat16
    B, T = x_prompt.shape[:2]
    DB, S = x_sample.shape[:2]
    depth = norm_mix.shape[0]
    P = page_table.shape[1] * PAGE_SIZE
    WB = cache_nsa_win.shape[2]
    WP = min(WINDOW, T)
    pos_p = jnp.arange(T, dtype=jnp.int32)
    pos_s = P + jnp.arange(S, dtype=jnp.int32)
    win_pos_s = P - WB + jnp.arange(WB + S, dtype=jnp.int32)
    nsa_pad = -(-NSA_IN // LANE) * LANE
    y_p, y_s = x_prompt, x_sample
    kv_p, kv_s, win_p, win_s, rw_p, rw_s, sh_p, sh_s, gl_p, gl_s = ([] for _ in range(10))
    for layer in range(depth):
        li = layer // 2
        if layer % 2 == 0:
            w_in = jnp.concatenate([jnp.pad(w_in_even[li][:, :NSA_IN], ((0, 0), (0, nsa_pad - NSA_IN))),
                                    w_in_even[li][:, NSA_IN:]], axis=1).astype(bf16)
            even_splits = ((0, nsa_pad), (nsa_pad, nsa_pad + RWKV_IN))
            w_out = w_out_even[li].astype(bf16)
            hn_p, hr_p = norm_proj(y_p, norm_mix[layer], w_in, even_splits)
            hn_s, hr_s = norm_proj(y_s, norm_mix[layer], w_in, even_splits)
            q, g, rows, win = nsa_project(hn_p, pos_p, nsa_qnorm[li], nsa_knorm[li])
            o_nsa_p = nsa_prompt(q, g, rows, win, cmp_w1[li], cmp_w2[li], cmp_pe[li])
            kv_p.append(rows.astype(dt))
            win_p.append(win[:, T - WP:].astype(dt))
            q, g, rows, win = nsa_project(hn_s, pos_s, nsa_qnorm[li], nsa_knorm[li])
            pool = cache_nsa_kv.reshape(-1, PAGE_SIZE, 4 * NSA_KV * HEAD_DIM)
            past = pool[li * cache_nsa_kv.shape[1] + page_table].reshape(DB, P, 4 * NSA_KV * HEAD_DIM)
            win_all = jnp.concatenate([cache_nsa_win[li], win.astype(cache_nsa_win.dtype)], axis=1)
            o_nsa_s = nsa_sample(q, g, past, rows.astype(past.dtype), win_all, cmp_w1[li], cmp_w2[li], cmp_pe[li])
            kv_s.append(rows.astype(dt))
            win_s.append(win_all[:, S:].astype(dt))
            rw_par = (rwkv_mu[li], rwkv_w0[li], rwkv_wB[li], rwkv_a0[li], rwkv_aB[li], rwkv_gB[li],
                      rwkv_kk[li], rwkv_ka[li], rwkv_rk[li], rwkv_ln_w[li], rwkv_ln_b[li])
            o_rw_p, st, last = rwkv_mix(hr_p, jnp.zeros((B, RWKV_IN), dt),
                                        jnp.zeros((B, RWKV_HEADS, HEAD_DIM, HEAD_DIM), dt), *rw_par)
            rw_p.append(st.astype(dt))
            sh_p.append(last.astype(dt))
            o_rw_s, st, last = rwkv_mix(hr_s, state_rwkv_shift[li], state_rwkv[li], *rw_par)
            rw_s.append(st.astype(dt))
            sh_s.append(last.astype(dt))
            w_outs = [w_out[:NSA_Q], w_out[NSA_Q:]]
            y_p = proj_residual([o_nsa_p.astype(dt), o_rw_p.astype(dt)], w_outs, y_p)
            y_s = proj_residual([o_nsa_s.astype(dt), o_rw_s.astype(dt)], w_outs, y_s)
        else:
            wo = w_in_odd[li]
            o2 = 2 * GLA_KW + GLA_VW
            w_in = jnp.concatenate([wo[:, :o2], wo[:, o2 + GLA_RANK:],
                                    jnp.pad(wo[:, o2:o2 + GLA_RANK], ((0, 0), (0, LANE - GLA_RANK)))],
                                   axis=1).astype(bf16)
            odd_splits = ((0, 2 * GLA_KW), (2 * GLA_KW, o2), (o2, o2 + GLA_VW), (o2 + GLA_VW, o2 + GLA_VW + LANE))
            w_out = w_out_odd[li].astype(bf16)
            qk_p, v_p, r_p, gd_p = norm_proj(y_p, norm_mix[layer], w_in, odd_splits)
            qk_s, v_s, r_s, gd_s = norm_proj(y_s, norm_mix[layer], w_in, odd_splits)
            o_p, st = gla_mix_prompt(qk_p, v_p, gd_p, r_p, gla_gate_up[li], gla_gate_b[li], gla_norm[li])
            gl_p.append(st.astype(dt))
            o_s, st = gla_mix(qk_s, v_s, gd_s, r_s, state_gla[li], gla_gate_up[li], gla_gate_b[li], gla_norm[li])
            gl_s.append(st.astype(dt))
            y_p = proj_residual([o_p.astype(dt)], [w_out], y_p)
            y_s = proj_residual([o_s.astype(dt)], [w_out], y_s)
        wg, wu, wd = ffn_gate[layer].astype(bf16), ffn_up[layer].astype(bf16), ffn_down[layer].astype(bf16)
        y_p = ffn(y_p, norm_ffn[layer], wg, wu, wd)
        y_s = ffn(y_s, norm_ffn[layer], wg, wu, wd)
    return (y_p, y_s, jnp.stack(kv_p), jnp.stack(kv_s), jnp.stack(win_p), jnp.stack(win_s),
            jnp.stack(rw_p), jnp.stack(rw_s), jnp.stack(sh_p), jnp.stack(sh_s),
            jnp.stack(gl_p), jnp.stack(gl_s))
```

```python
import functools

import jax
import jax.numpy as jnp
from jax import lax
from jax.experimental import pallas as pl
from jax.experimental.pallas import tpu as pltpu

D_MODEL = 1024
PAGE_SIZE = 128
HEAD_DIM = 64
NSA_HEADS = 8
NSA_KV = 2
NSA_REP = NSA_HEADS // NSA_KV
CMP_STRIDE = 16
CMP_BLOCK = 2 * CMP_STRIDE
CMP_HIDDEN = 2 * HEAD_DIM
SEL_BLOCK = 64
N_SELECT = 16
WINDOW = 512
ROPE_DIM = HEAD_DIM // 4
ROPE_THETA = 500000.0
RWKV_HEADS = 8
RWKV_WIDTH = RWKV_HEADS * HEAD_DIM
LORA_W = 64
LORA_A = 64
LORA_G = 128
RWKV_GN_EPS = 64e-5
GLA_HEADS = 4
GLA_DK = D_MODEL // 2 // GLA_HEADS
GLA_DV = D_MODEL // GLA_HEADS
GLA_RANK = 16
GLA_TAU = 16.0
GLA_CHUNK = 64
D_FF = ((8 * D_MODEL // 3 + 255) // 256) * 256
NSA_Q = NSA_HEADS * HEAD_DIM
NSA_KVW = 3 * 2 * NSA_KV * HEAD_DIM
NSA_IN = NSA_Q + NSA_KVW + 3 * NSA_HEADS
RWKV_IN = 3 * RWKV_WIDTH + LORA_W + LORA_A + LORA_G
EVEN_IN = NSA_IN + RWKV_IN
MIX_WIDTH = NSA_Q + RWKV_WIDTH
GLA_KW = GLA_HEADS * GLA_DK
GLA_VW = GLA_HEADS * GLA_DV
ODD_IN = 2 * GLA_KW + GLA_VW + GLA_RANK + GLA_VW

LANE = 128
VMEM_LIMIT = 48 * 1024 * 1024


ROW_TILE = 512
NORM_EPS = 1e-6


def _row_tile(m):
    return ROW_TILE if m % ROW_TILE == 0 else m


def _norm_bf16(x_ref, g_ref):
    x = x_ref[...]
    y = x * lax.rsqrt(jnp.mean(x * x, axis=-1, keepdims=True) + NORM_EPS)
    return (y * g_ref[...]).astype(jnp.bfloat16)


def _norm_proj_body(x_ref, g_ref, w_ref, *o_refs, splits):
    xn = _norm_bf16(x_ref, g_ref)
    for (a, b), o_ref in zip(splits, o_refs):
        o_ref[...] = jnp.dot(xn, w_ref[:, a:b], preferred_element_type=jnp.float32)


def norm_proj(x, gain, w, splits):
    bsz, t, k = x.shape
    m = bsz * t
    tm = _row_tile(m)
    outs = pl.pallas_call(
        functools.partial(_norm_proj_body, splits=splits),
        grid=(m // tm,),
        in_specs=[pl.BlockSpec((tm, k), lambda i: (i, 0)),
                  pl.BlockSpec((1, k), lambda i: (0, 0)),
                  pl.BlockSpec(w.shape, lambda i: (0, 0))],
        out_specs=[pl.BlockSpec((tm, b - a), lambda i: (i, 0)) for a, b in splits],
        out_shape=[jax.ShapeDtypeStruct((m, b - a), jnp.float32) for a, b in splits],
        compiler_params=pltpu.CompilerParams(dimension_semantics=("parallel",), vmem_limit_bytes=VMEM_LIMIT),
        name="norm_proj",
    )(x.reshape(m, k), gain.reshape(1, k), w)
    return [o.reshape(bsz, t, -1) for o in outs]


def _proj_res_body(*refs, n_in):
    x_refs, w_refs, res_ref, o_ref = refs[:n_in], refs[n_in:2 * n_in], refs[2 * n_in], refs[2 * n_in + 1]
    acc = res_ref[...]
    for x_ref, w_ref in zip(x_refs, w_refs):
        acc = acc + jnp.dot(x_ref[...].astype(jnp.bfloat16), w_ref[...], preferred_element_type=jnp.float32)
    o_ref[...] = acc


def proj_residual(xs, ws, res):
    bsz, t, n = res.shape
    m = bsz * t
    tm = _row_tile(m)
    out = pl.pallas_call(
        functools.partial(_proj_res_body, n_in=len(xs)),
        grid=(m // tm,),
        in_specs=([pl.BlockSpec((tm, x.shape[-1]), lambda i: (i, 0)) for x in xs]
                  + [pl.BlockSpec(w.shape, lambda i: (0, 0)) for w in ws]
                  + [pl.BlockSpec((tm, n), lambda i: (i, 0))]),
        out_specs=pl.BlockSpec((tm, n), lambda i: (i, 0)),
        out_shape=jax.ShapeDtypeStruct((m, n), jnp.float32),
        compiler_params=pltpu.CompilerParams(dimension_semantics=("parallel",), vmem_limit_bytes=VMEM_LIMIT),
        name="proj_residual",
    )(*[x.reshape(m, x.shape[-1]) for x in xs], *ws, res.reshape(m, n))
    return out.reshape(bsz, t, n)


FFN_COL_CHUNKS = 2


def _ffn_up_body(x_ref, g_ref, wg_ref, wu_ref, h_ref):
    xn = _norm_bf16(x_ref, g_ref)
    cw = h_ref.shape[1] // FFN_COL_CHUNKS
    for c in range(FFN_COL_CHUNKS):
        cols = slice(c * cw, (c + 1) * cw)
        g = jnp.dot(xn, wg_ref[:, cols], preferred_element_type=jnp.float32)
        u = jnp.dot(xn, wu_ref[:, cols], preferred_element_type=jnp.float32)
        h_ref[:, cols] = (g * jax.nn.sigmoid(g) * u).astype(h_ref.dtype)


def ffn(y, gain, wg, wu, wd):
    bsz, t, k = y.shape
    m = bsz * t
    tm = _row_tile(m)
    f = wg.shape[1]
    h = pl.pallas_call(
        _ffn_up_body,
        grid=(m // tm,),
        in_specs=[pl.BlockSpec((tm, k), lambda i: (i, 0)),
                  pl.BlockSpec((1, k), lambda i: (0, 0)),
                  pl.BlockSpec(wg.shape, lambda i: (0, 0)),
                  pl.BlockSpec(wu.shape, lambda i: (0, 0))],
        out_specs=pl.BlockSpec((tm, f), lambda i: (i, 0)),
        out_shape=jax.ShapeDtypeStruct((m, f), jnp.bfloat16),
        compiler_params=pltpu.CompilerParams(dimension_semantics=("parallel",), vmem_limit_bytes=VMEM_LIMIT),
        name="ffn_up",
    )(y.reshape(m, k), gain.reshape(1, k), wg, wu)
    return proj_residual([h.reshape(bsz, t, f)], [wd], y)


def rmsnorm(x, g, eps=1e-6):
    xf = x.astype(jnp.float32)
    y = xf * lax.rsqrt(jnp.mean(xf * xf, axis=-1, keepdims=True) + eps)
    return (y * g.astype(jnp.float32)).astype(x.dtype)


def rope(x, pos):
    half = ROPE_DIM // 2
    inv = ROPE_THETA ** (-jnp.arange(half, dtype=jnp.float32) / half)
    ang = pos.astype(jnp.float32)[:, None] * inv[None, :]
    shp = (pos.shape[0],) + (1,) * (x.ndim - 3) + (half,)
    cos, sin = jnp.cos(ang).reshape(shp), jnp.sin(ang).reshape(shp)
    xr = x[..., :ROPE_DIM].astype(jnp.float32)
    x1, x2 = xr[..., :half], xr[..., half:]
    rot = jnp.concatenate([x1 * cos - x2 * sin, x2 * cos + x1 * sin], axis=-1)
    return jnp.concatenate([rot.astype(x.dtype), x[..., ROPE_DIM:]], axis=-1)


def nsa_project(h, pos, qn, kn):
    B, T = h.shape[:2]
    q = h[..., :NSA_Q].reshape(B, T, NSA_HEADS, HEAD_DIM)
    kv = h[..., NSA_Q:NSA_Q + NSA_KVW].reshape(B, T, 3, 2, NSA_KV, HEAD_DIM)
    gates = jax.nn.sigmoid(h[..., NSA_Q + NSA_KVW:NSA_IN].astype(jnp.float32)).reshape(B, T, NSA_HEADS, 3)
    q = rope(rmsnorm(q, qn), pos)
    k = rope(rmsnorm(kv[:, :, :, 0], kn[:, None, :]), pos)
    kv = jnp.stack([k, kv[:, :, :, 1]], axis=3)
    rows = kv[:, :, :2].reshape(B, T, 4, NSA_KV, HEAD_DIM)
    win = kv[:, :, 2]
    return q, gates, rows, win


NEG = -1e30


NSA_TQ = 256
NSA_TK = 512
NSA_TQ_DECODE = 32


def _nsa_attn_body(qt_ref, kc_ref, vct_ref, map_ref, ks_ref, vst_ref, kw_ref, vw_ref, g_ref, o_ref, thr_scr,
                   *, tq, n_cmp, n_sel, q_tile0, win_base, natural):
    f32, bf16 = jnp.float32, jnp.bfloat16
    tk = NSA_TK
    nl = NSA_REP * tq
    i = pl.program_id(1) + q_tile0
    if natural:
        xq = qt_ref[0] * (HEAD_DIM ** -0.5)
        qt = jnp.concatenate([xq[:, HEAD_DIM * r:HEAD_DIM * (r + 1)].T for r in range(NSA_REP)], axis=1).astype(bf16)
    else:
        qt = qt_ref[0, 0]
    qpos = i * tq + lax.broadcasted_iota(jnp.int32, (1, nl), 1) % tq

    ncp = kc_ref.shape[1]
    n = lax.broadcasted_iota(jnp.int32, (ncp, 1), 0)
    ok = (n * CMP_STRIDE + (CMP_BLOCK - 1) <= qpos) & (n < n_cmp)
    s = jnp.dot(kc_ref[0], qt, preferred_element_type=f32)
    s = jnp.where(ok, s, NEG)
    m = jnp.max(s, axis=0, keepdims=True)
    e = jnp.where(ok, jnp.exp(s - m), 0.0)
    l = jnp.sum(e, axis=0, keepdims=True)
    p = e * jnp.where(l > 0.0, 1.0 / l, 0.0)
    o_cmp = jnp.dot(vct_ref[0], p.astype(bf16), preferred_element_type=f32)

    psum = p[:, 0:tq]
    for r in range(1, NSA_REP):
        psum = psum + p[:, r * tq:(r + 1) * tq]
    hi = psum.astype(bf16)
    lo = (psum - hi.astype(f32)).astype(bf16)
    imp = (jnp.dot(map_ref[...], hi, preferred_element_type=f32)
           + jnp.dot(map_ref[...], lo, preferred_element_type=f32))
    nsp = map_ref.shape[0]
    qp = qpos[:, 0:tq]
    blk = lax.broadcasted_iota(jnp.int32, (nsp, 1), 0)
    cur = qp // SEL_BLOCK
    valid = (blk * SEL_BLOCK <= qp) & (blk < n_sel)
    forced = (blk == 0) | (blk == cur) | (blk == cur - 1)
    x = jnp.where(valid, jnp.where(forced, 1e30, imp), NEG)
    x = jnp.where(blk < n_sel, x, -3e38)
    blkf = blk.astype(f32)
    sel = jnp.zeros((nsp, tq), f32)
    for _ in range(min(N_SELECT, n_sel)):
        mx = jnp.max(x, axis=0, keepdims=True)
        first = jnp.min(jnp.where(x == mx, blkf, float(nsp)), axis=0, keepdims=True)
        hit = blkf == first
        sel = jnp.where(hit, 1.0, sel)
        x = jnp.where(hit, -3e38, x)
    thr = jnp.where(valid & (sel > 0.5), qp, -1)
    thr_scr[...] = jnp.concatenate([thr] * NSA_REP, axis=1)

    def flash(s, vt, carry):
        m_old, l_old, acc = carry
        m_new = jnp.maximum(m_old, jnp.max(s, axis=0, keepdims=True))
        alpha = jnp.exp(m_old - m_new)
        p = jnp.exp(s - m_new)
        l_new = alpha * l_old + jnp.sum(p, axis=0, keepdims=True)
        acc = alpha * acc + jnp.dot(vt, p.astype(bf16), preferred_element_type=f32)
        return m_new, l_new, acc

    init = (jnp.full((1, nl), NEG, f32), jnp.zeros((1, nl), f32), jnp.zeros((HEAD_DIM, nl), f32))

    krow = lax.broadcasted_iota(jnp.int32, (SEL_BLOCK, 1), 0)

    def masked_scores(j):
        s = jnp.dot(ks_ref[0, j], qt, preferred_element_type=f32)
        th = thr_scr[pl.ds(pl.multiple_of(j * (tk // SEL_BLOCK), 8), tk // SEL_BLOCK), :]
        return jnp.concatenate(
            [jnp.where((j * tk + b * SEL_BLOCK + krow) <= th[b:b + 1], s[b * SEL_BLOCK:(b + 1) * SEL_BLOCK], NEG)
             for b in range(tk // SEL_BLOCK)], axis=0)

    def sel_pair(j2, carry):
        m_old, l_old, acc = carry
        s0, s1 = masked_scores(2 * j2), masked_scores(2 * j2 + 1)
        m_new = jnp.maximum(m_old, jnp.maximum(jnp.max(s0, axis=0, keepdims=True), jnp.max(s1, axis=0, keepdims=True)))
        alpha = jnp.exp(m_old - m_new)
        p0, p1 = jnp.exp(s0 - m_new), jnp.exp(s1 - m_new)
        l_new = alpha * l_old + jnp.sum(p0, axis=0, keepdims=True) + jnp.sum(p1, axis=0, keepdims=True)
        acc = (alpha * acc + jnp.dot(vst_ref[0, 2 * j2], p0.astype(bf16), preferred_element_type=f32)
               + jnp.dot(vst_ref[0, 2 * j2 + 1], p1.astype(bf16), preferred_element_type=f32))
        return m_new, l_new, acc

    n_tiles = (i * tq + tq - 1) // tk + 1
    carry = lax.fori_loop(0, n_tiles // 2, sel_pair, init)
    carry = lax.cond(n_tiles % 2 == 1,
                     lambda c: flash(masked_scores(n_tiles - 1), vst_ref[0, n_tiles - 1], c),
                     lambda c: c, carry)
    _, l_s, acc_s = carry
    o_slc = acc_s / l_s

    nkw = WINDOW + tq
    k_first = i * tq - WINDOW
    row0 = pl.multiple_of(k_first - win_base, 16)
    kpos = k_first + lax.broadcasted_iota(jnp.int32, (nkw, 1), 0)
    allowed = (kpos <= qpos) & (kpos > qpos - WINDOW) & (kpos >= 0)
    s = jnp.dot(kw_ref[0, pl.ds(row0, nkw), :], qt, preferred_element_type=f32)
    s = jnp.where(allowed, s, NEG)
    e = jnp.where(allowed, jnp.exp(s - jnp.max(s, axis=0, keepdims=True)), 0.0)
    o_win = (lax.dot_general(vw_ref[0, pl.ds(row0, nkw), :], e.astype(bf16), (((0,), (0,)), ((), ())),
                             preferred_element_type=f32) / jnp.sum(e, axis=0, keepdims=True))

    g = g_ref[0, 0]
    res = g[0:1] * o_cmp + g[1:2] * o_slc + g[2:3] * o_win
    if natural:
        o_ref[0] = jnp.concatenate([res[:, tq * r:tq * (r + 1)].T for r in range(NSA_REP)], axis=1)
    else:
        o_ref[0, 0] = res


def nsa_attention(q, gates, kc, vc, ksl, vsl, kwin, vwin, *, n_keys, tq=NSA_TQ, q_pos0=0, win_pos0=0):
    f32, bf16 = jnp.float32, jnp.bfloat16
    B, T = q.shape[:2]
    G, R, D = NSA_KV, NSA_REP, HEAD_DIM
    tk = NSA_TK
    nq, nl = T // tq, R * tq
    n_cmp, n_sel = kc.shape[1], -(-n_keys // SEL_BLOCK)
    lk = ksl.shape[1]
    assert T % tq == 0 and nl % LANE == 0 and lk % tk == 0 and q_pos0 % tq == 0 and tq % 16 == 0
    assert lk >= q_pos0 + T and win_pos0 <= max(q_pos0 - WINDOW, 0)
    win_base = win_pos0 - WINDOW
    lw = q_pos0 + T - win_base
    wpad = lambda x: jnp.pad(x[:, :lw - WINDOW], ((0, 0), (WINDOW, max(lw - WINDOW - x.shape[1], 0)), (0, 0), (0, 0)))
    kwin, vwin = wpad(kwin), wpad(vwin)
    ncp = -(-n_cmp // LANE) * LANE
    nsp = -(-n_sel // LANE) * LANE
    n = jnp.arange(ncp)[None, :]
    jb = jnp.arange(nsp)[:, None]
    map_t = ((n * CMP_STRIDE <= jb * SEL_BLOCK + SEL_BLOCK - 1) & (n * CMP_STRIDE + CMP_BLOCK - 1 >= jb * SEL_BLOCK)
             & (n < n_cmp) & (jb < n_sel)).astype(bf16)
    natural = tq % LANE == 0
    if natural:
        qt = q.reshape(B, T, NSA_Q)
        q_spec = pl.BlockSpec((1, tq, R * D), lambda b, i: (b // G, i, b % G))
        o_spec, o_shape = q_spec, jax.ShapeDtypeStruct((B, T, NSA_Q), f32)
    else:
        qt = (q * (D ** -0.5)).reshape(B, nq, tq, G, R, D).transpose(0, 3, 1, 5, 4, 2)
        qt = qt.reshape(B * G, nq, D, nl).astype(bf16)
        q_spec = pl.BlockSpec((1, 1, D, nl), lambda b, i: (b, i, 0, 0))
        o_spec, o_shape = q_spec, jax.ShapeDtypeStruct((B * G, nq, D, nl), f32)
    gt = gates.reshape(B, nq, tq, G, R, 3).transpose(0, 3, 1, 5, 4, 2).reshape(B * G, nq, 3, nl)
    gt = jnp.pad(gt, ((0, 0), (0, 0), (0, 5), (0, 0)))
    rows = lambda x, t: x.transpose(0, 2, 1, 3).reshape(B * G, -1, t, D).astype(bf16)
    cols = lambda x, t: x.reshape(B, -1, t, G, D).transpose(0, 3, 1, 4, 2).reshape(B * G, -1, D, t).astype(bf16)
    kcp = jnp.pad(kc, ((0, 0), (0, ncp - n_cmp), (0, 0), (0, 0)))
    vcp = jnp.pad(vc, ((0, 0), (0, ncp - n_cmp), (0, 0), (0, 0)))
    kc_r = rows(kcp, ncp)[:, 0]
    vc_c = cols(vcp, ncp)[:, 0]
    full = lambda shape: pl.BlockSpec((1,) + shape, lambda b, i: (b,) + (0,) * len(shape))
    per_q = lambda shape: pl.BlockSpec((1, 1) + shape, lambda b, i: (b, i) + (0,) * len(shape))
    out = pl.pallas_call(
        functools.partial(_nsa_attn_body, tq=tq, n_cmp=n_cmp, n_sel=n_sel,
                          q_tile0=q_pos0 // tq, win_base=win_base, natural=natural),
        grid=(B * G, nq),
        in_specs=[q_spec, full((ncp, D)), full((D, ncp)),
                  pl.BlockSpec((nsp, ncp), lambda b, i: (0, 0)),
                  full((lk // tk, tk, D)), full((lk // tk, D, tk)),
                  full((lw, D)), full((lw, D)),
                  per_q((8, nl))],
        out_specs=o_spec,
        out_shape=o_shape,
        scratch_shapes=[pltpu.VMEM((nsp, nl), jnp.int32)],
        compiler_params=pltpu.CompilerParams(
            dimension_semantics=("parallel", "parallel"), vmem_limit_bytes=VMEM_LIMIT),
        name="nsa_attention",
    )(qt, kc_r, vc_c, map_t, rows(ksl, tk), cols(vsl, tk), rows(kwin, lw)[:, 0], rows(vwin, lw)[:, 0], gt)
    if natural:
        return out
    return out.reshape(B, G, nq, D, R, tq).transpose(0, 2, 5, 1, 4, 3).reshape(B, T, NSA_Q)


def nsa_prompt(q, gates, rows, win, cw1, cw2, cpe):
    B, T = q.shape[:2]
    kc = compress_pair(rows[:, :, 0].reshape(B, T, -1), cw1[0], cw2[0], cpe[0])
    vc = compress_pair(rows[:, :, 1].reshape(B, T, -1), cw1[1], cw2[1], cpe[1])
    return nsa_attention(q, gates, kc, vc, rows[:, :, 2], rows[:, :, 3], win[:, :, 0], win[:, :, 1],
                         n_keys=T)


def compress_pair(x, w1, w2, pe):
    B, L, gd = x.shape
    n_chunks = L // CMP_STRIDE
    c = x[:, :n_chunks * CMP_STRIDE].reshape(B, n_chunks, CMP_STRIDE, gd)
    eye = jnp.eye(NSA_KV, dtype=w1.dtype)
    big = lambda w: jnp.einsum('ldh,gk->lgdkh', w, eye).reshape(CMP_STRIDE, gd, NSA_KV * CMP_HIDDEN)
    a = jnp.einsum('bnlx,lxy->bny', c, big(w1[:CMP_STRIDE]))
    bc = jnp.einsum('bnlx,lxy->bny', c, big(w1[CMP_STRIDE:]))
    h = a[:, :-1] + bc[:, 1:] + jnp.tile(jnp.einsum('ld,ldh->h', pe, w1), NSA_KV)
    w2_big = jnp.einsum('hd,gk->ghkd', w2, eye).reshape(NSA_KV * CMP_HIDDEN, gd)
    return (jax.nn.gelu(h) @ w2_big).reshape(B, n_chunks - 1, NSA_KV, HEAD_DIM)


def nsa_sample(q, gates, past, rows, win_all, cw1, cw2, cpe):
    DB, S = q.shape[:2]
    P, WB = past.shape[1], win_all.shape[1] - S
    gd = NSA_KV * HEAD_DIM
    kind = lambda w: past[:, :, w * gd:(w + 1) * gd]
    new = lambda w: rows[:, :, w].reshape(DB, S, gd)
    up = lambda n, m: -(-n // m) * m
    pad_rows = lambda x, n: jnp.pad(x, ((0, 0), (0, n - x.shape[1])) + ((0, 0),) * (x.ndim - 2))
    n_chunk_rows = (P + S) // CMP_STRIDE * CMP_STRIDE
    cmp_in = [kind(w)[:, :n_chunk_rows] if n_chunk_rows <= P
              else jnp.concatenate([kind(w), new(w)[:, :n_chunk_rows - P]], axis=1) for w in (0, 1)]
    kc = compress_pair(cmp_in[0], cw1[0], cw2[0], cpe[0])
    vc = compress_pair(cmp_in[1], cw1[1], cw2[1], cpe[1])
    tq = up(S, NSA_TQ_DECODE)
    lk = up(P + tq, NSA_TK)
    groups = lambda x: x.reshape(DB, -1, NSA_KV, HEAD_DIM)
    ksl = groups(pad_rows(jnp.concatenate([kind(2), new(2)], axis=1), lk))
    vsl = groups(pad_rows(jnp.concatenate([kind(3), new(3)], axis=1), lk))
    o = nsa_attention(pad_rows(q, tq), pad_rows(gates, tq), kc, vc, ksl, vsl, win_all[:, :, 0], win_all[:, :, 1],
                      n_keys=P + S, tq=tq, q_pos0=P, win_pos0=P - WB)
    return o[:, :S]


RW_PAIRS = RWKV_HEADS // 2
RW_CHUNK = 64


def _rwkv_scan_body(r_ref, w_ref, k_ref, kk_ref, b_ref, c1_ref, c2_ref, vt_ref, s0_ref, oh_ref,
                    y_ref, st_ref, vhi_scr, *, nb, tc):
    f32, bf16 = jnp.float32, jnp.bfloat16
    c = pl.program_id(1)

    @pl.when(c == 0)
    def _():
        st_ref[...] = s0_ref[...]

    row = lax.broadcasted_iota(jnp.int32, (LANE, LANE), 0)
    col = lax.broadcasted_iota(jnp.int32, (LANE, LANE), 1)
    ones1 = ((row // HEAD_DIM) == (col // HEAD_DIM)).astype(bf16)
    lane_t = lax.broadcasted_iota(jnp.int32, (HEAD_DIM, LANE), 1) % HEAD_DIM

    vt = vt_ref[...].reshape(nb * RW_PAIRS, HEAD_DIM, LANE)
    vhi_scr[...] = vt.astype(bf16)
    y_ref[...] = jnp.zeros_like(y_ref)

    pairs = [(b, p) for b in range(nb) for p in range(RW_PAIRS)]
    n_pairs = len(pairs)

    def group(t8, carry):
        t0 = pl.multiple_of(t8 * 8, 8)
        for j in range(0, 8, 2):
            ra = lambda ref, b, sl: ref[b, pl.ds(t0, 8), sl][j:j + 1]
            rb = lambda ref, b, sl: ref[b, pl.ds(t0, 8), sl][j + 1:j + 2]
            oh_a, oh_b = oh_ref[t0 + j], oh_ref[t0 + j + 1]
            vl = jnp.concatenate([vhi_scr[i] * oh for oh in (oh_a, oh_b) for i in range(n_pairs)], axis=0)
            vb_all = jnp.dot(vl, ones1, preferred_element_type=f32)
            pieces = []
            for b, p in pairs:
                sl = slice(LANE * p, LANE * (p + 1))
                s = st_ref[b, p]
                pieces.append((s * ra(kk_ref, b, sl)).astype(bf16))
                pieces.append((s * (ra(w_ref, b, sl) * rb(kk_ref, b, sl))).astype(bf16))
            u_all = jnp.dot(jnp.concatenate(pieces, axis=0), ones1, preferred_element_type=f32)
            py = []
            for i, (b, p) in enumerate(pairs):
                sl = slice(LANE * p, LANE * (p + 1))
                u1 = u_all[2 * HEAD_DIM * i:2 * HEAD_DIM * i + HEAD_DIM]
                u2 = u_all[2 * HEAD_DIM * i + HEAD_DIM:2 * HEAD_DIM * (i + 1)]
                vb_a = vb_all[HEAD_DIM * i:HEAD_DIM * (i + 1)]
                vb_b = vb_all[HEAD_DIM * (n_pairs + i):HEAD_DIM * (n_pairs + i + 1)]
                sa_b = u2 - u1 * ra(c1_ref, b, sl) + vb_a * ra(c2_ref, b, sl)
                s_a = st_ref[b, p] * ra(w_ref, b, sl) - u1 * ra(b_ref, b, sl) + vb_a * ra(k_ref, b, sl)
                s_b = s_a * rb(w_ref, b, sl) - sa_b * rb(b_ref, b, sl) + vb_b * rb(k_ref, b, sl)
                st_ref[b, p] = s_b
                py.append((s_a * ra(r_ref, b, sl)).astype(bf16))
                py.append((s_b * rb(r_ref, b, sl)).astype(bf16))
            y_all = jnp.dot(jnp.concatenate(py, axis=0), ones1, preferred_element_type=f32)
            m_a, m_b = lane_t == t0 + j, lane_t == t0 + j + 1
            for i, (b, p) in enumerate(pairs):
                y_a = y_all[2 * HEAD_DIM * i:2 * HEAD_DIM * i + HEAD_DIM]
                y_b = y_all[2 * HEAD_DIM * i + HEAD_DIM:2 * HEAD_DIM * (i + 1)]
                y_ref[b, 0, p] = jnp.where(m_a, y_a, jnp.where(m_b, y_b, y_ref[b, 0, p]))
        return carry

    lax.fori_loop(0, tc // 8, group, 0)


def rwkv_scan(r, w, k, v, kk, b, S0):
    f32 = jnp.float32
    B, T, W = r.shape
    tc = RW_CHUNK if T % RW_CHUNK == 0 else T
    assert tc <= RW_CHUNK and T % tc == 0 and tc % 8 == 0 and B % 2 == 0
    nc, nb = T // tc, 2
    vt = v.reshape(B, nc, tc, RW_PAIRS, 2, HEAD_DIM).transpose(0, 1, 3, 5, 4, 2)
    vt = jnp.pad(vt, ((0, 0),) * 5 + ((0, RW_CHUNK - tc),)).reshape(B, nc, RW_PAIRS, HEAD_DIM, LANE)
    s0 = S0.astype(f32).reshape(B, RW_PAIRS, 2, HEAD_DIM, HEAD_DIM).transpose(0, 1, 3, 2, 4)
    s0 = s0.reshape(B, RW_PAIRS, HEAD_DIM, LANE)
    oh = jnp.arange(LANE)[None, None, :] % HEAD_DIM == jnp.arange(RW_CHUNK)[:, None, None]
    oh = jnp.broadcast_to(oh, (RW_CHUNK, HEAD_DIM, LANE)).astype(jnp.bfloat16)
    kk_next = jnp.concatenate([kk[:, 1:], kk[:, :1]], axis=1)

    def head_dot(x):
        d = jnp.sum((x * kk_next).reshape(B, T, RWKV_HEADS, HEAD_DIM), axis=-1, keepdims=True)
        return jnp.broadcast_to(d, (B, T, RWKV_HEADS, HEAD_DIM)).reshape(B, T, W)

    c1, c2 = head_dot(b), head_dot(k)
    tok = pl.BlockSpec((nb, tc, W), lambda i, c: (i, c, 0))
    chk = pl.BlockSpec((nb, 1, RW_PAIRS, HEAD_DIM, LANE), lambda i, c: (i, c, 0, 0, 0))
    stt = pl.BlockSpec((nb, RW_PAIRS, HEAD_DIM, LANE), lambda i, c: (i, 0, 0, 0))
    y, st = pl.pallas_call(
        functools.partial(_rwkv_scan_body, nb=nb, tc=tc),
        grid=(B // nb, nc),
        in_specs=[tok, tok, tok, tok, tok, tok, tok, chk, stt,
                  pl.BlockSpec((RW_CHUNK, HEAD_DIM, LANE), lambda i, c: (0, 0, 0))],
        out_specs=[chk, stt],
        out_shape=[jax.ShapeDtypeStruct((B, nc, RW_PAIRS, HEAD_DIM, LANE), f32),
                   jax.ShapeDtypeStruct((B, RW_PAIRS, HEAD_DIM, LANE), f32)],
        scratch_shapes=[pltpu.VMEM((nb * RW_PAIRS, HEAD_DIM, LANE), jnp.bfloat16)],
        compiler_params=pltpu.CompilerParams(
            dimension_semantics=("parallel", "arbitrary"), vmem_limit_bytes=VMEM_LIMIT),
        name="rwkv_scan",
    )(r, w, k, kk, b, c1, c2, vt, s0, oh)
    y = y.reshape(B, nc, RW_PAIRS, HEAD_DIM, 2, RW_CHUNK)[..., :tc]
    y = y.transpose(0, 1, 5, 2, 4, 3).reshape(B, T, W)
    st = st.reshape(B, RW_PAIRS, HEAD_DIM, 2, HEAD_DIM).transpose(0, 1, 3, 2, 4)
    return y, st.reshape(B, RWKV_HEADS, HEAD_DIM, HEAD_DIM)


def rwkv_mix(p, prev, S0, mu, w0, wB, a0, aB, gB, k_k, k_a, r_k, ln_w, ln_b):
    f32 = jnp.float32
    B, T = p.shape[:2]
    W = RWKV_WIDTH
    p = p.astype(f32)
    p_prev = jnp.concatenate([prev.astype(f32)[:, None], p[:, :-1]], axis=1)
    ps = p + mu * (p_prev - p)
    r, k, v = ps[..., :W], ps[..., W:2 * W], ps[..., 2 * W:3 * W]
    o = 3 * W
    xw, xa, xg = ps[..., o:o + LORA_W], ps[..., o + LORA_W:o + LORA_W + LORA_A], ps[..., o + LORA_W + LORA_A:]
    z = w0 + jnp.tanh(xw) @ wB
    w = jnp.exp(-jnp.exp(-jax.nn.softplus(-z) - 0.5))
    a = jax.nn.sigmoid(a0 + xa @ aB)
    g = jax.nn.sigmoid(xg) @ gB
    heads = lambda t: t.reshape(B, T, RWKV_HEADS, HEAD_DIM)
    kk = heads(k * k_k)
    kk = kk * lax.rsqrt(jnp.maximum(jnp.sum(kk * kk, axis=-1, keepdims=True), 1e-24))
    k = k * (1.0 + (a - 1.0) * k_a)
    kk = kk.reshape(B, T, W)
    y, S = rwkv_scan(r, w, k, v, kk, kk * a, S0)
    y, k = heads(y), heads(k)
    r, v = heads(r), heads(v)
    mean = jnp.mean(y, axis=-1, keepdims=True)
    var = jnp.mean(jnp.square(y - mean), axis=-1, keepdims=True)
    y = ((y - mean) * lax.rsqrt(var + RWKV_GN_EPS)).reshape(B, T, W) * ln_w + ln_b
    y = y + (jnp.sum(r * k * r_k, axis=-1, keepdims=True) * v).reshape(B, T, W)
    return y * g, S, p[:, -1]


GLA_SUB = 16


def _gla_body(q_ref, k_ref, v_ref, g_ref, r_ref, gn_ref, s0_ref, o_ref, st_ref):
    f32, bf16 = jnp.float32, jnp.bfloat16
    C, SUB = q_ref.shape[1], GLA_SUB
    nsub = C // SUB
    c = pl.program_id(1)

    @pl.when(c == 0)
    def _():
        st_ref[...] = s0_ref[...]

    ti = lax.broadcasted_iota(jnp.int32, (C, C), 0)
    si = lax.broadcasted_iota(jnp.int32, (C, C), 1)
    tri = (si <= ti).astype(bf16)
    sub_t = lax.broadcasted_iota(jnp.int32, (SUB, 1), 0)
    sub_l = lax.broadcasted_iota(jnp.int32, (SUB, SUB), 1)
    for h in range(GLA_HEADS):
        kq = slice(h * GLA_DK, (h + 1) * GLA_DK)
        vv = slice(h * GLA_DV, (h + 1) * GLA_DV)
        q = q_ref[0, :, kq] * (GLA_DK ** -0.5)
        k = k_ref[0, :, kq]
        v = v_ref[0, :, vv].astype(bf16)
        g = g_ref[0, :, kq]
        g1 = g.astype(bf16)
        g2 = (g - g1.astype(f32)).astype(bf16)
        g3 = (g - g1.astype(f32) - g2.astype(f32)).astype(bf16)
        b = (jnp.dot(tri, g1, preferred_element_type=f32) + jnp.dot(tri, g2, preferred_element_type=f32)
             + jnp.dot(tri, g3, preferred_element_type=f32))
        b_last = b[C - 1:C]
        a_rows = []
        for I in range(nsub):
            rows = slice(I * SUB, (I + 1) * SUB)
            beta = b[I * SUB - 1:I * SUB] if I > 0 else jnp.zeros((1, GLA_DK), f32)
            qb, kb, bb = q[rows], k[rows], b[rows]
            a_diag = jnp.zeros((SUB, SUB), f32)
            for s in range(SUB):
                d = jnp.where(sub_t >= s, bb - bb[s:s + 1], -jnp.inf)
                col = jnp.sum(qb * kb[s:s + 1] * jnp.exp(d), axis=1, keepdims=True)
                a_diag = jnp.where(sub_l == s, col, a_diag)
            blocks = []
            if I > 0:
                qe = (qb * jnp.exp(bb - beta)).astype(bf16)
                ke = (k[:I * SUB] * jnp.exp(beta - b[:I * SUB])).astype(bf16)
                blocks.append(lax.dot_general(qe, ke, (((1,), (1,)), ((), ())), preferred_element_type=f32))
            blocks.append(a_diag)
            if I < nsub - 1:
                blocks.append(jnp.zeros((SUB, C - (I + 1) * SUB), f32))
            a_rows.append(jnp.concatenate(blocks, axis=1))
        a = jnp.concatenate(a_rows, axis=0).astype(bf16)
        st = st_ref[0, h]
        qd = (q * jnp.exp(b)).astype(bf16)
        o = (jnp.dot(a, v, preferred_element_type=f32)
             + lax.dot_general(qd, st.astype(bf16), (((1,), (1,)), ((), ())), preferred_element_type=f32))
        kd = (k * jnp.exp(b_last - b)).astype(bf16)
        st_ref[0, h] = (st * jnp.exp(b_last)
                        + lax.dot_general(v, kd, (((0,), (0,)), ((), ())), preferred_element_type=f32))
        y = o * lax.rsqrt(jnp.mean(o * o, axis=1, keepdims=True) + 1e-6) * gn_ref[...]
        rr = r_ref[0, :, vv]
        o_ref[0, :, vv] = y * (rr * jax.nn.sigmoid(rr))


def gla_mix(qk, v, gd, r, S0, gate_up, gate_b, gn):
    f32 = jnp.float32
    B, T = qk.shape[:2]
    KW, VW = GLA_KW, GLA_VW
    logg = jax.nn.log_sigmoid(gd[..., :GLA_RANK] @ gate_up + gate_b) / GLA_TAU
    C = GLA_CHUNK if T % GLA_CHUNK == 0 else GLA_SUB
    tp = -(-T // C) * C
    if tp != T:
        pad = lambda x: jnp.pad(x, ((0, 0), (0, tp - T), (0, 0)))
        qk, v, r, logg = pad(qk), pad(v), pad(r), pad(logg)
    tokv = pl.BlockSpec((1, C, VW), lambda b, c: (b, c, 0))
    stt = pl.BlockSpec((1, GLA_HEADS, GLA_DV, GLA_DK), lambda b, c: (b, 0, 0, 0))
    out, st = pl.pallas_call(
        _gla_body,
        grid=(B, tp // C),
        in_specs=[pl.BlockSpec((1, C, KW), lambda b, c: (b, c, 0)),
                  pl.BlockSpec((1, C, KW), lambda b, c: (b, c, 1)),
                  tokv, pl.BlockSpec((1, C, KW), lambda b, c: (b, c, 0)), tokv,
                  pl.BlockSpec((1, GLA_DV), lambda b, c: (0, 0)), stt],
        out_specs=[tokv, stt],
        out_shape=[jax.ShapeDtypeStruct((B, tp, VW), f32),
                   jax.ShapeDtypeStruct((B, GLA_HEADS, GLA_DV, GLA_DK), f32)],
        compiler_params=pltpu.CompilerParams(
            dimension_semantics=("parallel", "arbitrary"), vmem_limit_bytes=VMEM_LIMIT),
        name="gla_chunk",
    )(qk, qk, v, logg, r, gn.reshape(1, GLA_DV), S0.astype(f32).transpose(0, 1, 3, 2))
    return out[:, :T], st.transpose(0, 1, 3, 2)


def kernel(x_prompt, x_sample, cache_nsa_kv, cache_nsa_win, state_rwkv, state_rwkv_shift, state_gla, page_table, norm_mix, norm_ffn, w_in_even, nsa_qnorm, nsa_knorm, cmp_w1, cmp_w2, cmp_pe, rwkv_mu, rwkv_w0, rwkv_wB, rwkv_a0, rwkv_aB, rwkv_gB, rwkv_kk, rwkv_ka, rwkv_rk, rwkv_ln_w, rwkv_ln_b, w_out_even, w_in_odd, gla_gate_up, gla_gate_b, gla_norm, w_out_odd, ffn_gate, ffn_up, ffn_down):
    dt = x_prompt.dtype
    bf16 = jnp.bfloat16
    B, T = x_prompt.shape[:2]
    DB, S = x_sample.shape[:2]
    depth = norm_mix.shape[0]
    P = page_table.shape[1] * PAGE_SIZE
    WB = cache_nsa_win.shape[2]
    WP = min(WINDOW, T)
    pos_p = jnp.arange(T, dtype=jnp.int32)
    pos_s = P + jnp.arange(S, dtype=jnp.int32)
    win_pos_s = P - WB + jnp.arange(WB + S, dtype=jnp.int32)
    nsa_pad = -(-NSA_IN // LANE) * LANE
    y_p, y_s = x_prompt, x_sample
    kv_p, kv_s, win_p, win_s, rw_p, rw_s, sh_p, sh_s, gl_p, gl_s = ([] for _ in range(10))
    for layer in range(depth):
        li = layer // 2
        if layer % 2 == 0:
            w_in = jnp.concatenate([jnp.pad(w_in_even[li][:, :NSA_IN], ((0, 0), (0, nsa_pad - NSA_IN))),
                                    w_in_even[li][:, NSA_IN:]], axis=1).astype(bf16)
            even_splits = ((0, nsa_pad), (nsa_pad, nsa_pad + RWKV_IN))
            w_out = w_out_even[li].astype(bf16)
            hn_p, hr_p = norm_proj(y_p, norm_mix[layer], w_in, even_splits)
            hn_s, hr_s = norm_proj(y_s, norm_mix[layer], w_in, even_splits)
            q, g, rows, win = nsa_project(hn_p, pos_p, nsa_qnorm[li], nsa_knorm[li])
            o_nsa_p = nsa_prompt(q, g, rows, win, cmp_w1[li], cmp_w2[li], cmp_pe[li])
            kv_p.append(rows.astype(dt))
            win_p.append(win[:, T - WP:].astype(dt))
            q, g, rows, win = nsa_project(hn_s, pos_s, nsa_qnorm[li], nsa_knorm[li])
            pool = cache_nsa_kv.reshape(-1, PAGE_SIZE, 4 * NSA_KV * HEAD_DIM)
            past = pool[li * cache_nsa_kv.shape[1] + page_table].reshape(DB, P, 4 * NSA_KV * HEAD_DIM)
            win_all = jnp.concatenate([cache_nsa_win[li], win.astype(cache_nsa_win.dtype)], axis=1)
            o_nsa_s = nsa_sample(q, g, past, rows.astype(past.dtype), win_all, cmp_w1[li], cmp_w2[li], cmp_pe[li])
            kv_s.append(rows.astype(dt))
            win_s.append(win_all[:, S:].astype(dt))
            rw_par = (rwkv_mu[li], rwkv_w0[li], rwkv_wB[li], rwkv_a0[li], rwkv_aB[li], rwkv_gB[li],
                      rwkv_kk[li], rwkv_ka[li], rwkv_rk[li], rwkv_ln_w[li], rwkv_ln_b[li])
            o_rw_p, st, last = rwkv_mix(hr_p, jnp.zeros((B, RWKV_IN), dt),
                                        jnp.zeros((B, RWKV_HEADS, HEAD_DIM, HEAD_DIM), dt), *rw_par)
            rw_p.append(st.astype(dt))
            sh_p.append(last.astype(dt))
            o_rw_s, st, last = rwkv_mix(hr_s, state_rwkv_shift[li], state_rwkv[li], *rw_par)
            rw_s.append(st.astype(dt))
            sh_s.append(last.astype(dt))
            w_outs = [w_out[:NSA_Q], w_out[NSA_Q:]]
            y_p = proj_residual([o_nsa_p.astype(dt), o_rw_p.astype(dt)], w_outs, y_p)
            y_s = proj_residual([o_nsa_s.astype(dt), o_rw_s.astype(dt)], w_outs, y_s)
        else:
            wo = w_in_odd[li]
            o2 = 2 * GLA_KW + GLA_VW
            w_in = jnp.concatenate([wo[:, :o2], wo[:, o2 + GLA_RANK:],
                                    jnp.pad(wo[:, o2:o2 + GLA_RANK], ((0, 0), (0, LANE - GLA_RANK)))],
                                   axis=1).astype(bf16)
            odd_splits = ((0, 2 * GLA_KW), (2 * GLA_KW, o2), (o2, o2 + GLA_VW), (o2 + GLA_VW, o2 + GLA_VW + LANE))
            w_out = w_out_odd[li].astype(bf16)
            qk_p, v_p, r_p, gd_p = norm_proj(y_p, norm_mix[layer], w_in, odd_splits)
            qk_s, v_s, r_s, gd_s = norm_proj(y_s, norm_mix[layer], w_in, odd_splits)
            o_p, st = gla_mix(qk_p, v_p, gd_p, r_p, jnp.zeros((B, GLA_HEADS, GLA_DK, GLA_DV), dt),
                              gla_gate_up[li], gla_gate_b[li], gla_norm[li])
            gl_p.append(st.astype(dt))
            o_s, st = gla_mix(qk_s, v_s, gd_s, r_s, state_gla[li], gla_gate_up[li], gla_gate_b[li], gla_norm[li])
            gl_s.append(st.astype(dt))
            y_p = proj_residual([o_p.astype(dt)], [w_out], y_p)
            y_s = proj_residual([o_s.astype(dt)], [w_out], y_s)
        wg, wu, wd = ffn_gate[layer].astype(bf16), ffn_up[layer].astype(bf16), ffn_down[layer].astype(bf16)
        y_p = ffn(y_p, norm_ffn[layer], wg, wu, wd)
        y_s = ffn(y_s, norm_ffn[layer], wg, wu, wd)
    return (y_p, y_s, jnp.stack(kv_p), jnp.stack(kv_s), jnp.stack(win_p), jnp.stack(win_s),
            jnp.stack(rw_p), jnp.stack(rw_s), jnp.stack(sh_p), jnp.stack(sh_s),
            jnp.stack(gl_p), jnp.stack(gl_s))
```

```python
import functools

import jax
import jax.numpy as jnp
from jax import lax
from jax.experimental import pallas as pl
from jax.experimental.pallas import tpu as pltpu

D_MODEL = 1024
PAGE_SIZE = 128
HEAD_DIM = 64
NSA_HEADS = 8
NSA_KV = 2
NSA_REP = NSA_HEADS // NSA_KV
CMP_STRIDE = 16
CMP_BLOCK = 2 * CMP_STRIDE
CMP_HIDDEN = 2 * HEAD_DIM
SEL_BLOCK = 64
N_SELECT = 16
WINDOW = 512
ROPE_DIM = HEAD_DIM // 4
ROPE_THETA = 500000.0
RWKV_HEADS = 8
RWKV_WIDTH = RWKV_HEADS * HEAD_DIM
LORA_W = 64
LORA_A = 64
LORA_G = 128
RWKV_GN_EPS = 64e-5
GLA_HEADS = 4
GLA_DK = D_MODEL // 2 // GLA_HEADS
GLA_DV = D_MODEL // GLA_HEADS
GLA_RANK = 16
GLA_TAU = 16.0
GLA_CHUNK = 64
D_FF = ((8 * D_MODEL // 3 + 255) // 256) * 256
NSA_Q = NSA_HEADS * HEAD_DIM
NSA_KVW = 3 * 2 * NSA_KV * HEAD_DIM
NSA_IN = NSA_Q + NSA_KVW + 3 * NSA_HEADS
RWKV_IN = 3 * RWKV_WIDTH + LORA_W + LORA_A + LORA_G
EVEN_IN = NSA_IN + RWKV_IN
MIX_WIDTH = NSA_Q + RWKV_WIDTH
GLA_KW = GLA_HEADS * GLA_DK
GLA_VW = GLA_HEADS * GLA_DV
ODD_IN = 2 * GLA_KW + GLA_VW + GLA_RANK + GLA_VW

LANE = 128
VMEM_LIMIT = 48 * 1024 * 1024


ROW_TILE = 512
NORM_EPS = 1e-6


def _row_tile(m):
    return ROW_TILE if m % ROW_TILE == 0 else m


def _norm_bf16(x_ref, g_ref):
    x = x_ref[...]
    y = x * lax.rsqrt(jnp.mean(x * x, axis=-1, keepdims=True) + NORM_EPS)
    return (y * g_ref[...]).astype(jnp.bfloat16)


def _norm_proj_body(x_ref, g_ref, w_ref, *o_refs, splits):
    xn = _norm_bf16(x_ref, g_ref)
    for (a, b), o_ref in zip(splits, o_refs):
        o_ref[...] = jnp.dot(xn, w_ref[:, a:b], preferred_element_type=jnp.float32)


def norm_proj(x, gain, w, splits):
    bsz, t, k = x.shape
    m = bsz * t
    tm = _row_tile(m)
    outs = pl.pallas_call(
        functools.partial(_norm_proj_body, splits=splits),
        grid=(m // tm,),
        in_specs=[pl.BlockSpec((tm, k), lambda i: (i, 0)),
                  pl.BlockSpec((1, k), lambda i: (0, 0)),
                  pl.BlockSpec(w.shape, lambda i: (0, 0))],
        out_specs=[pl.BlockSpec((tm, b - a), lambda i: (i, 0)) for a, b in splits],
        out_shape=[jax.ShapeDtypeStruct((m, b - a), jnp.float32) for a, b in splits],
        compiler_params=pltpu.CompilerParams(dimension_semantics=("parallel",), vmem_limit_bytes=VMEM_LIMIT),
        name="norm_proj",
    )(x.reshape(m, k), gain.reshape(1, k), w)
    return [o.reshape(bsz, t, -1) for o in outs]


def _proj_res_body(*refs, n_in):
    x_refs, w_refs, res_ref, o_ref = refs[:n_in], refs[n_in:2 * n_in], refs[2 * n_in], refs[2 * n_in + 1]
    acc = res_ref[...]
    for x_ref, w_ref in zip(x_refs, w_refs):
        acc = acc + jnp.dot(x_ref[...].astype(jnp.bfloat16), w_ref[...], preferred_element_type=jnp.float32)
    o_ref[...] = acc


def proj_residual(xs, ws, res):
    bsz, t, n = res.shape
    m = bsz * t
    tm = _row_tile(m)
    out = pl.pallas_call(
        functools.partial(_proj_res_body, n_in=len(xs)),
        grid=(m // tm,),
        in_specs=([pl.BlockSpec((tm, x.shape[-1]), lambda i: (i, 0)) for x in xs]
                  + [pl.BlockSpec(w.shape, lambda i: (0, 0)) for w in ws]
                  + [pl.BlockSpec((tm, n), lambda i: (i, 0))]),
        out_specs=pl.BlockSpec((tm, n), lambda i: (i, 0)),
        out_shape=jax.ShapeDtypeStruct((m, n), jnp.float32),
        compiler_params=pltpu.CompilerParams(dimension_semantics=("parallel",), vmem_limit_bytes=VMEM_LIMIT),
        name="proj_residual",
    )(*[x.reshape(m, x.shape[-1]) for x in xs], *ws, res.reshape(m, n))
    return out.reshape(bsz, t, n)


FFN_COL_CHUNKS = 2


def _ffn_up_body(x_ref, g_ref, wg_ref, wu_ref, h_ref):
    xn = _norm_bf16(x_ref, g_ref)
    cw = h_ref.shape[1] // FFN_COL_CHUNKS
    for c in range(FFN_COL_CHUNKS):
        cols = slice(c * cw, (c + 1) * cw)
        g = jnp.dot(xn, wg_ref[:, cols], preferred_element_type=jnp.float32)
        u = jnp.dot(xn, wu_ref[:, cols], preferred_element_type=jnp.float32)
        h_ref[:, cols] = (g * jax.nn.sigmoid(g) * u).astype(h_ref.dtype)


def ffn(y, gain, wg, wu, wd):
    bsz, t, k = y.shape
    m = bsz * t
    tm = _row_tile(m)
    f = wg.shape[1]
    h = pl.pallas_call(
        _ffn_up_body,
        grid=(m // tm,),
        in_specs=[pl.BlockSpec((tm, k), lambda i: (i, 0)),
                  pl.BlockSpec((1, k), lambda i: (0, 0)),
                  pl.BlockSpec(wg.shape, lambda i: (0, 0)),
                  pl.BlockSpec(wu.shape, lambda i: (0, 0))],
        out_specs=pl.BlockSpec((tm, f), lambda i: (i, 0)),
        out_shape=jax.ShapeDtypeStruct((m, f), jnp.bfloat16),
        compiler_params=pltpu.CompilerParams(dimension_semantics=("parallel",), vmem_limit_bytes=VMEM_LIMIT),
        name="ffn_up",
    )(y.reshape(m, k), gain.reshape(1, k), wg, wu)
    return proj_residual([h.reshape(bsz, t, f)], [wd], y)


def rmsnorm(x, g, eps=1e-6):
    xf = x.astype(jnp.float32)
    y = xf * lax.rsqrt(jnp.mean(xf * xf, axis=-1, keepdims=True) + eps)
    return (y * g.astype(jnp.float32)).astype(x.dtype)


def rope(x, pos):
    half = ROPE_DIM // 2
    inv = ROPE_THETA ** (-jnp.arange(half, dtype=jnp.float32) / half)
    ang = pos.astype(jnp.float32)[:, None] * inv[None, :]
    shp = (pos.shape[0],) + (1,) * (x.ndim - 3) + (half,)
    cos, sin = jnp.cos(ang).reshape(shp), jnp.sin(ang).reshape(shp)
    xr = x[..., :ROPE_DIM].astype(jnp.float32)
    x1, x2 = xr[..., :half], xr[..., half:]
    rot = jnp.concatenate([x1 * cos - x2 * sin, x2 * cos + x1 * sin], axis=-1)
    return jnp.concatenate([rot.astype(x.dtype), x[..., ROPE_DIM:]], axis=-1)


def nsa_project(h, pos, qn, kn):
    B, T = h.shape[:2]
    q = h[..., :NSA_Q].reshape(B, T, NSA_HEADS, HEAD_DIM)
    kv = h[..., NSA_Q:NSA_Q + NSA_KVW].reshape(B, T, 3, 2, NSA_KV, HEAD_DIM)
    gates = jax.nn.sigmoid(h[..., NSA_Q + NSA_KVW:NSA_IN].astype(jnp.float32)).reshape(B, T, NSA_HEADS, 3)
    q = rope(rmsnorm(q, qn), pos)
    k = rope(rmsnorm(kv[:, :, :, 0], kn[:, None, :]), pos)
    kv = jnp.stack([k, kv[:, :, :, 1]], axis=3)
    rows = kv[:, :, :2].reshape(B, T, 4, NSA_KV, HEAD_DIM)
    win = kv[:, :, 2]
    return q, gates, rows, win


NEG = -1e30


NSA_TQ = 256
NSA_TK = 512
NSA_TQ_DECODE = 32


def _nsa_attn_body(qt_ref, kc_ref, vct_ref, map_ref, ks_ref, vst_ref, kw_ref, vw_ref, g_ref, o_ref, thr_scr,
                   *, tq, n_cmp, n_sel, q_tile0, win_base, natural):
    f32, bf16 = jnp.float32, jnp.bfloat16
    tk = NSA_TK
    nl = NSA_REP * tq
    i = pl.program_id(1) + q_tile0
    if natural:
        xq = qt_ref[0] * (HEAD_DIM ** -0.5)
        qt = jnp.concatenate([xq[:, HEAD_DIM * r:HEAD_DIM * (r + 1)].T for r in range(NSA_REP)], axis=1).astype(bf16)
    else:
        qt = qt_ref[0, 0]
    qpos = i * tq + lax.broadcasted_iota(jnp.int32, (1, nl), 1) % tq

    ncp = kc_ref.shape[1]
    n = lax.broadcasted_iota(jnp.int32, (ncp, 1), 0)
    ok = (n * CMP_STRIDE + (CMP_BLOCK - 1) <= qpos) & (n < n_cmp)
    s = jnp.dot(kc_ref[0], qt, preferred_element_type=f32)
    s = jnp.where(ok, s, NEG)
    m = jnp.max(s, axis=0, keepdims=True)
    e = jnp.where(ok, jnp.exp(s - m), 0.0)
    l = jnp.sum(e, axis=0, keepdims=True)
    p = e * jnp.where(l > 0.0, 1.0 / l, 0.0)
    o_cmp = jnp.dot(vct_ref[0], p.astype(bf16), preferred_element_type=f32)

    psum = p[:, 0:tq]
    for r in range(1, NSA_REP):
        psum = psum + p[:, r * tq:(r + 1) * tq]
    hi = psum.astype(bf16)
    lo = (psum - hi.astype(f32)).astype(bf16)
    imp = (jnp.dot(map_ref[...], hi, preferred_element_type=f32)
           + jnp.dot(map_ref[...], lo, preferred_element_type=f32))
    nsp = map_ref.shape[0]
    qp = qpos[:, 0:tq]
    blk = lax.broadcasted_iota(jnp.int32, (nsp, 1), 0)
    cur = qp // SEL_BLOCK
    valid = (blk * SEL_BLOCK <= qp) & (blk < n_sel)
    forced = (blk == 0) | (blk == cur) | (blk == cur - 1)
    x = jnp.where(valid, jnp.where(forced, 1e30, imp), NEG)
    x = jnp.where(blk < n_sel, x, -3e38)
    blkf = blk.astype(f32)
    sel = jnp.zeros((nsp, tq), f32)
    for _ in range(min(N_SELECT, n_sel)):
        mx = jnp.max(x, axis=0, keepdims=True)
        first = jnp.min(jnp.where(x == mx, blkf, float(nsp)), axis=0, keepdims=True)
        hit = blkf == first
        sel = jnp.where(hit, 1.0, sel)
        x = jnp.where(hit, -3e38, x)
    thr = jnp.where(valid & (sel > 0.5), qp, -1)
    thr_scr[...] = jnp.concatenate([thr] * NSA_REP, axis=1)

    def flash(s, vt, carry):
        m_old, l_old, acc = carry
        m_new = jnp.maximum(m_old, jnp.max(s, axis=0, keepdims=True))
        alpha = jnp.exp(m_old - m_new)
        p = jnp.exp(s - m_new)
        l_new = alpha * l_old + jnp.sum(p, axis=0, keepdims=True)
        acc = alpha * acc + jnp.dot(vt, p.astype(bf16), preferred_element_type=f32)
        return m_new, l_new, acc

    init = (jnp.full((1, nl), NEG, f32), jnp.zeros((1, nl), f32), jnp.zeros((HEAD_DIM, nl), f32))

    krow = lax.broadcasted_iota(jnp.int32, (SEL_BLOCK, 1), 0)

    def masked_scores(j):
        s = jnp.dot(ks_ref[0, j], qt, preferred_element_type=f32)
        th = thr_scr[pl.ds(pl.multiple_of(j * (tk // SEL_BLOCK), 8), tk // SEL_BLOCK), :]
        return jnp.concatenate(
            [jnp.where((j * tk + b * SEL_BLOCK + krow) <= th[b:b + 1], s[b * SEL_BLOCK:(b + 1) * SEL_BLOCK], NEG)
             for b in range(tk // SEL_BLOCK)], axis=0)

    def sel_pair(j2, carry):
        m_old, l_old, acc = carry
        s0, s1 = masked_scores(2 * j2), masked_scores(2 * j2 + 1)
        m_new = jnp.maximum(m_old, jnp.maximum(jnp.max(s0, axis=0, keepdims=True), jnp.max(s1, axis=0, keepdims=True)))
        alpha = jnp.exp(m_old - m_new)
        p0, p1 = jnp.exp(s0 - m_new), jnp.exp(s1 - m_new)
        l_new = alpha * l_old + jnp.sum(p0, axis=0, keepdims=True) + jnp.sum(p1, axis=0, keepdims=True)
        acc = (alpha * acc + jnp.dot(vst_ref[0, 2 * j2], p0.astype(bf16), preferred_element_type=f32)
               + jnp.dot(vst_ref[0, 2 * j2 + 1], p1.astype(bf16), preferred_element_type=f32))
        return m_new, l_new, acc

    n_tiles = (i * tq + tq - 1) // tk + 1
    carry = lax.fori_loop(0, n_tiles // 2, sel_pair, init)
    carry = lax.cond(n_tiles % 2 == 1,
                     lambda c: flash(masked_scores(n_tiles - 1), vst_ref[0, n_tiles - 1], c),
                     lambda c: c, carry)
    _, l_s, acc_s = carry
    o_slc = acc_s / l_s

    nkw = WINDOW + tq
    k_first = i * tq - WINDOW
    row0 = pl.multiple_of(k_first - win_base, 16)
    kpos = k_first + lax.broadcasted_iota(jnp.int32, (nkw, 1), 0)
    allowed = (kpos <= qpos) & (kpos > qpos - WINDOW) & (kpos >= 0)
    s = jnp.dot(kw_ref[0, pl.ds(row0, nkw), :], qt, preferred_element_type=f32)
    s = jnp.where(allowed, s, NEG)
    e = jnp.where(allowed, jnp.exp(s - jnp.max(s, axis=0, keepdims=True)), 0.0)
    o_win = (lax.dot_general(vw_ref[0, pl.ds(row0, nkw), :], e.astype(bf16), (((0,), (0,)), ((), ())),
                             preferred_element_type=f32) / jnp.sum(e, axis=0, keepdims=True))

    g = g_ref[0, 0]
    res = g[0:1] * o_cmp + g[1:2] * o_slc + g[2:3] * o_win
    if natural:
        o_ref[0] = jnp.concatenate([res[:, tq * r:tq * (r + 1)].T for r in range(NSA_REP)], axis=1)
    else:
        o_ref[0, 0] = res


def nsa_attention(q, gates, kc, vc, ksl, vsl, kwin, vwin, *, n_keys, tq=NSA_TQ, q_pos0=0, win_pos0=0):
    f32, bf16 = jnp.float32, jnp.bfloat16
    B, T = q.shape[:2]
    G, R, D = NSA_KV, NSA_REP, HEAD_DIM
    tk = NSA_TK
    nq, nl = T // tq, R * tq
    n_cmp, n_sel = kc.shape[1], -(-n_keys // SEL_BLOCK)
    lk = ksl.shape[1]
    assert T % tq == 0 and nl % LANE == 0 and lk % tk == 0 and q_pos0 % tq == 0 and tq % 16 == 0
    assert lk >= q_pos0 + T and win_pos0 <= max(q_pos0 - WINDOW, 0)
    win_base = win_pos0 - WINDOW
    lw = q_pos0 + T - win_base
    wpad = lambda x: jnp.pad(x[:, :lw - WINDOW], ((0, 0), (WINDOW, max(lw - WINDOW - x.shape[1], 0)), (0, 0), (0, 0)))
    kwin, vwin = wpad(kwin), wpad(vwin)
    ncp = -(-n_cmp // LANE) * LANE
    nsp = -(-n_sel // LANE) * LANE
    n = jnp.arange(ncp)[None, :]
    jb = jnp.arange(nsp)[:, None]
    map_t = ((n * CMP_STRIDE <= jb * SEL_BLOCK + SEL_BLOCK - 1) & (n * CMP_STRIDE + CMP_BLOCK - 1 >= jb * SEL_BLOCK)
             & (n < n_cmp) & (jb < n_sel)).astype(bf16)
    natural = tq % LANE == 0
    if natural:
        qt = q.reshape(B, T, NSA_Q)
        q_spec = pl.BlockSpec((1, tq, R * D), lambda b, i: (b // G, i, b % G))
        o_spec, o_shape = q_spec, jax.ShapeDtypeStruct((B, T, NSA_Q), f32)
    else:
        qt = (q * (D ** -0.5)).reshape(B, nq, tq, G, R, D).transpose(0, 3, 1, 5, 4, 2)
        qt = qt.reshape(B * G, nq, D, nl).astype(bf16)
        q_spec = pl.BlockSpec((1, 1, D, nl), lambda b, i: (b, i, 0, 0))
        o_spec, o_shape = q_spec, jax.ShapeDtypeStruct((B * G, nq, D, nl), f32)
    gt = gates.reshape(B, nq, tq, G, R, 3).transpose(0, 3, 1, 5, 4, 2).reshape(B * G, nq, 3, nl)
    gt = jnp.pad(gt, ((0, 0), (0, 0), (0, 5), (0, 0)))
    rows = lambda x, t: x.transpose(0, 2, 1, 3).reshape(B * G, -1, t, D).astype(bf16)
    cols = lambda x, t: x.reshape(B, -1, t, G, D).transpose(0, 3, 1, 4, 2).reshape(B * G, -1, D, t).astype(bf16)
    kcp = jnp.pad(kc, ((0, 0), (0, ncp - n_cmp), (0, 0), (0, 0)))
    vcp = jnp.pad(vc, ((0, 0), (0, ncp - n_cmp), (0, 0), (0, 0)))
    kc_r = rows(kcp, ncp)[:, 0]
    vc_c = cols(vcp, ncp)[:, 0]
    full = lambda shape: pl.BlockSpec((1,) + shape, lambda b, i: (b,) + (0,) * len(shape))
    per_q = lambda shape: pl.BlockSpec((1, 1) + shape, lambda b, i: (b, i) + (0,) * len(shape))
    out = pl.pallas_call(
        functools.partial(_nsa_attn_body, tq=tq, n_cmp=n_cmp, n_sel=n_sel,
                          q_tile0=q_pos0 // tq, win_base=win_base, natural=natural),
        grid=(B * G, nq),
        in_specs=[q_spec, full((ncp, D)), full((D, ncp)),
                  pl.BlockSpec((nsp, ncp), lambda b, i: (0, 0)),
                  full((lk // tk, tk, D)), full((lk // tk, D, tk)),
                  full((lw, D)), full((lw, D)),
                  per_q((8, nl))],
        out_specs=o_spec,
        out_shape=o_shape,
        scratch_shapes=[pltpu.VMEM((nsp, nl), jnp.int32)],
        compiler_params=pltpu.CompilerParams(
            dimension_semantics=("parallel", "parallel"), vmem_limit_bytes=VMEM_LIMIT),
        name="nsa_attention",
    )(qt, kc_r, vc_c, map_t, rows(ksl, tk), cols(vsl, tk), rows(kwin, lw)[:, 0], rows(vwin, lw)[:, 0], gt)
    if natural:
        return out
    return out.reshape(B, G, nq, D, R, tq).transpose(0, 2, 5, 1, 4, 3).reshape(B, T, NSA_Q)


def nsa_prompt(q, gates, rows, win, cw1, cw2, cpe):
    B, T = q.shape[:2]
    kc = compress_pair(rows[:, :, 0].reshape(B, T, -1), cw1[0], cw2[0], cpe[0])
    vc = compress_pair(rows[:, :, 1].reshape(B, T, -1), cw1[1], cw2[1], cpe[1])
    return nsa_attention(q, gates, kc, vc, rows[:, :, 2], rows[:, :, 3], win[:, :, 0], win[:, :, 1],
                         n_keys=T)


def compress_pair(x, w1, w2, pe):
    B, L, gd = x.shape
    n_chunks = L // CMP_STRIDE
    c = x[:, :n_chunks * CMP_STRIDE].reshape(B, n_chunks, CMP_STRIDE, gd)
    eye = jnp.eye(NSA_KV, dtype=w1.dtype)
    big = lambda w: jnp.einsum('ldh,gk->lgdkh', w, eye).reshape(CMP_STRIDE, gd, NSA_KV * CMP_HIDDEN)
    a = jnp.einsum('bnlx,lxy->bny', c, big(w1[:CMP_STRIDE]))
    bc = jnp.einsum('bnlx,lxy->bny', c, big(w1[CMP_STRIDE:]))
    h = a[:, :-1] + bc[:, 1:] + jnp.tile(jnp.einsum('ld,ldh->h', pe, w1), NSA_KV)
    w2_big = jnp.einsum('hd,gk->ghkd', w2, eye).reshape(NSA_KV * CMP_HIDDEN, gd)
    return (jax.nn.gelu(h) @ w2_big).reshape(B, n_chunks - 1, NSA_KV, HEAD_DIM)


def nsa_sample(q, gates, past, rows, win_all, cw1, cw2, cpe):
    DB, S = q.shape[:2]
    P, WB = past.shape[1], win_all.shape[1] - S
    gd = NSA_KV * HEAD_DIM
    kind = lambda w: past[:, :, w * gd:(w + 1) * gd]
    new = lambda w: rows[:, :, w].reshape(DB, S, gd)
    up = lambda n, m: -(-n // m) * m
    pad_rows = lambda x, n: jnp.pad(x, ((0, 0), (0, n - x.shape[1])) + ((0, 0),) * (x.ndim - 2))
    n_chunk_rows = (P + S) // CMP_STRIDE * CMP_STRIDE
    cmp_in = [kind(w)[:, :n_chunk_rows] if n_chunk_rows <= P
              else jnp.concatenate([kind(w), new(w)[:, :n_chunk_rows - P]], axis=1) for w in (0, 1)]
    kc = compress_pair(cmp_in[0], cw1[0], cw2[0], cpe[0])
    vc = compress_pair(cmp_in[1], cw1[1], cw2[1], cpe[1])
    tq = up(S, NSA_TQ_DECODE)
    lk = up(P + tq, NSA_TK)
    groups = lambda x: x.reshape(DB, -1, NSA_KV, HEAD_DIM)
    ksl = groups(pad_rows(jnp.concatenate([kind(2), new(2)], axis=1), lk))
    vsl = groups(pad_rows(jnp.concatenate([kind(3), new(3)], axis=1), lk))
    o = nsa_attention(pad_rows(q, tq), pad_rows(gates, tq), kc, vc, ksl, vsl, win_all[:, :, 0], win_all[:, :, 1],
                      n_keys=P + S, tq=tq, q_pos0=P, win_pos0=P - WB)
    return o[:, :S]


RW_PAIRS = RWKV_HEADS // 2
RW_CHUNK = 64


def _rwkv_scan_body(r_ref, w_ref, k_ref, kk_ref, b_ref, c1_ref, c2_ref, vt_ref, s0_ref, oh_ref,
                    y_ref, st_ref, vhi_scr, yacc_scr, *, nb, tc, natural):
    f32, bf16 = jnp.float32, jnp.bfloat16
    c = pl.program_id(1)
    halves = lambda x: jnp.concatenate([x[:, :HEAD_DIM].T, x[:, HEAD_DIM:].T], axis=1)

    @pl.when(c == 0)
    def _():
        st_ref[...] = s0_ref[...]

    row = lax.broadcasted_iota(jnp.int32, (LANE, LANE), 0)
    col = lax.broadcasted_iota(jnp.int32, (LANE, LANE), 1)
    ones1 = ((row // HEAD_DIM) == (col // HEAD_DIM)).astype(bf16)
    lane_t = lax.broadcasted_iota(jnp.int32, (HEAD_DIM, LANE), 1) % HEAD_DIM

    for b in range(nb):
        for p in range(RW_PAIRS):
            vt = halves(vt_ref[b, :, LANE * p:LANE * (p + 1)]) if natural else vt_ref[b, 0, p]
            vhi_scr[b * RW_PAIRS + p] = vt.astype(bf16)
    yacc_scr[...] = jnp.zeros_like(yacc_scr)

    pairs = [(b, p) for b in range(nb) for p in range(RW_PAIRS)]
    n_pairs = len(pairs)

    def group(t8, carry):
        t0 = pl.multiple_of(t8 * 8, 8)
        for j in range(0, 8, 2):
            ra = lambda ref, b, sl: ref[b, pl.ds(t0, 8), sl][j:j + 1]
            rb = lambda ref, b, sl: ref[b, pl.ds(t0, 8), sl][j + 1:j + 2]
            oh_a, oh_b = oh_ref[t0 + j], oh_ref[t0 + j + 1]
            vl = jnp.concatenate([vhi_scr[i] * oh for oh in (oh_a, oh_b) for i in range(n_pairs)], axis=0)
            vb_all = jnp.dot(vl, ones1, preferred_element_type=f32)
            pieces = []
            for b, p in pairs:
                sl = slice(LANE * p, LANE * (p + 1))
                s = st_ref[b, p]
                pieces.append((s * ra(kk_ref, b, sl)).astype(bf16))
                pieces.append((s * (ra(w_ref, b, sl) * rb(kk_ref, b, sl))).astype(bf16))
            u_all = jnp.dot(jnp.concatenate(pieces, axis=0), ones1, preferred_element_type=f32)
            py = []
            for i, (b, p) in enumerate(pairs):
                sl = slice(LANE * p, LANE * (p + 1))
                u1 = u_all[2 * HEAD_DIM * i:2 * HEAD_DIM * i + HEAD_DIM]
                u2 = u_all[2 * HEAD_DIM * i + HEAD_DIM:2 * HEAD_DIM * (i + 1)]
                vb_a = vb_all[HEAD_DIM * i:HEAD_DIM * (i + 1)]
                vb_b = vb_all[HEAD_DIM * (n_pairs + i):HEAD_DIM * (n_pairs + i + 1)]
                sa_b = u2 - u1 * ra(c1_ref, b, sl) + vb_a * ra(c2_ref, b, sl)
                s_a = st_ref[b, p] * ra(w_ref, b, sl) - u1 * ra(b_ref, b, sl) + vb_a * ra(k_ref, b, sl)
                s_b = s_a * rb(w_ref, b, sl) - sa_b * rb(b_ref, b, sl) + vb_b * rb(k_ref, b, sl)
                st_ref[b, p] = s_b
                py.append((s_a * ra(r_ref, b, sl)).astype(bf16))
                py.append((s_b * rb(r_ref, b, sl)).astype(bf16))
            y_all = jnp.dot(jnp.concatenate(py, axis=0), ones1, preferred_element_type=f32)
            m_a, m_b = lane_t == t0 + j, lane_t == t0 + j + 1
            for i, (b, p) in enumerate(pairs):
                y_a = y_all[2 * HEAD_DIM * i:2 * HEAD_DIM * i + HEAD_DIM]
                y_b = y_all[2 * HEAD_DIM * i + HEAD_DIM:2 * HEAD_DIM * (i + 1)]
                yacc_scr[i] = jnp.where(m_a, y_a, jnp.where(m_b, y_b, yacc_scr[i]))
        return carry

    lax.fori_loop(0, tc // 8, group, 0)
    for i, (b, p) in enumerate(pairs):
        if natural:
            y_ref[b, :, LANE * p:LANE * (p + 1)] = halves(yacc_scr[i])
        else:
            y_ref[b, 0, p] = yacc_scr[i]


def rwkv_scan(r, w, k, v, kk, b, S0):
    f32 = jnp.float32
    B, T, W = r.shape
    tc = RW_CHUNK if T % RW_CHUNK == 0 else T
    assert tc <= RW_CHUNK and T % tc == 0 and tc % 8 == 0 and B % 2 == 0
    nc, nb = T // tc, 2
    natural = tc == RW_CHUNK
    tok = pl.BlockSpec((nb, tc, W), lambda i, c: (i, c, 0))
    chk = pl.BlockSpec((nb, 1, RW_PAIRS, HEAD_DIM, LANE), lambda i, c: (i, c, 0, 0, 0))
    if natural:
        vt, vy_spec, y_shape = v, tok, jax.ShapeDtypeStruct((B, T, W), f32)
    else:
        vt = v.reshape(B, nc, tc, RW_PAIRS, 2, HEAD_DIM).transpose(0, 1, 3, 5, 4, 2)
        vt = jnp.pad(vt, ((0, 0),) * 5 + ((0, RW_CHUNK - tc),)).reshape(B, nc, RW_PAIRS, HEAD_DIM, LANE)
        vy_spec, y_shape = chk, jax.ShapeDtypeStruct((B, nc, RW_PAIRS, HEAD_DIM, LANE), f32)
    s0 = S0.astype(f32).reshape(B, RW_PAIRS, 2, HEAD_DIM, HEAD_DIM).transpose(0, 1, 3, 2, 4)
    s0 = s0.reshape(B, RW_PAIRS, HEAD_DIM, LANE)
    oh = jnp.arange(LANE)[None, None, :] % HEAD_DIM == jnp.arange(RW_CHUNK)[:, None, None]
    oh = jnp.broadcast_to(oh, (RW_CHUNK, HEAD_DIM, LANE)).astype(jnp.bfloat16)
    kk_next = jnp.concatenate([kk[:, 1:], kk[:, :1]], axis=1)

    def head_dot(x):
        d = jnp.sum((x * kk_next).reshape(B, T, RWKV_HEADS, HEAD_DIM), axis=-1, keepdims=True)
        return jnp.broadcast_to(d, (B, T, RWKV_HEADS, HEAD_DIM)).reshape(B, T, W)

    c1, c2 = head_dot(b), head_dot(k)
    stt = pl.BlockSpec((nb, RW_PAIRS, HEAD_DIM, LANE), lambda i, c: (i, 0, 0, 0))
    y, st = pl.pallas_call(
        functools.partial(_rwkv_scan_body, nb=nb, tc=tc, natural=natural),
        grid=(B // nb, nc),
        in_specs=[tok, tok, tok, tok, tok, tok, tok, vy_spec, stt,
                  pl.BlockSpec((RW_CHUNK, HEAD_DIM, LANE), lambda i, c: (0, 0, 0))],
        out_specs=[vy_spec, stt],
        out_shape=[y_shape, jax.ShapeDtypeStruct((B, RW_PAIRS, HEAD_DIM, LANE), f32)],
        scratch_shapes=[pltpu.VMEM((nb * RW_PAIRS, HEAD_DIM, LANE), jnp.bfloat16),
                        pltpu.VMEM((nb * RW_PAIRS, HEAD_DIM, LANE), f32)],
        compiler_params=pltpu.CompilerParams(
            dimension_semantics=("parallel", "arbitrary"), vmem_limit_bytes=VMEM_LIMIT),
        name="rwkv_scan",
    )(r, w, k, kk, b, c1, c2, vt, s0, oh)
    if not natural:
        y = y.reshape(B, nc, RW_PAIRS, HEAD_DIM, 2, RW_CHUNK)[..., :tc]
        y = y.transpose(0, 1, 5, 2, 4, 3).reshape(B, T, W)
    st = st.reshape(B, RW_PAIRS, HEAD_DIM, 2, HEAD_DIM).transpose(0, 1, 3, 2, 4)
    return y, st.reshape(B, RWKV_HEADS, HEAD_DIM, HEAD_DIM)


def rwkv_mix(p, prev, S0, mu, w0, wB, a0, aB, gB, k_k, k_a, r_k, ln_w, ln_b):
    f32 = jnp.float32
    B, T = p.shape[:2]
    W = RWKV_WIDTH
    p = p.astype(f32)
    p_prev = jnp.concatenate([prev.astype(f32)[:, None], p[:, :-1]], axis=1)
    ps = p + mu * (p_prev - p)
    r, k, v = ps[..., :W], ps[..., W:2 * W], ps[..., 2 * W:3 * W]
    o = 3 * W
    xw, xa, xg = ps[..., o:o + LORA_W], ps[..., o + LORA_W:o + LORA_W + LORA_A], ps[..., o + LORA_W + LORA_A:]
    z = w0 + jnp.tanh(xw) @ wB
    w = jnp.exp(-jnp.exp(-jax.nn.softplus(-z) - 0.5))
    a = jax.nn.sigmoid(a0 + xa @ aB)
    g = jax.nn.sigmoid(xg) @ gB
    heads = lambda t: t.reshape(B, T, RWKV_HEADS, HEAD_DIM)
    kk = heads(k * k_k)
    kk = kk * lax.rsqrt(jnp.maximum(jnp.sum(kk * kk, axis=-1, keepdims=True), 1e-24))
    k = k * (1.0 + (a - 1.0) * k_a)
    kk = kk.reshape(B, T, W)
    y, S = rwkv_scan(r, w, k, v, kk, kk * a, S0)
    y, k = heads(y), heads(k)
    r, v = heads(r), heads(v)
    mean = jnp.mean(y, axis=-1, keepdims=True)
    var = jnp.mean(jnp.square(y - mean), axis=-1, keepdims=True)
    y = ((y - mean) * lax.rsqrt(var + RWKV_GN_EPS)).reshape(B, T, W) * ln_w + ln_b
    y = y + (jnp.sum(r * k * r_k, axis=-1, keepdims=True) * v).reshape(B, T, W)
    return y * g, S, p[:, -1]


GLA_SUB = 16


def _gla_body(q_ref, k_ref, v_ref, g_ref, r_ref, gn_ref, s0_ref, o_ref, st_ref):
    f32, bf16 = jnp.float32, jnp.bfloat16
    C, SUB = q_ref.shape[1], GLA_SUB
    nsub = C // SUB
    c = pl.program_id(1)

    @pl.when(c == 0)
    def _():
        st_ref[...] = s0_ref[...]

    ti = lax.broadcasted_iota(jnp.int32, (C, C), 0)
    si = lax.broadcasted_iota(jnp.int32, (C, C), 1)
    tri = (si <= ti).astype(bf16)
    sub_t = lax.broadcasted_iota(jnp.int32, (SUB, 1), 0)
    sub_l = lax.broadcasted_iota(jnp.int32, (SUB, SUB), 1)
    for h in range(GLA_HEADS):
        kq = slice(h * GLA_DK, (h + 1) * GLA_DK)
        vv = slice(h * GLA_DV, (h + 1) * GLA_DV)
        q = q_ref[0, :, kq] * (GLA_DK ** -0.5)
        k = k_ref[0, :, kq]
        v = v_ref[0, :, vv].astype(bf16)
        g = g_ref[0, :, kq]
        g1 = g.astype(bf16)
        g2 = (g - g1.astype(f32)).astype(bf16)
        g3 = (g - g1.astype(f32) - g2.astype(f32)).astype(bf16)
        b = (jnp.dot(tri, g1, preferred_element_type=f32) + jnp.dot(tri, g2, preferred_element_type=f32)
             + jnp.dot(tri, g3, preferred_element_type=f32))
        b_last = b[C - 1:C]
        a_rows = []
        for I in range(nsub):
            rows = slice(I * SUB, (I + 1) * SUB)
            beta = b[I * SUB - 1:I * SUB] if I > 0 else jnp.zeros((1, GLA_DK), f32)
            qb, kb, bb = q[rows], k[rows], b[rows]
            a_diag = jnp.zeros((SUB, SUB), f32)
            for s in range(SUB):
                d = jnp.where(sub_t >= s, bb - bb[s:s + 1], -jnp.inf)
                col = jnp.sum(qb * kb[s:s + 1] * jnp.exp(d), axis=1, keepdims=True)
                a_diag = jnp.where(sub_l == s, col, a_diag)
            blocks = []
            if I > 0:
                qe = (qb * jnp.exp(bb - beta)).astype(bf16)
                ke = (k[:I * SUB] * jnp.exp(beta - b[:I * SUB])).astype(bf16)
                blocks.append(lax.dot_general(qe, ke, (((1,), (1,)), ((), ())), preferred_element_type=f32))
            blocks.append(a_diag)
            if I < nsub - 1:
                blocks.append(jnp.zeros((SUB, C - (I + 1) * SUB), f32))
            a_rows.append(jnp.concatenate(blocks, axis=1))
        a = jnp.concatenate(a_rows, axis=0).astype(bf16)
        st = st_ref[0, h]
        qd = (q * jnp.exp(b)).astype(bf16)
        o = (jnp.dot(a, v, preferred_element_type=f32)
             + lax.dot_general(qd, st.astype(bf16), (((1,), (1,)), ((), ())), preferred_element_type=f32))
        kd = (k * jnp.exp(b_last - b)).astype(bf16)
        st_ref[0, h] = (st * jnp.exp(b_last)
                        + lax.dot_general(v, kd, (((0,), (0,)), ((), ())), preferred_element_type=f32))
        y = o * lax.rsqrt(jnp.mean(o * o, axis=1, keepdims=True) + 1e-6) * gn_ref[...]
        rr = r_ref[0, :, vv]
        o_ref[0, :, vv] = y * (rr * jax.nn.sigmoid(rr))


def gla_mix(qk, v, gd, r, S0, gate_up, gate_b, gn):
    f32 = jnp.float32
    B, T = qk.shape[:2]
    KW, VW = GLA_KW, GLA_VW
    logg = jax.nn.log_sigmoid(gd[..., :GLA_RANK] @ gate_up + gate_b) / GLA_TAU
    C = GLA_CHUNK if T % GLA_CHUNK == 0 else GLA_SUB
    tp = -(-T // C) * C
    if tp != T:
        pad = lambda x: jnp.pad(x, ((0, 0), (0, tp - T), (0, 0)))
        qk, v, r, logg = pad(qk), pad(v), pad(r), pad(logg)
    tokv = pl.BlockSpec((1, C, VW), lambda b, c: (b, c, 0))
    stt = pl.BlockSpec((1, GLA_HEADS, GLA_DV, GLA_DK), lambda b, c: (b, 0, 0, 0))
    out, st = pl.pallas_call(
        _gla_body,
        grid=(B, tp // C),
        in_specs=[pl.BlockSpec((1, C, KW), lambda b, c: (b, c, 0)),
                  pl.BlockSpec((1, C, KW), lambda b, c: (b, c, 1)),
                  tokv, pl.BlockSpec((1, C, KW), lambda b, c: (b, c, 0)), tokv,
                  pl.BlockSpec((1, GLA_DV), lambda b, c: (0, 0)), stt],
        out_specs=[tokv, stt],
        out_shape=[jax.ShapeDtypeStruct((B, tp, VW), f32),
                   jax.ShapeDtypeStruct((B, GLA_HEADS, GLA_DV, GLA_DK), f32)],
        compiler_params=pltpu.CompilerParams(
            dimension_semantics=("parallel", "arbitrary"), vmem_limit_bytes=VMEM_LIMIT),
        name="gla_chunk",
    )(qk, qk, v, logg, r, gn.reshape(1, GLA_DV), S0.astype(f32).transpose(0, 1, 3, 2))
    return out[:, :T], st.transpose(0, 1, 3, 2)


def kernel(x_prompt, x_sample, cache_nsa_kv, cache_nsa_win, state_rwkv, state_rwkv_shift, state_gla, page_table, norm_mix, norm_ffn, w_in_even, nsa_qnorm, nsa_knorm, cmp_w1, cmp_w2, cmp_pe, rwkv_mu, rwkv_w0, rwkv_wB, rwkv_a0, rwkv_aB, rwkv_gB, rwkv_kk, rwkv_ka, rwkv_rk, rwkv_ln_w, rwkv_ln_b, w_out_even, w_in_odd, gla_gate_up, gla_gate_b, gla_norm, w_out_odd, ffn_gate, ffn_up, ffn_down):
    dt = x_prompt.dtype
    bf16 = jnp.bfloat16
    B, T = x_prompt.shape[:2]
    DB, S = x_sample.shape[:2]
    depth = norm_mix.shape[0]
    P = page_table.shape[1] * PAGE_SIZE
    WB = cache_nsa_win.shape[2]
    WP = min(WINDOW, T)
    pos_p = jnp.arange(T, dtype=jnp.int32)
    pos_s = P + jnp.arange(S, dtype=jnp.int32)
    win_pos_s = P - WB + jnp.arange(WB + S, dtype=jnp.int32)
    nsa_pad = -(-NSA_IN // LANE) * LANE
    y_p, y_s = x_prompt, x_sample
    kv_p, kv_s, win_p, win_s, rw_p, rw_s, sh_p, sh_s, gl_p, gl_s = ([] for _ in range(10))
    for layer in range(depth):
        li = layer // 2
        if layer % 2 == 0:
            w_in = jnp.concatenate([jnp.pad(w_in_even[li][:, :NSA_IN], ((0, 0), (0, nsa_pad - NSA_IN))),
                                    w_in_even[li][:, NSA_IN:]], axis=1).astype(bf16)
            even_splits = ((0, nsa_pad), (nsa_pad, nsa_pad + RWKV_IN))
            w_out = w_out_even[li].astype(bf16)
            hn_p, hr_p = norm_proj(y_p, norm_mix[layer], w_in, even_splits)
            hn_s, hr_s = norm_proj(y_s, norm_mix[layer], w_in, even_splits)
            q, g, rows, win = nsa_project(hn_p, pos_p, nsa_qnorm[li], nsa_knorm[li])
            o_nsa_p = nsa_prompt(q, g, rows, win, cmp_w1[li], cmp_w2[li], cmp_pe[li])
            kv_p.append(rows.astype(dt))
            win_p.append(win[:, T - WP:].astype(dt))
            q, g, rows, win = nsa_project(hn_s, pos_s, nsa_qnorm[li], nsa_knorm[li])
            pool = cache_nsa_kv.reshape(-1, PAGE_SIZE, 4 * NSA_KV * HEAD_DIM)
            past = pool[li * cache_nsa_kv.shape[1] + page_table].reshape(DB, P, 4 * NSA_KV * HEAD_DIM)
            win_all = jnp.concatenate([cache_nsa_win[li], win.astype(cache_nsa_win.dtype)], axis=1)
            o_nsa_s = nsa_sample(q, g, past, rows.astype(past.dtype), win_all, cmp_w1[li], cmp_w2[li], cmp_pe[li])
            kv_s.append(rows.astype(dt))
            win_s.append(win_all[:, S:].astype(dt))
            rw_par = (rwkv_mu[li], rwkv_w0[li], rwkv_wB[li], rwkv_a0[li], rwkv_aB[li], rwkv_gB[li],
                      rwkv_kk[li], rwkv_ka[li], rwkv_rk[li], rwkv_ln_w[li], rwkv_ln_b[li])
            o_rw_p, st, last = rwkv_mix(hr_p, jnp.zeros((B, RWKV_IN), dt),
                                        jnp.zeros((B, RWKV_HEADS, HEAD_DIM, HEAD_DIM), dt), *rw_par)
            rw_p.append(st.astype(dt))
            sh_p.append(last.astype(dt))
            o_rw_s, st, last = rwkv_mix(hr_s, state_rwkv_shift[li], state_rwkv[li], *rw_par)
            rw_s.append(st.astype(dt))
            sh_s.append(last.astype(dt))
            w_outs = [w_out[:NSA_Q], w_out[NSA_Q:]]
            y_p = proj_residual([o_nsa_p.astype(dt), o_rw_p.astype(dt)], w_outs, y_p)
            y_s = proj_residual([o_nsa_s.astype(dt), o_rw_s.astype(dt)], w_outs, y_s)
        else:
            wo = w_in_odd[li]
            o2 = 2 * GLA_KW + GLA_VW
            w_in = jnp.concatenate([wo[:, :o2], wo[:, o2 + GLA_RANK:],
                                    jnp.pad(wo[:, o2:o2 + GLA_RANK], ((0, 0), (0, LANE - GLA_RANK)))],
                                   axis=1).astype(bf16)
            odd_splits = ((0, 2 * GLA_KW), (2 * GLA_KW, o2), (o2, o2 + GLA_VW), (o2 + GLA_VW, o2 + GLA_VW + LANE))
            w_out = w_out_odd[li].astype(bf16)
            qk_p, v_p, r_p, gd_p = norm_proj(y_p, norm_mix[layer], w_in, odd_splits)
            qk_s, v_s, r_s, gd_s = norm_proj(y_s, norm_mix[layer], w_in, odd_splits)
            o_p, st = gla_mix(qk_p, v_p, gd_p, r_p, jnp.zeros((B, GLA_HEADS, GLA_DK, GLA_DV), dt),
                              gla_gate_up[li], gla_gate_b[li], gla_norm[li])
            gl_p.append(st.astype(dt))
            o_s, st = gla_mix(qk_s, v_s, gd_s, r_s, state_gla[li], gla_gate_up[li], gla_gate_b[li], gla_norm[li])
            gl_s.append(st.astype(dt))
            y_p = proj_residual([o_p.astype(dt)], [w_out], y_p)
            y_s = proj_residual([o_s.astype(dt)], [w_out], y_s)
        wg, wu, wd = ffn_gate[layer].astype(bf16), ffn_up[layer].astype(bf16), ffn_down[layer].astype(bf16)
        y_p = ffn(y_p, norm_ffn[layer], wg, wu, wd)
        y_s = ffn(y_s, norm_ffn[layer], wg, wu, wd)
    return (y_p, y_s, jnp.stack(kv_p), jnp.stack(kv_s), jnp.stack(win_p), jnp.stack(win_s),
            jnp.stack(rw_p), jnp.stack(rw_s), jnp.stack(sh_p), jnp.stack(sh_s),
            jnp.stack(gl_p), jnp.stack(gl_s))
```

```python
import functools

import jax
import jax.numpy as jnp
from jax import lax
from jax.experimental import pallas as pl
from jax.experimental.pallas import tpu as pltpu

D_MODEL = 1024
PAGE_SIZE = 128
HEAD_DIM = 64
NSA_HEADS = 8
NSA_KV = 2
NSA_REP = NSA_HEADS // NSA_KV
CMP_STRIDE = 16
CMP_BLOCK = 2 * CMP_STRIDE
CMP_HIDDEN = 2 * HEAD_DIM
SEL_BLOCK = 64
N_SELECT = 16
WINDOW = 512
ROPE_DIM = HEAD_DIM // 4
ROPE_THETA = 500000.0
RWKV_HEADS = 8
RWKV_WIDTH = RWKV_HEADS * HEAD_DIM
LORA_W = 64
LORA_A = 64
LORA_G = 128
RWKV_GN_EPS = 64e-5
GLA_HEADS = 4
GLA_DK = D_MODEL // 2 // GLA_HEADS
GLA_DV = D_MODEL // GLA_HEADS
GLA_RANK = 16
GLA_TAU = 16.0
GLA_CHUNK = 64
D_FF = ((8 * D_MODEL // 3 + 255) // 256) * 256
NSA_Q = NSA_HEADS * HEAD_DIM
NSA_KVW = 3 * 2 * NSA_KV * HEAD_DIM
NSA_IN = NSA_Q + NSA_KVW + 3 * NSA_HEADS
RWKV_IN = 3 * RWKV_WIDTH + LORA_W + LORA_A + LORA_G
EVEN_IN = NSA_IN + RWKV_IN
MIX_WIDTH = NSA_Q + RWKV_WIDTH
GLA_KW = GLA_HEADS * GLA_DK
GLA_VW = GLA_HEADS * GLA_DV
ODD_IN = 2 * GLA_KW + GLA_VW + GLA_RANK + GLA_VW

LANE = 128
VMEM_LIMIT = 48 * 1024 * 1024


ROW_TILE = 512
NORM_EPS = 1e-6


def _row_tile(m):
    return ROW_TILE if m % ROW_TILE == 0 else m


def _norm_bf16(x_ref, g_ref):
    x = x_ref[...]
    y = x * lax.rsqrt(jnp.mean(x * x, axis=-1, keepdims=True) + NORM_EPS)
    return (y * g_ref[...]).astype(jnp.bfloat16)


def _norm_proj_body(x_ref, g_ref, w_ref, *o_refs, splits):
    xn = _norm_bf16(x_ref, g_ref)
    for (a, b), o_ref in zip(splits, o_refs):
        o_ref[...] = jnp.dot(xn, w_ref[:, a:b], preferred_element_type=jnp.float32)


def norm_proj(x, gain, w, splits):
    bsz, t, k = x.shape
    m = bsz * t
    tm = _row_tile(m)
    outs = pl.pallas_call(
        functools.partial(_norm_proj_body, splits=splits),
        grid=(m // tm,),
        in_specs=[pl.BlockSpec((tm, k), lambda i: (i, 0)),
                  pl.BlockSpec((1, k), lambda i: (0, 0)),
                  pl.BlockSpec(w.shape, lambda i: (0, 0))],
        out_specs=[pl.BlockSpec((tm, b - a), lambda i: (i, 0)) for a, b in splits],
        out_shape=[jax.ShapeDtypeStruct((m, b - a), jnp.float32) for a, b in splits],
        compiler_params=pltpu.CompilerParams(dimension_semantics=("parallel",), vmem_limit_bytes=VMEM_LIMIT),
        name="norm_proj",
    )(x.reshape(m, k), gain.reshape(1, k), w)
    return [o.reshape(bsz, t, -1) for o in outs]


def _proj_res_body(*refs, n_in):
    x_refs, w_refs, res_ref, o_ref = refs[:n_in], refs[n_in:2 * n_in], refs[2 * n_in], refs[2 * n_in + 1]
    acc = res_ref[...]
    for x_ref, w_ref in zip(x_refs, w_refs):
        acc = acc + jnp.dot(x_ref[...].astype(jnp.bfloat16), w_ref[...], preferred_element_type=jnp.float32)
    o_ref[...] = acc


def proj_residual(xs, ws, res):
    bsz, t, n = res.shape
    m = bsz * t
    tm = _row_tile(m)
    out = pl.pallas_call(
        functools.partial(_proj_res_body, n_in=len(xs)),
        grid=(m // tm,),
        in_specs=([pl.BlockSpec((tm, x.shape[-1]), lambda i: (i, 0)) for x in xs]
                  + [pl.BlockSpec(w.shape, lambda i: (0, 0)) for w in ws]
                  + [pl.BlockSpec((tm, n), lambda i: (i, 0))]),
        out_specs=pl.BlockSpec((tm, n), lambda i: (i, 0)),
        out_shape=jax.ShapeDtypeStruct((m, n), jnp.float32),
        compiler_params=pltpu.CompilerParams(dimension_semantics=("parallel",), vmem_limit_bytes=VMEM_LIMIT),
        name="proj_residual",
    )(*[x.reshape(m, x.shape[-1]) for x in xs], *ws, res.reshape(m, n))
    return out.reshape(bsz, t, n)


FFN_COL_CHUNKS = 2


def _ffn_up_body(x_ref, g_ref, wg_ref, wu_ref, h_ref):
    xn = _norm_bf16(x_ref, g_ref)
    cw = h_ref.shape[1] // FFN_COL_CHUNKS
    for c in range(FFN_COL_CHUNKS):
        cols = slice(c * cw, (c + 1) * cw)
        g = jnp.dot(xn, wg_ref[:, cols], preferred_element_type=jnp.float32)
        u = jnp.dot(xn, wu_ref[:, cols], preferred_element_type=jnp.float32)
        h_ref[:, cols] = (g * jax.nn.sigmoid(g) * u).astype(h_ref.dtype)


def ffn(y, gain, wg, wu, wd):
    bsz, t, k = y.shape
    m = bsz * t
    tm = _row_tile(m)
    f = wg.shape[1]
    h = pl.pallas_call(
        _ffn_up_body,
        grid=(m // tm,),
        in_specs=[pl.BlockSpec((tm, k), lambda i: (i, 0)),
                  pl.BlockSpec((1, k), lambda i: (0, 0)),
                  pl.BlockSpec(wg.shape, lambda i: (0, 0)),
                  pl.BlockSpec(wu.shape, lambda i: (0, 0))],
        out_specs=pl.BlockSpec((tm, f), lambda i: (i, 0)),
        out_shape=jax.ShapeDtypeStruct((m, f), jnp.bfloat16),
        compiler_params=pltpu.CompilerParams(dimension_semantics=("parallel",), vmem_limit_bytes=VMEM_LIMIT),
        name="ffn_up",
    )(y.reshape(m, k), gain.reshape(1, k), wg, wu)
    return proj_residual([h.reshape(bsz, t, f)], [wd], y)


def rmsnorm(x, g, eps=1e-6):
    xf = x.astype(jnp.float32)
    y = xf * lax.rsqrt(jnp.mean(xf * xf, axis=-1, keepdims=True) + eps)
    return (y * g.astype(jnp.float32)).astype(x.dtype)


def rope(x, pos):
    half = ROPE_DIM // 2
    inv = ROPE_THETA ** (-jnp.arange(half, dtype=jnp.float32) / half)
    ang = pos.astype(jnp.float32)[:, None] * inv[None, :]
    shp = (pos.shape[0],) + (1,) * (x.ndim - 3) + (half,)
    cos, sin = jnp.cos(ang).reshape(shp), jnp.sin(ang).reshape(shp)
    xr = x[..., :ROPE_DIM].astype(jnp.float32)
    x1, x2 = xr[..., :half], xr[..., half:]
    rot = jnp.concatenate([x1 * cos - x2 * sin, x2 * cos + x1 * sin], axis=-1)
    return jnp.concatenate([rot.astype(x.dtype), x[..., ROPE_DIM:]], axis=-1)


def nsa_project(h, pos, qn, kn):
    B, T = h.shape[:2]
    q = h[..., :NSA_Q].reshape(B, T, NSA_HEADS, HEAD_DIM)
    kv = h[..., NSA_Q:NSA_Q + NSA_KVW].reshape(B, T, 3, 2, NSA_KV, HEAD_DIM)
    gates = jax.nn.sigmoid(h[..., NSA_Q + NSA_KVW:NSA_IN].astype(jnp.float32)).reshape(B, T, NSA_HEADS, 3)
    q = rope(rmsnorm(q, qn), pos)
    k = rope(rmsnorm(kv[:, :, :, 0], kn[:, None, :]), pos)
    kv = jnp.stack([k, kv[:, :, :, 1]], axis=3)
    rows = kv[:, :, :2].reshape(B, T, 4, NSA_KV, HEAD_DIM)
    win = kv[:, :, 2]
    return q, gates, rows, win


NEG = -1e30


NSA_TQ = 256
NSA_TK = 512
NSA_TQ_DECODE = 32


def _nsa_attn_body(qt_ref, kc_ref, vct_ref, map_ref, ks_ref, vst_ref, kw_ref, vw_ref, g_ref, o_ref, thr_scr,
                   *, tq, n_cmp, n_sel, q_tile0, win_base, natural):
    f32, bf16 = jnp.float32, jnp.bfloat16
    tk = NSA_TK
    nl = NSA_REP * tq
    i = pl.program_id(1) + q_tile0
    if natural:
        xq = qt_ref[0] * (HEAD_DIM ** -0.5)
        qt = jnp.concatenate([xq[:, HEAD_DIM * r:HEAD_DIM * (r + 1)].T for r in range(NSA_REP)], axis=1).astype(bf16)
    else:
        qt = qt_ref[0, 0]
    qpos = i * tq + lax.broadcasted_iota(jnp.int32, (1, nl), 1) % tq

    ncp = kc_ref.shape[1]
    n = lax.broadcasted_iota(jnp.int32, (ncp, 1), 0)
    ok = (n * CMP_STRIDE + (CMP_BLOCK - 1) <= qpos) & (n < n_cmp)
    s = jnp.dot(kc_ref[0], qt, preferred_element_type=f32)
    s = jnp.where(ok, s, NEG)
    m = jnp.max(s, axis=0, keepdims=True)
    e = jnp.where(ok, jnp.exp(s - m), 0.0)
    l = jnp.sum(e, axis=0, keepdims=True)
    p = e * jnp.where(l > 0.0, 1.0 / l, 0.0)
    o_cmp = jnp.dot(vct_ref[0], p.astype(bf16), preferred_element_type=f32)

    psum = p[:, 0:tq]
    for r in range(1, NSA_REP):
        psum = psum + p[:, r * tq:(r + 1) * tq]
    hi = psum.astype(bf16)
    lo = (psum - hi.astype(f32)).astype(bf16)
    imp = (jnp.dot(map_ref[...], hi, preferred_element_type=f32)
           + jnp.dot(map_ref[...], lo, preferred_element_type=f32))
    nsp = map_ref.shape[0]
    qp = qpos[:, 0:tq]
    blk = lax.broadcasted_iota(jnp.int32, (nsp, 1), 0)
    cur = qp // SEL_BLOCK
    valid = (blk * SEL_BLOCK <= qp) & (blk < n_sel)
    forced = (blk == 0) | (blk == cur) | (blk == cur - 1)
    x = jnp.where(valid, jnp.where(forced, 1e30, imp), NEG)
    x = jnp.where(blk < n_sel, x, -3e38)
    blkf = blk.astype(f32)
    sel = jnp.zeros((nsp, tq), f32)
    for _ in range(min(N_SELECT, n_sel)):
        mx = jnp.max(x, axis=0, keepdims=True)
        first = jnp.min(jnp.where(x == mx, blkf, float(nsp)), axis=0, keepdims=True)
        hit = blkf == first
        sel = jnp.where(hit, 1.0, sel)
        x = jnp.where(hit, -3e38, x)
    thr = jnp.where(valid & (sel > 0.5), qp, -1)
    thr_scr[...] = jnp.concatenate([thr] * NSA_REP, axis=1)

    def flash(s, vt, carry):
        m_old, l_old, acc = carry
        m_new = jnp.maximum(m_old, jnp.max(s, axis=0, keepdims=True))
        alpha = jnp.exp(m_old - m_new)
        p = jnp.exp(s - m_new)
        l_new = alpha * l_old + jnp.sum(p, axis=0, keepdims=True)
        acc = alpha * acc + jnp.dot(vt, p.astype(bf16), preferred_element_type=f32)
        return m_new, l_new, acc

    init = (jnp.full((1, nl), NEG, f32), jnp.zeros((1, nl), f32), jnp.zeros((HEAD_DIM, nl), f32))

    krow = lax.broadcasted_iota(jnp.int32, (SEL_BLOCK, 1), 0)

    def masked_scores(j):
        s = jnp.dot(ks_ref[0, j], qt, preferred_element_type=f32)
        th = thr_scr[pl.ds(pl.multiple_of(j * (tk // SEL_BLOCK), 8), tk // SEL_BLOCK), :]
        return jnp.concatenate(
            [jnp.where((j * tk + b * SEL_BLOCK + krow) <= th[b:b + 1], s[b * SEL_BLOCK:(b + 1) * SEL_BLOCK], NEG)
             for b in range(tk // SEL_BLOCK)], axis=0)

    def sel_pair(j2, carry):
        m_old, l_old, acc = carry
        s0, s1 = masked_scores(2 * j2), masked_scores(2 * j2 + 1)
        m_new = jnp.maximum(m_old, jnp.maximum(jnp.max(s0, axis=0, keepdims=True), jnp.max(s1, axis=0, keepdims=True)))
        alpha = jnp.exp(m_old - m_new)
        p0, p1 = jnp.exp(s0 - m_new), jnp.exp(s1 - m_new)
        l_new = alpha * l_old + jnp.sum(p0, axis=0, keepdims=True) + jnp.sum(p1, axis=0, keepdims=True)
        acc = (alpha * acc + jnp.dot(vst_ref[0, 2 * j2], p0.astype(bf16), preferred_element_type=f32)
               + jnp.dot(vst_ref[0, 2 * j2 + 1], p1.astype(bf16), preferred_element_type=f32))
        return m_new, l_new, acc

    n_tiles = (i * tq + tq - 1) // tk + 1
    carry = lax.fori_loop(0, n_tiles // 2, sel_pair, init)
    carry = lax.cond(n_tiles % 2 == 1,
                     lambda c: flash(masked_scores(n_tiles - 1), vst_ref[0, n_tiles - 1], c),
                     lambda c: c, carry)
    _, l_s, acc_s = carry
    o_slc = acc_s / l_s

    nkw = WINDOW + tq
    k_first = i * tq - WINDOW
    row0 = pl.multiple_of(k_first - win_base, 16)
    kpos = k_first + lax.broadcasted_iota(jnp.int32, (nkw, 1), 0)
    allowed = (kpos <= qpos) & (kpos > qpos - WINDOW) & (kpos >= 0)
    s = jnp.dot(kw_ref[0, pl.ds(row0, nkw), :], qt, preferred_element_type=f32)
    s = jnp.where(allowed, s, NEG)
    e = jnp.where(allowed, jnp.exp(s - jnp.max(s, axis=0, keepdims=True)), 0.0)
    o_win = (lax.dot_general(vw_ref[0, pl.ds(row0, nkw), :], e.astype(bf16), (((0,), (0,)), ((), ())),
                             preferred_element_type=f32) / jnp.sum(e, axis=0, keepdims=True))

    g = g_ref[0, 0]
    res = g[0:1] * o_cmp + g[1:2] * o_slc + g[2:3] * o_win
    if natural:
        o_ref[0] = jnp.concatenate([res[:, tq * r:tq * (r + 1)].T for r in range(NSA_REP)], axis=1)
    else:
        o_ref[0, 0] = res


def nsa_attention(q, gates, kc, vc, ksl, vsl, kwin, vwin, *, n_keys, tq=NSA_TQ, q_pos0=0, win_pos0=0):
    f32, bf16 = jnp.float32, jnp.bfloat16
    B, T = q.shape[:2]
    G, R, D = NSA_KV, NSA_REP, HEAD_DIM
    tk = NSA_TK
    nq, nl = T // tq, R * tq
    n_cmp, n_sel = kc.shape[1], -(-n_keys // SEL_BLOCK)
    lk = ksl.shape[1]
    assert T % tq == 0 and nl % LANE == 0 and lk % tk == 0 and q_pos0 % tq == 0 and tq % 16 == 0
    assert lk >= q_pos0 + T and win_pos0 <= max(q_pos0 - WINDOW, 0)
    win_base = win_pos0 - WINDOW
    lw = q_pos0 + T - win_base
    wpad = lambda x: jnp.pad(x[:, :lw - WINDOW], ((0, 0), (WINDOW, max(lw - WINDOW - x.shape[1], 0)), (0, 0), (0, 0)))
    kwin, vwin = wpad(kwin), wpad(vwin)
    ncp = -(-n_cmp // LANE) * LANE
    nsp = -(-n_sel // LANE) * LANE
    n = jnp.arange(ncp)[None, :]
    jb = jnp.arange(nsp)[:, None]
    map_t = ((n * CMP_STRIDE <= jb * SEL_BLOCK + SEL_BLOCK - 1) & (n * CMP_STRIDE + CMP_BLOCK - 1 >= jb * SEL_BLOCK)
             & (n < n_cmp) & (jb < n_sel)).astype(bf16)
    natural = tq % LANE == 0
    if natural:
        qt = q.reshape(B, T, NSA_Q)
        q_spec = pl.BlockSpec((1, tq, R * D), lambda b, i: (b // G, i, b % G))
        o_spec, o_shape = q_spec, jax.ShapeDtypeStruct((B, T, NSA_Q), f32)
    else:
        qt = (q * (D ** -0.5)).reshape(B, nq, tq, G, R, D).transpose(0, 3, 1, 5, 4, 2)
        qt = qt.reshape(B * G, nq, D, nl).astype(bf16)
        q_spec = pl.BlockSpec((1, 1, D, nl), lambda b, i: (b, i, 0, 0))
        o_spec, o_shape = q_spec, jax.ShapeDtypeStruct((B * G, nq, D, nl), f32)
    gt = gates.reshape(B, nq, tq, G, R, 3).transpose(0, 3, 1, 5, 4, 2).reshape(B * G, nq, 3, nl)
    gt = jnp.pad(gt, ((0, 0), (0, 0), (0, 5), (0, 0)))
    rows = lambda x, t: x.transpose(0, 2, 1, 3).reshape(B * G, -1, t, D).astype(bf16)
    cols = lambda x, t: x.reshape(B, -1, t, G, D).transpose(0, 3, 1, 4, 2).reshape(B * G, -1, D, t).astype(bf16)
    kcp = jnp.pad(kc, ((0, 0), (0, ncp - n_cmp), (0, 0), (0, 0)))
    vcp = jnp.pad(vc, ((0, 0), (0, ncp - n_cmp), (0, 0), (0, 0)))
    kc_r = rows(kcp, ncp)[:, 0]
    vc_c = cols(vcp, ncp)[:, 0]
    full = lambda shape: pl.BlockSpec((1,) + shape, lambda b, i: (b,) + (0,) * len(shape))
    per_q = lambda shape: pl.BlockSpec((1, 1) + shape, lambda b, i: (b, i) + (0,) * len(shape))
    out = pl.pallas_call(
        functools.partial(_nsa_attn_body, tq=tq, n_cmp=n_cmp, n_sel=n_sel,
                          q_tile0=q_pos0 // tq, win_base=win_base, natural=natural),
        grid=(B * G, nq),
        in_specs=[q_spec, full((ncp, D)), full((D, ncp)),
                  pl.BlockSpec((nsp, ncp), lambda b, i: (0, 0)),
                  full((lk // tk, tk, D)), full((lk // tk, D, tk)),
                  full((lw, D)), full((lw, D)),
                  per_q((8, nl))],
        out_specs=o_spec,
        out_shape=o_shape,
        scratch_shapes=[pltpu.VMEM((nsp, nl), jnp.int32)],
        compiler_params=pltpu.CompilerParams(
            dimension_semantics=("parallel", "parallel"), vmem_limit_bytes=VMEM_LIMIT),
        name="nsa_attention",
    )(qt, kc_r, vc_c, map_t, rows(ksl, tk), cols(vsl, tk), rows(kwin, lw)[:, 0], rows(vwin, lw)[:, 0], gt)
    if natural:
        return out
    return out.reshape(B, G, nq, D, R, tq).transpose(0, 2, 5, 1, 4, 3).reshape(B, T, NSA_Q)


def nsa_prompt(q, gates, rows, win, cw1, cw2, cpe):
    B, T = q.shape[:2]
    kc = compress_pair(rows[:, :, 0].reshape(B, T, -1), cw1[0], cw2[0], cpe[0])
    vc = compress_pair(rows[:, :, 1].reshape(B, T, -1), cw1[1], cw2[1], cpe[1])
    return nsa_attention(q, gates, kc, vc, rows[:, :, 2], rows[:, :, 3], win[:, :, 0], win[:, :, 1],
                         n_keys=T)


def compress_pair(x, w1, w2, pe):
    B, L, gd = x.shape
    n_chunks = L // CMP_STRIDE
    c = x[:, :n_chunks * CMP_STRIDE].reshape(B, n_chunks, CMP_STRIDE, gd)
    eye = jnp.eye(NSA_KV, dtype=w1.dtype)
    big = lambda w: jnp.einsum('ldh,gk->lgdkh', w, eye).reshape(CMP_STRIDE, gd, NSA_KV * CMP_HIDDEN)
    a = jnp.einsum('bnlx,lxy->bny', c, big(w1[:CMP_STRIDE]))
    bc = jnp.einsum('bnlx,lxy->bny', c, big(w1[CMP_STRIDE:]))
    h = a[:, :-1] + bc[:, 1:] + jnp.tile(jnp.einsum('ld,ldh->h', pe, w1), NSA_KV)
    w2_big = jnp.einsum('hd,gk->ghkd', w2, eye).reshape(NSA_KV * CMP_HIDDEN, gd)
    return (jax.nn.gelu(h) @ w2_big).reshape(B, n_chunks - 1, NSA_KV, HEAD_DIM)


def nsa_sample(q, gates, past, rows, win_all, cw1, cw2, cpe):
    DB, S = q.shape[:2]
    P, WB = past.shape[1], win_all.shape[1] - S
    gd = NSA_KV * HEAD_DIM
    kind = lambda w: past[:, :, w * gd:(w + 1) * gd]
    new = lambda w: rows[:, :, w].reshape(DB, S, gd)
    up = lambda n, m: -(-n // m) * m
    pad_rows = lambda x, n: jnp.pad(x, ((0, 0), (0, n - x.shape[1])) + ((0, 0),) * (x.ndim - 2))
    n_chunk_rows = (P + S) // CMP_STRIDE * CMP_STRIDE
    cmp_in = [kind(w)[:, :n_chunk_rows] if n_chunk_rows <= P
              else jnp.concatenate([kind(w), new(w)[:, :n_chunk_rows - P]], axis=1) for w in (0, 1)]
    kc = compress_pair(cmp_in[0], cw1[0], cw2[0], cpe[0])
    vc = compress_pair(cmp_in[1], cw1[1], cw2[1], cpe[1])
    tq = up(S, NSA_TQ_DECODE)
    lk = up(P + tq, NSA_TK)
    groups = lambda x: x.reshape(DB, -1, NSA_KV, HEAD_DIM)
    ksl = groups(pad_rows(jnp.concatenate([kind(2), new(2)], axis=1), lk))
    vsl = groups(pad_rows(jnp.concatenate([kind(3), new(3)], axis=1), lk))
    o = nsa_attention(pad_rows(q, tq), pad_rows(gates, tq), kc, vc, ksl, vsl, win_all[:, :, 0], win_all[:, :, 1],
                      n_keys=P + S, tq=tq, q_pos0=P, win_pos0=P - WB)
    return o[:, :S]


RW_PAIRS = RWKV_HEADS // 2
RW_CHUNK = 64


def _rwkv_scan_body(r_ref, w_ref, k_ref, kk_ref, b_ref, c1_ref, c2_ref, vt_ref, s0_ref, oh_ref,
                    y_ref, st_ref, vhi_scr, yacc_scr, *, nb, tc, natural):
    f32, bf16 = jnp.float32, jnp.bfloat16
    c = pl.program_id(1)
    eye = (lax.broadcasted_iota(jnp.int32, (HEAD_DIM, HEAD_DIM), 0)
           == lax.broadcasted_iota(jnp.int32, (HEAD_DIM, HEAD_DIM), 1)).astype(bf16)

    def halves(x, terms):
        out, rest = None, x
        for _ in range(terms):
            piece = rest.astype(bf16)
            rest = rest - piece.astype(f32)
            t = jnp.concatenate([lax.dot_general(eye, piece[:, HEAD_DIM * h:HEAD_DIM * (h + 1)],
                                                 (((1,), (1,)), ((), ())), preferred_element_type=f32)
                                 for h in range(2)], axis=1)
            out = t if out is None else out + t
        return out

    @pl.when(c == 0)
    def _():
        st_ref[...] = s0_ref[...]

    row = lax.broadcasted_iota(jnp.int32, (LANE, LANE), 0)
    col = lax.broadcasted_iota(jnp.int32, (LANE, LANE), 1)
    ones1 = ((row // HEAD_DIM) == (col // HEAD_DIM)).astype(bf16)
    lane_t = lax.broadcasted_iota(jnp.int32, (HEAD_DIM, LANE), 1) % HEAD_DIM

    for b in range(nb):
        for p in range(RW_PAIRS):
            vt = halves(vt_ref[b, :, LANE * p:LANE * (p + 1)], 1) if natural else vt_ref[b, 0, p]
            vhi_scr[b * RW_PAIRS + p] = vt.astype(bf16)
    yacc_scr[...] = jnp.zeros_like(yacc_scr)

    pairs = [(b, p) for b in range(nb) for p in range(RW_PAIRS)]
    n_pairs = len(pairs)

    def group(t8, carry):
        t0 = pl.multiple_of(t8 * 8, 8)
        for j in range(0, 8, 2):
            ra = lambda ref, b, sl: ref[b, pl.ds(t0, 8), sl][j:j + 1]
            rb = lambda ref, b, sl: ref[b, pl.ds(t0, 8), sl][j + 1:j + 2]
            oh_a, oh_b = oh_ref[t0 + j], oh_ref[t0 + j + 1]
            vl = jnp.concatenate([vhi_scr[i] * oh for oh in (oh_a, oh_b) for i in range(n_pairs)], axis=0)
            vb_all = jnp.dot(vl, ones1, preferred_element_type=f32)
            pieces = []
            for b, p in pairs:
                sl = slice(LANE * p, LANE * (p + 1))
                s = st_ref[b, p]
                pieces.append((s * ra(kk_ref, b, sl)).astype(bf16))
                pieces.append((s * (ra(w_ref, b, sl) * rb(kk_ref, b, sl))).astype(bf16))
            u_all = jnp.dot(jnp.concatenate(pieces, axis=0), ones1, preferred_element_type=f32)
            py = []
            for i, (b, p) in enumerate(pairs):
                sl = slice(LANE * p, LANE * (p + 1))
                u1 = u_all[2 * HEAD_DIM * i:2 * HEAD_DIM * i + HEAD_DIM]
                u2 = u_all[2 * HEAD_DIM * i + HEAD_DIM:2 * HEAD_DIM * (i + 1)]
                vb_a = vb_all[HEAD_DIM * i:HEAD_DIM * (i + 1)]
                vb_b = vb_all[HEAD_DIM * (n_pairs + i):HEAD_DIM * (n_pairs + i + 1)]
                sa_b = u2 - u1 * ra(c1_ref, b, sl) + vb_a * ra(c2_ref, b, sl)
                s_a = st_ref[b, p] * ra(w_ref, b, sl) - u1 * ra(b_ref, b, sl) + vb_a * ra(k_ref, b, sl)
                s_b = s_a * rb(w_ref, b, sl) - sa_b * rb(b_ref, b, sl) + vb_b * rb(k_ref, b, sl)
                st_ref[b, p] = s_b
                py.append((s_a * ra(r_ref, b, sl)).astype(bf16))
                py.append((s_b * rb(r_ref, b, sl)).astype(bf16))
            y_all = jnp.dot(jnp.concatenate(py, axis=0), ones1, preferred_element_type=f32)
            m_a, m_b = lane_t == t0 + j, lane_t == t0 + j + 1
            for i, (b, p) in enumerate(pairs):
                y_a = y_all[2 * HEAD_DIM * i:2 * HEAD_DIM * i + HEAD_DIM]
                y_b = y_all[2 * HEAD_DIM * i + HEAD_DIM:2 * HEAD_DIM * (i + 1)]
                yacc_scr[i] = jnp.where(m_a, y_a, jnp.where(m_b, y_b, yacc_scr[i]))
        return carry

    lax.fori_loop(0, tc // 8, group, 0)
    for i, (b, p) in enumerate(pairs):
        if natural:
            y_ref[b, :, LANE * p:LANE * (p + 1)] = halves(yacc_scr[i], 3)
        else:
            y_ref[b, 0, p] = yacc_scr[i]


def rwkv_scan(r, w, k, v, kk, b, S0):
    f32 = jnp.float32
    B, T, W = r.shape
    tc = RW_CHUNK if T % RW_CHUNK == 0 else T
    assert tc <= RW_CHUNK and T % tc == 0 and tc % 8 == 0 and B % 2 == 0
    nc, nb = T // tc, 2
    natural = tc == RW_CHUNK
    tok = pl.BlockSpec((nb, tc, W), lambda i, c: (i, c, 0))
    chk = pl.BlockSpec((nb, 1, RW_PAIRS, HEAD_DIM, LANE), lambda i, c: (i, c, 0, 0, 0))
    if natural:
        vt, vy_spec, y_shape = v, tok, jax.ShapeDtypeStruct((B, T, W), f32)
    else:
        vt = v.reshape(B, nc, tc, RW_PAIRS, 2, HEAD_DIM).transpose(0, 1, 3, 5, 4, 2)
        vt = jnp.pad(vt, ((0, 0),) * 5 + ((0, RW_CHUNK - tc),)).reshape(B, nc, RW_PAIRS, HEAD_DIM, LANE)
        vy_spec, y_shape = chk, jax.ShapeDtypeStruct((B, nc, RW_PAIRS, HEAD_DIM, LANE), f32)
    s0 = S0.astype(f32).reshape(B, RW_PAIRS, 2, HEAD_DIM, HEAD_DIM).transpose(0, 1, 3, 2, 4)
    s0 = s0.reshape(B, RW_PAIRS, HEAD_DIM, LANE)
    oh = jnp.arange(LANE)[None, None, :] % HEAD_DIM == jnp.arange(RW_CHUNK)[:, None, None]
    oh = jnp.broadcast_to(oh, (RW_CHUNK, HEAD_DIM, LANE)).astype(jnp.bfloat16)
    kk_next = jnp.concatenate([kk[:, 1:], kk[:, :1]], axis=1)

    def head_dot(x):
        d = jnp.sum((x * kk_next).reshape(B, T, RWKV_HEADS, HEAD_DIM), axis=-1, keepdims=True)
        return jnp.broadcast_to(d, (B, T, RWKV_HEADS, HEAD_DIM)).reshape(B, T, W)

    c1, c2 = head_dot(b), head_dot(k)
    stt = pl.BlockSpec((nb, RW_PAIRS, HEAD_DIM, LANE), lambda i, c: (i, 0, 0, 0))
    y, st = pl.pallas_call(
        functools.partial(_rwkv_scan_body, nb=nb, tc=tc, natural=natural),
        grid=(B // nb, nc),
        in_specs=[tok, tok, tok, tok, tok, tok, tok, vy_spec, stt,
                  pl.BlockSpec((RW_CHUNK, HEAD_DIM, LANE), lambda i, c: (0, 0, 0))],
        out_specs=[vy_spec, stt],
        out_shape=[y_shape, jax.ShapeDtypeStruct((B, RW_PAIRS, HEAD_DIM, LANE), f32)],
        scratch_shapes=[pltpu.VMEM((nb * RW_PAIRS, HEAD_DIM, LANE), jnp.bfloat16),
                        pltpu.VMEM((nb * RW_PAIRS, HEAD_DIM, LANE), f32)],
        compiler_params=pltpu.CompilerParams(
            dimension_semantics=("parallel", "arbitrary"), vmem_limit_bytes=VMEM_LIMIT),
        name="rwkv_scan",
    )(r, w, k, kk, b, c1, c2, vt, s0, oh)
    if not natural:
        y = y.reshape(B, nc, RW_PAIRS, HEAD_DIM, 2, RW_CHUNK)[..., :tc]
        y = y.transpose(0, 1, 5, 2, 4, 3).reshape(B, T, W)
    st = st.reshape(B, RW_PAIRS, HEAD_DIM, 2, HEAD_DIM).transpose(0, 1, 3, 2, 4)
    return y, st.reshape(B, RWKV_HEADS, HEAD_DIM, HEAD_DIM)


def rwkv_mix(p, prev, S0, mu, w0, wB, a0, aB, gB, k_k, k_a, r_k, ln_w, ln_b):
    f32 = jnp.float32
    B, T = p.shape[:2]
    W = RWKV_WIDTH
    p = p.astype(f32)
    p_prev = jnp.concatenate([prev.astype(f32)[:, None], p[:, :-1]], axis=1)
    ps = p + mu * (p_prev - p)
    r, k, v = ps[..., :W], ps[..., W:2 * W], ps[..., 2 * W:3 * W]
    o = 3 * W
    xw, xa, xg = ps[..., o:o + LORA_W], ps[..., o + LORA_W:o + LORA_W + LORA_A], ps[..., o + LORA_W + LORA_A:]
    z = w0 + jnp.tanh(xw) @ wB
    w = jnp.exp(-jnp.exp(-jax.nn.softplus(-z) - 0.5))
    a = jax.nn.sigmoid(a0 + xa @ aB)
    g = jax.nn.sigmoid(xg) @ gB
    heads = lambda t: t.reshape(B, T, RWKV_HEADS, HEAD_DIM)
    kk = heads(k * k_k)
    kk = kk * lax.rsqrt(jnp.maximum(jnp.sum(kk * kk, axis=-1, keepdims=True), 1e-24))
    k = k * (1.0 + (a - 1.0) * k_a)
    kk = kk.reshape(B, T, W)
    y, S = rwkv_scan(r, w, k, v, kk, kk * a, S0)
    y, k = heads(y), heads(k)
    r, v = heads(r), heads(v)
    mean = jnp.mean(y, axis=-1, keepdims=True)
    var = jnp.mean(jnp.square(y - mean), axis=-1, keepdims=True)
    y = ((y - mean) * lax.rsqrt(var + RWKV_GN_EPS)).reshape(B, T, W) * ln_w + ln_b
    y = y + (jnp.sum(r * k * r_k, axis=-1, keepdims=True) * v).reshape(B, T, W)
    return y * g, S, p[:, -1]


GLA_SUB = 16


def _gla_body(q_ref, k_ref, v_ref, g_ref, r_ref, gn_ref, s0_ref, o_ref, st_ref):
    f32, bf16 = jnp.float32, jnp.bfloat16
    C, SUB = q_ref.shape[1], GLA_SUB
    nsub = C // SUB
    c = pl.program_id(1)

    @pl.when(c == 0)
    def _():
        st_ref[...] = s0_ref[...]

    ti = lax.broadcasted_iota(jnp.int32, (C, C), 0)
    si = lax.broadcasted_iota(jnp.int32, (C, C), 1)
    tri = (si <= ti).astype(bf16)
    sub_t = lax.broadcasted_iota(jnp.int32, (SUB, 1), 0)
    sub_l = lax.broadcasted_iota(jnp.int32, (SUB, SUB), 1)
    for h in range(GLA_HEADS):
        kq = slice(h * GLA_DK, (h + 1) * GLA_DK)
        vv = slice(h * GLA_DV, (h + 1) * GLA_DV)
        q = q_ref[0, :, kq] * (GLA_DK ** -0.5)
        k = k_ref[0, :, kq]
        v = v_ref[0, :, vv].astype(bf16)
        g = g_ref[0, :, kq]
        g1 = g.astype(bf16)
        g2 = (g - g1.astype(f32)).astype(bf16)
        g3 = (g - g1.astype(f32) - g2.astype(f32)).astype(bf16)
        b = (jnp.dot(tri, g1, preferred_element_type=f32) + jnp.dot(tri, g2, preferred_element_type=f32)
             + jnp.dot(tri, g3, preferred_element_type=f32))
        b_last = b[C - 1:C]
        a_rows = []
        for I in range(nsub):
            rows = slice(I * SUB, (I + 1) * SUB)
            beta = b[I * SUB - 1:I * SUB] if I > 0 else jnp.zeros((1, GLA_DK), f32)
            qb, kb, bb = q[rows], k[rows], b[rows]
            a_diag = jnp.zeros((SUB, SUB), f32)
            for s in range(SUB):
                d = jnp.where(sub_t >= s, bb - bb[s:s + 1], -jnp.inf)
                col = jnp.sum(qb * kb[s:s + 1] * jnp.exp(d), axis=1, keepdims=True)
                a_diag = jnp.where(sub_l == s, col, a_diag)
            blocks = []
            if I > 0:
                qe = (qb * jnp.exp(bb - beta)).astype(bf16)
                ke = (k[:I * SUB] * jnp.exp(beta - b[:I * SUB])).astype(bf16)
                blocks.append(lax.dot_general(qe, ke, (((1,), (1,)), ((), ())), preferred_element_type=f32))
            blocks.append(a_diag)
            if I < nsub - 1:
                blocks.append(jnp.zeros((SUB, C - (I + 1) * SUB), f32))
            a_rows.append(jnp.concatenate(blocks, axis=1))
        a = jnp.concatenate(a_rows, axis=0).astype(bf16)
        st = st_ref[0, h]
        qd = (q * jnp.exp(b)).astype(bf16)
        o = (jnp.dot(a, v, preferred_element_type=f32)
             + lax.dot_general(qd, st.astype(bf16), (((1,), (1,)), ((), ())), preferred_element_type=f32))
        kd = (k * jnp.exp(b_last - b)).astype(bf16)
        st_ref[0, h] = (st * jnp.exp(b_last)
                        + lax.dot_general(v, kd, (((0,), (0,)), ((), ())), preferred_element_type=f32))
        y = o * lax.rsqrt(jnp.mean(o * o, axis=1, keepdims=True) + 1e-6) * gn_ref[...]
        rr = r_ref[0, :, vv]
        o_ref[0, :, vv] = y * (rr * jax.nn.sigmoid(rr))


def gla_mix(qk, v, gd, r, S0, gate_up, gate_b, gn):
    f32 = jnp.float32
    B, T = qk.shape[:2]
    KW, VW = GLA_KW, GLA_VW
    logg = jax.nn.log_sigmoid(gd[..., :GLA_RANK] @ gate_up + gate_b) / GLA_TAU
    C = GLA_CHUNK if T % GLA_CHUNK == 0 else GLA_SUB
    tp = -(-T // C) * C
    if tp != T:
        pad = lambda x: jnp.pad(x, ((0, 0), (0, tp - T), (0, 0)))
        qk, v, r, logg = pad(qk), pad(v), pad(r), pad(logg)
    tokv = pl.BlockSpec((1, C, VW), lambda b, c: (b, c, 0))
    stt = pl.BlockSpec((1, GLA_HEADS, GLA_DV, GLA_DK), lambda b, c: (b, 0, 0, 0))
    out, st = pl.pallas_call(
        _gla_body,
        grid=(B, tp // C),
        in_specs=[pl.BlockSpec((1, C, KW), lambda b, c: (b, c, 0)),
                  pl.BlockSpec((1, C, KW), lambda b, c: (b, c, 1)),
                  tokv, pl.BlockSpec((1, C, KW), lambda b, c: (b, c, 0)), tokv,
                  pl.BlockSpec((1, GLA_DV), lambda b, c: (0, 0)), stt],
        out_specs=[tokv, stt],
        out_shape=[jax.ShapeDtypeStruct((B, tp, VW), f32),
                   jax.ShapeDtypeStruct((B, GLA_HEADS, GLA_DV, GLA_DK), f32)],
        compiler_params=pltpu.CompilerParams(
            dimension_semantics=("parallel", "arbitrary"), vmem_limit_bytes=VMEM_LIMIT),
        name="gla_chunk",
    )(qk, qk, v, logg, r, gn.reshape(1, GLA_DV), S0.astype(f32).transpose(0, 1, 3, 2))
    return out[:, :T], st.transpose(0, 1, 3, 2)


def kernel(x_prompt, x_sample, cache_nsa_kv, cache_nsa_win, state_rwkv, state_rwkv_shift, state_gla, page_table, norm_mix, norm_ffn, w_in_even, nsa_qnorm, nsa_knorm, cmp_w1, cmp_w2, cmp_pe, rwkv_mu, rwkv_w0, rwkv_wB, rwkv_a0, rwkv_aB, rwkv_gB, rwkv_kk, rwkv_ka, rwkv_rk, rwkv_ln_w, rwkv_ln_b, w_out_even, w_in_odd, gla_gate_up, gla_gate_b, gla_norm, w_out_odd, ffn_gate, ffn_up, ffn_down):
    dt = x_prompt.dtype
    bf16 = jnp.bfloat16
    B, T = x_prompt.shape[:2]
    DB, S = x_sample.shape[:2]
    depth = norm_mix.shape[0]
    P = page_table.shape[1] * PAGE_SIZE
    WB = cache_nsa_win.shape[2]
    WP = min(WINDOW, T)
    pos_p = jnp.arange(T, dtype=jnp.int32)
    pos_s = P + jnp.arange(S, dtype=jnp.int32)
    win_pos_s = P - WB + jnp.arange(WB + S, dtype=jnp.int32)
    nsa_pad = -(-NSA_IN // LANE) * LANE
    y_p, y_s = x_prompt, x_sample
    kv_p, kv_s, win_p, win_s, rw_p, rw_s, sh_p, sh_s, gl_p, gl_s = ([] for _ in range(10))
    for layer in range(depth):
        li = layer // 2
        if layer % 2 == 0:
            w_in = jnp.concatenate([jnp.pad(w_in_even[li][:, :NSA_IN], ((0, 0), (0, nsa_pad - NSA_IN))),
                                    w_in_even[li][:, NSA_IN:]], axis=1).astype(bf16)
            even_splits = ((0, nsa_pad), (nsa_pad, nsa_pad + RWKV_IN))
            w_out = w_out_even[li].astype(bf16)
            hn_p, hr_p = norm_proj(y_p, norm_mix[layer], w_in, even_splits)
            hn_s, hr_s = norm_proj(y_s, norm_mix[layer], w_in, even_splits)
            q, g, rows, win = nsa_project(hn_p, pos_p, nsa_qnorm[li], nsa_knorm[li])
            o_nsa_p = nsa_prompt(q, g, rows, win, cmp_w1[li], cmp_w2[li], cmp_pe[li])
            kv_p.append(rows.astype(dt))
            win_p.append(win[:, T - WP:].astype(dt))
            q, g, rows, win = nsa_project(hn_s, pos_s, nsa_qnorm[li], nsa_knorm[li])
            pool = cache_nsa_kv.reshape(-1, PAGE_SIZE, 4 * NSA_KV * HEAD_DIM)
            past = pool[li * cache_nsa_kv.shape[1] + page_table].reshape(DB, P, 4 * NSA_KV * HEAD_DIM)
            win_all = jnp.concatenate([cache_nsa_win[li], win.astype(cache_nsa_win.dtype)], axis=1)
            o_nsa_s = nsa_sample(q, g, past, rows.astype(past.dtype), win_all, cmp_w1[li], cmp_w2[li], cmp_pe[li])
            kv_s.append(rows.astype(dt))
            win_s.append(win_all[:, S:].astype(dt))
            rw_par = (rwkv_mu[li], rwkv_w0[li], rwkv_wB[li], rwkv_a0[li], rwkv_aB[li], rwkv_gB[li],
                      rwkv_kk[li], rwkv_ka[li], rwkv_rk[li], rwkv_ln_w[li], rwkv_ln_b[li])
            o_rw_p, st, last = rwkv_mix(hr_p, jnp.zeros((B, RWKV_IN), dt),
                                        jnp.zeros((B, RWKV_HEADS, HEAD_DIM, HEAD_DIM), dt), *rw_par)
            rw_p.append(st.astype(dt))
            sh_p.append(last.astype(dt))
            o_rw_s, st, last = rwkv_mix(hr_s, state_rwkv_shift[li], state_rwkv[li], *rw_par)
            rw_s.append(st.astype(dt))
            sh_s.append(last.astype(dt))
            w_outs = [w_out[:NSA_Q], w_out[NSA_Q:]]
            y_p = proj_residual([o_nsa_p.astype(dt), o_rw_p.astype(dt)], w_outs, y_p)
            y_s = proj_residual([o_nsa_s.astype(dt), o_rw_s.astype(dt)], w_outs, y_s)
        else:
            wo = w_in_odd[li]
            o2 = 2 * GLA_KW + GLA_VW
            w_in = jnp.concatenate([wo[:, :o2], wo[:, o2 + GLA_RANK:],
                                    jnp.pad(wo[:, o2:o2 + GLA_RANK], ((0, 0), (0, LANE - GLA_RANK)))],
                                   axis=1).astype(bf16)
            odd_splits = ((0, 2 * GLA_KW), (2 * GLA_KW, o2), (o2, o2 + GLA_VW), (o2 + GLA_VW, o2 + GLA_VW + LANE))
            w_out = w_out_odd[li].astype(bf16)
            qk_p, v_p, r_p, gd_p = norm_proj(y_p, norm_mix[layer], w_in, odd_splits)
            qk_s, v_s, r_s, gd_s = norm_proj(y_s, norm_mix[layer], w_in, odd_splits)
            o_p, st = gla_mix(qk_p, v_p, gd_p, r_p, jnp.zeros((B, GLA_HEADS, GLA_DK, GLA_DV), dt),
                              gla_gate_up[li], gla_gate_b[li], gla_norm[li])
            gl_p.append(st.astype(dt))
            o_s, st = gla_mix(qk_s, v_s, gd_s, r_s, state_gla[li], gla_gate_up[li], gla_gate_b[li], gla_norm[li])
            gl_s.append(st.astype(dt))
            y_p = proj_residual([o_p.astype(dt)], [w_out], y_p)
            y_s = proj_residual([o_s.astype(dt)], [w_out], y_s)
        wg, wu, wd = ffn_gate[layer].astype(bf16), ffn_up[layer].astype(bf16), ffn_down[layer].astype(bf16)
        y_p = ffn(y_p, norm_ffn[layer], wg, wu, wd)
        y_s = ffn(y_s, norm_ffn[layer], wg, wu, wd)
    return (y_p, y_s, jnp.stack(kv_p), jnp.stack(kv_s), jnp.stack(win_p), jnp.stack(win_s),
            jnp.stack(rw_p), jnp.stack(rw_s), jnp.stack(sh_p), jnp.stack(sh_s),
            jnp.stack(gl_p), jnp.stack(gl_s))
```

```python
import functools

import jax
import jax.numpy as jnp
from jax import lax
from jax.experimental import pallas as pl
from jax.experimental.pallas import tpu as pltpu

D_MODEL = 1024
PAGE_SIZE = 128
HEAD_DIM = 64
NSA_HEADS = 8
NSA_KV = 2
NSA_REP = NSA_HEADS // NSA_KV
CMP_STRIDE = 16
CMP_BLOCK = 2 * CMP_STRIDE
CMP_HIDDEN = 2 * HEAD_DIM
SEL_BLOCK = 64
N_SELECT = 16
WINDOW = 512
ROPE_DIM = HEAD_DIM // 4
ROPE_THETA = 500000.0
RWKV_HEADS = 8
RWKV_WIDTH = RWKV_HEADS * HEAD_DIM
LORA_W = 64
LORA_A = 64
LORA_G = 128
RWKV_GN_EPS = 64e-5
GLA_HEADS = 4
GLA_DK = D_MODEL // 2 // GLA_HEADS
GLA_DV = D_MODEL // GLA_HEADS
GLA_RANK = 16
GLA_TAU = 16.0
GLA_CHUNK = 64
D_FF = ((8 * D_MODEL // 3 + 255) // 256) * 256
NSA_Q = NSA_HEADS * HEAD_DIM
NSA_KVW = 3 * 2 * NSA_KV * HEAD_DIM
NSA_IN = NSA_Q + NSA_KVW + 3 * NSA_HEADS
RWKV_IN = 3 * RWKV_WIDTH + LORA_W + LORA_A + LORA_G
EVEN_IN = NSA_IN + RWKV_IN
MIX_WIDTH = NSA_Q + RWKV_WIDTH
GLA_KW = GLA_HEADS * GLA_DK
GLA_VW = GLA_HEADS * GLA_DV
ODD_IN = 2 * GLA_KW + GLA_VW + GLA_RANK + GLA_VW

LANE = 128
VMEM_LIMIT = 48 * 1024 * 1024


ROW_TILE = 512
NORM_EPS = 1e-6


def _row_tile(m):
    return ROW_TILE if m % ROW_TILE == 0 else m


def _norm_bf16(x_ref, g_ref):
    x = x_ref[...]
    y = x * lax.rsqrt(jnp.mean(x * x, axis=-1, keepdims=True) + NORM_EPS)
    return (y * g_ref[...]).astype(jnp.bfloat16)


def _norm_proj_body(x_ref, g_ref, w_ref, *o_refs, splits):
    xn = _norm_bf16(x_ref, g_ref)
    for (a, b), o_ref in zip(splits, o_refs):
        o_ref[...] = jnp.dot(xn, w_ref[:, a:b], preferred_element_type=jnp.float32)


def norm_proj(x, gain, w, splits):
    bsz, t, k = x.shape
    m = bsz * t
    tm = _row_tile(m)
    outs = pl.pallas_call(
        functools.partial(_norm_proj_body, splits=splits),
        grid=(m // tm,),
        in_specs=[pl.BlockSpec((tm, k), lambda i: (i, 0)),
                  pl.BlockSpec((1, k), lambda i: (0, 0)),
                  pl.BlockSpec(w.shape, lambda i: (0, 0))],
        out_specs=[pl.BlockSpec((tm, b - a), lambda i: (i, 0)) for a, b in splits],
        out_shape=[jax.ShapeDtypeStruct((m, b - a), jnp.float32) for a, b in splits],
        compiler_params=pltpu.CompilerParams(dimension_semantics=("parallel",), vmem_limit_bytes=VMEM_LIMIT),
        name="norm_proj",
    )(x.reshape(m, k), gain.reshape(1, k), w)
    return [o.reshape(bsz, t, -1) for o in outs]


def _proj_res_body(*refs, n_in):
    x_refs, w_refs, res_ref, o_ref = refs[:n_in], refs[n_in:2 * n_in], refs[2 * n_in], refs[2 * n_in + 1]
    acc = res_ref[...]
    for x_ref, w_ref in zip(x_refs, w_refs):
        acc = acc + jnp.dot(x_ref[...].astype(jnp.bfloat16), w_ref[...], preferred_element_type=jnp.float32)
    o_ref[...] = acc


def proj_residual(xs, ws, res):
    bsz, t, n = res.shape
    m = bsz * t
    tm = _row_tile(m)
    out = pl.pallas_call(
        functools.partial(_proj_res_body, n_in=len(xs)),
        grid=(m // tm,),
        in_specs=([pl.BlockSpec((tm, x.shape[-1]), lambda i: (i, 0)) for x in xs]
                  + [pl.BlockSpec(w.shape, lambda i: (0, 0)) for w in ws]
                  + [pl.BlockSpec((tm, n), lambda i: (i, 0))]),
        out_specs=pl.BlockSpec((tm, n), lambda i: (i, 0)),
        out_shape=jax.ShapeDtypeStruct((m, n), jnp.float32),
        compiler_params=pltpu.CompilerParams(dimension_semantics=("parallel",), vmem_limit_bytes=VMEM_LIMIT),
        name="proj_residual",
    )(*[x.reshape(m, x.shape[-1]) for x in xs], *ws, res.reshape(m, n))
    return out.reshape(bsz, t, n)


FFN_COL_CHUNKS = 2


def _ffn_up_body(x_ref, g_ref, wg_ref, wu_ref, h_ref):
    xn = _norm_bf16(x_ref, g_ref)
    cw = h_ref.shape[1] // FFN_COL_CHUNKS
    for c in range(FFN_COL_CHUNKS):
        cols = slice(c * cw, (c + 1) * cw)
        g = jnp.dot(xn, wg_ref[:, cols], preferred_element_type=jnp.float32)
        u = jnp.dot(xn, wu_ref[:, cols], preferred_element_type=jnp.float32)
        h_ref[:, cols] = (g * jax.nn.sigmoid(g) * u).astype(h_ref.dtype)


def ffn(y, gain, wg, wu, wd):
    bsz, t, k = y.shape
    m = bsz * t
    tm = _row_tile(m)
    f = wg.shape[1]
    h = pl.pallas_call(
        _ffn_up_body,
        grid=(m // tm,),
        in_specs=[pl.BlockSpec((tm, k), lambda i: (i, 0)),
                  pl.BlockSpec((1, k), lambda i: (0, 0)),
                  pl.BlockSpec(wg.shape, lambda i: (0, 0)),
                  pl.BlockSpec(wu.shape, lambda i: (0, 0))],
        out_specs=pl.BlockSpec((tm, f), lambda i: (i, 0)),
        out_shape=jax.ShapeDtypeStruct((m, f), jnp.bfloat16),
        compiler_params=pltpu.CompilerParams(dimension_semantics=("parallel",), vmem_limit_bytes=VMEM_LIMIT),
        name="ffn_up",
    )(y.reshape(m, k), gain.reshape(1, k), wg, wu)
    return proj_residual([h.reshape(bsz, t, f)], [wd], y)


def rmsnorm(x, g, eps=1e-6):
    xf = x.astype(jnp.float32)
    y = xf * lax.rsqrt(jnp.mean(xf * xf, axis=-1, keepdims=True) + eps)
    return (y * g.astype(jnp.float32)).astype(x.dtype)


def rope(x, pos):
    half = ROPE_DIM // 2
    inv = ROPE_THETA ** (-jnp.arange(half, dtype=jnp.float32) / half)
    ang = pos.astype(jnp.float32)[:, None] * inv[None, :]
    shp = (pos.shape[0],) + (1,) * (x.ndim - 3) + (half,)
    cos, sin = jnp.cos(ang).reshape(shp), jnp.sin(ang).reshape(shp)
    xr = x[..., :ROPE_DIM].astype(jnp.float32)
    x1, x2 = xr[..., :half], xr[..., half:]
    rot = jnp.concatenate([x1 * cos - x2 * sin, x2 * cos + x1 * sin], axis=-1)
    return jnp.concatenate([rot.astype(x.dtype), x[..., ROPE_DIM:]], axis=-1)


def nsa_project(h, pos, qn, kn):
    B, T = h.shape[:2]
    q = h[..., :NSA_Q].reshape(B, T, NSA_HEADS, HEAD_DIM)
    kv = h[..., NSA_Q:NSA_Q + NSA_KVW].reshape(B, T, 3, 2, NSA_KV, HEAD_DIM)
    gates = jax.nn.sigmoid(h[..., NSA_Q + NSA_KVW:NSA_IN].astype(jnp.float32)).reshape(B, T, NSA_HEADS, 3)
    q = rope(rmsnorm(q, qn), pos)
    k = rope(rmsnorm(kv[:, :, :, 0], kn[:, None, :]), pos)
    kv = jnp.stack([k, kv[:, :, :, 1]], axis=3)
    rows = kv[:, :, :2].reshape(B, T, 4, NSA_KV, HEAD_DIM)
    win = kv[:, :, 2]
    return q, gates, rows, win


NEG = -1e30


NSA_TQ = 256
NSA_TK = 512
NSA_TQ_DECODE = 32


def _nsa_attn_body(qt_ref, kc_ref, vct_ref, map_ref, ks_ref, vst_ref, kw_ref, vw_ref, g_ref, o_ref, thr_scr,
                   *, tq, n_cmp, n_sel, q_tile0, win_base, natural):
    f32, bf16 = jnp.float32, jnp.bfloat16
    tk = NSA_TK
    nl = NSA_REP * tq
    i = pl.program_id(1) + q_tile0
    if natural:
        xq = qt_ref[0] * (HEAD_DIM ** -0.5)
        qt = jnp.concatenate([xq[:, HEAD_DIM * r:HEAD_DIM * (r + 1)].T for r in range(NSA_REP)], axis=1).astype(bf16)
    else:
        qt = qt_ref[0, 0]
    qpos = i * tq + lax.broadcasted_iota(jnp.int32, (1, nl), 1) % tq

    ncp = kc_ref.shape[1]
    n = lax.broadcasted_iota(jnp.int32, (ncp, 1), 0)
    ok = (n * CMP_STRIDE + (CMP_BLOCK - 1) <= qpos) & (n < n_cmp)
    s = jnp.dot(kc_ref[0], qt, preferred_element_type=f32)
    s = jnp.where(ok, s, NEG)
    m = jnp.max(s, axis=0, keepdims=True)
    e = jnp.where(ok, jnp.exp(s - m), 0.0)
    l = jnp.sum(e, axis=0, keepdims=True)
    p = e * jnp.where(l > 0.0, 1.0 / l, 0.0)
    o_cmp = jnp.dot(vct_ref[0], p.astype(bf16), preferred_element_type=f32)

    psum = p[:, 0:tq]
    for r in range(1, NSA_REP):
        psum = psum + p[:, r * tq:(r + 1) * tq]
    hi = psum.astype(bf16)
    lo = (psum - hi.astype(f32)).astype(bf16)
    imp = (jnp.dot(map_ref[...], hi, preferred_element_type=f32)
           + jnp.dot(map_ref[...], lo, preferred_element_type=f32))
    nsp = map_ref.shape[0]
    qp = qpos[:, 0:tq]
    blk = lax.broadcasted_iota(jnp.int32, (nsp, 1), 0)
    cur = qp // SEL_BLOCK
    valid = (blk * SEL_BLOCK <= qp) & (blk < n_sel)
    forced = (blk == 0) | (blk == cur) | (blk == cur - 1)
    x = jnp.where(valid, jnp.where(forced, 1e30, imp), NEG)
    x = jnp.where(blk < n_sel, x, -3e38)
    blkf = blk.astype(f32)
    sel = jnp.zeros((nsp, tq), f32)
    for _ in range(min(N_SELECT, n_sel)):
        mx = jnp.max(x, axis=0, keepdims=True)
        first = jnp.min(jnp.where(x == mx, blkf, float(nsp)), axis=0, keepdims=True)
        hit = blkf == first
        sel = jnp.where(hit, 1.0, sel)
        x = jnp.where(hit, -3e38, x)
    thr = jnp.where(valid & (sel > 0.5), qp, -1)
    thr_scr[...] = jnp.concatenate([thr] * NSA_REP, axis=1)

    def flash(s, vt, carry):
        m_old, l_old, acc = carry
        m_new = jnp.maximum(m_old, jnp.max(s, axis=0, keepdims=True))
        alpha = jnp.exp(m_old - m_new)
        p = jnp.exp(s - m_new)
        l_new = alpha * l_old + jnp.sum(p, axis=0, keepdims=True)
        acc = alpha * acc + jnp.dot(vt, p.astype(bf16), preferred_element_type=f32)
        return m_new, l_new, acc

    init = (jnp.full((1, nl), NEG, f32), jnp.zeros((1, nl), f32), jnp.zeros((HEAD_DIM, nl), f32))

    krow = lax.broadcasted_iota(jnp.int32, (SEL_BLOCK, 1), 0)

    def masked_scores(j):
        s = jnp.dot(ks_ref[0, j], qt, preferred_element_type=f32)
        th = thr_scr[pl.ds(pl.multiple_of(j * (tk // SEL_BLOCK), 8), tk // SEL_BLOCK), :]
        return jnp.concatenate(
            [jnp.where((j * tk + b * SEL_BLOCK + krow) <= th[b:b + 1], s[b * SEL_BLOCK:(b + 1) * SEL_BLOCK], NEG)
             for b in range(tk // SEL_BLOCK)], axis=0)

    def sel_pair(j2, carry):
        m_old, l_old, acc = carry
        s0, s1 = masked_scores(2 * j2), masked_scores(2 * j2 + 1)
        m_new = jnp.maximum(m_old, jnp.maximum(jnp.max(s0, axis=0, keepdims=True), jnp.max(s1, axis=0, keepdims=True)))
        alpha = jnp.exp(m_old - m_new)
        p0, p1 = jnp.exp(s0 - m_new), jnp.exp(s1 - m_new)
        l_new = alpha * l_old + jnp.sum(p0, axis=0, keepdims=True) + jnp.sum(p1, axis=0, keepdims=True)
        acc = (alpha * acc + jnp.dot(vst_ref[0, 2 * j2], p0.astype(bf16), preferred_element_type=f32)
               + jnp.dot(vst_ref[0, 2 * j2 + 1], p1.astype(bf16), preferred_element_type=f32))
        return m_new, l_new, acc

    n_tiles = (i * tq + tq - 1) // tk + 1
    carry = lax.fori_loop(0, n_tiles // 2, sel_pair, init)
    carry = lax.cond(n_tiles % 2 == 1,
                     lambda c: flash(masked_scores(n_tiles - 1), vst_ref[0, n_tiles - 1], c),
                     lambda c: c, carry)
    _, l_s, acc_s = carry
    o_slc = acc_s / l_s

    nkw = WINDOW + tq
    k_first = i * tq - WINDOW
    row0 = pl.multiple_of(k_first - win_base, 16)
    kpos = k_first + lax.broadcasted_iota(jnp.int32, (nkw, 1), 0)
    allowed = (kpos <= qpos) & (kpos > qpos - WINDOW) & (kpos >= 0)
    s = jnp.dot(kw_ref[0, pl.ds(row0, nkw), :], qt, preferred_element_type=f32)
    s = jnp.where(allowed, s, NEG)
    e = jnp.where(allowed, jnp.exp(s - jnp.max(s, axis=0, keepdims=True)), 0.0)
    o_win = (lax.dot_general(vw_ref[0, pl.ds(row0, nkw), :], e.astype(bf16), (((0,), (0,)), ((), ())),
                             preferred_element_type=f32) / jnp.sum(e, axis=0, keepdims=True))

    g = g_ref[0, 0]
    res = g[0:1] * o_cmp + g[1:2] * o_slc + g[2:3] * o_win
    if natural:
        o_ref[0] = jnp.concatenate([res[:, tq * r:tq * (r + 1)].T for r in range(NSA_REP)], axis=1)
    else:
        o_ref[0, 0] = res


def nsa_attention(q, gates, kc, vc, ksl, vsl, kwin, vwin, *, n_keys, tq=NSA_TQ, q_pos0=0, win_pos0=0):
    f32, bf16 = jnp.float32, jnp.bfloat16
    B, T = q.shape[:2]
    G, R, D = NSA_KV, NSA_REP, HEAD_DIM
    tk = NSA_TK
    nq, nl = T // tq, R * tq
    n_cmp, n_sel = kc.shape[1], -(-n_keys // SEL_BLOCK)
    lk = ksl.shape[1]
    assert T % tq == 0 and nl % LANE == 0 and lk % tk == 0 and q_pos0 % tq == 0 and tq % 16 == 0
    assert lk >= q_pos0 + T and win_pos0 <= max(q_pos0 - WINDOW, 0)
    win_base = win_pos0 - WINDOW
    lw = q_pos0 + T - win_base
    wpad = lambda x: jnp.pad(x[:, :lw - WINDOW], ((0, 0), (WINDOW, max(lw - WINDOW - x.shape[1], 0)), (0, 0), (0, 0)))
    kwin, vwin = wpad(kwin), wpad(vwin)
    ncp = -(-n_cmp // LANE) * LANE
    nsp = -(-n_sel // LANE) * LANE
    n = jnp.arange(ncp)[None, :]
    jb = jnp.arange(nsp)[:, None]
    map_t = ((n * CMP_STRIDE <= jb * SEL_BLOCK + SEL_BLOCK - 1) & (n * CMP_STRIDE + CMP_BLOCK - 1 >= jb * SEL_BLOCK)
             & (n < n_cmp) & (jb < n_sel)).astype(bf16)
    natural = tq % LANE == 0
    if natural:
        qt = q.reshape(B, T, NSA_Q)
        q_spec = pl.BlockSpec((1, tq, R * D), lambda b, i: (b // G, i, b % G))
        o_spec, o_shape = q_spec, jax.ShapeDtypeStruct((B, T, NSA_Q), f32)
    else:
        qt = (q * (D ** -0.5)).reshape(B, nq, tq, G, R, D).transpose(0, 3, 1, 5, 4, 2)
        qt = qt.reshape(B * G, nq, D, nl).astype(bf16)
        q_spec = pl.BlockSpec((1, 1, D, nl), lambda b, i: (b, i, 0, 0))
        o_spec, o_shape = q_spec, jax.ShapeDtypeStruct((B * G, nq, D, nl), f32)
    gt = gates.reshape(B, nq, tq, G, R, 3).transpose(0, 3, 1, 5, 4, 2).reshape(B * G, nq, 3, nl)
    gt = jnp.pad(gt, ((0, 0), (0, 0), (0, 5), (0, 0)))
    rows = lambda x, t: x.transpose(0, 2, 1, 3).reshape(B * G, -1, t, D).astype(bf16)
    cols = lambda x, t: x.reshape(B, -1, t, G, D).transpose(0, 3, 1, 4, 2).reshape(B * G, -1, D, t).astype(bf16)
    kcp = jnp.pad(kc, ((0, 0), (0, ncp - n_cmp), (0, 0), (0, 0)))
    vcp = jnp.pad(vc, ((0, 0), (0, ncp - n_cmp), (0, 0), (0, 0)))
    kc_r = rows(kcp, ncp)[:, 0]
    vc_c = cols(vcp, ncp)[:, 0]
    full = lambda shape: pl.BlockSpec((1,) + shape, lambda b, i: (b,) + (0,) * len(shape))
    per_q = lambda shape: pl.BlockSpec((1, 1) + shape, lambda b, i: (b, i) + (0,) * len(shape))
    out = pl.pallas_call(
        functools.partial(_nsa_attn_body, tq=tq, n_cmp=n_cmp, n_sel=n_sel,
                          q_tile0=q_pos0 // tq, win_base=win_base, natural=natural),
        grid=(B * G, nq),
        in_specs=[q_spec, full((ncp, D)), full((D, ncp)),
                  pl.BlockSpec((nsp, ncp), lambda b, i: (0, 0)),
                  full((lk // tk, tk, D)), full((lk // tk, D, tk)),
                  full((lw, D)), full((lw, D)),
                  per_q((8, nl))],
        out_specs=o_spec,
        out_shape=o_shape,
        scratch_shapes=[pltpu.VMEM((nsp, nl), jnp.int32)],
        compiler_params=pltpu.CompilerParams(
            dimension_semantics=("parallel", "parallel"), vmem_limit_bytes=VMEM_LIMIT),
        name="nsa_attention",
    )(qt, kc_r, vc_c, map_t, rows(ksl, tk), cols(vsl, tk), rows(kwin, lw)[:, 0], rows(vwin, lw)[:, 0], gt)
    if natural:
        return out
    return out.reshape(B, G, nq, D, R, tq).transpose(0, 2, 5, 1, 4, 3).reshape(B, T, NSA_Q)


def nsa_prompt(q, gates, rows, win, cw1, cw2, cpe):
    B, T = q.shape[:2]
    kc = compress_pair(rows[:, :, 0].reshape(B, T, -1), cw1[0], cw2[0], cpe[0])
    vc = compress_pair(rows[:, :, 1].reshape(B, T, -1), cw1[1], cw2[1], cpe[1])
    return nsa_attention(q, gates, kc, vc, rows[:, :, 2], rows[:, :, 3], win[:, :, 0], win[:, :, 1],
                         n_keys=T)


def compress_pair(x, w1, w2, pe):
    B, L, gd = x.shape
    n_chunks = L // CMP_STRIDE
    c = x[:, :n_chunks * CMP_STRIDE].reshape(B, n_chunks, CMP_STRIDE, gd)
    eye = jnp.eye(NSA_KV, dtype=w1.dtype)
    big = lambda w: jnp.einsum('ldh,gk->lgdkh', w, eye).reshape(CMP_STRIDE, gd, NSA_KV * CMP_HIDDEN)
    a = jnp.einsum('bnlx,lxy->bny', c, big(w1[:CMP_STRIDE]))
    bc = jnp.einsum('bnlx,lxy->bny', c, big(w1[CMP_STRIDE:]))
    h = a[:, :-1] + bc[:, 1:] + jnp.tile(jnp.einsum('ld,ldh->h', pe, w1), NSA_KV)
    w2_big = jnp.einsum('hd,gk->ghkd', w2, eye).reshape(NSA_KV * CMP_HIDDEN, gd)
    return (jax.nn.gelu(h) @ w2_big).reshape(B, n_chunks - 1, NSA_KV, HEAD_DIM)


def nsa_sample(q, gates, past, rows, win_all, cw1, cw2, cpe):
    DB, S = q.shape[:2]
    P, WB = past.shape[1], win_all.shape[1] - S
    gd = NSA_KV * HEAD_DIM
    kind = lambda w: past[:, :, w * gd:(w + 1) * gd]
    new = lambda w: rows[:, :, w].reshape(DB, S, gd)
    up = lambda n, m: -(-n // m) * m
    pad_rows = lambda x, n: jnp.pad(x, ((0, 0), (0, n - x.shape[1])) + ((0, 0),) * (x.ndim - 2))
    n_chunk_rows = (P + S) // CMP_STRIDE * CMP_STRIDE
    cmp_in = [kind(w)[:, :n_chunk_rows] if n_chunk_rows <= P
              else jnp.concatenate([kind(w), new(w)[:, :n_chunk_rows - P]], axis=1) for w in (0, 1)]
    kc = compress_pair(cmp_in[0], cw1[0], cw2[0], cpe[0])
    vc = compress_pair(cmp_in[1], cw1[1], cw2[1], cpe[1])
    tq = up(S, NSA_TQ_DECODE)
    lk = up(P + tq, NSA_TK)
    groups = lambda x: x.reshape(DB, -1, NSA_KV, HEAD_DIM)
    ksl = groups(pad_rows(jnp.concatenate([kind(2), new(2)], axis=1), lk))
    vsl = groups(pad_rows(jnp.concatenate([kind(3), new(3)], axis=1), lk))
    o = nsa_attention(pad_rows(q, tq), pad_rows(gates, tq), kc, vc, ksl, vsl, win_all[:, :, 0], win_all[:, :, 1],
                      n_keys=P + S, tq=tq, q_pos0=P, win_pos0=P - WB)
    return o[:, :S]


RW_PAIRS = RWKV_HEADS // 2
RW_CHUNK = 64


def _rwkv_scan_body(r_ref, w_ref, k_ref, kk_ref, b_ref, vt_ref, s0_ref, oh_ref,
                    y_ref, st_ref, vhi_scr, yacc_scr, *, nb, tc, natural):
    f32, bf16 = jnp.float32, jnp.bfloat16
    c = pl.program_id(1)
    eye = (lax.broadcasted_iota(jnp.int32, (HEAD_DIM, HEAD_DIM), 0)
           == lax.broadcasted_iota(jnp.int32, (HEAD_DIM, HEAD_DIM), 1)).astype(bf16)

    def halves(x, terms):
        out, rest = None, x
        for _ in range(terms):
            piece = rest.astype(bf16)
            rest = rest - piece.astype(f32)
            t = jnp.concatenate([lax.dot_general(eye, piece[:, HEAD_DIM * h:HEAD_DIM * (h + 1)],
                                                 (((1,), (1,)), ((), ())), preferred_element_type=f32)
                                 for h in range(2)], axis=1)
            out = t if out is None else out + t
        return out

    @pl.when(c == 0)
    def _():
        st_ref[...] = s0_ref[...]

    row = lax.broadcasted_iota(jnp.int32, (LANE, LANE), 0)
    col = lax.broadcasted_iota(jnp.int32, (LANE, LANE), 1)
    ones1 = ((row // HEAD_DIM) == (col // HEAD_DIM)).astype(bf16)
    lane_t = lax.broadcasted_iota(jnp.int32, (HEAD_DIM, LANE), 1) % HEAD_DIM

    for b in range(nb):
        for p in range(RW_PAIRS):
            vt = halves(vt_ref[b, :, LANE * p:LANE * (p + 1)], 1) if natural else vt_ref[b, 0, p]
            vhi_scr[b * RW_PAIRS + p] = vt.astype(bf16)
    yacc_scr[...] = jnp.zeros_like(yacc_scr)

    pairs = [(b, p) for b in range(nb) for p in range(RW_PAIRS)]
    n_pairs = len(pairs)

    def group(t8, carry):
        t0 = pl.multiple_of(t8 * 8, 8)
        for j in range(0, 8, 2):
            ra = lambda ref, b, sl: ref[b, pl.ds(t0, 8), sl][j:j + 1]
            rb = lambda ref, b, sl: ref[b, pl.ds(t0, 8), sl][j + 1:j + 2]
            oh_a, oh_b = oh_ref[t0 + j], oh_ref[t0 + j + 1]
            cr = jnp.concatenate([ra(x_ref, b, slice(LANE * p, LANE * (p + 1)))
                                  * rb(kk_ref, b, slice(LANE * p, LANE * (p + 1)))
                                  for b, p in pairs for x_ref in (b_ref, k_ref)], axis=0)
            cr_hi = cr.astype(bf16)
            c_all = (jnp.dot(cr_hi, ones1, preferred_element_type=f32)
                     + jnp.dot((cr - cr_hi.astype(f32)).astype(bf16), ones1, preferred_element_type=f32))
            vl = jnp.concatenate([vhi_scr[i] * oh for oh in (oh_a, oh_b) for i in range(n_pairs)], axis=0)
            vb_all = jnp.dot(vl, ones1, preferred_element_type=f32)
            pieces = []
            for b, p in pairs:
                sl = slice(LANE * p, LANE * (p + 1))
                s = st_ref[b, p]
                pieces.append((s * ra(kk_ref, b, sl)).astype(bf16))
                pieces.append((s * (ra(w_ref, b, sl) * rb(kk_ref, b, sl))).astype(bf16))
            u_all = jnp.dot(jnp.concatenate(pieces, axis=0), ones1, preferred_element_type=f32)
            py = []
            for i, (b, p) in enumerate(pairs):
                sl = slice(LANE * p, LANE * (p + 1))
                u1 = u_all[2 * HEAD_DIM * i:2 * HEAD_DIM * i + HEAD_DIM]
                u2 = u_all[2 * HEAD_DIM * i + HEAD_DIM:2 * HEAD_DIM * (i + 1)]
                vb_a = vb_all[HEAD_DIM * i:HEAD_DIM * (i + 1)]
                vb_b = vb_all[HEAD_DIM * (n_pairs + i):HEAD_DIM * (n_pairs + i + 1)]
                sa_b = u2 - u1 * c_all[2 * i:2 * i + 1] + vb_a * c_all[2 * i + 1:2 * i + 2]
                s_a = st_ref[b, p] * ra(w_ref, b, sl) - u1 * ra(b_ref, b, sl) + vb_a * ra(k_ref, b, sl)
                s_b = s_a * rb(w_ref, b, sl) - sa_b * rb(b_ref, b, sl) + vb_b * rb(k_ref, b, sl)
                st_ref[b, p] = s_b
                py.append((s_a * ra(r_ref, b, sl)).astype(bf16))
                py.append((s_b * rb(r_ref, b, sl)).astype(bf16))
            y_all = jnp.dot(jnp.concatenate(py, axis=0), ones1, preferred_element_type=f32)
            m_a, m_b = lane_t == t0 + j, lane_t == t0 + j + 1
            for i, (b, p) in enumerate(pairs):
                y_a = y_all[2 * HEAD_DIM * i:2 * HEAD_DIM * i + HEAD_DIM]
                y_b = y_all[2 * HEAD_DIM * i + HEAD_DIM:2 * HEAD_DIM * (i + 1)]
                yacc_scr[i] = jnp.where(m_a, y_a, jnp.where(m_b, y_b, yacc_scr[i]))
        return carry

    lax.fori_loop(0, tc // 8, group, 0)
    for i, (b, p) in enumerate(pairs):
        if natural:
            y_ref[b, :, LANE * p:LANE * (p + 1)] = halves(yacc_scr[i], 3)
        else:
            y_ref[b, 0, p] = yacc_scr[i]


def rwkv_scan(r, w, k, v, kk, b, S0):
    f32 = jnp.float32
    B, T, W = r.shape
    tc = RW_CHUNK if T % RW_CHUNK == 0 else T
    assert tc <= RW_CHUNK and T % tc == 0 and tc % 8 == 0 and B % 2 == 0
    nc, nb = T // tc, 2
    natural = tc == RW_CHUNK
    tok = pl.BlockSpec((nb, tc, W), lambda i, c: (i, c, 0))
    chk = pl.BlockSpec((nb, 1, RW_PAIRS, HEAD_DIM, LANE), lambda i, c: (i, c, 0, 0, 0))
    if natural:
        vt, vy_spec, y_shape = v, tok, jax.ShapeDtypeStruct((B, T, W), f32)
    else:
        vt = v.reshape(B, nc, tc, RW_PAIRS, 2, HEAD_DIM).transpose(0, 1, 3, 5, 4, 2)
        vt = jnp.pad(vt, ((0, 0),) * 5 + ((0, RW_CHUNK - tc),)).reshape(B, nc, RW_PAIRS, HEAD_DIM, LANE)
        vy_spec, y_shape = chk, jax.ShapeDtypeStruct((B, nc, RW_PAIRS, HEAD_DIM, LANE), f32)
    s0 = S0.astype(f32).reshape(B, RW_PAIRS, 2, HEAD_DIM, HEAD_DIM).transpose(0, 1, 3, 2, 4)
    s0 = s0.reshape(B, RW_PAIRS, HEAD_DIM, LANE)
    oh = jnp.arange(LANE)[None, None, :] % HEAD_DIM == jnp.arange(RW_CHUNK)[:, None, None]
    oh = jnp.broadcast_to(oh, (RW_CHUNK, HEAD_DIM, LANE)).astype(jnp.bfloat16)
    stt = pl.BlockSpec((nb, RW_PAIRS, HEAD_DIM, LANE), lambda i, c: (i, 0, 0, 0))
    y, st = pl.pallas_call(
        functools.partial(_rwkv_scan_body, nb=nb, tc=tc, natural=natural),
        grid=(B // nb, nc),
        in_specs=[tok, tok, tok, tok, tok, vy_spec, stt,
                  pl.BlockSpec((RW_CHUNK, HEAD_DIM, LANE), lambda i, c: (0, 0, 0))],
        out_specs=[vy_spec, stt],
        out_shape=[y_shape, jax.ShapeDtypeStruct((B, RW_PAIRS, HEAD_DIM, LANE), f32)],
        scratch_shapes=[pltpu.VMEM((nb * RW_PAIRS, HEAD_DIM, LANE), jnp.bfloat16),
                        pltpu.VMEM((nb * RW_PAIRS, HEAD_DIM, LANE), f32)],
        compiler_params=pltpu.CompilerParams(
            dimension_semantics=("parallel", "arbitrary"), vmem_limit_bytes=VMEM_LIMIT),
        name="rwkv_scan",
    )(r, w, k, kk, b, vt, s0, oh)
    if not natural:
        y = y.reshape(B, nc, RW_PAIRS, HEAD_DIM, 2, RW_CHUNK)[..., :tc]
        y = y.transpose(0, 1, 5, 2, 4, 3).reshape(B, T, W)
    st = st.reshape(B, RW_PAIRS, HEAD_DIM, 2, HEAD_DIM).transpose(0, 1, 3, 2, 4)
    return y, st.reshape(B, RWKV_HEADS, HEAD_DIM, HEAD_DIM)


def rwkv_mix(p, prev, S0, mu, w0, wB, a0, aB, gB, k_k, k_a, r_k, ln_w, ln_b):
    f32 = jnp.float32
    B, T = p.shape[:2]
    W = RWKV_WIDTH
    p = p.astype(f32)
    p_prev = jnp.concatenate([prev.astype(f32)[:, None], p[:, :-1]], axis=1)
    ps = p + mu * (p_prev - p)
    r, k, v = ps[..., :W], ps[..., W:2 * W], ps[..., 2 * W:3 * W]
    o = 3 * W
    xw, xa, xg = ps[..., o:o + LORA_W], ps[..., o + LORA_W:o + LORA_W + LORA_A], ps[..., o + LORA_W + LORA_A:]
    z = w0 + jnp.tanh(xw) @ wB
    w = jnp.exp(-jnp.exp(-jax.nn.softplus(-z) - 0.5))
    a = jax.nn.sigmoid(a0 + xa @ aB)
    g = jax.nn.sigmoid(xg) @ gB
    heads = lambda t: t.reshape(B, T, RWKV_HEADS, HEAD_DIM)
    kk = heads(k * k_k)
    kk = kk * lax.rsqrt(jnp.maximum(jnp.sum(kk * kk, axis=-1, keepdims=True), 1e-24))
    k = k * (1.0 + (a - 1.0) * k_a)
    kk = kk.reshape(B, T, W)
    y, S = rwkv_scan(r, w, k, v, kk, kk * a, S0)
    y, k = heads(y), heads(k)
    r, v = heads(r), heads(v)
    mean = jnp.mean(y, axis=-1, keepdims=True)
    var = jnp.mean(jnp.square(y - mean), axis=-1, keepdims=True)
    y = ((y - mean) * lax.rsqrt(var + RWKV_GN_EPS)).reshape(B, T, W) * ln_w + ln_b
    y = y + (jnp.sum(r * k * r_k, axis=-1, keepdims=True) * v).reshape(B, T, W)
    return y * g, S, p[:, -1]


GLA_SUB = 16


def _gla_body(q_ref, k_ref, v_ref, g_ref, r_ref, gn_ref, s0_ref, o_ref, st_ref):
    f32, bf16 = jnp.float32, jnp.bfloat16
    C, SUB = q_ref.shape[1], GLA_SUB
    nsub = C // SUB
    c = pl.program_id(1)

    @pl.when(c == 0)
    def _():
        st_ref[...] = s0_ref[...]

    ti = lax.broadcasted_iota(jnp.int32, (C, C), 0)
    si = lax.broadcasted_iota(jnp.int32, (C, C), 1)
    tri = (si <= ti).astype(bf16)
    sub_t = lax.broadcasted_iota(jnp.int32, (SUB, 1), 0)
    sub_l = lax.broadcasted_iota(jnp.int32, (SUB, SUB), 1)
    for h in range(GLA_HEADS):
        kq = slice(h * GLA_DK, (h + 1) * GLA_DK)
        vv = slice(h * GLA_DV, (h + 1) * GLA_DV)
        q = q_ref[0, :, kq] * (GLA_DK ** -0.5)
        k = k_ref[0, :, kq]
        v = v_ref[0, :, vv].astype(bf16)
        g = g_ref[0, :, kq]
        g1 = g.astype(bf16)
        g2 = (g - g1.astype(f32)).astype(bf16)
        g3 = (g - g1.astype(f32) - g2.astype(f32)).astype(bf16)
        b = (jnp.dot(tri, g1, preferred_element_type=f32) + jnp.dot(tri, g2, preferred_element_type=f32)
             + jnp.dot(tri, g3, preferred_element_type=f32))
        b_last = b[C - 1:C]
        a_rows = []
        for I in range(nsub):
            rows = slice(I * SUB, (I + 1) * SUB)
            beta = b[I * SUB - 1:I * SUB] if I > 0 else jnp.zeros((1, GLA_DK), f32)
            qb, kb, bb = q[rows], k[rows], b[rows]
            a_diag = jnp.zeros((SUB, SUB), f32)
            for s in range(SUB):
                d = jnp.where(sub_t >= s, bb - bb[s:s + 1], -jnp.inf)
                col = jnp.sum(qb * kb[s:s + 1] * jnp.exp(d), axis=1, keepdims=True)
                a_diag = jnp.where(sub_l == s, col, a_diag)
            blocks = []
            if I > 0:
                qe = (qb * jnp.exp(bb - beta)).astype(bf16)
                ke = (k[:I * SUB] * jnp.exp(beta - b[:I * SUB])).astype(bf16)
                blocks.append(lax.dot_general(qe, ke, (((1,), (1,)), ((), ())), preferred_element_type=f32))
            blocks.append(a_diag)
            if I < nsub - 1:
                blocks.append(jnp.zeros((SUB, C - (I + 1) * SUB), f32))
            a_rows.append(jnp.concatenate(blocks, axis=1))
        a = jnp.concatenate(a_rows, axis=0).astype(bf16)
        st = st_ref[0, h]
        qd = (q * jnp.exp(b)).astype(bf16)
        o = (jnp.dot(a, v, preferred_element_type=f32)
             + lax.dot_general(qd, st.astype(bf16), (((1,), (1,)), ((), ())), preferred_element_type=f32))
        kd = (k * jnp.exp(b_last - b)).astype(bf16)
        st_ref[0, h] = (st * jnp.exp(b_last)
                        + lax.dot_general(v, kd, (((0,), (0,)), ((), ())), preferred_element_type=f32))
        y = o * lax.rsqrt(jnp.mean(o * o, axis=1, keepdims=True) + 1e-6) * gn_ref[...]
        rr = r_ref[0, :, vv]
        o_ref[0, :, vv] = y * (rr * jax.nn.sigmoid(rr))


def gla_mix(qk, v, gd, r, S0, gate_up, gate_b, gn):
    f32 = jnp.float32
    B, T = qk.shape[:2]
    KW, VW = GLA_KW, GLA_VW
    logg = jax.nn.log_sigmoid(gd[..., :GLA_RANK] @ gate_up + gate_b) / GLA_TAU
    C = GLA_CHUNK if T % GLA_CHUNK == 0 else GLA_SUB
    tp = -(-T // C) * C
    if tp != T:
        pad = lambda x: jnp.pad(x, ((0, 0), (0, tp - T), (0, 0)))
        qk, v, r, logg = pad(qk), pad(v), pad(r), pad(logg)
    tokv = pl.BlockSpec((1, C, VW), lambda b, c: (b, c, 0))
    stt = pl.BlockSpec((1, GLA_HEADS, GLA_DV, GLA_DK), lambda b, c: (b, 0, 0, 0))
    out, st = pl.pallas_call(
        _gla_body,
        grid=(B, tp // C),
        in_specs=[pl.BlockSpec((1, C, KW), lambda b, c: (b, c, 0)),
                  pl.BlockSpec((1, C, KW), lambda b, c: (b, c, 1)),
                  tokv, pl.BlockSpec((1, C, KW), lambda b, c: (b, c, 0)), tokv,
                  pl.BlockSpec((1, GLA_DV), lambda b, c: (0, 0)), stt],
        out_specs=[tokv, stt],
        out_shape=[jax.ShapeDtypeStruct((B, tp, VW), f32),
                   jax.ShapeDtypeStruct((B, GLA_HEADS, GLA_DV, GLA_DK), f32)],
        compiler_params=pltpu.CompilerParams(
            dimension_semantics=("parallel", "arbitrary"), vmem_limit_bytes=VMEM_LIMIT),
        name="gla_chunk",
    )(qk, qk, v, logg, r, gn.reshape(1, GLA_DV), S0.astype(f32).transpose(0, 1, 3, 2))
    return out[:, :T], st.transpose(0, 1, 3, 2)


def kernel(x_prompt, x_sample, cache_nsa_kv, cache_nsa_win, state_rwkv, state_rwkv_shift, state_gla, page_table, norm_mix, norm_ffn, w_in_even, nsa_qnorm, nsa_knorm, cmp_w1, cmp_w2, cmp_pe, rwkv_mu, rwkv_w0, rwkv_wB, rwkv_a0, rwkv_aB, rwkv_gB, rwkv_kk, rwkv_ka, rwkv_rk, rwkv_ln_w, rwkv_ln_b, w_out_even, w_in_odd, gla_gate_up, gla_gate_b, gla_norm, w_out_odd, ffn_gate, ffn_up, ffn_down):
    dt = x_prompt.dtype
    bf16 = jnp.bfloat16
    B, T = x_prompt.shape[:2]
    DB, S = x_sample.shape[:2]
    depth = norm_mix.shape[0]
    P = page_table.shape[1] * PAGE_SIZE
    WB = cache_nsa_win.shape[2]
    WP = min(WINDOW, T)
    pos_p = jnp.arange(T, dtype=jnp.int32)
    pos_s = P + jnp.arange(S, dtype=jnp.int32)
    win_pos_s = P - WB + jnp.arange(WB + S, dtype=jnp.int32)
    nsa_pad = -(-NSA_IN // LANE) * LANE
    y_p, y_s = x_prompt, x_sample
    kv_p, kv_s, win_p, win_s, rw_p, rw_s, sh_p, sh_s, gl_p, gl_s = ([] for _ in range(10))
    for layer in range(depth):
        li = layer // 2
        if layer % 2 == 0:
            w_in = jnp.concatenate([jnp.pad(w_in_even[li][:, :NSA_IN], ((0, 0), (0, nsa_pad - NSA_IN))),
                                    w_in_even[li][:, NSA_IN:]], axis=1).astype(bf16)
            even_splits = ((0, nsa_pad), (nsa_pad, nsa_pad + RWKV_IN))
            w_out = w_out_even[li].astype(bf16)
            hn_p, hr_p = norm_proj(y_p, norm_mix[layer], w_in, even_splits)
            hn_s, hr_s = norm_proj(y_s, norm_mix[layer], w_in, even_splits)
            q, g, rows, win = nsa_project(hn_p, pos_p, nsa_qnorm[li], nsa_knorm[li])
            o_nsa_p = nsa_prompt(q, g, rows, win, cmp_w1[li], cmp_w2[li], cmp_pe[li])
            kv_p.append(rows.astype(dt))
            win_p.append(win[:, T - WP:].astype(dt))
            q, g, rows, win = nsa_project(hn_s, pos_s, nsa_qnorm[li], nsa_knorm[li])
            pool = cache_nsa_kv.reshape(-1, PAGE_SIZE, 4 * NSA_KV * HEAD_DIM)
            past = pool[li * cache_nsa_kv.shape[1] + page_table].reshape(DB, P, 4 * NSA_KV * HEAD_DIM)
            win_all = jnp.concatenate([cache_nsa_win[li], win.astype(cache_nsa_win.dtype)], axis=1)
            o_nsa_s = nsa_sample(q, g, past, rows.astype(past.dtype), win_all, cmp_w1[li], cmp_w2[li], cmp_pe[li])
            kv_s.append(rows.astype(dt))
            win_s.append(win_all[:, S:].astype(dt))
            rw_par = (rwkv_mu[li], rwkv_w0[li], rwkv_wB[li], rwkv_a0[li], rwkv_aB[li], rwkv_gB[li],
                      rwkv_kk[li], rwkv_ka[li], rwkv_rk[li], rwkv_ln_w[li], rwkv_ln_b[li])
            o_rw_p, st, last = rwkv_mix(hr_p, jnp.zeros((B, RWKV_IN), dt),
                                        jnp.zeros((B, RWKV_HEADS, HEAD_DIM, HEAD_DIM), dt), *rw_par)
            rw_p.append(st.astype(dt))
            sh_p.append(last.astype(dt))
            o_rw_s, st, last = rwkv_mix(hr_s, state_rwkv_shift[li], state_rwkv[li], *rw_par)
            rw_s.append(st.astype(dt))
            sh_s.append(last.astype(dt))
            w_outs = [w_out[:NSA_Q], w_out[NSA_Q:]]
            y_p = proj_residual([o_nsa_p.astype(dt), o_rw_p.astype(dt)], w_outs, y_p)
            y_s = proj_residual([o_nsa_s.astype(dt), o_rw_s.astype(dt)], w_outs, y_s)
        else:
            wo = w_in_odd[li]
            o2 = 2 * GLA_KW + GLA_VW
            w_in = jnp.concatenate([wo[:, :o2], wo[:, o2 + GLA_RANK:],
                                    jnp.pad(wo[:, o2:o2 + GLA_RANK], ((0, 0), (0, LANE - GLA_RANK)))],
                                   axis=1).astype(bf16)
            odd_splits = ((0, 2 * GLA_KW), (2 * GLA_KW, o2), (o2, o2 + GLA_VW), (o2 + GLA_VW, o2 + GLA_VW + LANE))
            w_out = w_out_odd[li].astype(bf16)
            qk_p, v_p, r_p, gd_p = norm_proj(y_p, norm_mix[layer], w_in, odd_splits)
            qk_s, v_s, r_s, gd_s = norm_proj(y_s, norm_mix[layer], w_in, odd_splits)
            o_p, st = gla_mix(qk_p, v_p, gd_p, r_p, jnp.zeros((B, GLA_HEADS, GLA_DK, GLA_DV), dt),
                              gla_gate_up[li], gla_gate_b[li], gla_norm[li])
            gl_p.append(st.astype(dt))
            o_s, st = gla_mix(qk_s, v_s, gd_s, r_s, state_gla[li], gla_gate_up[li], gla_gate_b[li], gla_norm[li])
            gl_s.append(st.astype(dt))
            y_p = proj_residual([o_p.astype(dt)], [w_out], y_p)
            y_s = proj_residual([o_s.astype(dt)], [w_out], y_s)
        wg, wu, wd = ffn_gate[layer].astype(bf16), ffn_up[layer].astype(bf16), ffn_down[layer].astype(bf16)
        y_p = ffn(y_p, norm_ffn[layer], wg, wu, wd)
        y_s = ffn(y_s, norm_ffn[layer], wg, wu, wd)
    return (y_p, y_s, jnp.stack(kv_p), jnp.stack(kv_s), jnp.stack(win_p), jnp.stack(win_s),
            jnp.stack(rw_p), jnp.stack(rw_s), jnp.stack(sh_p), jnp.stack(sh_s),
            jnp.stack(gl_p), jnp.stack(gl_s))
```

```python
import functools

import jax
import jax.numpy as jnp
from jax import lax
from jax.experimental import pallas as pl
from jax.experimental.pallas import tpu as pltpu

D_MODEL = 1024
PAGE_SIZE = 128
HEAD_DIM = 64
NSA_HEADS = 8
NSA_KV = 2
NSA_REP = NSA_HEADS // NSA_KV
CMP_STRIDE = 16
CMP_BLOCK = 2 * CMP_STRIDE
CMP_HIDDEN = 2 * HEAD_DIM
SEL_BLOCK = 64
N_SELECT = 16
WINDOW = 512
ROPE_DIM = HEAD_DIM // 4
ROPE_THETA = 500000.0
RWKV_HEADS = 8
RWKV_WIDTH = RWKV_HEADS * HEAD_DIM
LORA_W = 64
LORA_A = 64
LORA_G = 128
RWKV_GN_EPS = 64e-5
GLA_HEADS = 4
GLA_DK = D_MODEL // 2 // GLA_HEADS
GLA_DV = D_MODEL // GLA_HEADS
GLA_RANK = 16
GLA_TAU = 16.0
GLA_CHUNK = 64
D_FF = ((8 * D_MODEL // 3 + 255) // 256) * 256
NSA_Q = NSA_HEADS * HEAD_DIM
NSA_KVW = 3 * 2 * NSA_KV * HEAD_DIM
NSA_IN = NSA_Q + NSA_KVW + 3 * NSA_HEADS
RWKV_IN = 3 * RWKV_WIDTH + LORA_W + LORA_A + LORA_G
EVEN_IN = NSA_IN + RWKV_IN
MIX_WIDTH = NSA_Q + RWKV_WIDTH
GLA_KW = GLA_HEADS * GLA_DK
GLA_VW = GLA_HEADS * GLA_DV
ODD_IN = 2 * GLA_KW + GLA_VW + GLA_RANK + GLA_VW

LANE = 128
VMEM_LIMIT = 48 * 1024 * 1024


ROW_TILE = 512
NORM_EPS = 1e-6


def _row_tile(m):
    return ROW_TILE if m % ROW_TILE == 0 else m


def _norm_bf16(x_ref, g_ref):
    x = x_ref[...]
    y = x * lax.rsqrt(jnp.mean(x * x, axis=-1, keepdims=True) + NORM_EPS)
    return (y * g_ref[...]).astype(jnp.bfloat16)


def _norm_proj_body(x_ref, g_ref, w_ref, *o_refs, splits):
    xn = _norm_bf16(x_ref, g_ref)
    for (a, b), o_ref in zip(splits, o_refs):
        o_ref[...] = jnp.dot(xn, w_ref[:, a:b], preferred_element_type=jnp.float32)


def norm_proj(x, gain, w, splits):
    bsz, t, k = x.shape
    m = bsz * t
    tm = _row_tile(m)
    outs = pl.pallas_call(
        functools.partial(_norm_proj_body, splits=splits),
        grid=(m // tm,),
        in_specs=[pl.BlockSpec((tm, k), lambda i: (i, 0)),
                  pl.BlockSpec((1, k), lambda i: (0, 0)),
                  pl.BlockSpec(w.shape, lambda i: (0, 0))],
        out_specs=[pl.BlockSpec((tm, b - a), lambda i: (i, 0)) for a, b in splits],
        out_shape=[jax.ShapeDtypeStruct((m, b - a), jnp.float32) for a, b in splits],
        compiler_params=pltpu.CompilerParams(dimension_semantics=("parallel",), vmem_limit_bytes=VMEM_LIMIT),
        name="norm_proj",
    )(x.reshape(m, k), gain.reshape(1, k), w)
    return [o.reshape(bsz, t, -1) for o in outs]


def _proj_res_body(*refs, n_in):
    x_refs, w_refs, res_ref, o_ref = refs[:n_in], refs[n_in:2 * n_in], refs[2 * n_in], refs[2 * n_in + 1]
    acc = res_ref[...]
    for x_ref, w_ref in zip(x_refs, w_refs):
        acc = acc + jnp.dot(x_ref[...].astype(jnp.bfloat16), w_ref[...], preferred_element_type=jnp.float32)
    o_ref[...] = acc


def proj_residual(xs, ws, res):
    bsz, t, n = res.shape
    m = bsz * t
    tm = _row_tile(m)
    out = pl.pallas_call(
        functools.partial(_proj_res_body, n_in=len(xs)),
        grid=(m // tm,),
        in_specs=([pl.BlockSpec((tm, x.shape[-1]), lambda i: (i, 0)) for x in xs]
                  + [pl.BlockSpec(w.shape, lambda i: (0, 0)) for w in ws]
                  + [pl.BlockSpec((tm, n), lambda i: (i, 0))]),
        out_specs=pl.BlockSpec((tm, n), lambda i: (i, 0)),
        out_shape=jax.ShapeDtypeStruct((m, n), jnp.float32),
        compiler_params=pltpu.CompilerParams(dimension_semantics=("parallel",), vmem_limit_bytes=VMEM_LIMIT),
        name="proj_residual",
    )(*[x.reshape(m, x.shape[-1]) for x in xs], *ws, res.reshape(m, n))
    return out.reshape(bsz, t, n)


FFN_COL_CHUNKS = 2


def _ffn_up_body(x_ref, g_ref, wg_ref, wu_ref, h_ref):
    xn = _norm_bf16(x_ref, g_ref)
    cw = h_ref.shape[1] // FFN_COL_CHUNKS
    for c in range(FFN_COL_CHUNKS):
        cols = slice(c * cw, (c + 1) * cw)
        g = jnp.dot(xn, wg_ref[:, cols], preferred_element_type=jnp.float32)
        u = jnp.dot(xn, wu_ref[:, cols], preferred_element_type=jnp.float32)
        h_ref[:, cols] = (g * jax.nn.sigmoid(g) * u).astype(h_ref.dtype)


def ffn(y, gain, wg, wu, wd):
    bsz, t, k = y.shape
    m = bsz * t
    tm = _row_tile(m)
    f = wg.shape[1]
    h = pl.pallas_call(
        _ffn_up_body,
        grid=(m // tm,),
        in_specs=[pl.BlockSpec((tm, k), lambda i: (i, 0)),
                  pl.BlockSpec((1, k), lambda i: (0, 0)),
                  pl.BlockSpec(wg.shape, lambda i: (0, 0)),
                  pl.BlockSpec(wu.shape, lambda i: (0, 0))],
        out_specs=pl.BlockSpec((tm, f), lambda i: (i, 0)),
        out_shape=jax.ShapeDtypeStruct((m, f), jnp.bfloat16),
        compiler_params=pltpu.CompilerParams(dimension_semantics=("parallel",), vmem_limit_bytes=VMEM_LIMIT),
        name="ffn_up",
    )(y.reshape(m, k), gain.reshape(1, k), wg, wu)
    return proj_residual([h.reshape(bsz, t, f)], [wd], y)


def rmsnorm(x, g, eps=1e-6):
    xf = x.astype(jnp.float32)
    y = xf * lax.rsqrt(jnp.mean(xf * xf, axis=-1, keepdims=True) + eps)
    return (y * g.astype(jnp.float32)).astype(x.dtype)


def rope(x, pos):
    half = ROPE_DIM // 2
    inv = ROPE_THETA ** (-jnp.arange(half, dtype=jnp.float32) / half)
    ang = pos.astype(jnp.float32)[:, None] * inv[None, :]
    shp = (pos.shape[0],) + (1,) * (x.ndim - 3) + (half,)
    cos, sin = jnp.cos(ang).reshape(shp), jnp.sin(ang).reshape(shp)
    xr = x[..., :ROPE_DIM].astype(jnp.float32)
    x1, x2 = xr[..., :half], xr[..., half:]
    rot = jnp.concatenate([x1 * cos - x2 * sin, x2 * cos + x1 * sin], axis=-1)
    return jnp.concatenate([rot.astype(x.dtype), x[..., ROPE_DIM:]], axis=-1)


def nsa_project(h, pos, qn, kn):
    B, T = h.shape[:2]
    q = h[..., :NSA_Q].reshape(B, T, NSA_HEADS, HEAD_DIM)
    kv = h[..., NSA_Q:NSA_Q + NSA_KVW].reshape(B, T, 3, 2, NSA_KV, HEAD_DIM)
    gates = jax.nn.sigmoid(h[..., NSA_Q + NSA_KVW:NSA_IN].astype(jnp.float32)).reshape(B, T, NSA_HEADS, 3)
    q = rope(rmsnorm(q, qn), pos)
    k = rope(rmsnorm(kv[:, :, :, 0], kn[:, None, :]), pos)
    kv = jnp.stack([k, kv[:, :, :, 1]], axis=3)
    rows = kv[:, :, :2].reshape(B, T, 4, NSA_KV, HEAD_DIM)
    win = kv[:, :, 2]
    return q, gates, rows, win


NEG = -1e30


NSA_TQ = 256
NSA_TK = 512
NSA_TQ_DECODE = 32


def _nsa_attn_body(qt_ref, kc_ref, vct_ref, map_ref, ks_ref, vst_ref, kw_ref, vw_ref, g_ref, o_ref, thr_scr,
                   *, tq, n_cmp, n_sel, q_tile0, win_base, natural):
    f32, bf16 = jnp.float32, jnp.bfloat16
    tk = NSA_TK
    nl = NSA_REP * tq
    i = pl.program_id(1) + q_tile0
    if natural:
        xq = qt_ref[0] * (HEAD_DIM ** -0.5)
        qt = jnp.concatenate([xq[:, HEAD_DIM * r:HEAD_DIM * (r + 1)].T for r in range(NSA_REP)], axis=1).astype(bf16)
    else:
        qt = qt_ref[0, 0]
    qpos = i * tq + lax.broadcasted_iota(jnp.int32, (1, nl), 1) % tq

    ncp = kc_ref.shape[1]
    n = lax.broadcasted_iota(jnp.int32, (ncp, 1), 0)
    ok = (n * CMP_STRIDE + (CMP_BLOCK - 1) <= qpos) & (n < n_cmp)
    s = jnp.dot(kc_ref[0], qt, preferred_element_type=f32)
    s = jnp.where(ok, s, NEG)
    m = jnp.max(s, axis=0, keepdims=True)
    e = jnp.where(ok, jnp.exp(s - m), 0.0)
    l = jnp.sum(e, axis=0, keepdims=True)
    p = e * jnp.where(l > 0.0, 1.0 / l, 0.0)
    o_cmp = jnp.dot(vct_ref[0], p.astype(bf16), preferred_element_type=f32)

    psum = p[:, 0:tq]
    for r in range(1, NSA_REP):
        psum = psum + p[:, r * tq:(r + 1) * tq]
    hi = psum.astype(bf16)
    lo = (psum - hi.astype(f32)).astype(bf16)
    imp = (jnp.dot(map_ref[...], hi, preferred_element_type=f32)
           + jnp.dot(map_ref[...], lo, preferred_element_type=f32))
    nsp = map_ref.shape[0]
    qp = qpos[:, 0:tq]
    blk = lax.broadcasted_iota(jnp.int32, (nsp, 1), 0)
    cur = qp // SEL_BLOCK
    valid = (blk * SEL_BLOCK <= qp) & (blk < n_sel)
    forced = (blk == 0) | (blk == cur) | (blk == cur - 1)
    x = jnp.where(valid, jnp.where(forced, 1e30, imp), NEG)
    x = jnp.where(blk < n_sel, x, -3e38)
    blkf = blk.astype(f32)
    sel = jnp.zeros((nsp, tq), f32)
    for _ in range(min(N_SELECT, n_sel)):
        mx = jnp.max(x, axis=0, keepdims=True)
        first = jnp.min(jnp.where(x == mx, blkf, float(nsp)), axis=0, keepdims=True)
        hit = blkf == first
        sel = jnp.where(hit, 1.0, sel)
        x = jnp.where(hit, -3e38, x)
    thr = jnp.where(valid & (sel > 0.5), qp, -1)
    thr_scr[...] = jnp.concatenate([thr] * NSA_REP, axis=1)

    def flash(s, vt, carry):
        m_old, l_old, acc = carry
        m_new = jnp.maximum(m_old, jnp.max(s, axis=0, keepdims=True))
        alpha = jnp.exp(m_old - m_new)
        p = jnp.exp(s - m_new)
        l_new = alpha * l_old + jnp.sum(p, axis=0, keepdims=True)
        acc = alpha * acc + jnp.dot(vt, p.astype(bf16), preferred_element_type=f32)
        return m_new, l_new, acc

    init = (jnp.full((1, nl), NEG, f32), jnp.zeros((1, nl), f32), jnp.zeros((HEAD_DIM, nl), f32))

    krow = lax.broadcasted_iota(jnp.int32, (SEL_BLOCK, 1), 0)

    def masked_scores(j):
        s = jnp.dot(ks_ref[0, j], qt, preferred_element_type=f32)
        th = thr_scr[pl.ds(pl.multiple_of(j * (tk // SEL_BLOCK), 8), tk // SEL_BLOCK), :]
        return jnp.concatenate(
            [jnp.where((j * tk + b * SEL_BLOCK + krow) <= th[b:b + 1], s[b * SEL_BLOCK:(b + 1) * SEL_BLOCK], NEG)
             for b in range(tk // SEL_BLOCK)], axis=0)

    def sel_pair(j2, carry):
        m_old, l_old, acc = carry
        s0, s1 = masked_scores(2 * j2), masked_scores(2 * j2 + 1)
        m_new = jnp.maximum(m_old, jnp.maximum(jnp.max(s0, axis=0, keepdims=True), jnp.max(s1, axis=0, keepdims=True)))
        alpha = jnp.exp(m_old - m_new)
        p0, p1 = jnp.exp(s0 - m_new), jnp.exp(s1 - m_new)
        l_new = alpha * l_old + jnp.sum(p0, axis=0, keepdims=True) + jnp.sum(p1, axis=0, keepdims=True)
        acc = (alpha * acc + jnp.dot(vst_ref[0, 2 * j2], p0.astype(bf16), preferred_element_type=f32)
               + jnp.dot(vst_ref[0, 2 * j2 + 1], p1.astype(bf16), preferred_element_type=f32))
        return m_new, l_new, acc

    n_tiles = (i * tq + tq - 1) // tk + 1
    carry = lax.fori_loop(0, n_tiles // 2, sel_pair, init)
    carry = lax.cond(n_tiles % 2 == 1,
                     lambda c: flash(masked_scores(n_tiles - 1), vst_ref[0, n_tiles - 1], c),
                     lambda c: c, carry)
    _, l_s, acc_s = carry
    o_slc = acc_s / l_s

    nkw = WINDOW + tq
    k_first = i * tq - WINDOW
    row0 = pl.multiple_of(k_first - win_base, 16)
    kpos = k_first + lax.broadcasted_iota(jnp.int32, (nkw, 1), 0)
    allowed = (kpos <= qpos) & (kpos > qpos - WINDOW) & (kpos >= 0)
    s = jnp.dot(kw_ref[0, pl.ds(row0, nkw), :], qt, preferred_element_type=f32)
    s = jnp.where(allowed, s, NEG)
    e = jnp.where(allowed, jnp.exp(s - jnp.max(s, axis=0, keepdims=True)), 0.0)
    o_win = (lax.dot_general(vw_ref[0, pl.ds(row0, nkw), :], e.astype(bf16), (((0,), (0,)), ((), ())),
                             preferred_element_type=f32) / jnp.sum(e, axis=0, keepdims=True))

    g = g_ref[0, 0]
    res = g[0:1] * o_cmp + g[1:2] * o_slc + g[2:3] * o_win
    if natural:
        o_ref[0] = jnp.concatenate([res[:, tq * r:tq * (r + 1)].T for r in range(NSA_REP)], axis=1)
    else:
        o_ref[0, 0] = res


def nsa_attention(q, gates, kc, vc, ksl, vsl, kwin, vwin, *, n_keys, tq=NSA_TQ, q_pos0=0, win_pos0=0):
    f32, bf16 = jnp.float32, jnp.bfloat16
    B, T = q.shape[:2]
    G, R, D = NSA_KV, NSA_REP, HEAD_DIM
    tk = NSA_TK
    nq, nl = T // tq, R * tq
    n_cmp, n_sel = kc.shape[1], -(-n_keys // SEL_BLOCK)
    lk = ksl.shape[1]
    assert T % tq == 0 and nl % LANE == 0 and lk % tk == 0 and q_pos0 % tq == 0 and tq % 16 == 0
    assert lk >= q_pos0 + T and win_pos0 <= max(q_pos0 - WINDOW, 0)
    win_base = win_pos0 - WINDOW
    lw = q_pos0 + T - win_base
    wpad = lambda x: jnp.pad(x[:, :lw - WINDOW], ((0, 0), (WINDOW, max(lw - WINDOW - x.shape[1], 0)), (0, 0), (0, 0)))
    kwin, vwin = wpad(kwin), wpad(vwin)
    ncp = -(-n_cmp // LANE) * LANE
    nsp = -(-n_sel // LANE) * LANE
    n = jnp.arange(ncp)[None, :]
    jb = jnp.arange(nsp)[:, None]
    map_t = ((n * CMP_STRIDE <= jb * SEL_BLOCK + SEL_BLOCK - 1) & (n * CMP_STRIDE + CMP_BLOCK - 1 >= jb * SEL_BLOCK)
             & (n < n_cmp) & (jb < n_sel)).astype(bf16)
    natural = tq % LANE == 0
    if natural:
        qt = q.reshape(B, T, NSA_Q)
        q_spec = pl.BlockSpec((1, tq, R * D), lambda b, i: (b // G, i, b % G))
        o_spec, o_shape = q_spec, jax.ShapeDtypeStruct((B, T, NSA_Q), f32)
    else:
        qt = (q * (D ** -0.5)).reshape(B, nq, tq, G, R, D).transpose(0, 3, 1, 5, 4, 2)
        qt = qt.reshape(B * G, nq, D, nl).astype(bf16)
        q_spec = pl.BlockSpec((1, 1, D, nl), lambda b, i: (b, i, 0, 0))
        o_spec, o_shape = q_spec, jax.ShapeDtypeStruct((B * G, nq, D, nl), f32)
    gt = gates.reshape(B, nq, tq, G, R, 3).transpose(0, 3, 1, 5, 4, 2).reshape(B * G, nq, 3, nl)
    gt = jnp.pad(gt, ((0, 0), (0, 0), (0, 5), (0, 0)))
    rows = lambda x, t: x.transpose(0, 2, 1, 3).reshape(B * G, -1, t, D).astype(bf16)
    cols = lambda x, t: x.reshape(B, -1, t, G, D).transpose(0, 3, 1, 4, 2).reshape(B * G, -1, D, t).astype(bf16)
    kcp = jnp.pad(kc, ((0, 0), (0, ncp - n_cmp), (0, 0), (0, 0)))
    vcp = jnp.pad(vc, ((0, 0), (0, ncp - n_cmp), (0, 0), (0, 0)))
    kc_r = rows(kcp, ncp)[:, 0]
    vc_c = cols(vcp, ncp)[:, 0]
    full = lambda shape: pl.BlockSpec((1,) + shape, lambda b, i: (b,) + (0,) * len(shape))
    per_q = lambda shape: pl.BlockSpec((1, 1) + shape, lambda b, i: (b, i) + (0,) * len(shape))
    out = pl.pallas_call(
        functools.partial(_nsa_attn_body, tq=tq, n_cmp=n_cmp, n_sel=n_sel,
                          q_tile0=q_pos0 // tq, win_base=win_base, natural=natural),
        grid=(B * G, nq),
        in_specs=[q_spec, full((ncp, D)), full((D, ncp)),
                  pl.BlockSpec((nsp, ncp), lambda b, i: (0, 0)),
                  full((lk // tk, tk, D)), full((lk // tk, D, tk)),
                  full((lw, D)), full((lw, D)),
                  per_q((8, nl))],
        out_specs=o_spec,
        out_shape=o_shape,
        scratch_shapes=[pltpu.VMEM((nsp, nl), jnp.int32)],
        compiler_params=pltpu.CompilerParams(
            dimension_semantics=("parallel", "parallel"), vmem_limit_bytes=VMEM_LIMIT),
        name="nsa_attention",
    )(qt, kc_r, vc_c, map_t, rows(ksl, tk), cols(vsl, tk), rows(kwin, lw)[:, 0], rows(vwin, lw)[:, 0], gt)
    if natural:
        return out
    return out.reshape(B, G, nq, D, R, tq).transpose(0, 2, 5, 1, 4, 3).reshape(B, T, NSA_Q)


def nsa_prompt(q, gates, rows, win, cw1, cw2, cpe):
    B, T = q.shape[:2]
    kc = compress_pair(rows[:, :, 0].reshape(B, T, -1), cw1[0], cw2[0], cpe[0])
    vc = compress_pair(rows[:, :, 1].reshape(B, T, -1), cw1[1], cw2[1], cpe[1])
    return nsa_attention(q, gates, kc, vc, rows[:, :, 2], rows[:, :, 3], win[:, :, 0], win[:, :, 1],
                         n_keys=T)


def compress_pair(x, w1, w2, pe):
    B, L, gd = x.shape
    n_chunks = L // CMP_STRIDE
    c = x[:, :n_chunks * CMP_STRIDE].reshape(B, n_chunks, CMP_STRIDE, gd)
    eye = jnp.eye(NSA_KV, dtype=w1.dtype)
    big = lambda w: jnp.einsum('ldh,gk->lgdkh', w, eye).reshape(CMP_STRIDE, gd, NSA_KV * CMP_HIDDEN)
    a = jnp.einsum('bnlx,lxy->bny', c, big(w1[:CMP_STRIDE]))
    bc = jnp.einsum('bnlx,lxy->bny', c, big(w1[CMP_STRIDE:]))
    h = a[:, :-1] + bc[:, 1:] + jnp.tile(jnp.einsum('ld,ldh->h', pe, w1), NSA_KV)
    w2_big = jnp.einsum('hd,gk->ghkd', w2, eye).reshape(NSA_KV * CMP_HIDDEN, gd)
    return (jax.nn.gelu(h) @ w2_big).reshape(B, n_chunks - 1, NSA_KV, HEAD_DIM)


def nsa_sample(q, gates, past, rows, win_all, cw1, cw2, cpe):
    DB, S = q.shape[:2]
    P, WB = past.shape[1], win_all.shape[1] - S
    gd = NSA_KV * HEAD_DIM
    kind = lambda w: past[:, :, w * gd:(w + 1) * gd]
    new = lambda w: rows[:, :, w].reshape(DB, S, gd)
    up = lambda n, m: -(-n // m) * m
    pad_rows = lambda x, n: jnp.pad(x, ((0, 0), (0, n - x.shape[1])) + ((0, 0),) * (x.ndim - 2))
    n_chunk_rows = (P + S) // CMP_STRIDE * CMP_STRIDE
    cmp_in = [kind(w)[:, :n_chunk_rows] if n_chunk_rows <= P
              else jnp.concatenate([kind(w), new(w)[:, :n_chunk_rows - P]], axis=1) for w in (0, 1)]
    kc = compress_pair(cmp_in[0], cw1[0], cw2[0], cpe[0])
    vc = compress_pair(cmp_in[1], cw1[1], cw2[1], cpe[1])
    tq = up(S, NSA_TQ_DECODE)
    lk = up(P + tq, NSA_TK)
    groups = lambda x: x.reshape(DB, -1, NSA_KV, HEAD_DIM)
    ksl = groups(pad_rows(jnp.concatenate([kind(2), new(2)], axis=1), lk))
    vsl = groups(pad_rows(jnp.concatenate([kind(3), new(3)], axis=1), lk))
    o = nsa_attention(pad_rows(q, tq), pad_rows(gates, tq), kc, vc, ksl, vsl, win_all[:, :, 0], win_all[:, :, 1],
                      n_keys=P + S, tq=tq, q_pos0=P, win_pos0=P - WB)
    return o[:, :S]


RW_PAIRS = RWKV_HEADS // 2
RW_CHUNK = 64


def _rwkv_scan_body(r_ref, w_ref, k_ref, kk_ref, b_ref, vt_ref, s0_ref, oh_ref,
                    y_ref, st_ref, vhi_scr, yacc_scr, *, nb, tc, natural):
    f32, bf16 = jnp.float32, jnp.bfloat16
    c = pl.program_id(1)
    eye = (lax.broadcasted_iota(jnp.int32, (HEAD_DIM, HEAD_DIM), 0)
           == lax.broadcasted_iota(jnp.int32, (HEAD_DIM, HEAD_DIM), 1)).astype(bf16)

    def halves(x, terms):
        out, rest = None, x
        for _ in range(terms):
            piece = rest.astype(bf16)
            rest = rest - piece.astype(f32)
            t = jnp.concatenate([lax.dot_general(eye, piece[:, HEAD_DIM * h:HEAD_DIM * (h + 1)],
                                                 (((1,), (1,)), ((), ())), preferred_element_type=f32)
                                 for h in range(2)], axis=1)
            out = t if out is None else out + t
        return out

    @pl.when(c == 0)
    def _():
        st_ref[...] = s0_ref[...]

    row = lax.broadcasted_iota(jnp.int32, (LANE, LANE), 0)
    col = lax.broadcasted_iota(jnp.int32, (LANE, LANE), 1)
    ones1 = ((row // HEAD_DIM) == (col // HEAD_DIM)).astype(bf16)
    lane_t = lax.broadcasted_iota(jnp.int32, (HEAD_DIM, LANE), 1) % HEAD_DIM

    for b in range(nb):
        for p in range(RW_PAIRS):
            vt = halves(vt_ref[b, :, LANE * p:LANE * (p + 1)], 1) if natural else vt_ref[b, 0, p]
            vhi_scr[b * RW_PAIRS + p] = vt.astype(bf16)
    yacc_scr[...] = jnp.zeros_like(yacc_scr)

    pairs = [(b, p) for b in range(nb) for p in range(RW_PAIRS)]
    n_pairs = len(pairs)

    def group(t8, carry):
        t0 = pl.multiple_of(t8 * 8, 8)
        for j in range(0, 8, 2):
            ra = lambda ref, b, sl: ref[b, pl.ds(t0, 8), sl][j:j + 1]
            rb = lambda ref, b, sl: ref[b, pl.ds(t0, 8), sl][j + 1:j + 2]
            oh_a, oh_b = oh_ref[t0 + j], oh_ref[t0 + j + 1]
            cr = jnp.concatenate([ra(x_ref, b, slice(LANE * p, LANE * (p + 1)))
                                  * rb(kk_ref, b, slice(LANE * p, LANE * (p + 1)))
                                  for b, p in pairs for x_ref in (b_ref, k_ref)], axis=0)
            cr_hi = cr.astype(bf16)
            c_all = (jnp.dot(cr_hi, ones1, preferred_element_type=f32)
                     + jnp.dot((cr - cr_hi.astype(f32)).astype(bf16), ones1, preferred_element_type=f32))
            vl = jnp.concatenate([vhi_scr[i] * oh for oh in (oh_a, oh_b) for i in range(n_pairs)], axis=0)
            vb_all = jnp.dot(vl, ones1, preferred_element_type=f32)
            pieces = []
            for b, p in pairs:
                sl = slice(LANE * p, LANE * (p + 1))
                s = st_ref[b, p]
                pieces.append((s * ra(kk_ref, b, sl)).astype(bf16))
                pieces.append((s * (ra(w_ref, b, sl) * rb(kk_ref, b, sl))).astype(bf16))
            u_all = jnp.dot(jnp.concatenate(pieces, axis=0), ones1, preferred_element_type=f32)
            py = []
            for i, (b, p) in enumerate(pairs):
                sl = slice(LANE * p, LANE * (p + 1))
                u1 = u_all[2 * HEAD_DIM * i:2 * HEAD_DIM * i + HEAD_DIM]
                u2 = u_all[2 * HEAD_DIM * i + HEAD_DIM:2 * HEAD_DIM * (i + 1)]
                vb_a = vb_all[HEAD_DIM * i:HEAD_DIM * (i + 1)]
                vb_b = vb_all[HEAD_DIM * (n_pairs + i):HEAD_DIM * (n_pairs + i + 1)]
                sa_b = u2 - u1 * c_all[2 * i:2 * i + 1] + vb_a * c_all[2 * i + 1:2 * i + 2]
                s_a = st_ref[b, p] * ra(w_ref, b, sl) - u1 * ra(b_ref, b, sl) + vb_a * ra(k_ref, b, sl)
                s_b = s_a * rb(w_ref, b, sl) - sa_b * rb(b_ref, b, sl) + vb_b * rb(k_ref, b, sl)
                st_ref[b, p] = s_b
                py.append((s_a * ra(r_ref, b, sl)).astype(bf16))
                py.append((s_b * rb(r_ref, b, sl)).astype(bf16))
            y_all = jnp.dot(jnp.concatenate(py, axis=0), ones1, preferred_element_type=f32)
            m_a, m_b = lane_t == t0 + j, lane_t == t0 + j + 1
            for i, (b, p) in enumerate(pairs):
                y_a = y_all[2 * HEAD_DIM * i:2 * HEAD_DIM * i + HEAD_DIM]
                y_b = y_all[2 * HEAD_DIM * i + HEAD_DIM:2 * HEAD_DIM * (i + 1)]
                yacc_scr[i] = jnp.where(m_a, y_a, jnp.where(m_b, y_b, yacc_scr[i]))
        return carry

    lax.fori_loop(0, tc // 8, group, 0)
    for i, (b, p) in enumerate(pairs):
        if natural:
            y_ref[b, :, LANE * p:LANE * (p + 1)] = halves(yacc_scr[i], 3)
        else:
            y_ref[b, 0, p] = yacc_scr[i]


def rwkv_scan(r, w, k, v, kk, b, S0):
    f32 = jnp.float32
    B, T, W = r.shape
    tc = RW_CHUNK if T % RW_CHUNK == 0 else T
    assert tc <= RW_CHUNK and T % tc == 0 and tc % 8 == 0 and B % 2 == 0
    nc, nb = T // tc, 2
    natural = tc == RW_CHUNK
    tok = pl.BlockSpec((nb, tc, W), lambda i, c: (i, c, 0))
    chk = pl.BlockSpec((nb, 1, RW_PAIRS, HEAD_DIM, LANE), lambda i, c: (i, c, 0, 0, 0))
    if natural:
        vt, vy_spec, y_shape = v, tok, jax.ShapeDtypeStruct((B, T, W), f32)
    else:
        vt = v.reshape(B, nc, tc, RW_PAIRS, 2, HEAD_DIM).transpose(0, 1, 3, 5, 4, 2)
        vt = jnp.pad(vt, ((0, 0),) * 5 + ((0, RW_CHUNK - tc),)).reshape(B, nc, RW_PAIRS, HEAD_DIM, LANE)
        vy_spec, y_shape = chk, jax.ShapeDtypeStruct((B, nc, RW_PAIRS, HEAD_DIM, LANE), f32)
    s0 = S0.astype(f32).reshape(B, RW_PAIRS, 2, HEAD_DIM, HEAD_DIM).transpose(0, 1, 3, 2, 4)
    s0 = s0.reshape(B, RW_PAIRS, HEAD_DIM, LANE)
    oh = jnp.arange(LANE)[None, None, :] % HEAD_DIM == jnp.arange(RW_CHUNK)[:, None, None]
    oh = jnp.broadcast_to(oh, (RW_CHUNK, HEAD_DIM, LANE)).astype(jnp.bfloat16)
    stt = pl.BlockSpec((nb, RW_PAIRS, HEAD_DIM, LANE), lambda i, c: (i, 0, 0, 0))
    y, st = pl.pallas_call(
        functools.partial(_rwkv_scan_body, nb=nb, tc=tc, natural=natural),
        grid=(B // nb, nc),
        in_specs=[tok, tok, tok, tok, tok, vy_spec, stt,
                  pl.BlockSpec((RW_CHUNK, HEAD_DIM, LANE), lambda i, c: (0, 0, 0))],
        out_specs=[vy_spec, stt],
        out_shape=[y_shape, jax.ShapeDtypeStruct((B, RW_PAIRS, HEAD_DIM, LANE), f32)],
        scratch_shapes=[pltpu.VMEM((nb * RW_PAIRS, HEAD_DIM, LANE), jnp.bfloat16),
                        pltpu.VMEM((nb * RW_PAIRS, HEAD_DIM, LANE), f32)],
        compiler_params=pltpu.CompilerParams(
            dimension_semantics=("parallel", "arbitrary"), vmem_limit_bytes=VMEM_LIMIT),
        name="rwkv_scan",
    )(r, w, k, kk, b, vt, s0, oh)
    if not natural:
        y = y.reshape(B, nc, RW_PAIRS, HEAD_DIM, 2, RW_CHUNK)[..., :tc]
        y = y.transpose(0, 1, 5, 2, 4, 3).reshape(B, T, W)
    st = st.reshape(B, RW_PAIRS, HEAD_DIM, 2, HEAD_DIM).transpose(0, 1, 3, 2, 4)
    return y, st.reshape(B, RWKV_HEADS, HEAD_DIM, HEAD_DIM)


def rwkv_mix(p, prev, S0, mu, w0, wB, a0, aB, gB, k_k, k_a, r_k, ln_w, ln_b):
    f32 = jnp.float32
    B, T = p.shape[:2]
    W = RWKV_WIDTH
    p = p.astype(f32)
    p_prev = jnp.concatenate([prev.astype(f32)[:, None], p[:, :-1]], axis=1)
    ps = p + mu * (p_prev - p)
    r, k, v = ps[..., :W], ps[..., W:2 * W], ps[..., 2 * W:3 * W]
    o = 3 * W
    xw, xa, xg = ps[..., o:o + LORA_W], ps[..., o + LORA_W:o + LORA_W + LORA_A], ps[..., o + LORA_W + LORA_A:]
    z = w0 + jnp.tanh(xw) @ wB
    w = jnp.exp(-jnp.exp(-jax.nn.softplus(-z) - 0.5))
    a = jax.nn.sigmoid(a0 + xa @ aB)
    g = jax.nn.sigmoid(xg) @ gB
    heads = lambda t: t.reshape(B, T, RWKV_HEADS, HEAD_DIM)
    hsel = (jnp.arange(W)[:, None] // HEAD_DIM == jnp.arange(RWKV_HEADS)[None, :]).astype(f32)
    kk = k * k_k
    n2 = jnp.einsum('btx,xh->bth', kk * kk, hsel, precision=lax.Precision.HIGHEST)
    kk = kk * jnp.einsum('bth,xh->btx', lax.rsqrt(jnp.maximum(n2, 1e-24)), hsel, precision=lax.Precision.HIGHEST)
    k = k * (1.0 + (a - 1.0) * k_a)
    y, S = rwkv_scan(r, w, k, v, kk, kk * a, S0)
    y, k = heads(y), heads(k)
    r, v = heads(r), heads(v)
    mean = jnp.mean(y, axis=-1, keepdims=True)
    var = jnp.mean(jnp.square(y - mean), axis=-1, keepdims=True)
    y = ((y - mean) * lax.rsqrt(var + RWKV_GN_EPS)).reshape(B, T, W) * ln_w + ln_b
    y = y + (jnp.sum(r * k * r_k, axis=-1, keepdims=True) * v).reshape(B, T, W)
    return y * g, S, p[:, -1]


GLA_SUB = 16


def _gla_body(q_ref, k_ref, v_ref, g_ref, r_ref, gn_ref, s0_ref, o_ref, st_ref):
    f32, bf16 = jnp.float32, jnp.bfloat16
    C, SUB = q_ref.shape[1], GLA_SUB
    nsub = C // SUB
    c = pl.program_id(1)

    @pl.when(c == 0)
    def _():
        st_ref[...] = s0_ref[...]

    ti = lax.broadcasted_iota(jnp.int32, (C, C), 0)
    si = lax.broadcasted_iota(jnp.int32, (C, C), 1)
    tri = (si <= ti).astype(bf16)
    sub_t = lax.broadcasted_iota(jnp.int32, (SUB, 1), 0)
    sub_l = lax.broadcasted_iota(jnp.int32, (SUB, SUB), 1)
    for h in range(GLA_HEADS):
        kq = slice(h * GLA_DK, (h + 1) * GLA_DK)
        vv = slice(h * GLA_DV, (h + 1) * GLA_DV)
        q = q_ref[0, :, kq] * (GLA_DK ** -0.5)
        k = k_ref[0, :, kq]
        v = v_ref[0, :, vv].astype(bf16)
        g = g_ref[0, :, kq]
        g1 = g.astype(bf16)
        g2 = (g - g1.astype(f32)).astype(bf16)
        g3 = (g - g1.astype(f32) - g2.astype(f32)).astype(bf16)
        b = (jnp.dot(tri, g1, preferred_element_type=f32) + jnp.dot(tri, g2, preferred_element_type=f32)
             + jnp.dot(tri, g3, preferred_element_type=f32))
        b_last = b[C - 1:C]
        a_rows = []
        for I in range(nsub):
            rows = slice(I * SUB, (I + 1) * SUB)
            beta = b[I * SUB - 1:I * SUB] if I > 0 else jnp.zeros((1, GLA_DK), f32)
            qb, kb, bb = q[rows], k[rows], b[rows]
            a_diag = jnp.zeros((SUB, SUB), f32)
            for s in range(SUB):
                d = jnp.where(sub_t >= s, bb - bb[s:s + 1], -jnp.inf)
                col = jnp.sum(qb * kb[s:s + 1] * jnp.exp(d), axis=1, keepdims=True)
                a_diag = jnp.where(sub_l == s, col, a_diag)
            blocks = []
            if I > 0:
                qe = (qb * jnp.exp(bb - beta)).astype(bf16)
                ke = (k[:I * SUB] * jnp.exp(beta - b[:I * SUB])).astype(bf16)
                blocks.append(lax.dot_general(qe, ke, (((1,), (1,)), ((), ())), preferred_element_type=f32))
            blocks.append(a_diag)
            if I < nsub - 1:
                blocks.append(jnp.zeros((SUB, C - (I + 1) * SUB), f32))
            a_rows.append(jnp.concatenate(blocks, axis=1))
        a = jnp.concatenate(a_rows, axis=0).astype(bf16)
        st = st_ref[0, h]
        qd = (q * jnp.exp(b)).astype(bf16)
        o = (jnp.dot(a, v, preferred_element_type=f32)
             + lax.dot_general(qd, st.astype(bf16), (((1,), (1,)), ((), ())), preferred_element_type=f32))
        kd = (k * jnp.exp(b_last - b)).astype(bf16)
        st_ref[0, h] = (st * jnp.exp(b_last)
                        + lax.dot_general(v, kd, (((0,), (0,)), ((), ())), preferred_element_type=f32))
        y = o * lax.rsqrt(jnp.mean(o * o, axis=1, keepdims=True) + 1e-6) * gn_ref[...]
        rr = r_ref[0, :, vv]
        o_ref[0, :, vv] = y * (rr * jax.nn.sigmoid(rr))


def gla_mix(qk, v, gd, r, S0, gate_up, gate_b, gn):
    f32 = jnp.float32
    B, T = qk.shape[:2]
    KW, VW = GLA_KW, GLA_VW
    logg = jax.nn.log_sigmoid(gd[..., :GLA_RANK] @ gate_up + gate_b) / GLA_TAU
    C = GLA_CHUNK if T % GLA_CHUNK == 0 else GLA_SUB
    tp = -(-T // C) * C
    if tp != T:
        pad = lambda x: jnp.pad(x, ((0, 0), (0, tp - T), (0, 0)))
        qk, v, r, logg = pad(qk), pad(v), pad(r), pad(logg)
    tokv = pl.BlockSpec((1, C, VW), lambda b, c: (b, c, 0))
    stt = pl.BlockSpec((1, GLA_HEADS, GLA_DV, GLA_DK), lambda b, c: (b, 0, 0, 0))
    out, st = pl.pallas_call(
        _gla_body,
        grid=(B, tp // C),
        in_specs=[pl.BlockSpec((1, C, KW), lambda b, c: (b, c, 0)),
                  pl.BlockSpec((1, C, KW), lambda b, c: (b, c, 1)),
                  tokv, pl.BlockSpec((1, C, KW), lambda b, c: (b, c, 0)), tokv,
                  pl.BlockSpec((1, GLA_DV), lambda b, c: (0, 0)), stt],
        out_specs=[tokv, stt],
        out_shape=[jax.ShapeDtypeStruct((B, tp, VW), f32),
                   jax.ShapeDtypeStruct((B, GLA_HEADS, GLA_DV, GLA_DK), f32)],
        compiler_params=pltpu.CompilerParams(
            dimension_semantics=("parallel", "arbitrary"), vmem_limit_bytes=VMEM_LIMIT),
        name="gla_chunk",
    )(qk, qk, v, logg, r, gn.reshape(1, GLA_DV), S0.astype(f32).transpose(0, 1, 3, 2))
    return out[:, :T], st.transpose(0, 1, 3, 2)


def kernel(x_prompt, x_sample, cache_nsa_kv, cache_nsa_win, state_rwkv, state_rwkv_shift, state_gla, page_table, norm_mix, norm_ffn, w_in_even, nsa_qnorm, nsa_knorm, cmp_w1, cmp_w2, cmp_pe, rwkv_mu, rwkv_w0, rwkv_wB, rwkv_a0, rwkv_aB, rwkv_gB, rwkv_kk, rwkv_ka, rwkv_rk, rwkv_ln_w, rwkv_ln_b, w_out_even, w_in_odd, gla_gate_up, gla_gate_b, gla_norm, w_out_odd, ffn_gate, ffn_up, ffn_down):
    dt = x_prompt.dtype
    bf16 = jnp.bfloat16
    B, T = x_prompt.shape[:2]
    DB, S = x_sample.shape[:2]
    depth = norm_mix.shape[0]
    P = page_table.shape[1] * PAGE_SIZE
    WB = cache_nsa_win.shape[2]
    WP = min(WINDOW, T)
    pos_p = jnp.arange(T, dtype=jnp.int32)
    pos_s = P + jnp.arange(S, dtype=jnp.int32)
    win_pos_s = P - WB + jnp.arange(WB + S, dtype=jnp.int32)
    nsa_pad = -(-NSA_IN // LANE) * LANE
    y_p, y_s = x_prompt, x_sample
    kv_p, kv_s, win_p, win_s, rw_p, rw_s, sh_p, sh_s, gl_p, gl_s = ([] for _ in range(10))
    for layer in range(depth):
        li = layer // 2
        if layer % 2 == 0:
            w_in = jnp.concatenate([jnp.pad(w_in_even[li][:, :NSA_IN], ((0, 0), (0, nsa_pad - NSA_IN))),
                                    w_in_even[li][:, NSA_IN:]], axis=1).astype(bf16)
            even_splits = ((0, nsa_pad), (nsa_pad, nsa_pad + RWKV_IN))
            w_out = w_out_even[li].astype(bf16)
            hn_p, hr_p = norm_proj(y_p, norm_mix[layer], w_in, even_splits)
            hn_s, hr_s = norm_proj(y_s, norm_mix[layer], w_in, even_splits)
            q, g, rows, win = nsa_project(hn_p, pos_p, nsa_qnorm[li], nsa_knorm[li])
            o_nsa_p = nsa_prompt(q, g, rows, win, cmp_w1[li], cmp_w2[li], cmp_pe[li])
            kv_p.append(rows.astype(dt))
            win_p.append(win[:, T - WP:].astype(dt))
            q, g, rows, win = nsa_project(hn_s, pos_s, nsa_qnorm[li], nsa_knorm[li])
            pool = cache_nsa_kv.reshape(-1, PAGE_SIZE, 4 * NSA_KV * HEAD_DIM)
            past = pool[li * cache_nsa_kv.shape[1] + page_table].reshape(DB, P, 4 * NSA_KV * HEAD_DIM)
            win_all = jnp.concatenate([cache_nsa_win[li], win.astype(cache_nsa_win.dtype)], axis=1)
            o_nsa_s = nsa_sample(q, g, past, rows.astype(past.dtype), win_all, cmp_w1[li], cmp_w2[li], cmp_pe[li])
            kv_s.append(rows.astype(dt))
            win_s.append(win_all[:, S:].astype(dt))
            rw_par = (rwkv_mu[li], rwkv_w0[li], rwkv_wB[li], rwkv_a0[li], rwkv_aB[li], rwkv_gB[li],
                      rwkv_kk[li], rwkv_ka[li], rwkv_rk[li], rwkv_ln_w[li], rwkv_ln_b[li])
            o_rw_p, st, last = rwkv_mix(hr_p, jnp.zeros((B, RWKV_IN), dt),
                                        jnp.zeros((B, RWKV_HEADS, HEAD_DIM, HEAD_DIM), dt), *rw_par)
            rw_p.append(st.astype(dt))
            sh_p.append(last.astype(dt))
            o_rw_s, st, last = rwkv_mix(hr_s, state_rwkv_shift[li], state_rwkv[li], *rw_par)
            rw_s.append(st.astype(dt))
            sh_s.append(last.astype(dt))
            w_outs = [w_out[:NSA_Q], w_out[NSA_Q:]]
            y_p = proj_residual([o_nsa_p.astype(dt), o_rw_p.astype(dt)], w_outs, y_p)
            y_s = proj_residual([o_nsa_s.astype(dt), o_rw_s.astype(dt)], w_outs, y_s)
        else:
            wo = w_in_odd[li]
            o2 = 2 * GLA_KW + GLA_VW
            w_in = jnp.concatenate([wo[:, :o2], wo[:, o2 + GLA_RANK:],
                                    jnp.pad(wo[:, o2:o2 + GLA_RANK], ((0, 0), (0, LANE - GLA_RANK)))],
                                   axis=1).astype(bf16)
            odd_splits = ((0, 2 * GLA_KW), (2 * GLA_KW, o2), (o2, o2 + GLA_VW), (o2 + GLA_VW, o2 + GLA_VW + LANE))
            w_out = w_out_odd[li].astype(bf16)
            qk_p, v_p, r_p, gd_p = norm_proj(y_p, norm_mix[layer], w_in, odd_splits)
            qk_s, v_s, r_s, gd_s = norm_proj(y_s, norm_mix[layer], w_in, odd_splits)
            o_p, st = gla_mix(qk_p, v_p, gd_p, r_p, jnp.zeros((B, GLA_HEADS, GLA_DK, GLA_DV), dt),
                              gla_gate_up[li], gla_gate_b[li], gla_norm[li])
            gl_p.append(st.astype(dt))
            o_s, st = gla_mix(qk_s, v_s, gd_s, r_s, state_gla[li], gla_gate_up[li], gla_gate_b[li], gla_norm[li])
            gl_s.append(st.astype(dt))
            y_p = proj_residual([o_p.astype(dt)], [w_out], y_p)
            y_s = proj_residual([o_s.astype(dt)], [w_out], y_s)
        wg, wu, wd = ffn_gate[layer].astype(bf16), ffn_up[layer].astype(bf16), ffn_down[layer].astype(bf16)
        y_p = ffn(y_p, norm_ffn[layer], wg, wu, wd)
        y_s = ffn(y_s, norm_ffn[layer], wg, wu, wd)
    return (y_p, y_s, jnp.stack(kv_p), jnp.stack(kv_s), jnp.stack(win_p), jnp.stack(win_s),
            jnp.stack(rw_p), jnp.stack(rw_s), jnp.stack(sh_p), jnp.stack(sh_s),
            jnp.stack(gl_p), jnp.stack(gl_s))
```
